```python
import math
import jax, jax.numpy as jnp
from jax import lax
import numpy as np

D_MODEL = 4096
BATCH = 4
SEQ = 4096
DEPTH = 1

GRID_W = 64
CTX_LEN = 256
D_MIX = D_MODEL
HEAD_DIM = 128
D_SSM = D_MIX // 4
D_ATTN = D_MIX - D_SSM
N_HEADS = D_ATTN // HEAD_DIM
WIN_ROWS = 8
WIN_COLS = 16
ROPE_THETA = 10000.0
SSM_GROUP_CH = 16
SSM_GROUPS = D_SSM // SSM_GROUP_CH
SSM_STATE = 64
N_IN = 3 * D_ATTN + D_SSM
N_EXPERTS = 128
TOP_K = 8
D_EXPERT = (3 * D_MODEL) // 32
D_SHARED = D_EXPERT
ROUTED_SCALE = 2.5
EXPERT_BLOCK = 128
EPS = 1e-6
NEG_INF = -1e30

kernel_name = 'hybrid_natten_s5_moe_diffusion_block'


def rms_norm(t, g):
    tf = t.astype(jnp.float32)
    n = tf * lax.rsqrt(jnp.mean(tf * tf, axis=-1, keepdims=True) + EPS)
    return (n * g.astype(jnp.float32)).astype(t.dtype)


def modulate(t, g, shift, scale):
    return rms_norm(t, g) * (1.0 + scale) + shift


def axial_rope(t, rows, cols):
    half = HEAD_DIM // 2
    inv_freq = 1.0 / (ROPE_THETA ** (jnp.arange(0, half, 2, dtype=jnp.float32) / half))

    def rot(xh, pos):
        ang = pos.astype(jnp.float32)[:, None] * inv_freq[None, :]
        cos = jnp.cos(ang)[None, :, None, :]
        sin = jnp.sin(ang)[None, :, None, :]
        x1, x2 = jnp.split(xh, 2, axis=-1)
        return jnp.concatenate([x1 * cos - x2 * sin, x1 * sin + x2 * cos], axis=-1)

    tf = t.astype(jnp.float32)
    out = jnp.concatenate([rot(tf[..., :half], rows), rot(tf[..., half:], cols)], axis=-1)
    return out.astype(t.dtype)


def neighborhood_attention(q_rot, q_raw, k_rot, v, k_ctx, v_ctx, rpb):
    b, l, h, d = q_rot.shape
    rows_n = l // GRID_W
    kh = min(WIN_ROWS, rows_n)
    n_loc = kh * GRID_W
    scale = HEAD_DIM ** -0.5
    q5 = q_rot.reshape(b, rows_n, GRID_W, h, d)
    qn5 = q_raw.reshape(b, rows_n, GRID_W, h, d)
    k5 = k_rot.reshape(b, rows_n, GRID_W, h, d)
    v5 = v.reshape(b, rows_n, GRID_W, h, d)
    col = jnp.arange(GRID_W)
    cs = jnp.clip(col - WIN_COLS // 2, 0, GRID_W - WIN_COLS)
    col_mask = (col[None, :] >= cs[:, None]) & (col[None, :] < cs[:, None] + WIN_COLS)
    col_off = jnp.clip(col[None, :] - col[:, None], -(WIN_COLS - 1), WIN_COLS - 1) + (WIN_COLS - 1)
    rpb_cols = rpb[:, :, col_off]
    key_mask = jnp.tile(col_mask, (1, kh))

    def one_row(r):
        rs = jnp.clip(r - kh // 2, 0, rows_n - kh)
        qr = lax.dynamic_index_in_dim(q5, r, axis=1, keepdims=False)
        qn = lax.dynamic_index_in_dim(qn5, r, axis=1, keepdims=False)
        kb = lax.dynamic_slice_in_dim(k5, rs, kh, axis=1).reshape(b, n_loc, h, d)
        vb = lax.dynamic_slice_in_dim(v5, rs, kh, axis=1).reshape(b, n_loc, h, d)
        row_off = rs + jnp.arange(kh) - r + (WIN_ROWS - 1)
        bias = jnp.transpose(rpb_cols[:, row_off], (0, 2, 1, 3)).reshape(h, GRID_W, n_loc)
        s_loc = jnp.einsum('bqhd,bkhd->bhqk', qr, kb).astype(jnp.float32) * scale + bias.astype(jnp.float32)
        s_loc = jnp.where(key_mask, s_loc, NEG_INF)
        s_ctx = jnp.einsum('bqhd,bchd->bhqc', qn, k_ctx).astype(jnp.float32) * scale
        p = jax.nn.softmax(jnp.concatenate([s_loc, s_ctx], axis=-1), axis=-1).astype(v.dtype)
        return (jnp.einsum('bhqk,bkhd->bqhd', p[..., :n_loc], vb)
                + jnp.einsum('bhqc,bchd->bqhd', p[..., n_loc:], v_ctx))

    out = lax.map(one_row, jnp.arange(rows_n))
    return jnp.transpose(out, (1, 0, 2, 3, 4)).reshape(b, l, h * d)


def context_attention(q, k, v):
    b, lc, h, d = q.shape
    s = jnp.einsum('bqhd,bkhd->bhqk', q, k).astype(jnp.float32) * (HEAD_DIM ** -0.5)
    p = jax.nn.softmax(s, axis=-1).astype(v.dtype)
    return jnp.einsum('bhqk,bkhd->bqhd', p, v).reshape(b, lc, h * d)


def _ssm_combine(e1, e2):
    a1, b1 = e1
    a2, b2 = e2
    return a1 * a2, a2 * b1 + b2


def ssm_discretise(a_re, a_im, log_dt, b_re, b_im):
    a = lax.complex(a_re.astype(jnp.float32), a_im.astype(jnp.float32))
    dt = jnp.exp(log_dt.astype(jnp.float32))[..., None]
    a_bar = jnp.exp(dt * a)
    b_c = lax.complex(b_re.astype(jnp.float32), b_im.astype(jnp.float32))
    b_bar = ((a_bar - 1.0) / a)[..., None] * b_c
    return a_bar, b_bar


def ssm_states(u, a_bar, b_bar, h0, reverse):
    bu = jnp.einsum('blgn,gpn->blgp', u.astype(jnp.complex64), b_bar)
    a = jnp.broadcast_to(a_bar, bu.shape)
    a_cum, hs = lax.associative_scan(_ssm_combine, (a, bu), axis=1, reverse=reverse)
    if h0 is not None:
        hs = hs + a_cum * h0[:, None]
    return hs


def ssm_readout(c_mat, hs):
    return jnp.einsum('gnp,blgp->blgn', c_mat, hs).real


def s5_glu(y, w_glu, b_glu):
    z = jax.nn.gelu(y)
    return z * jax.nn.sigmoid(z @ w_glu.astype(jnp.float32) + b_glu.astype(jnp.float32))


def s5_mixer(u_lat, u_ctx, a_re, a_im, log_dt, b_re, b_im, c_re, c_im, d_skip, w_glu, b_glu, with_ctx_out):
    b, l = u_lat.shape[:2]
    a_bar, b_bar = ssm_discretise(a_re, a_im, log_dt, b_re, b_im)
    c_mat = lax.complex(c_re.astype(jnp.float32), c_im.astype(jnp.float32))
    d = d_skip.astype(jnp.float32)
    uc = u_ctx.astype(jnp.float32)
    ul = u_lat.astype(jnp.float32)
    hc_f = ssm_states(uc, a_bar[0], b_bar[0], None, False)
    hc_b = ssm_states(uc, a_bar[1], b_bar[1], None, True)
    y_lat = (ssm_readout(c_mat[0], ssm_states(ul, a_bar[0], b_bar[0], hc_f[:, -1], False))
             + ssm_readout(c_mat[1], ssm_states(ul, a_bar[1], b_bar[1], hc_b[:, 0], True))
             + d * ul)
    out_lat = s5_glu(y_lat.reshape(b, l, D_SSM), w_glu, b_glu).astype(u_lat.dtype)
    if not with_ctx_out:
        return out_lat, None
    y_ctx = ssm_readout(c_mat[0], hc_f) + ssm_readout(c_mat[1], hc_b) + d * uc
    out_ctx = s5_glu(y_ctx.reshape(b, uc.shape[1], D_SSM), w_glu, b_glu).astype(u_ctx.dtype)
    return out_lat, out_ctx


def merge_groups(attn, ssm, g_attn, g_ssm, w_out):
    return jnp.concatenate([rms_norm(attn, g_attn), rms_norm(ssm, g_ssm)], axis=-1) @ w_out


def token_mixer(h_lat, h_ctx, w_in, rpb, a_re, a_im, log_dt, b_re, b_im, c_re, c_im, d_skip,
                w_glu, b_glu, g_attn_out, g_ssm_out, w_out, with_ctx_out):
    b, l, _ = h_lat.shape
    lc = h_ctx.shape[1]
    rows = jnp.arange(l) // GRID_W
    cols = jnp.arange(l) % GRID_W
    p_lat = h_lat @ w_in
    q_l = p_lat[..., :D_ATTN].reshape(b, l, N_HEADS, HEAD_DIM)
    k_l = p_lat[..., D_ATTN:2 * D_ATTN].reshape(b, l, N_HEADS, HEAD_DIM)
    v_l = p_lat[..., 2 * D_ATTN:3 * D_ATTN].reshape(b, l, N_HEADS, HEAD_DIM)
    u_l = p_lat[..., 3 * D_ATTN:].reshape(b, l, SSM_GROUPS, SSM_GROUP_CH)
    p_ctx = h_ctx @ (w_in if with_ctx_out else w_in[:, D_ATTN:])
    kvu = p_ctx[..., -(2 * D_ATTN + D_SSM):]
    k_c = kvu[..., :D_ATTN].reshape(b, lc, N_HEADS, HEAD_DIM)
    v_c = kvu[..., D_ATTN:2 * D_ATTN].reshape(b, lc, N_HEADS, HEAD_DIM)
    u_c = kvu[..., 2 * D_ATTN:].reshape(b, lc, SSM_GROUPS, SSM_GROUP_CH)

    attn_l = neighborhood_attention(axial_rope(q_l, rows, cols), q_l, axial_rope(k_l, rows, cols),
                                    v_l, k_c, v_c, rpb)
    ssm_l, ssm_c = s5_mixer(u_l, u_c, a_re, a_im, log_dt, b_re, b_im, c_re, c_im, d_skip,
                            w_glu, b_glu, with_ctx_out)
    out_l = merge_groups(attn_l, ssm_l, g_attn_out, g_ssm_out, w_out)
    if not with_ctx_out:
        return out_l, None
    q_c = p_ctx[..., :D_ATTN].reshape(b, lc, N_HEADS, HEAD_DIM)
    attn_c = context_attention(q_c, k_c, v_c)
    out_c = merge_groups(attn_c, ssm_c, g_attn_out, g_ssm_out, w_out)
    return out_l, out_c


def swiglu(t, wg, wu, wd):
    return (jax.nn.silu(t @ wg) * (t @ wu)) @ wd


def moe_ffn(t, w_router, b_router, wg, wu, wd, wsg, wsu, wsd):
    n, dm = t.shape
    scores = jax.nn.sigmoid((t @ w_router).astype(jnp.float32))
    _, idx = lax.top_k(scores + b_router.astype(jnp.float32), TOP_K)
    sel = jnp.take_along_axis(scores, idx, axis=-1)
    gates = sel / jnp.sum(sel, axis=-1, keepdims=True) * ROUTED_SCALE
    m = n * TOP_K
    flat_e = idx.reshape(m)
    order = jnp.argsort(flat_e)
    sorted_e = flat_e[order]
    counts = jnp.bincount(flat_e, length=N_EXPERTS)
    padded = (counts + EXPERT_BLOCK - 1) // EXPERT_BLOCK * EXPERT_BLOCK
    pad_end = jnp.cumsum(padded)
    pad_start = pad_end - padded
    start = jnp.cumsum(counts) - counts
    dest = pad_start[sorted_e] + jnp.arange(m) - start[sorted_e]
    n_blocks = -(-m // EXPERT_BLOCK) + N_EXPERTS
    slots = n_blocks * EXPERT_BLOCK
    slot_tok = jnp.full((slots,), n, jnp.int32).at[dest].set((order // TOP_K).astype(jnp.int32))
    slot_gate = jnp.zeros((slots,), jnp.float32).at[dest].set(gates.reshape(m)[order])
    block_e = jnp.minimum(jnp.searchsorted(pad_end, jnp.arange(n_blocks) * EXPERT_BLOCK, side='right'),
                          N_EXPERTS - 1)
    t_pad = jnp.concatenate([t, jnp.zeros((1, dm), t.dtype)], axis=0)

    def expert_block(acc, blk):
        tok, g, e = blk
        yb = swiglu(t_pad[tok], wg[e], wu[e], wd[e]) * g[:, None].astype(t.dtype)
        return acc.at[tok].add(yb), None

    routed, _ = lax.scan(expert_block, jnp.zeros((n + 1, dm), t.dtype),
                         (slot_tok.reshape(n_blocks, EXPERT_BLOCK),
                          slot_gate.reshape(n_blocks, EXPERT_BLOCK), block_e))
    return routed[:n] + swiglu(t, wsg, wsu, wsd)


def setup_inputs(seed: int = 0) -> dict:
    key = jax.random.key(seed)
    ks = jax.random.split(key, 40)
    f32 = jnp.float32
    nrm = lambda k, shape, s: jax.random.normal(k, shape, f32) * s
    gain = lambda k, shape: 1.0 + 0.02 * jax.random.normal(k, shape, f32)
    a_im = (jnp.pi * jnp.arange(SSM_STATE, dtype=f32))[None, None, None, :] \
        + 0.01 * jax.random.normal(ks[12], (DEPTH, 2, SSM_GROUPS, SSM_STATE), f32)
    return {
        'x': nrm(ks[0], (BATCH, SEQ, D_MODEL), 1.0),
        'c': nrm(ks[1], (BATCH, D_MODEL), 1.0),
        'ctx': nrm(ks[2], (BATCH, CTX_LEN, D_MODEL), 1.0),
        'c_ctx': nrm(ks[3], (D_MODEL,), 1.0),
        'w_ada': nrm(ks[4], (DEPTH, D_MODEL, 6 * D_MODEL), D_MODEL ** -0.5),
        'b_ada': nrm(ks[5], (DEPTH, 6 * D_MODEL), 0.01),
        'g_pre_mix': gain(ks[6], (DEPTH, D_MODEL)),
        'g_post_mix': gain(ks[7], (DEPTH, D_MODEL)),
        'g_pre_ffn': gain(ks[8], (DEPTH, D_MODEL)),
        'g_post_ffn': gain(ks[9], (DEPTH, D_MODEL)),
        'w_in': nrm(ks[10], (DEPTH, D_MODEL, N_IN), D_MODEL ** -0.5),
        'rpb': nrm(ks[11], (DEPTH, N_HEADS, 2 * WIN_ROWS - 1, 2 * WIN_COLS - 1), 0.02),
        'ssm_a_re': -0.5 * jnp.exp(0.05 * jax.random.normal(ks[13], (DEPTH, 2, SSM_GROUPS, SSM_STATE), f32)),
        'ssm_a_im': a_im,
        'ssm_log_dt': jax.random.uniform(ks[14], (DEPTH, 2, SSM_GROUPS), f32, math.log(1e-3), math.log(1e-1)),
        'ssm_b_re': nrm(ks[15], (DEPTH, 2, SSM_GROUPS, SSM_STATE, SSM_GROUP_CH), (2 * SSM_GROUP_CH) ** -0.5),
        'ssm_b_im': nrm(ks[16], (DEPTH, 2, SSM_GROUPS, SSM_STATE, SSM_GROUP_CH), (2 * SSM_GROUP_CH) ** -0.5),
        'ssm_c_re': nrm(ks[17], (DEPTH, 2, SSM_GROUPS, SSM_GROUP_CH, SSM_STATE), (2 * SSM_STATE) ** -0.5),
        'ssm_c_im': nrm(ks[18], (DEPTH, 2, SSM_GROUPS, SSM_GROUP_CH, SSM_STATE), (2 * SSM_STATE) ** -0.5),
        'ssm_d': nrm(ks[19], (DEPTH, SSM_GROUPS, SSM_GROUP_CH), 1.0),
        'w_glu': nrm(ks[20], (DEPTH, D_SSM, D_SSM), D_SSM ** -0.5),
        'b_glu': nrm(ks[21], (DEPTH, D_SSM), 0.01),
        'g_attn_out': gain(ks[22], (DEPTH, D_ATTN)),
        'g_ssm_out': gain(ks[23], (DEPTH, D_SSM)),
        'w_out': nrm(ks[24], (DEPTH, D_MIX, D_MODEL), D_MIX ** -0.5),
        'w_router': nrm(ks[25], (DEPTH, D_MODEL, N_EXPERTS), D_MODEL ** -0.5),
        'b_router': nrm(ks[26], (DEPTH, N_EXPERTS), 0.01),
        'w_exp_gate': nrm(ks[27], (DEPTH, N_EXPERTS, D_MODEL, D_EXPERT), D_MODEL ** -0.5),
        'w_exp_up': nrm(ks[28], (DEPTH, N_EXPERTS, D_MODEL, D_EXPERT), D_MODEL ** -0.5),
        'w_exp_down': nrm(ks[29], (DEPTH, N_EXPERTS, D_EXPERT, D_MODEL), D_EXPERT ** -0.5),
        'w_sh_gate': nrm(ks[30], (DEPTH, D_MODEL, D_SHARED), D_MODEL ** -0.5),
        'w_sh_up': nrm(ks[31], (DEPTH, D_MODEL, D_SHARED), D_MODEL ** -0.5),
        'w_sh_down': nrm(ks[32], (DEPTH, D_SHARED, D_MODEL), D_SHARED ** -0.5),
    }


def reference(x, c, ctx, c_ctx, w_ada, b_ada, g_pre_mix, g_post_mix, g_pre_ffn, g_post_ffn, w_in, rpb,
              ssm_a_re, ssm_a_im, ssm_log_dt, ssm_b_re, ssm_b_im, ssm_c_re, ssm_c_im, ssm_d, w_glu, b_glu,
              g_attn_out, g_ssm_out, w_out, w_router, b_router, w_exp_gate, w_exp_up, w_exp_down,
              w_sh_gate, w_sh_up, w_sh_down):
    b, l, dm = x.shape
    for i in range(DEPTH):
        last = i == DEPTH - 1
        mod = jax.nn.silu(c) @ w_ada[i] + b_ada[i]
        sh_m, sc_m, gt_m, sh_f, sc_f, gt_f = [m[:, None, :] for m in jnp.split(mod, 6, axis=-1)]
        mod_c = jax.nn.silu(c_ctx) @ w_ada[i] + b_ada[i]
        csh_m, csc_m, cgt_m, csh_f, csc_f, cgt_f = jnp.split(mod_c, 6, axis=-1)

        h_lat = modulate(x, g_pre_mix[i], sh_m, sc_m)
        h_ctx = modulate(ctx, g_pre_mix[i], csh_m, csc_m)
        mix_lat, mix_ctx = token_mixer(h_lat, h_ctx, w_in[i], rpb[i], ssm_a_re[i], ssm_a_im[i], ssm_log_dt[i],
                                       ssm_b_re[i], ssm_b_im[i], ssm_c_re[i], ssm_c_im[i], ssm_d[i],
                                       w_glu[i], b_glu[i], g_attn_out[i], g_ssm_out[i], w_out[i],
                                       not last)
        x = x + gt_m * rms_norm(mix_lat, g_post_mix[i])
        h = modulate(x, g_pre_ffn[i], sh_f, sc_f)
        ffn = moe_ffn(h.reshape(b * l, dm), w_router[i], b_router[i], w_exp_gate[i], w_exp_up[i],
                      w_exp_down[i], w_sh_gate[i], w_sh_up[i], w_sh_down[i]).reshape(b, l, dm)
        x = x + gt_f * rms_norm(ffn, g_post_ffn[i])
        if not last:
            ctx = ctx + cgt_m * rms_norm(mix_ctx, g_post_mix[i])
            hc = modulate(ctx, g_pre_ffn[i], csh_f, csc_f)
            ffn_c = moe_ffn(hc.reshape(-1, dm), w_router[i], b_router[i], w_exp_gate[i], w_exp_up[i],
                            w_exp_down[i], w_sh_gate[i], w_sh_up[i], w_sh_down[i]).reshape(ctx.shape)
            ctx = ctx + cgt_f * rms_norm(ffn_c, g_post_ffn[i])
    return x
```

```python
import functools
import math

import numpy as np
import jax
import jax.numpy as jnp
from jax import lax
from jax.experimental import pallas as pl
from jax.experimental.pallas import tpu as pltpu

F32 = jnp.float32
BF16 = jnp.bfloat16

GRID_W = 64
HEAD_DIM = 128
WIN_ROWS = 8
WIN_COLS = 16
ROPE_THETA = 10000.0
SSM_GROUP_CH = 16
SSM_STATE = 64
TOP_K = 8
ROUTED_SCALE = 2.5
EPS = 1e-6
NEG_INF = -1e30

Q_ROWS = 4
BAND_ROWS = Q_ROWS + WIN_ROWS - 1
VMEM_LIMIT = 56 * 1024 * 1024


def _cparams(*sem):
    return pltpu.CompilerParams(dimension_semantics=sem, vmem_limit_bytes=VMEM_LIMIT)


def _ada_kernel(c_ref, w_ref, b_ref, o_ref):
    a = jax.nn.silu(c_ref[...]).astype(BF16)
    o_ref[...] = jnp.dot(a, w_ref[...].astype(BF16), preferred_element_type=F32) + b_ref[...]


def ada_mod(c8, w_ada, b_ada):
    d, n = w_ada.shape
    tn = 512
    return pl.pallas_call(
        _ada_kernel,
        grid=(n // tn,),
        in_specs=[pl.BlockSpec((8, d), lambda j: (0, 0)),
                  pl.BlockSpec((d, tn), lambda j: (0, j)),
                  pl.BlockSpec((1, tn), lambda j: (0, j))],
        out_specs=pl.BlockSpec((8, tn), lambda j: (0, j)),
        out_shape=jax.ShapeDtypeStruct((8, n), F32),
        compiler_params=_cparams("arbitrary"),
        name="ada_mod",
    )(c8, w_ada, b_ada.reshape(1, n))


def _rms(x, g):
    return x * lax.rsqrt(jnp.mean(x * x, axis=-1, keepdims=True) + EPS) * g


def _modproj_kernel(x_ref, g_ref, sc_ref, sh_ref, w_ref, o_ref, h_scr):
    @pl.when(pl.program_id(2) == 0)
    def _():
        h = _rms(x_ref[0], g_ref[...]) * (1.0 + sc_ref[0]) + sh_ref[0]
        h_scr[...] = h.astype(BF16)

    o_ref[0] = jnp.dot(h_scr[...], w_ref[...], preferred_element_type=F32).astype(o_ref.dtype)


def mod_proj(x, g, scale, shift, w_bf16, col_blk_off, n_out, tm, tn):
    b, l, d = x.shape
    return pl.pallas_call(
        _modproj_kernel,
        grid=(b, l // tm, n_out // tn),
        in_specs=[pl.BlockSpec((1, tm, d), lambda bi, i, j: (bi, i, 0)),
                  pl.BlockSpec((1, d), lambda bi, i, j: (0, 0)),
                  pl.BlockSpec((1, 1, d), lambda bi, i, j: (bi, 0, 0)),
                  pl.BlockSpec((1, 1, d), lambda bi, i, j: (bi, 0, 0)),
                  pl.BlockSpec((d, tn), lambda bi, i, j: (0, j + col_blk_off))],
        out_specs=pl.BlockSpec((1, tm, tn), lambda bi, i, j: (bi, i, j)),
        out_shape=jax.ShapeDtypeStruct((b, l, n_out), BF16),
        scratch_shapes=[pltpu.VMEM((tm, d), BF16)],
        compiler_params=_cparams("parallel", "parallel", "arbitrary"),
        name="mod_proj",
    )(x, g.reshape(1, d), scale, shift, w_bf16)


def _rope_tables(l):
    half = HEAD_DIM // 2
    quarter = half // 2
    inv_freq = 1.0 / (ROPE_THETA ** (jnp.arange(0, half, 2, dtype=F32) / half))
    rows = (jnp.arange(l) // GRID_W).astype(F32)
    cols = (jnp.arange(l) % GRID_W).astype(F32)

    def cs(pos):
        ang = pos[:, None] * inv_freq[None, :]
        return jnp.cos(ang), jnp.sin(ang)

    cr, sr = cs(rows)
    cc, sc = cs(cols)
    zero = jnp.zeros((l, quarter), F32)
    cos = jnp.concatenate([cr, cr, cc, cc], axis=-1)
    s_lo = jnp.concatenate([-sr, zero, -sc, zero], axis=-1)
    s_hi = jnp.concatenate([zero, sr, zero, sc], axis=-1)
    return cos, s_lo, s_hi


def _attn_bias(rpb, rows_n):
    kh = min(WIN_ROWS, rows_n)
    cases = [(0, 0), (Q_ROWS, 0), (rows_n - Q_ROWS, rows_n - BAND_ROWS)]
    ro_all, co_all, ok_all = [], [], []
    col = np.arange(GRID_W)
    cstart = np.clip(col - WIN_COLS // 2, 0, GRID_W - WIN_COLS)
    for r0, sb in cases:
        qrow = r0 + np.arange(Q_ROWS)
        krow = sb + np.arange(BAND_ROWS)
        rs = np.clip(qrow - kh // 2, 0, rows_n - kh)
        ok_r = (krow[None, :] >= rs[:, None]) & (krow[None, :] < rs[:, None] + kh)
        ok_c = (col[None, :] >= cstart[:, None]) & (col[None, :] < cstart[:, None] + WIN_COLS)
        ro = np.clip(krow[None, :] - qrow[:, None] + (WIN_ROWS - 1), 0, 2 * WIN_ROWS - 2)
        co = np.clip(col[None, :] - col[:, None], -(WIN_COLS - 1), WIN_COLS - 1) + (WIN_COLS - 1)
        shape = (Q_ROWS, GRID_W, BAND_ROWS, GRID_W)
        ro_all.append(np.broadcast_to(ro[:, None, :, None], shape).reshape(Q_ROWS * GRID_W, BAND_ROWS * GRID_W))
        co_all.append(np.broadcast_to(co[None, :, None, :], shape).reshape(Q_ROWS * GRID_W, BAND_ROWS * GRID_W))
        ok = ok_r[:, None, :, None] & ok_c[None, :, None, :]
        ok_all.append(np.broadcast_to(ok, shape).reshape(Q_ROWS * GRID_W, BAND_ROWS * GRID_W))
    ro_all, co_all, ok_all = np.stack(ro_all), np.stack(co_all), np.stack(ok_all)
    bias = rpb.astype(F32)[:, ro_all, co_all]
    return jnp.where(ok_all[None], bias, NEG_INF)


def _attn_kernel(q_ref, k_ref, v_ref, kc_ref, vc_ref, bias_ref, cos_ref, slo_ref, shi_ref, o_ref,
                 qr_scr, kr_scr, *, n_blk, rows_n):
    quarter = HEAD_DIM // 4
    qn = Q_ROWS * GRID_W
    kn = BAND_ROWS * GRID_W
    scale = HEAD_DIM ** -0.5

    def rope_body(i, _):
        sl = pl.ds(pl.multiple_of(i * qn, qn), qn)
        for src, dst in ((q_ref, qr_scr), (k_ref, kr_scr)):
            x = src[0, sl, :].astype(F32)
            y = (x * cos_ref[sl, :] + pltpu.roll(x, HEAD_DIM - quarter, 1) * slo_ref[sl, :]
                 + pltpu.roll(x, quarter, 1) * shi_ref[sl, :])
            dst[sl, :] = y.astype(BF16)
        return 0

    lax.fori_loop(0, n_blk, rope_body, 0)

    kc = kc_ref[0]
    vc = vc_ref[0]
    nt = (((1,), (1,)), ((), ()))

    def body(i, _):
        sb = jnp.clip(i * Q_ROWS - WIN_ROWS // 2, 0, rows_n - BAND_ROWS)
        ks = pl.ds(pl.multiple_of(sb * GRID_W, GRID_W), kn)
        qs = pl.ds(pl.multiple_of(i * qn, qn), qn)
        case = jnp.where(i == 0, 0, jnp.where(i == n_blk - 1, 2, 1))
        s = lax.dot_general(qr_scr[qs, :], kr_scr[ks, :], nt, preferred_element_type=F32) * scale + bias_ref[0, case]
        sc = lax.dot_general(q_ref[0, qs, :], kc, nt, preferred_element_type=F32) * scale
        m = jnp.maximum(jnp.max(s, axis=-1, keepdims=True), jnp.max(sc, axis=-1, keepdims=True))
        p = jnp.exp(s - m)
        pc = jnp.exp(sc - m)
        den = jnp.sum(p, axis=-1, keepdims=True) + jnp.sum(pc, axis=-1, keepdims=True)
        o = (jnp.dot(p.astype(BF16), v_ref[0, ks, :], preferred_element_type=F32)
             + jnp.dot(pc.astype(BF16), vc, preferred_element_type=F32))
        o_ref[0, qs, :] = (o / den).astype(o_ref.dtype)
        return 0

    lax.fori_loop(0, n_blk, body, 0)


def neighborhood_attention(p_lat, p_ctx, rpb, n_heads):
    b, l, _ = p_lat.shape
    lc = p_ctx.shape[1]
    rows_n = l // GRID_W
    assert rows_n % Q_ROWS == 0 and rows_n >= BAND_ROWS + Q_ROWS
    n_blk = rows_n // Q_ROWS
    bias = _attn_bias(rpb, rows_n)
    cos, s_lo, s_hi = _rope_tables(l)
    qn, kn = Q_ROWS * GRID_W, BAND_ROWS * GRID_W
    h = n_heads
    tab = pl.BlockSpec((l, HEAD_DIM), lambda bi, hi: (0, 0))
    return pl.pallas_call(
        functools.partial(_attn_kernel, n_blk=n_blk, rows_n=rows_n),
        grid=(b, h),
        in_specs=[pl.BlockSpec((1, l, HEAD_DIM), lambda bi, hi: (bi, 0, hi)),
                  pl.BlockSpec((1, l, HEAD_DIM), lambda bi, hi: (bi, 0, hi + h)),
                  pl.BlockSpec((1, l, HEAD_DIM), lambda bi, hi: (bi, 0, hi + 2 * h)),
                  pl.BlockSpec((1, lc, HEAD_DIM), lambda bi, hi: (bi, 0, hi)),
                  pl.BlockSpec((1, lc, HEAD_DIM), lambda bi, hi: (bi, 0, hi + h)),
                  pl.BlockSpec((1, 3, qn, kn), lambda bi, hi: (hi, 0, 0, 0)),
                  tab, tab, tab],
        out_specs=pl.BlockSpec((1, l, HEAD_DIM), lambda bi, hi: (bi, 0, hi)),
        out_shape=jax.ShapeDtypeStruct((b, l, h * HEAD_DIM), BF16),
        scratch_shapes=[pltpu.VMEM((l, HEAD_DIM), BF16), pltpu.VMEM((l, HEAD_DIM), BF16)],
        compiler_params=_cparams("parallel", "parallel"),
        name="nbr_attn",
    )(p_lat, p_lat, p_lat, p_ctx, p_ctx, bias, cos, s_lo, s_hi)


def _glu_kernel(y_ref, w_ref, b_ref, o_ref):
    z = jax.nn.gelu(y_ref[...].astype(F32))
    gate = jax.nn.sigmoid(jnp.dot(z.astype(BF16), w_ref[...], preferred_element_type=F32) + b_ref[...])
    o_ref[...] = (z * gate).astype(o_ref.dtype)


def s5_glu(y, w_bf16, b_glu, tm=512):
    n, d = y.shape
    return pl.pallas_call(
        _glu_kernel,
        grid=(n // tm,),
        in_specs=[pl.BlockSpec((tm, d), lambda i: (i, 0)),
                  pl.BlockSpec((d, d), lambda i: (0, 0)),
                  pl.BlockSpec((1, d), lambda i: (0, 0))],
        out_specs=pl.BlockSpec((tm, d), lambda i: (i, 0)),
        out_shape=jax.ShapeDtypeStruct((n, d), BF16),
        compiler_params=_cparams("parallel"),
        name="s5_glu",
    )(y, w_bf16, b_glu.reshape(1, d))


def _merge_kernel(a_ref, s_ref, ga_ref, gs_ref, w_ref, o_ref, h_scr):
    da = a_ref.shape[-1]

    @pl.when(pl.program_id(1) == 0)
    def _():
        h_scr[:, :da] = _rms(a_ref[...].astype(F32), ga_ref[...]).astype(BF16)
        h_scr[:, da:] = _rms(s_ref[...].astype(F32), gs_ref[...]).astype(BF16)

    o_ref[...] = jnp.dot(h_scr[...], w_ref[...], preferred_element_type=F32)


def merge_proj(attn, ssm, g_attn, g_ssm, w_bf16, tm=512, tn=1024):
    n, da = attn.shape
    ds = ssm.shape[1]
    d, dout = w_bf16.shape
    return pl.pallas_call(
        _merge_kernel,
        grid=(n // tm, dout // tn),
        in_specs=[pl.BlockSpec((tm, da), lambda i, j: (i, 0)),
                  pl.BlockSpec((tm, ds), lambda i, j: (i, 0)),
                  pl.BlockSpec((1, da), lambda i, j: (0, 0)),
                  pl.BlockSpec((1, ds), lambda i, j: (0, 0)),
                  pl.BlockSpec((d, tn), lambda i, j: (0, j))],
        out_specs=pl.BlockSpec((tm, tn), lambda i, j: (i, j)),
        out_shape=jax.ShapeDtypeStruct((n, dout), F32),
        scratch_shapes=[pltpu.VMEM((tm, d), BF16)],
        compiler_params=_cparams("parallel", "arbitrary"),
        name="merge_proj",
    )(attn, ssm, g_attn.reshape(1, da), g_ssm.reshape(1, ds), w_bf16)


def _post_mix_kernel(x_ref, mix_ref, gpost_ref, gt_ref, gpre_ref, sc_ref, sh_ref, wr_ref, br_ref,
                     x1_ref, h_ref, idx_ref, gate_ref, mask_ref):
    x1 = x_ref[0] + gt_ref[0] * _rms(mix_ref[0], gpost_ref[...])
    x1_ref[0] = x1
    h = (_rms(x1, gpre_ref[...]) * (1.0 + sc_ref[0]) + sh_ref[0]).astype(BF16)
    h_ref[0] = h
    scores = jax.nn.sigmoid(jnp.dot(h, wr_ref[...], preferred_element_type=F32))
    n_e = scores.shape[-1]
    lane = lax.broadcasted_iota(jnp.int32, scores.shape, 1)
    biased = scores + br_ref[...]
    idx_out = jnp.zeros(scores.shape, jnp.int32)
    sel_out = jnp.zeros(scores.shape, F32)
    mask = jnp.zeros(scores.shape, jnp.bool_)
    for k in range(TOP_K):
        m = jnp.max(biased, axis=-1, keepdims=True)
        ik = jnp.min(jnp.where(biased == m, lane, n_e), axis=-1, keepdims=True)
        hit = lane == ik
        sel_k = jnp.sum(jnp.where(hit, scores, 0.0), axis=-1, keepdims=True)
        idx_out = jnp.where(lane == k, ik, idx_out)
        sel_out = jnp.where(lane == k, sel_k, sel_out)
        mask = jnp.logical_or(mask, hit)
        biased = jnp.where(hit, -jnp.inf, biased)
    idx_ref[0] = idx_out
    gate_ref[0] = sel_out / jnp.sum(sel_out, axis=-1, keepdims=True) * ROUTED_SCALE
    mask_ref[0] = mask.astype(BF16)


def post_mix(x, mix, g_post, gt, g_pre, scale, shift, wr_bf16, b_router, tm=256):
    b, l, d = x.shape
    n_e = wr_bf16.shape[1]
    row = pl.BlockSpec((1, tm, d), lambda bi, i: (bi, i, 0))
    vec = pl.BlockSpec((1, d), lambda bi, i: (0, 0))
    bvec = pl.BlockSpec((1, 1, d), lambda bi, i: (bi, 0, 0))
    small = pl.BlockSpec((1, tm, n_e), lambda bi, i: (bi, i, 0))
    return pl.pallas_call(
        _post_mix_kernel,
        grid=(b, l // tm),
        in_specs=[row, row, vec, bvec, vec, bvec, bvec,
                  pl.BlockSpec((d, n_e), lambda bi, i: (0, 0)),
                  pl.BlockSpec((1, n_e), lambda bi, i: (0, 0))],
        out_specs=[row, row, small, small, small],
        out_shape=[jax.ShapeDtypeStruct((b, l, d), F32),
                   jax.ShapeDtypeStruct((b, l, d), BF16),
                   jax.ShapeDtypeStruct((b, l, n_e), jnp.int32),
                   jax.ShapeDtypeStruct((b, l, n_e), F32),
                   jax.ShapeDtypeStruct((b, l, n_e), BF16)],
        compiler_params=_cparams("parallel", "parallel"),
        name="post_mix",
    )(x, mix, g_post.reshape(1, d), gt, g_pre.reshape(1, d), scale, shift, wr_bf16, b_router.reshape(1, n_e))


def _expert_kernel(be_ref, nu_ref, x_ref, wg_ref, wu_ref, wd_ref, o_ref):
    i = pl.program_id(0)

    @pl.when(i < nu_ref[0])
    def _():
        x = x_ref[...]
        a = jnp.dot(x, wg_ref[0], preferred_element_type=F32)
        u = jnp.dot(x, wu_ref[0], preferred_element_type=F32)
        hid = (jax.nn.silu(a) * u).astype(BF16)
        o_ref[...] = jnp.dot(hid, wd_ref[0], preferred_element_type=F32).astype(o_ref.dtype)

    @pl.when(i >= nu_ref[0])
    def _():
        o_ref[...] = jnp.zeros_like(o_ref)


def expert_ffn(xs, block_e, n_used, wg, wu, wd, blk):
    slots, d = xs.shape
    de = wg.shape[-1]
    n_blocks = slots // blk
    grid_spec = pltpu.PrefetchScalarGridSpec(
        num_scalar_prefetch=2,
        grid=(n_blocks,),
        in_specs=[pl.BlockSpec((blk, d), lambda i, be, nu: (jnp.minimum(i, nu[0] - 1), 0)),
                  pl.BlockSpec((1, d, de), lambda i, be, nu: (be[i], 0, 0)),
                  pl.BlockSpec((1, d, de), lambda i, be, nu: (be[i], 0, 0)),
                  pl.BlockSpec((1, de, d), lambda i, be, nu: (be[i], 0, 0))],
        out_specs=pl.BlockSpec((blk, d), lambda i, be, nu: (i, 0)),
    )
    return pl.pallas_call(
        _expert_kernel,
        grid_spec=grid_spec,
        out_shape=jax.ShapeDtypeStruct((slots, d), BF16),
        compiler_params=_cparams("arbitrary"),
        name="expert_ffn",
    )(block_e, n_used, xs, wg, wu, wd)


def _final_kernel(x1_ref, h_ref, r_ref, wsg_ref, wsu_ref, wsd_ref, gpost_ref, gt_ref, o_ref):
    h = h_ref[0]
    a = jnp.dot(h, wsg_ref[...], preferred_element_type=F32)
    u = jnp.dot(h, wsu_ref[...], preferred_element_type=F32)
    hid = (jax.nn.silu(a) * u).astype(BF16)
    ffn = r_ref[0] + jnp.dot(hid, wsd_ref[...], preferred_element_type=F32)
    o_ref[0] = x1_ref[0] + gt_ref[0] * _rms(ffn, gpost_ref[...])


def final_mix(x1, h, routed, wsg, wsu, wsd, g_post, gt, tm=256):
    b, l, d = x1.shape
    ds = wsg.shape[1]
    row = pl.BlockSpec((1, tm, d), lambda bi, i: (bi, i, 0))
    return pl.pallas_call(
        _final_kernel,
        grid=(b, l // tm),
        in_specs=[row, row, row,
                  pl.BlockSpec((d, ds), lambda bi, i: (0, 0)),
                  pl.BlockSpec((d, ds), lambda bi, i: (0, 0)),
                  pl.BlockSpec((ds, d), lambda bi, i: (0, 0)),
                  pl.BlockSpec((1, d), lambda bi, i: (0, 0)),
                  pl.BlockSpec((1, 1, d), lambda bi, i: (bi, 0, 0))],
        out_specs=row,
        out_shape=jax.ShapeDtypeStruct((b, l, d), F32),
        compiler_params=_cparams("parallel", "parallel"),
        name="final_mix",
    )(x1, h, routed, wsg, wsu, wsd, g_post.reshape(1, d), gt)


def _ssm_combine(e1, e2):
    a1, b1 = e1
    a2, b2 = e2
    return a1 * a2, a2 * b1 + b2


def _ssm_states(u, a_bar, b_bar, h0, reverse):
    bu = jnp.einsum('blgn,gpn->blgp', u.astype(jnp.complex64), b_bar)
    a = jnp.broadcast_to(a_bar, bu.shape)
    a_cum, hs = lax.associative_scan(_ssm_combine, (a, bu), axis=1, reverse=reverse)
    if h0 is not None:
        hs = hs + a_cum * h0[:, None]
    return hs


def _s5_jnp(u_lat, u_ctx, a_re, a_im, log_dt, b_re, b_im, c_re, c_im, d_skip):
    a = lax.complex(a_re, a_im)
    dt = jnp.exp(log_dt)[..., None]
    a_bar = jnp.exp(dt * a)
    b_bar = ((a_bar - 1.0) / a)[..., None] * lax.complex(b_re, b_im)
    c_mat = lax.complex(c_re, c_im)
    uc = u_ctx.astype(F32)
    ul = u_lat.astype(F32)
    hc_f = _ssm_states(uc, a_bar[0], b_bar[0], None, False)
    hc_b = _ssm_states(uc, a_bar[1], b_bar[1], None, True)
    ro = lambda cm, hs: jnp.einsum('gnp,blgp->blgn', cm, hs).real
    return (ro(c_mat[0], _ssm_states(ul, a_bar[0], b_bar[0], hc_f[:, -1], False))
            + ro(c_mat[1], _ssm_states(ul, a_bar[1], b_bar[1], hc_b[:, 0], True))
            + d_skip * ul)


EXPERT_BLK = 256
PROJ_TN = 1024


def kernel(x, c, ctx, c_ctx, w_ada, b_ada, g_pre_mix, g_post_mix, g_pre_ffn, g_post_ffn, w_in, rpb, ssm_a_re, ssm_a_im, ssm_log_dt, ssm_b_re, ssm_b_im, ssm_c_re, ssm_c_im, ssm_d, w_glu, b_glu, g_attn_out, g_ssm_out, w_out, w_router, b_router, w_exp_gate, w_exp_up, w_exp_down, w_sh_gate, w_sh_up, w_sh_down):
    b, l, d = x.shape
    lc = ctx.shape[1]
    assert w_ada.shape[0] == 1 and b + 1 <= 8
    n_in = w_in.shape[-1]
    d_ssm = w_glu.shape[-1]
    d_attn = d - d_ssm
    n_heads = d_attn // HEAD_DIM
    n_groups = d_ssm // SSM_GROUP_CH
    n = b * l

    c8 = jnp.concatenate([c, c_ctx[None], jnp.zeros((8 - b - 1, d), F32)], axis=0)
    mod = ada_mod(c8, w_ada[0], b_ada[0]).reshape(8, 6, 1, d)
    sh_m, sc_m, gt_m, sh_f, sc_f, gt_f = [mod[:b, j] for j in range(6)]
    csh_m, csc_m = mod[b:b + 1, 0], mod[b:b + 1, 1]

    w_in_b = w_in[0].astype(BF16)
    tn = PROJ_TN
    p_lat = mod_proj(x, g_pre_mix[0], sc_m, sh_m, w_in_b, 0, n_in, 512, tn)
    p_ctx = mod_proj(ctx.reshape(1, b * lc, d), g_pre_mix[0], csc_m, csh_m, w_in_b,
                     d_attn // tn, n_in - d_attn, 512, tn).reshape(b, lc, n_in - d_attn)

    attn = neighborhood_attention(p_lat, p_ctx, rpb[0], n_heads)

    u_l = p_lat[..., 3 * d_attn:].reshape(b, l, n_groups, SSM_GROUP_CH)
    u_c = p_ctx[..., 2 * d_attn:].reshape(b, lc, n_groups, SSM_GROUP_CH)
    y = _s5_jnp(u_l, u_c, ssm_a_re[0], ssm_a_im[0], ssm_log_dt[0], ssm_b_re[0], ssm_b_im[0],
                ssm_c_re[0], ssm_c_im[0], ssm_d[0])
    ssm = s5_glu(y.reshape(n, d_ssm), w_glu[0].astype(BF16), b_glu[0])

    mix = merge_proj(attn.reshape(n, d_attn), ssm, g_attn_out[0], g_ssm_out[0], w_out[0].astype(BF16))
    x1, h2, idx, gates, mask = post_mix(x, mix.reshape(b, l, d), g_post_mix[0], gt_m, g_pre_ffn[0], sc_f, sh_f,
                                        w_router[0].astype(BF16), b_router[0])

    n_e = w_router.shape[-1]
    m = n * TOP_K
    idx8 = idx.reshape(n, n_e)[:, :TOP_K]
    gates8 = gates.reshape(n, n_e)[:, :TOP_K]
    maski = mask.reshape(n, n_e).astype(jnp.int32)
    rank = jnp.cumsum(maski, axis=0) - maski
    counts = jnp.sum(maski, axis=0)
    padded = (counts + EXPERT_BLK - 1) // EXPERT_BLK * EXPERT_BLK
    pad_end = jnp.cumsum(padded)
    pad_start = pad_end - padded
    dest = pad_start[idx8] + jnp.take_along_axis(rank, idx8, axis=1)
    n_blocks = m // EXPERT_BLK + n_e
    slots = n_blocks * EXPERT_BLK
    tok = jnp.broadcast_to(jnp.arange(n, dtype=jnp.int32)[:, None], (n, TOP_K))
    slot_tok = jnp.zeros((slots,), jnp.int32).at[dest.reshape(m)].set(tok.reshape(m))
    block_e = jnp.minimum(jnp.searchsorted(pad_end, jnp.arange(n_blocks) * EXPERT_BLK, side='right'),
                          n_e - 1).astype(jnp.int32)
    n_used = (pad_end[-1] // EXPERT_BLK).astype(jnp.int32).reshape(1)
    xs = h2.reshape(n, d)[slot_tok]
    ys = expert_ffn(xs, block_e, n_used, w_exp_gate[0].astype(BF16), w_exp_up[0].astype(BF16),
                    w_exp_down[0].astype(BF16), EXPERT_BLK)
    routed = jnp.einsum('nk,nkd->nd', gates8, ys[dest].astype(F32))

    return final_mix(x1, h2, routed.reshape(b, l, d), w_sh_gate[0].astype(BF16), w_sh_up[0].astype(BF16),
                     w_sh_down[0].astype(BF16), g_post_ffn[0], gt_f)
```

```python
import functools

import numpy as np
import jax
import jax.numpy as jnp
from jax import lax
from jax.experimental import pallas as pl
from jax.experimental.pallas import tpu as pltpu

F32 = jnp.float32
BF16 = jnp.bfloat16
I32 = jnp.int32

GRID_W = 64
HEAD_DIM = 128
WIN_ROWS = 8
WIN_COLS = 16
ROPE_THETA = 10000.0
SSM_GROUP_CH = 16
SSM_STATE = 64
TOP_K = 8
ROUTED_SCALE = 2.5
EPS = 1e-6
NEG_INF = -1e30

Q_ROWS = 4
BAND_ROWS = Q_ROWS + WIN_ROWS - 1
VMEM_LIMIT = 56 * 1024 * 1024
EXPERT_BLK = 256
PROJ_TN = 1024
DISPATCH_TM = 256
COMBINE_TM = 128


def _cparams(*sem):
    return pltpu.CompilerParams(dimension_semantics=sem, vmem_limit_bytes=VMEM_LIMIT)


def _rms(x, g):
    return x * lax.rsqrt(jnp.mean(x * x, axis=-1, keepdims=True) + EPS) * g


def _pack_halves(lo, hi):
    lo_bits = lax.bitcast_convert_type(lo.astype(BF16).astype(F32), I32)
    hi_bits = lax.bitcast_convert_type(hi.astype(BF16).astype(F32), I32)
    return (hi_bits & jnp.int32(-65536)) | lax.shift_right_logical(lo_bits, jnp.int32(16))


def _unpack_halves(w):
    lo = lax.bitcast_convert_type(lax.shift_left(w, jnp.int32(16)), F32)
    hi = lax.bitcast_convert_type(w & jnp.int32(-65536), F32)
    return lo, hi


def _ada_kernel(c_ref, w_ref, b_ref, o_ref):
    a = jax.nn.silu(c_ref[...]).astype(BF16)
    o_ref[...] = jnp.dot(a, w_ref[...].astype(BF16), preferred_element_type=F32) + b_ref[...]


def ada_mod(c8, w_ada, b_ada):
    d, n = w_ada.shape
    tn = 512
    return pl.pallas_call(
        _ada_kernel,
        grid=(n // tn,),
        in_specs=[pl.BlockSpec((8, d), lambda j: (0, 0)),
                  pl.BlockSpec((d, tn), lambda j: (0, j)),
                  pl.BlockSpec((1, tn), lambda j: (0, j))],
        out_specs=pl.BlockSpec((8, tn), lambda j: (0, j)),
        out_shape=jax.ShapeDtypeStruct((8, n), F32),
        compiler_params=_cparams("arbitrary"),
        name="ada_mod",
    )(c8, w_ada, b_ada.reshape(1, n))


def _modproj_kernel(x_ref, g_ref, sc_ref, sh_ref, w_ref, o_ref, h_scr):
    @pl.when(pl.program_id(2) == 0)
    def _():
        h = _rms(x_ref[0], g_ref[...]) * (1.0 + sc_ref[0]) + sh_ref[0]
        h_scr[...] = h.astype(BF16)

    o_ref[0] = jnp.dot(h_scr[...], w_ref[...], preferred_element_type=F32).astype(o_ref.dtype)


def mod_proj(x, g, scale, shift, w_bf16, col_blk_off, n_out, tm, tn):
    b, l, d = x.shape
    return pl.pallas_call(
        _modproj_kernel,
        grid=(b, l // tm, n_out // tn),
        in_specs=[pl.BlockSpec((1, tm, d), lambda bi, i, j: (bi, i, 0)),
                  pl.BlockSpec((1, d), lambda bi, i, j: (0, 0)),
                  pl.BlockSpec((1, 1, d), lambda bi, i, j: (bi, 0, 0)),
                  pl.BlockSpec((1, 1, d), lambda bi, i, j: (bi, 0, 0)),
                  pl.BlockSpec((d, tn), lambda bi, i, j: (0, j + col_blk_off))],
        out_specs=pl.BlockSpec((1, tm, tn), lambda bi, i, j: (bi, i, j)),
        out_shape=jax.ShapeDtypeStruct((b, l, n_out), BF16),
        scratch_shapes=[pltpu.VMEM((tm, d), BF16)],
        compiler_params=_cparams("parallel", "parallel", "arbitrary"),
        name="mod_proj",
    )(x, g.reshape(1, d), scale, shift, w_bf16)


def _rope_tables(l):
    half = HEAD_DIM // 2
    quarter = half // 2
    inv_freq = 1.0 / (ROPE_THETA ** (jnp.arange(0, half, 2, dtype=F32) / half))
    rows = (jnp.arange(l) // GRID_W).astype(F32)
    cols = (jnp.arange(l) % GRID_W).astype(F32)

    def cs(pos):
        ang = pos[:, None] * inv_freq[None, :]
        return jnp.cos(ang), jnp.sin(ang)

    cr, sr = cs(rows)
    cc, sc = cs(cols)
    zero = jnp.zeros((l, quarter), F32)
    cos = jnp.concatenate([cr, cr, cc, cc], axis=-1)
    s_lo = jnp.concatenate([-sr, zero, -sc, zero], axis=-1)
    s_hi = jnp.concatenate([zero, sr, zero, sc], axis=-1)
    return cos, s_lo, s_hi


def _attn_bias(rpb, rows_n):
    kh = min(WIN_ROWS, rows_n)
    cases = [(0, 0), (Q_ROWS, 0), (rows_n - Q_ROWS, rows_n - BAND_ROWS)]
    ro_all, co_all, ok_all = [], [], []
    col = np.arange(GRID_W)
    cstart = np.clip(col - WIN_COLS // 2, 0, GRID_W - WIN_COLS)
    for r0, sb in cases:
        qrow = r0 + np.arange(Q_ROWS)
        krow = sb + np.arange(BAND_ROWS)
        rs = np.clip(qrow - kh // 2, 0, rows_n - kh)
        ok_r = (krow[None, :] >= rs[:, None]) & (krow[None, :] < rs[:, None] + kh)
        ok_c = (col[None, :] >= cstart[:, None]) & (col[None, :] < cstart[:, None] + WIN_COLS)
        ro = np.clip(krow[None, :] - qrow[:, None] + (WIN_ROWS - 1), 0, 2 * WIN_ROWS - 2)
        co = np.clip(col[None, :] - col[:, None], -(WIN_COLS - 1), WIN_COLS - 1) + (WIN_COLS - 1)
        shape = (Q_ROWS, GRID_W, BAND_ROWS, GRID_W)
        ro_all.append(np.broadcast_to(ro[:, None, :, None], shape).reshape(Q_ROWS * GRID_W, BAND_ROWS * GRID_W))
        co_all.append(np.broadcast_to(co[None, :, None, :], shape).reshape(Q_ROWS * GRID_W, BAND_ROWS * GRID_W))
        ok = ok_r[:, None, :, None] & ok_c[None, :, None, :]
        ok_all.append(np.broadcast_to(ok, shape).reshape(Q_ROWS * GRID_W, BAND_ROWS * GRID_W))
    ro_all, co_all, ok_all = np.stack(ro_all), np.stack(co_all), np.stack(ok_all)
    bias = rpb.astype(F32)[:, ro_all, co_all]
    return jnp.where(ok_all[None], bias, NEG_INF)


def _attn_kernel(q_ref, k_ref, v_ref, kc_ref, vc_ref, bias_ref, cos_ref, slo_ref, shi_ref, o_ref,
                 qr_scr, kr_scr, *, n_blk, rows_n):
    quarter = HEAD_DIM // 4
    qn = Q_ROWS * GRID_W
    kn = BAND_ROWS * GRID_W
    scale = HEAD_DIM ** -0.5

    def rope_body(i, _):
        sl = pl.ds(pl.multiple_of(i * qn, qn), qn)
        for src, dst in ((q_ref, qr_scr), (k_ref, kr_scr)):
            x = src[0, sl, :].astype(F32)
            y = (x * cos_ref[sl, :] + pltpu.roll(x, HEAD_DIM - quarter, 1) * slo_ref[sl, :]
                 + pltpu.roll(x, quarter, 1) * shi_ref[sl, :])
            dst[sl, :] = y.astype(BF16)
        return 0

    lax.fori_loop(0, n_blk, rope_body, 0)

    kc = kc_ref[0]
    vc = vc_ref[0]
    nt = (((1,), (1,)), ((), ()))

    def body(i, _):
        sb = jnp.clip(i * Q_ROWS - WIN_ROWS // 2, 0, rows_n - BAND_ROWS)
        ks = pl.ds(pl.multiple_of(sb * GRID_W, GRID_W), kn)
        qs = pl.ds(pl.multiple_of(i * qn, qn), qn)
        case = jnp.where(i == 0, 0, jnp.where(i == n_blk - 1, 2, 1))
        s = lax.dot_general(qr_scr[qs, :], kr_scr[ks, :], nt, preferred_element_type=F32) * scale + bias_ref[0, case]
        sc = lax.dot_general(q_ref[0, qs, :], kc, nt, preferred_element_type=F32) * scale
        m = jnp.maximum(jnp.max(s, axis=-1, keepdims=True), jnp.max(sc, axis=-1, keepdims=True))
        p = jnp.exp(s - m)
        pc = jnp.exp(sc - m)
        den = jnp.sum(p, axis=-1, keepdims=True) + jnp.sum(pc, axis=-1, keepdims=True)
        o = (jnp.dot(p.astype(BF16), v_ref[0, ks, :], preferred_element_type=F32)
             + jnp.dot(pc.astype(BF16), vc, preferred_element_type=F32))
        o_ref[0, qs, :] = (o / den).astype(o_ref.dtype)
        return 0

    lax.fori_loop(0, n_blk, body, 0)


def neighborhood_attention(p_lat, p_ctx, rpb, n_heads):
    b, l, _ = p_lat.shape
    lc = p_ctx.shape[1]
    rows_n = l // GRID_W
    assert rows_n % Q_ROWS == 0 and rows_n >= BAND_ROWS + Q_ROWS
    n_blk = rows_n // Q_ROWS
    bias = _attn_bias(rpb, rows_n)
    cos, s_lo, s_hi = _rope_tables(l)
    qn, kn = Q_ROWS * GRID_W, BAND_ROWS * GRID_W
    h = n_heads
    tab = pl.BlockSpec((l, HEAD_DIM), lambda bi, hi: (0, 0))
    return pl.pallas_call(
        functools.partial(_attn_kernel, n_blk=n_blk, rows_n=rows_n),
        grid=(b, h),
        in_specs=[pl.BlockSpec((1, l, HEAD_DIM), lambda bi, hi: (bi, 0, hi)),
                  pl.BlockSpec((1, l, HEAD_DIM), lambda bi, hi: (bi, 0, hi + h)),
                  pl.BlockSpec((1, l, HEAD_DIM), lambda bi, hi: (bi, 0, hi + 2 * h)),
                  pl.BlockSpec((1, lc, HEAD_DIM), lambda bi, hi: (bi, 0, hi)),
                  pl.BlockSpec((1, lc, HEAD_DIM), lambda bi, hi: (bi, 0, hi + h)),
                  pl.BlockSpec((1, 3, qn, kn), lambda bi, hi: (hi, 0, 0, 0)),
                  tab, tab, tab],
        out_specs=pl.BlockSpec((1, l, HEAD_DIM), lambda bi, hi: (bi, 0, hi)),
        out_shape=jax.ShapeDtypeStruct((b, l, h * HEAD_DIM), BF16),
        scratch_shapes=[pltpu.VMEM((l, HEAD_DIM), BF16), pltpu.VMEM((l, HEAD_DIM), BF16)],
        compiler_params=_cparams("parallel", "parallel"),
        name="nbr_attn",
    )(p_lat, p_lat, p_lat, p_ctx, p_ctx, bias, cos, s_lo, s_hi)


SSM_CHUNK = 16
SSM_ROWS = 8


def _ssm_mats(a_re, a_im, log_dt, b_re, b_im, c_re, c_im, d_skip):
    t = SSM_CHUNK
    hi = lax.Precision.HIGHEST
    a = lax.complex(a_re.astype(F32), a_im.astype(F32))
    dta = jnp.exp(log_dt.astype(F32))[..., None] * a
    a_bar = jnp.exp(dta)
    b_bar = ((a_bar - 1.0) / a)[..., None] * lax.complex(b_re.astype(F32), b_im.astype(F32))
    cm = lax.complex(c_re.astype(F32), c_im.astype(F32))
    k = jnp.arange(t + 1, dtype=F32)
    ap = jnp.exp(dta[..., None] * k)
    g, p, n = a.shape[1], a.shape[2], b_re.shape[-1]
    kern = jnp.einsum('dgnp,dgpl,dgpm->dglnm', cm, ap[..., :t], b_bar, precision=hi).real
    s_i = np.arange(t)[:, None]
    t_i = np.arange(t)[None, :]
    kf = kern[0][:, np.clip(t_i - s_i, 0, t - 1)]
    kb = kern[1][:, np.clip(s_i - t_i, 0, t - 1)]
    eye = jnp.eye(n, dtype=F32)
    m_mat = (jnp.where((s_i <= t_i)[None, :, :, None, None], kf, 0.0)
             + jnp.where((s_i >= t_i)[None, :, :, None, None], kb, 0.0)
             + (s_i == t_i)[None, :, :, None, None] * (d_skip.astype(F32)[:, None, None, :, None] * eye))
    m_mat = jnp.transpose(m_mat, (0, 1, 4, 2, 3)).reshape(g, t * n, t * n)
    wf = ap[0][:, :, t - 1::-1][..., :t, None] * b_bar[0][:, :, None, :]
    wb = ap[1][:, :, :t, None] * b_bar[1][:, :, None, :]
    to_rows = lambda z: jnp.transpose(z, (0, 2, 3, 1)).reshape(g, t * n, p)
    w_mat = jnp.concatenate([to_rows(wf.real), to_rows(wb.real), to_rows(wf.imag), to_rows(wb.imag)], axis=-1)
    zf = jnp.transpose(cm[0], (0, 2, 1))[:, :, None, :] * ap[0][:, :, 1:t + 1, None]
    zb = jnp.transpose(cm[1], (0, 2, 1))[:, :, None, :] * ap[1][:, :, t:0:-1, None]
    flat = lambda z: z.reshape(g, p, t * n)
    r_mat = jnp.concatenate([flat(zf.real), flat(zb.real), -flat(zf.imag), -flat(zb.imag)], axis=1)
    a_t = ap[..., t]
    a_vec = jnp.stack([jnp.concatenate([a_t[0].real, a_t[1].real], -1),
                       jnp.concatenate([a_t[0].imag, a_t[1].imag], -1)], axis=1)
    a_vec = jnp.pad(a_vec, ((0, 0), (0, SSM_ROWS - 2), (0, 0)))
    return m_mat.astype(BF16), w_mat.astype(BF16), r_mat.astype(BF16), a_vec


def _s5_kernel(u_ref, m_ref, w_ref, r_ref, a_ref, y_ref, v_scr, ent_scr, *, n_ctx, n_chunks):
    u = u_ref[0]
    p2 = a_ref.shape[-1]
    p = p2 // 2
    v_scr[...] = jnp.dot(u, w_ref[0], preferred_element_type=F32)
    a_re = jnp.broadcast_to(a_ref[0, 0:1, :], (SSM_ROWS, p2))
    a_im = jnp.broadcast_to(a_ref[0, 1:2, :], (SSM_ROWS, p2))
    is_fwd = lax.broadcasted_iota(I32, (SSM_ROWS, p2), 1) < p

    def step(j, carry):
        s_re, s_im = carry
        cb = jnp.where(j < n_ctx, n_ctx - 1 - j, n_chunks + n_ctx - 1 - j)
        rf = pl.ds(pl.multiple_of(j * SSM_ROWS, SSM_ROWS), SSM_ROWS)
        rb = pl.ds(pl.multiple_of(cb * SSM_ROWS, SSM_ROWS), SSM_ROWS)
        ent_scr[rf, 0:p] = s_re[:, :p]
        ent_scr[rb, p:p2] = s_re[:, p:]
        ent_scr[rf, p2:p2 + p] = s_im[:, :p]
        ent_scr[rb, p2 + p:] = s_im[:, p:]
        in_re = jnp.where(is_fwd, v_scr[rf, :p2], v_scr[rb, :p2])
        in_im = jnp.where(is_fwd, v_scr[rf, p2:], v_scr[rb, p2:])
        return a_re * s_re - a_im * s_im + in_re, a_re * s_im + a_im * s_re + in_im

    zero = jnp.zeros((SSM_ROWS, p2), F32)
    lax.fori_loop(0, n_chunks, step, (zero, zero))
    y_ref[0] = (jnp.dot(u, m_ref[0], preferred_element_type=F32)
                + jnp.dot(ent_scr[...].astype(BF16), r_ref[0], preferred_element_type=F32))


def s5_mixer(u_lat, u_ctx, mats):
    m_mat, w_mat, r_mat, a_vec = mats
    b, l, dch = u_lat.shape
    lc = u_ctx.shape[1]
    g = m_mat.shape[0]
    n = dch // g
    t = SSM_CHUNK
    tn = t * n
    n_ctx, n_chunks = lc // t, (lc + l) // t
    assert b <= SSM_ROWS and lc % t == 0 and l % t == 0
    z = jnp.concatenate([u_ctx, u_lat], axis=1).reshape(b, n_chunks, t, g, n)
    z = jnp.pad(jnp.transpose(z, (3, 1, 0, 2, 4)), ((0, 0), (0, 0), (0, SSM_ROWS - b), (0, 0), (0, 0)))
    rows = n_chunks * SSM_ROWS
    z = z.reshape(g, rows, tn)
    mat = lambda k: pl.BlockSpec((1, k, tn), lambda gi: (gi, 0, 0))
    y = pl.pallas_call(
        functools.partial(_s5_kernel, n_ctx=n_ctx, n_chunks=n_chunks),
        grid=(g,),
        in_specs=[mat(rows), mat(tn), mat(tn), mat(r_mat.shape[1]),
                  pl.BlockSpec((1, SSM_ROWS, a_vec.shape[-1]), lambda gi: (gi, 0, 0))],
        out_specs=mat(rows),
        out_shape=jax.ShapeDtypeStruct((g, rows, tn), F32),
        scratch_shapes=[pltpu.VMEM((rows, w_mat.shape[-1]), F32), pltpu.VMEM((rows, r_mat.shape[1]), F32)],
        compiler_params=_cparams("parallel"),
        name="s5_mixer",
    )(z, m_mat, w_mat, r_mat, a_vec)
    y = y.reshape(g, n_chunks, SSM_ROWS, t, n)[:, n_ctx:, :b]
    return jnp.transpose(y, (2, 1, 3, 0, 4)).reshape(b, l, dch)


def _glu_kernel(y_ref, w_ref, b_ref, o_ref):
    z = jax.nn.gelu(y_ref[...].astype(F32))
    gate = jax.nn.sigmoid(jnp.dot(z.astype(BF16), w_ref[...], preferred_element_type=F32) + b_ref[...])
    o_ref[...] = (z * gate).astype(o_ref.dtype)


def s5_glu(y, w_bf16, b_glu, tm=512):
    n, d = y.shape
    return pl.pallas_call(
        _glu_kernel,
        grid=(n // tm,),
        in_specs=[pl.BlockSpec((tm, d), lambda i: (i, 0)),
                  pl.BlockSpec((d, d), lambda i: (0, 0)),
                  pl.BlockSpec((1, d), lambda i: (0, 0))],
        out_specs=pl.BlockSpec((tm, d), lambda i: (i, 0)),
        out_shape=jax.ShapeDtypeStruct((n, d), BF16),
        compiler_params=_cparams("parallel"),
        name="s5_glu",
    )(y, w_bf16, b_glu.reshape(1, d))


def _merge_kernel(a_ref, s_ref, ga_ref, gs_ref, w_ref, o_ref, h_scr):
    da = a_ref.shape[-1]

    @pl.when(pl.program_id(1) == 0)
    def _():
        h_scr[:, :da] = _rms(a_ref[...].astype(F32), ga_ref[...]).astype(BF16)
        h_scr[:, da:] = _rms(s_ref[...].astype(F32), gs_ref[...]).astype(BF16)

    o_ref[...] = jnp.dot(h_scr[...], w_ref[...], preferred_element_type=F32)


def merge_proj(attn, ssm, g_attn, g_ssm, w_bf16, tm=512, tn=1024):
    n, da = attn.shape
    ds = ssm.shape[1]
    d, dout = w_bf16.shape
    return pl.pallas_call(
        _merge_kernel,
        grid=(n // tm, dout // tn),
        in_specs=[pl.BlockSpec((tm, da), lambda i, j: (i, 0)),
                  pl.BlockSpec((tm, ds), lambda i, j: (i, 0)),
                  pl.BlockSpec((1, da), lambda i, j: (0, 0)),
                  pl.BlockSpec((1, ds), lambda i, j: (0, 0)),
                  pl.BlockSpec((d, tn), lambda i, j: (0, j))],
        out_specs=pl.BlockSpec((tm, tn), lambda i, j: (i, j)),
        out_shape=jax.ShapeDtypeStruct((n, dout), F32),
        scratch_shapes=[pltpu.VMEM((tm, d), BF16)],
        compiler_params=_cparams("parallel", "arbitrary"),
        name="merge_proj",
    )(attn, ssm, g_attn.reshape(1, da), g_ssm.reshape(1, ds), w_bf16)


def _post_mix_kernel(x_ref, mix_ref, gpost_ref, gt_ref, gpre_ref, sc_ref, sh_ref, wr_ref, br_ref,
                     x1_ref, hp_ref, idx_ref, gate_ref, mask_ref):
    x1 = x_ref[0] + gt_ref[0] * _rms(mix_ref[0], gpost_ref[...])
    x1_ref[0] = x1
    h = _rms(x1, gpre_ref[...]) * (1.0 + sc_ref[0]) + sh_ref[0]
    half = h.shape[-1] // 2
    hp_ref[0] = _pack_halves(h[:, :half], h[:, half:])
    scores = jax.nn.sigmoid(jnp.dot(h.astype(BF16), wr_ref[...], preferred_element_type=F32))
    n_e = scores.shape[-1]
    lane = lax.broadcasted_iota(I32, scores.shape, 1)
    biased = scores + br_ref[...]
    idx_out = jnp.zeros(scores.shape, I32)
    sel_out = jnp.zeros(scores.shape, F32)
    mask = jnp.zeros(scores.shape, jnp.bool_)
    for k in range(TOP_K):
        m = jnp.max(biased, axis=-1, keepdims=True)
        ik = jnp.min(jnp.where(biased == m, lane, n_e), axis=-1, keepdims=True)
        hit = lane == ik
        sel_k = jnp.sum(jnp.where(hit, scores, 0.0), axis=-1, keepdims=True)
        idx_out = jnp.where(lane == k, ik, idx_out)
        sel_out = jnp.where(lane == k, sel_k, sel_out)
        mask = jnp.logical_or(mask, hit)
        biased = jnp.where(hit, -jnp.inf, biased)
    idx_ref[0] = idx_out
    gate_ref[0] = sel_out / jnp.sum(sel_out, axis=-1, keepdims=True) * ROUTED_SCALE
    mask_ref[0] = mask.astype(BF16)


def post_mix(x, mix, g_post, gt, g_pre, scale, shift, wr_bf16, b_router, tm=256):
    b, l, d = x.shape
    n_e = wr_bf16.shape[1]
    row = pl.BlockSpec((1, tm, d), lambda bi, i: (bi, i, 0))
    prow = pl.BlockSpec((1, tm, d // 2), lambda bi, i: (bi, i, 0))
    vec = pl.BlockSpec((1, d), lambda bi, i: (0, 0))
    bvec = pl.BlockSpec((1, 1, d), lambda bi, i: (bi, 0, 0))
    small = pl.BlockSpec((1, tm, n_e), lambda bi, i: (bi, i, 0))
    return pl.pallas_call(
        _post_mix_kernel,
        grid=(b, l // tm),
        in_specs=[row, row, vec, bvec, vec, bvec, bvec,
                  pl.BlockSpec((d, n_e), lambda bi, i: (0, 0)),
                  pl.BlockSpec((1, n_e), lambda bi, i: (0, 0))],
        out_specs=[row, prow, small, small, small],
        out_shape=[jax.ShapeDtypeStruct((b, l, d), F32),
                   jax.ShapeDtypeStruct((b, l, d // 2), I32),
                   jax.ShapeDtypeStruct((b, l, n_e), I32),
                   jax.ShapeDtypeStruct((b, l, n_e), F32),
                   jax.ShapeDtypeStruct((b, l, n_e), BF16)],
        compiler_params=_cparams("parallel", "parallel"),
        name="post_mix",
    )(x, mix, g_post.reshape(1, d), gt, g_pre.reshape(1, d), scale, shift, wr_bf16, b_router.reshape(1, n_e))


def _dispatch_kernel(pe_ref, pd_ref, dest_ref, h_ref, xs_ref, z_scr, sem, *, n_e, tm, blk):
    def zero_copy(e):
        start = pl.multiple_of(pe_ref[e] - blk, blk)
        return pltpu.make_async_copy(z_scr, xs_ref.at[pl.ds(start, blk), :], sem)

    @pl.when(pl.program_id(0) == 0)
    def _():
        z_scr[...] = jnp.zeros_like(z_scr)

        def start(e, _):
            @pl.when(pd_ref[e] > 0)
            def _():
                zero_copy(e).start()
            return 0

        def wait(e, _):
            @pl.when(pd_ref[e] > 0)
            def _():
                zero_copy(e).wait()
            return 0

        lax.fori_loop(0, n_e, start, 0)
        lax.fori_loop(0, n_e, wait, 0)

    def row_copy(t, k):
        return pltpu.make_async_copy(h_ref.at[pl.ds(t, 1), :], xs_ref.at[pl.ds(dest_ref[t * TOP_K + k], 1), :], sem)

    def issue(t, _):
        for k in range(TOP_K):
            row_copy(t, k).start()
        return 0

    def drain(t, _):
        for k in range(TOP_K):
            row_copy(t, k).wait()
        return 0

    lax.fori_loop(0, tm, issue, 0)
    lax.fori_loop(0, tm, drain, 0)


def moe_dispatch(hp, dest_flat, pad_end, padded, slots, tm=DISPATCH_TM):
    n, dw = hp.shape
    n_e = pad_end.shape[0]
    grid_spec = pltpu.PrefetchScalarGridSpec(
        num_scalar_prefetch=2,
        grid=(n // tm,),
        in_specs=[pl.BlockSpec((tm * TOP_K,), lambda i, pe, pd: (i,), memory_space=pltpu.SMEM),
                  pl.BlockSpec((tm, dw), lambda i, pe, pd: (i, 0))],
        out_specs=pl.BlockSpec(memory_space=pl.ANY),
        scratch_shapes=[pltpu.VMEM((EXPERT_BLK, dw), I32), pltpu.SemaphoreType.DMA(())],
    )
    return pl.pallas_call(
        functools.partial(_dispatch_kernel, n_e=n_e, tm=tm, blk=EXPERT_BLK),
        grid_spec=grid_spec,
        out_shape=jax.ShapeDtypeStruct((slots, dw), I32),
        compiler_params=_cparams("arbitrary"),
        name="moe_dispatch",
    )(pad_end, padded, dest_flat, hp)


def _swiglu_packed(xp, wg, wu, wd):
    half = xp.shape[-1]
    lo, hi = _unpack_halves(xp)
    lo, hi = lo.astype(BF16), hi.astype(BF16)
    a = (jnp.dot(lo, wg[:half], preferred_element_type=F32) + jnp.dot(hi, wg[half:], preferred_element_type=F32))
    u = (jnp.dot(lo, wu[:half], preferred_element_type=F32) + jnp.dot(hi, wu[half:], preferred_element_type=F32))
    hid = (jax.nn.silu(a) * u).astype(BF16)
    return jnp.dot(hid, wd, preferred_element_type=F32)


def _expert_kernel(be_ref, nu_ref, x_ref, wg_ref, wu_ref, wd_ref, o_ref):
    i = pl.program_id(0)
    half = x_ref.shape[-1]

    @pl.when(i < nu_ref[0])
    def _():
        y = _swiglu_packed(x_ref[...], wg_ref[0], wu_ref[0], wd_ref[0])
        o_ref[...] = _pack_halves(y[:, :half], y[:, half:])

    @pl.when(i >= nu_ref[0])
    def _():
        o_ref[...] = jnp.zeros_like(o_ref)


def expert_ffn(xs, block_e, n_used, wg, wu, wd, blk):
    slots, dw = xs.shape
    d, de = wg.shape[1], wg.shape[2]
    n_blocks = slots // blk
    grid_spec = pltpu.PrefetchScalarGridSpec(
        num_scalar_prefetch=2,
        grid=(n_blocks,),
        in_specs=[pl.BlockSpec((blk, dw), lambda i, be, nu: (jnp.minimum(i, nu[0] - 1), 0)),
                  pl.BlockSpec((1, d, de), lambda i, be, nu: (be[i], 0, 0)),
                  pl.BlockSpec((1, d, de), lambda i, be, nu: (be[i], 0, 0)),
                  pl.BlockSpec((1, de, d), lambda i, be, nu: (be[i], 0, 0))],
        out_specs=pl.BlockSpec((blk, dw), lambda i, be, nu: (i, 0)),
    )
    return pl.pallas_call(
        _expert_kernel,
        grid_spec=grid_spec,
        out_shape=jax.ShapeDtypeStruct((slots, dw), I32),
        compiler_params=_cparams("arbitrary"),
        name="expert_ffn",
    )(block_e, n_used, xs, wg, wu, wd)


def _final_kernel(dc_ref, dn_ref, x1_ref, hp_ref, gate_ref, ys_ref, wsg_ref, wsu_ref, wsd_ref, gpost_ref, gt_ref,
                  o_ref, buf, sem, *, tm, n_tiles):
    i = pl.program_id(0)
    slot = i % 2

    def row_copy(dref, s, t, k):
        return pltpu.make_async_copy(ys_ref.at[pl.ds(dref[t * TOP_K + k], 1), :], buf.at[s, k, pl.ds(t, 1), :],
                                     sem.at[s])

    def issue(dref, s):
        def body(t, _):
            for k in range(TOP_K):
                row_copy(dref, s, t, k).start()
            return 0
        lax.fori_loop(0, tm, body, 0)

    @pl.when(i == 0)
    def _():
        issue(dc_ref, 0)

    @pl.when(i + 1 < n_tiles)
    def _():
        issue(dn_ref, 1 - slot)

    def drain(t, _):
        for k in range(TOP_K):
            row_copy(dc_ref, slot, t, k).wait()
        return 0

    lax.fori_loop(0, tm, drain, 0)

    g = gate_ref[...]
    half = hp_ref.shape[-1]
    acc_lo = jnp.zeros((tm, half), F32)
    acc_hi = jnp.zeros((tm, half), F32)
    for k in range(TOP_K):
        lo, hi = _unpack_halves(buf[slot, k])
        gk = g[:, k:k + 1]
        acc_lo = acc_lo + gk * lo
        acc_hi = acc_hi + gk * hi
    shared = _swiglu_packed(hp_ref[...], wsg_ref[...], wsu_ref[...], wsd_ref[...])
    ffn = jnp.concatenate([acc_lo, acc_hi], axis=-1) + shared
    o_ref[...] = x1_ref[...] + gt_ref[0] * _rms(ffn, gpost_ref[...])


def final_mix(x1, hp, gates, ys, dest_flat, wsg, wsu, wsd, g_post, gt, tiles_per_batch, tm=COMBINE_TM):
    n, d = x1.shape
    dw = hp.shape[1]
    n_e = gates.shape[1]
    ds = wsg.shape[1]
    n_tiles = n // tm
    const = lambda shape: pl.BlockSpec(shape, lambda i: (0,) * len(shape))
    return pl.pallas_call(
        functools.partial(_final_kernel, tm=tm, n_tiles=n_tiles),
        grid=(n_tiles,),
        in_specs=[pl.BlockSpec((tm * TOP_K,), lambda i: (i,), memory_space=pltpu.SMEM),
                  pl.BlockSpec((tm * TOP_K,), lambda i: (jnp.minimum(i + 1, n_tiles - 1),), memory_space=pltpu.SMEM),
                  pl.BlockSpec((tm, d), lambda i: (i, 0)),
                  pl.BlockSpec((tm, dw), lambda i: (i, 0)),
                  pl.BlockSpec((tm, n_e), lambda i: (i, 0)),
                  pl.BlockSpec(memory_space=pl.ANY),
                  const((d, ds)), const((d, ds)), const((ds, d)), const((1, d)),
                  pl.BlockSpec((1, 1, d), lambda i: (i // tiles_per_batch, 0, 0))],
        out_specs=pl.BlockSpec((tm, d), lambda i: (i, 0)),
        out_shape=jax.ShapeDtypeStruct((n, d), F32),
        scratch_shapes=[pltpu.VMEM((2, TOP_K, tm, dw), I32), pltpu.SemaphoreType.DMA((2,))],
        compiler_params=_cparams("arbitrary"),
        name="final_mix",
    )(dest_flat, dest_flat, x1, hp, gates, ys, wsg, wsu, wsd, g_post.reshape(1, d), gt)


def kernel(x, c, ctx, c_ctx, w_ada, b_ada, g_pre_mix, g_post_mix, g_pre_ffn, g_post_ffn, w_in, rpb, ssm_a_re, ssm_a_im, ssm_log_dt, ssm_b_re, ssm_b_im, ssm_c_re, ssm_c_im, ssm_d, w_glu, b_glu, g_attn_out, g_ssm_out, w_out, w_router, b_router, w_exp_gate, w_exp_up, w_exp_down, w_sh_gate, w_sh_up, w_sh_down):
    b, l, d = x.shape
    lc = ctx.shape[1]
    assert w_ada.shape[0] == 1 and b + 1 <= 8
    n_in = w_in.shape[-1]
    d_ssm = w_glu.shape[-1]
    d_attn = d - d_ssm
    n_heads = d_attn // HEAD_DIM
    n = b * l

    c8 = jnp.concatenate([c, c_ctx[None], jnp.zeros((8 - b - 1, d), F32)], axis=0)
    mod = ada_mod(c8, w_ada[0], b_ada[0]).reshape(8, 6, 1, d)
    sh_m, sc_m, gt_m, sh_f, sc_f, gt_f = [mod[:b, j] for j in range(6)]
    csh_m, csc_m = mod[b:b + 1, 0], mod[b:b + 1, 1]

    w_in_b = w_in[0].astype(BF16)
    tn = PROJ_TN
    p_lat = mod_proj(x, g_pre_mix[0], sc_m, sh_m, w_in_b, 0, n_in, 512, tn)
    p_ctx = mod_proj(ctx.reshape(1, b * lc, d), g_pre_mix[0], csc_m, csh_m, w_in_b,
                     d_attn // tn, n_in - d_attn, 512, tn).reshape(b, lc, n_in - d_attn)

    attn = neighborhood_attention(p_lat, p_ctx, rpb[0], n_heads)

    mats = _ssm_mats(ssm_a_re[0], ssm_a_im[0], ssm_log_dt[0], ssm_b_re[0], ssm_b_im[0],
                     ssm_c_re[0], ssm_c_im[0], ssm_d[0])
    y = s5_mixer(p_lat[..., 3 * d_attn:], p_ctx[..., 2 * d_attn:], mats)
    ssm = s5_glu(y.reshape(n, d_ssm), w_glu[0].astype(BF16), b_glu[0])

    mix = merge_proj(attn.reshape(n, d_attn), ssm, g_attn_out[0], g_ssm_out[0], w_out[0].astype(BF16))
    x1, hp, idx, gates, mask = post_mix(x, mix.reshape(b, l, d), g_post_mix[0], gt_m, g_pre_ffn[0], sc_f, sh_f,
                                        w_router[0].astype(BF16), b_router[0])

    n_e = w_router.shape[-1]
    m = n * TOP_K
    idx8 = idx.reshape(n, n_e)[:, :TOP_K]
    maski = mask.reshape(n, n_e).astype(I32)
    rank = jnp.cumsum(maski, axis=0) - maski
    counts = jnp.sum(maski, axis=0)
    padded = (counts + EXPERT_BLK - 1) // EXPERT_BLK * EXPERT_BLK
    pad_end = jnp.cumsum(padded).astype(I32)
    pad_start = pad_end - padded
    dest = (pad_start[idx8] + jnp.take_along_axis(rank, idx8, axis=1)).astype(I32).reshape(m)
    n_blocks = m // EXPERT_BLK + n_e
    slots = n_blocks * EXPERT_BLK
    block_e = jnp.minimum(jnp.searchsorted(pad_end, jnp.arange(n_blocks) * EXPERT_BLK, side='right'),
                          n_e - 1).astype(I32)
    n_used = (pad_end[-1] // EXPERT_BLK).astype(I32).reshape(1)

    hp2 = hp.reshape(n, d // 2)
    xs = moe_dispatch(hp2, dest, pad_end, padded.astype(I32), slots)
    ys = expert_ffn(xs, block_e, n_used, w_exp_gate[0].astype(BF16), w_exp_up[0].astype(BF16),
                    w_exp_down[0].astype(BF16), EXPERT_BLK)
    out = final_mix(x1.reshape(n, d), hp2, gates.reshape(n, n_e), ys, dest,
                    w_sh_gate[0].astype(BF16), w_sh_up[0].astype(BF16), w_sh_down[0].astype(BF16),
                    g_post_ffn[0], gt_f, l // COMBINE_TM)
    return out.reshape(b, l, d)
```

```python
import functools

import numpy as np
import jax
import jax.numpy as jnp
from jax import lax
from jax.experimental import pallas as pl
from jax.experimental.pallas import tpu as pltpu

F32 = jnp.float32
BF16 = jnp.bfloat16
I32 = jnp.int32

GRID_W = 64
HEAD_DIM = 128
WIN_ROWS = 8
WIN_COLS = 16
ROPE_THETA = 10000.0
SSM_GROUP_CH = 16
SSM_STATE = 64
TOP_K = 8
ROUTED_SCALE = 2.5
EPS = 1e-6
NEG_INF = -1e30

Q_ROWS = 4
BAND_ROWS = Q_ROWS + WIN_ROWS - 1
VMEM_LIMIT = 56 * 1024 * 1024
EXPERT_BLK = 256
PROJ_TN = 1024
DISPATCH_TM = 256
COMBINE_TM = 128


def _cparams():
    return pltpu.CompilerParams(vmem_limit_bytes=VMEM_LIMIT)


def _rms(x, g):
    return x * lax.rsqrt(jnp.mean(x * x, axis=-1, keepdims=True) + EPS) * g


def _pack_halves(lo, hi):
    lo_bits = lax.bitcast_convert_type(lo.astype(BF16).astype(F32), I32)
    hi_bits = lax.bitcast_convert_type(hi.astype(BF16).astype(F32), I32)
    return (hi_bits & jnp.int32(-65536)) | lax.shift_right_logical(lo_bits, jnp.int32(16))


def _unpack_halves(w):
    lo = lax.bitcast_convert_type(lax.shift_left(w, jnp.int32(16)), F32)
    hi = lax.bitcast_convert_type(w & jnp.int32(-65536), F32)
    return lo, hi


def _ada_kernel(c_ref, w_ref, b_ref, o_ref):
    a = jax.nn.silu(c_ref[...]).astype(BF16)
    o_ref[...] = jnp.dot(a, w_ref[...].astype(BF16), preferred_element_type=F32) + b_ref[...]


def ada_mod(c8, w_ada, b_ada):
    d, n = w_ada.shape
    tn = 512
    return pl.pallas_call(
        _ada_kernel,
        grid=(n // tn,),
        in_specs=[pl.BlockSpec((8, d), lambda j: (0, 0)),
                  pl.BlockSpec((d, tn), lambda j: (0, j)),
                  pl.BlockSpec((1, tn), lambda j: (0, j))],
        out_specs=pl.BlockSpec((8, tn), lambda j: (0, j)),
        out_shape=jax.ShapeDtypeStruct((8, n), F32),
        compiler_params=_cparams(),
        name="ada_mod",
    )(c8, w_ada, b_ada.reshape(1, n))


def _modproj_kernel(x_ref, g_ref, sc_ref, sh_ref, w_ref, o_ref, h_scr):
    @pl.when(pl.program_id(2) == 0)
    def _():
        h = _rms(x_ref[0], g_ref[...]) * (1.0 + sc_ref[0]) + sh_ref[0]
        h_scr[...] = h.astype(BF16)

    o_ref[0] = jnp.dot(h_scr[...], w_ref[...], preferred_element_type=F32).astype(o_ref.dtype)


def mod_proj(x, g, scale, shift, w_bf16, col_blk_off, n_out, tm, tn):
    b, l, d = x.shape
    return pl.pallas_call(
        _modproj_kernel,
        grid=(b, l // tm, n_out // tn),
        in_specs=[pl.BlockSpec((1, tm, d), lambda bi, i, j: (bi, i, 0)),
                  pl.BlockSpec((1, d), lambda bi, i, j: (0, 0)),
                  pl.BlockSpec((1, 1, d), lambda bi, i, j: (bi, 0, 0)),
                  pl.BlockSpec((1, 1, d), lambda bi, i, j: (bi, 0, 0)),
                  pl.BlockSpec((d, tn), lambda bi, i, j: (0, j + col_blk_off))],
        out_specs=pl.BlockSpec((1, tm, tn), lambda bi, i, j: (bi, i, j)),
        out_shape=jax.ShapeDtypeStruct((b, l, n_out), BF16),
        scratch_shapes=[pltpu.VMEM((tm, d), BF16)],
        compiler_params=_cparams(),
        name="mod_proj",
    )(x, g.reshape(1, d), scale, shift, w_bf16)


def _rope_tables(l):
    half = HEAD_DIM // 2
    quarter = half // 2
    inv_freq = 1.0 / (ROPE_THETA ** (jnp.arange(0, half, 2, dtype=F32) / half))
    rows = (jnp.arange(l) // GRID_W).astype(F32)
    cols = (jnp.arange(l) % GRID_W).astype(F32)

    def cs(pos):
        ang = pos[:, None] * inv_freq[None, :]
        return jnp.cos(ang), jnp.sin(ang)

    cr, sr = cs(rows)
    cc, sc = cs(cols)
    zero = jnp.zeros((l, quarter), F32)
    cos = jnp.concatenate([cr, cr, cc, cc], axis=-1)
    s_lo = jnp.concatenate([-sr, zero, -sc, zero], axis=-1)
    s_hi = jnp.concatenate([zero, sr, zero, sc], axis=-1)
    return cos, s_lo, s_hi


def _attn_bias(rpb, rows_n):
    kh = min(WIN_ROWS, rows_n)
    cases = [(0, 0), (Q_ROWS, 0), (rows_n - Q_ROWS, rows_n - BAND_ROWS)]
    col = np.arange(GRID_W)
    cstart = np.clip(col - WIN_COLS // 2, 0, GRID_W - WIN_COLS)
    ok_c = (col[None, :] >= cstart[:, None]) & (col[None, :] < cstart[:, None] + WIN_COLS)
    co = np.clip(col[None, :] - col[:, None], -(WIN_COLS - 1), WIN_COLS - 1) + (WIN_COLS - 1)
    ro_all, ok_all = [], []
    for r0, sb in cases:
        qrow = r0 + np.arange(Q_ROWS)
        krow = sb + np.arange(BAND_ROWS)
        rs = np.clip(qrow - kh // 2, 0, rows_n - kh)
        ok_all.append((krow[None, :] >= rs[:, None]) & (krow[None, :] < rs[:, None] + kh))
        ro_all.append(np.clip(krow[None, :] - qrow[:, None] + (WIN_ROWS - 1), 0, 2 * WIN_ROWS - 2))
    ro_all, ok_all = np.stack(ro_all), np.stack(ok_all)
    onehot = (co[None] == np.arange(2 * WIN_COLS - 1)[:, None, None]).astype(np.float32)
    toe = jnp.einsum('hrc,cqk->hrqk', rpb.astype(F32), onehot, precision=lax.Precision.HIGHEST)
    blocks = toe[:, ro_all]
    ok = ok_all[None, :, :, :, None, None] & ok_c[None, None, None, None]
    bias = jnp.transpose(jnp.where(ok, blocks, NEG_INF), (0, 1, 2, 4, 3, 5))
    return bias.reshape(rpb.shape[0], 3, Q_ROWS * GRID_W, BAND_ROWS * GRID_W)


def _attn_kernel(q_ref, k_ref, v_ref, kc_ref, vc_ref, bias_ref, cos_ref, slo_ref, shi_ref, o_ref,
                 qr_scr, kr_scr, *, n_blk, rows_n):
    quarter = HEAD_DIM // 4
    qn = Q_ROWS * GRID_W
    kn = BAND_ROWS * GRID_W
    scale = HEAD_DIM ** -0.5

    def rope_body(i, _):
        sl = pl.ds(pl.multiple_of(i * qn, qn), qn)
        for src, dst in ((q_ref, qr_scr), (k_ref, kr_scr)):
            x = src[0, sl, :].astype(F32)
            y = (x * cos_ref[sl, :] + pltpu.roll(x, HEAD_DIM - quarter, 1) * slo_ref[sl, :]
                 + pltpu.roll(x, quarter, 1) * shi_ref[sl, :])
            dst[sl, :] = y.astype(BF16)
        return 0

    lax.fori_loop(0, n_blk, rope_body, 0)

    kc = kc_ref[0]
    vc = vc_ref[0]
    nt = (((1,), (1,)), ((), ()))

    def body(i, _):
        sb = jnp.clip(i * Q_ROWS - WIN_ROWS // 2, 0, rows_n - BAND_ROWS)
        ks = pl.ds(pl.multiple_of(sb * GRID_W, GRID_W), kn)
        qs = pl.ds(pl.multiple_of(i * qn, qn), qn)
        case = jnp.where(i == 0, 0, jnp.where(i == n_blk - 1, 2, 1))
        s = lax.dot_general(qr_scr[qs, :], kr_scr[ks, :], nt, preferred_element_type=F32) * scale + bias_ref[0, case]
        sc = lax.dot_general(q_ref[0, qs, :], kc, nt, preferred_element_type=F32) * scale
        m = jnp.maximum(jnp.max(s, axis=-1, keepdims=True), jnp.max(sc, axis=-1, keepdims=True))
        p = jnp.exp(s - m)
        pc = jnp.exp(sc - m)
        den = jnp.sum(p, axis=-1, keepdims=True) + jnp.sum(pc, axis=-1, keepdims=True)
        o = (jnp.dot(p.astype(BF16), v_ref[0, ks, :], preferred_element_type=F32)
             + jnp.dot(pc.astype(BF16), vc, preferred_element_type=F32))
        o_ref[0, qs, :] = (o / den).astype(o_ref.dtype)
        return 0

    lax.fori_loop(0, n_blk, body, 0)


def neighborhood_attention(p_lat, p_ctx, rpb, n_heads):
    b, l, _ = p_lat.shape
    lc = p_ctx.shape[1]
    rows_n = l // GRID_W
    assert rows_n % Q_ROWS == 0 and rows_n >= BAND_ROWS + Q_ROWS
    n_blk = rows_n // Q_ROWS
    bias = _attn_bias(rpb, rows_n)
    cos, s_lo, s_hi = _rope_tables(l)
    qn, kn = Q_ROWS * GRID_W, BAND_ROWS * GRID_W
    h = n_heads
    tab = pl.BlockSpec((l, HEAD_DIM), lambda bi, hi: (0, 0))
    return pl.pallas_call(
        functools.partial(_attn_kernel, n_blk=n_blk, rows_n=rows_n),
        grid=(b, h),
        in_specs=[pl.BlockSpec((1, l, HEAD_DIM), lambda bi, hi: (bi, 0, hi)),
                  pl.BlockSpec((1, l, HEAD_DIM), lambda bi, hi: (bi, 0, hi + h)),
                  pl.BlockSpec((1, l, HEAD_DIM), lambda bi, hi: (bi, 0, hi + 2 * h)),
                  pl.BlockSpec((1, lc, HEAD_DIM), lambda bi, hi: (bi, 0, hi)),
                  pl.BlockSpec((1, lc, HEAD_DIM), lambda bi, hi: (bi, 0, hi + h)),
                  pl.BlockSpec((1, 3, qn, kn), lambda bi, hi: (hi, 0, 0, 0)),
                  tab, tab, tab],
        out_specs=pl.BlockSpec((1, l, HEAD_DIM), lambda bi, hi: (bi, 0, hi)),
        out_shape=jax.ShapeDtypeStruct((b, l, h * HEAD_DIM), BF16),
        scratch_shapes=[pltpu.VMEM((l, HEAD_DIM), BF16), pltpu.VMEM((l, HEAD_DIM), BF16)],
        compiler_params=_cparams(),
        name="nbr_attn",
    )(p_lat, p_lat, p_lat, p_ctx, p_ctx, bias, cos, s_lo, s_hi)


SSM_CHUNK = 16
SSM_ROWS = 8


def _ssm_mats(a_re, a_im, log_dt, b_re, b_im, c_re, c_im, d_skip):
    t = SSM_CHUNK
    hi = lax.Precision.HIGHEST
    a = lax.complex(a_re.astype(F32), a_im.astype(F32))
    dta = jnp.exp(log_dt.astype(F32))[..., None] * a
    a_bar = jnp.exp(dta)
    b_bar = ((a_bar - 1.0) / a)[..., None] * lax.complex(b_re.astype(F32), b_im.astype(F32))
    cm = lax.complex(c_re.astype(F32), c_im.astype(F32))
    k = jnp.arange(t + 1, dtype=F32)
    ap = jnp.exp(dta[..., None] * k)
    g, p, n = a.shape[1], a.shape[2], b_re.shape[-1]
    kern = jnp.einsum('dgnp,dgpl,dgpm->dglnm', cm, ap[..., :t], b_bar, precision=hi).real
    s_i = np.arange(t)[:, None]
    t_i = np.arange(t)[None, :]
    kf = kern[0][:, np.clip(t_i - s_i, 0, t - 1)]
    kb = kern[1][:, np.clip(s_i - t_i, 0, t - 1)]
    eye = jnp.eye(n, dtype=F32)
    m_mat = (jnp.where((s_i <= t_i)[None, :, :, None, None], kf, 0.0)
             + jnp.where((s_i >= t_i)[None, :, :, None, None], kb, 0.0)
             + (s_i == t_i)[None, :, :, None, None] * (d_skip.astype(F32)[:, None, None, :, None] * eye))
    m_mat = jnp.transpose(m_mat, (0, 1, 4, 2, 3)).reshape(g, t * n, t * n)
    wf = ap[0][:, :, t - 1::-1][..., :t, None] * b_bar[0][:, :, None, :]
    wb = ap[1][:, :, :t, None] * b_bar[1][:, :, None, :]
    to_rows = lambda z: jnp.transpose(z, (0, 2, 3, 1)).reshape(g, t * n, p)
    w_mat = jnp.concatenate([to_rows(wf.real), to_rows(wb.real), to_rows(wf.imag), to_rows(wb.imag)], axis=-1)
    zf = jnp.transpose(cm[0], (0, 2, 1))[:, :, None, :] * ap[0][:, :, 1:t + 1, None]
    zb = jnp.transpose(cm[1], (0, 2, 1))[:, :, None, :] * ap[1][:, :, t:0:-1, None]
    flat = lambda z: z.reshape(g, p, t * n)
    r_mat = jnp.concatenate([flat(zf.real), flat(zb.real), -flat(zf.imag), -flat(zb.imag)], axis=1)
    a_t = ap[..., t]
    a_vec = jnp.stack([jnp.concatenate([a_t[0].real, a_t[1].real], -1),
                       jnp.concatenate([a_t[0].imag, a_t[1].imag], -1)], axis=1)
    a_vec = jnp.pad(a_vec, ((0, 0), (0, SSM_ROWS - 2), (0, 0)))
    return m_mat.astype(BF16), w_mat.astype(BF16), r_mat.astype(BF16), a_vec


def _s5_kernel(u_ref, m_ref, w_ref, r_ref, a_ref, y_ref, v_scr, ent_scr, *, n_ctx, n_chunks):
    u = u_ref[0]
    p2 = a_ref.shape[-1]
    p = p2 // 2
    v_scr[...] = jnp.dot(u, w_ref[0], preferred_element_type=F32)
    a_re = jnp.broadcast_to(a_ref[0, 0:1, :], (SSM_ROWS, p2))
    a_im = jnp.broadcast_to(a_ref[0, 1:2, :], (SSM_ROWS, p2))
    is_fwd = lax.broadcasted_iota(I32, (SSM_ROWS, p2), 1) < p

    def step(j, carry):
        s_re, s_im = carry
        cb = jnp.where(j < n_ctx, n_ctx - 1 - j, n_chunks + n_ctx - 1 - j)
        rf = pl.ds(pl.multiple_of(j * SSM_ROWS, SSM_ROWS), SSM_ROWS)
        rb = pl.ds(pl.multiple_of(cb * SSM_ROWS, SSM_ROWS), SSM_ROWS)
        ent_scr[rf, 0:p] = s_re[:, :p]
        ent_scr[rb, p:p2] = s_re[:, p:]
        ent_scr[rf, p2:p2 + p] = s_im[:, :p]
        ent_scr[rb, p2 + p:] = s_im[:, p:]
        in_re = jnp.where(is_fwd, v_scr[rf, :p2], v_scr[rb, :p2])
        in_im = jnp.where(is_fwd, v_scr[rf, p2:], v_scr[rb, p2:])
        return a_re * s_re - a_im * s_im + in_re, a_re * s_im + a_im * s_re + in_im

    zero = jnp.zeros((SSM_ROWS, p2), F32)
    lax.fori_loop(0, n_chunks, step, (zero, zero))
    y_ref[0] = (jnp.dot(u, m_ref[0], preferred_element_type=F32)
                + jnp.dot(ent_scr[...].astype(BF16), r_ref[0], preferred_element_type=F32)).astype(y_ref.dtype)


def s5_mixer(u_lat, u_ctx, mats):
    m_mat, w_mat, r_mat, a_vec = mats
    b, l, dch = u_lat.shape
    lc = u_ctx.shape[1]
    g = m_mat.shape[0]
    n = dch // g
    t = SSM_CHUNK
    tn = t * n
    n_ctx, n_chunks = lc // t, (lc + l) // t
    assert b <= SSM_ROWS and lc % t == 0 and l % t == 0
    z = jnp.concatenate([u_ctx, u_lat], axis=1).reshape(b, n_chunks, t, g, n)
    z = jnp.pad(jnp.transpose(z, (3, 1, 0, 2, 4)), ((0, 0), (0, 0), (0, SSM_ROWS - b), (0, 0), (0, 0)))
    rows = n_chunks * SSM_ROWS
    z = z.reshape(g, rows, tn)
    mat = lambda k: pl.BlockSpec((1, k, tn), lambda gi: (gi, 0, 0))
    y = pl.pallas_call(
        functools.partial(_s5_kernel, n_ctx=n_ctx, n_chunks=n_chunks),
        grid=(g,),
        in_specs=[mat(rows), mat(tn), mat(tn), mat(r_mat.shape[1]),
                  pl.BlockSpec((1, SSM_ROWS, a_vec.shape[-1]), lambda gi: (gi, 0, 0))],
        out_specs=mat(rows),
        out_shape=jax.ShapeDtypeStruct((g, rows, tn), BF16),
        scratch_shapes=[pltpu.VMEM((rows, w_mat.shape[-1]), F32), pltpu.VMEM((rows, r_mat.shape[1]), F32)],
        compiler_params=_cparams(),
        name="s5_mixer",
    )(z, m_mat, w_mat, r_mat, a_vec)
    y = y.reshape(g, n_chunks, SSM_ROWS, t, n)[:, n_ctx:, :b]
    return jnp.transpose(y, (2, 1, 3, 0, 4)).reshape(b, l, dch)


def _glu_kernel(y_ref, w_ref, b_ref, o_ref):
    z = jax.nn.gelu(y_ref[...].astype(F32))
    gate = jax.nn.sigmoid(jnp.dot(z.astype(BF16), w_ref[...], preferred_element_type=F32) + b_ref[...])
    o_ref[...] = (z * gate).astype(o_ref.dtype)


def s5_glu(y, w_bf16, b_glu, tm=512):
    n, d = y.shape
    return pl.pallas_call(
        _glu_kernel,
        grid=(n // tm,),
        in_specs=[pl.BlockSpec((tm, d), lambda i: (i, 0)),
                  pl.BlockSpec((d, d), lambda i: (0, 0)),
                  pl.BlockSpec((1, d), lambda i: (0, 0))],
        out_specs=pl.BlockSpec((tm, d), lambda i: (i, 0)),
        out_shape=jax.ShapeDtypeStruct((n, d), BF16),
        compiler_params=_cparams(),
        name="s5_glu",
    )(y, w_bf16, b_glu.reshape(1, d))


def _merge_kernel(a_ref, s_ref, ga_ref, gs_ref, w_ref, o_ref, h_scr):
    da = a_ref.shape[-1]

    @pl.when(pl.program_id(1) == 0)
    def _():
        h_scr[:, :da] = _rms(a_ref[...].astype(F32), ga_ref[...]).astype(BF16)
        h_scr[:, da:] = _rms(s_ref[...].astype(F32), gs_ref[...]).astype(BF16)

    o_ref[...] = jnp.dot(h_scr[...], w_ref[...], preferred_element_type=F32)


def merge_proj(attn, ssm, g_attn, g_ssm, w_bf16, tm=512, tn=1024):
    n, da = attn.shape
    ds = ssm.shape[1]
    d, dout = w_bf16.shape
    return pl.pallas_call(
        _merge_kernel,
        grid=(n // tm, dout // tn),
        in_specs=[pl.BlockSpec((tm, da), lambda i, j: (i, 0)),
                  pl.BlockSpec((tm, ds), lambda i, j: (i, 0)),
                  pl.BlockSpec((1, da), lambda i, j: (0, 0)),
                  pl.BlockSpec((1, ds), lambda i, j: (0, 0)),
                  pl.BlockSpec((d, tn), lambda i, j: (0, j))],
        out_specs=pl.BlockSpec((tm, tn), lambda i, j: (i, j)),
        out_shape=jax.ShapeDtypeStruct((n, dout), F32),
        scratch_shapes=[pltpu.VMEM((tm, d), BF16)],
        compiler_params=_cparams(),
        name="merge_proj",
    )(attn, ssm, g_attn.reshape(1, da), g_ssm.reshape(1, ds), w_bf16)


def _post_mix_kernel(x_ref, mix_ref, gpost_ref, gt_ref, gpre_ref, sc_ref, sh_ref, wr_ref, br_ref,
                     x1_ref, hp_ref, idx_ref, gate_ref, mask_ref):
    x1 = x_ref[0] + gt_ref[0] * _rms(mix_ref[0], gpost_ref[...])
    x1_ref[0] = x1
    h = _rms(x1, gpre_ref[...]) * (1.0 + sc_ref[0]) + sh_ref[0]
    half = h.shape[-1] // 2
    hp_ref[0] = _pack_halves(h[:, :half], h[:, half:])
    scores = jax.nn.sigmoid(jnp.dot(h.astype(BF16), wr_ref[...], preferred_element_type=F32))
    n_e = scores.shape[-1]
    lane = lax.broadcasted_iota(I32, scores.shape, 1)
    biased = scores + br_ref[...]
    idx_out = jnp.zeros(scores.shape, I32)
    sel_out = jnp.zeros(scores.shape, F32)
    mask = jnp.zeros(scores.shape, jnp.bool_)
    for k in range(TOP_K):
        m = jnp.max(biased, axis=-1, keepdims=True)
        ik = jnp.min(jnp.where(biased == m, lane, n_e), axis=-1, keepdims=True)
        hit = lane == ik
        sel_k = jnp.sum(jnp.where(hit, scores, 0.0), axis=-1, keepdims=True)
        idx_out = jnp.where(lane == k, ik, idx_out)
        sel_out = jnp.where(lane == k, sel_k, sel_out)
        mask = jnp.logical_or(mask, hit)
        biased = jnp.where(hit, -jnp.inf, biased)
    idx_ref[0] = idx_out
    gate_ref[0] = sel_out / jnp.sum(sel_out, axis=-1, keepdims=True) * ROUTED_SCALE
    mask_ref[0] = mask.astype(BF16)


def post_mix(x, mix, g_post, gt, g_pre, scale, shift, wr_bf16, b_router, tm=256):
    b, l, d = x.shape
    n_e = wr_bf16.shape[1]
    row = pl.BlockSpec((1, tm, d), lambda bi, i: (bi, i, 0))
    prow = pl.BlockSpec((1, tm, d // 2), lambda bi, i: (bi, i, 0))
    vec = pl.BlockSpec((1, d), lambda bi, i: (0, 0))
    bvec = pl.BlockSpec((1, 1, d), lambda bi, i: (bi, 0, 0))
    small = pl.BlockSpec((1, tm, n_e), lambda bi, i: (bi, i, 0))
    return pl.pallas_call(
        _post_mix_kernel,
        grid=(b, l // tm),
        in_specs=[row, row, vec, bvec, vec, bvec, bvec,
                  pl.BlockSpec((d, n_e), lambda bi, i: (0, 0)),
                  pl.BlockSpec((1, n_e), lambda bi, i: (0, 0))],
        out_specs=[row, prow, small, small, small],
        out_shape=[jax.ShapeDtypeStruct((b, l, d), F32),
                   jax.ShapeDtypeStruct((b, l, d // 2), I32),
                   jax.ShapeDtypeStruct((b, l, n_e), I32),
                   jax.ShapeDtypeStruct((b, l, n_e), F32),
                   jax.ShapeDtypeStruct((b, l, n_e), BF16)],
        compiler_params=_cparams(),
        name="post_mix",
    )(x, mix, g_post.reshape(1, d), gt, g_pre.reshape(1, d), scale, shift, wr_bf16, b_router.reshape(1, n_e))


def _rank_kernel(mask_ref, idx_ref, rank_ref, cnt_ref, carry):
    @pl.when(pl.program_id(0) == 0)
    def _():
        carry[...] = jnp.zeros_like(carry)

    m = mask_ref[...]
    tm = m.shape[0]
    earlier = (lax.broadcasted_iota(I32, (tm, tm), 1) < lax.broadcasted_iota(I32, (tm, tm), 0)).astype(BF16)
    excl = jnp.dot(earlier, m, preferred_element_type=F32) + carry[0:1, :]
    lane = lax.broadcasted_iota(I32, m.shape, 1)
    idx = idx_ref[...]
    out = jnp.zeros(m.shape, F32)
    for k in range(TOP_K):
        rk = jnp.sum(jnp.where(lane == idx[:, k:k + 1], excl, 0.0), axis=-1, keepdims=True)
        out = jnp.where(lane == k, rk, out)
    rank_ref[...] = out.astype(I32)
    carry[0:1, :] = carry[0:1, :] + jnp.sum(m.astype(F32), axis=0, keepdims=True)
    cnt_ref[...] = carry[...]


def route_ranks(mask, idx, tm=512):
    n, n_e = mask.shape
    blk = pl.BlockSpec((tm, n_e), lambda i: (i, 0))
    rank, cnt = pl.pallas_call(
        _rank_kernel,
        grid=(n // tm,),
        in_specs=[blk, blk],
        out_specs=[blk, pl.BlockSpec((8, n_e), lambda i: (0, 0))],
        out_shape=[jax.ShapeDtypeStruct((n, n_e), I32), jax.ShapeDtypeStruct((8, n_e), F32)],
        scratch_shapes=[pltpu.VMEM((8, n_e), F32)],
        compiler_params=_cparams(),
        name="route_ranks",
    )(mask, idx)
    return rank[:, :TOP_K], cnt[0].astype(I32)


def _dispatch_kernel(pe_ref, pd_ref, dest_ref, h_ref, xs_ref, z_scr, sem, *, n_e, tm, blk):
    def zero_copy(e):
        start = pl.multiple_of(pe_ref[e] - blk, blk)
        return pltpu.make_async_copy(z_scr, xs_ref.at[pl.ds(start, blk), :], sem)

    @pl.when(pl.program_id(0) == 0)
    def _():
        z_scr[...] = jnp.zeros_like(z_scr)

        def start(e, _):
            @pl.when(pd_ref[e] > 0)
            def _():
                zero_copy(e).start()
            return 0

        def wait(e, _):
            @pl.when(pd_ref[e] > 0)
            def _():
                zero_copy(e).wait()
            return 0

        lax.fori_loop(0, n_e, start, 0)
        lax.fori_loop(0, n_e, wait, 0)

    def row_copy(t, k):
        return pltpu.make_async_copy(h_ref.at[pl.ds(t, 1), :], xs_ref.at[pl.ds(dest_ref[t * TOP_K + k], 1), :], sem)

    def issue(t, _):
        for k in range(TOP_K):
            row_copy(t, k).start()
        return 0

    lax.fori_loop(0, tm, issue, 0)
    rows = tm * TOP_K
    pltpu.make_async_copy(xs_ref.at[pl.ds(0, rows), :], xs_ref.at[pl.ds(0, rows), :], sem).wait()


def moe_dispatch(hp, dest_flat, pad_end, padded, slots, tm=DISPATCH_TM):
    n, dw = hp.shape
    n_e = pad_end.shape[0]
    grid_spec = pltpu.PrefetchScalarGridSpec(
        num_scalar_prefetch=2,
        grid=(n // tm,),
        in_specs=[pl.BlockSpec((tm * TOP_K,), lambda i, pe, pd: (i,), memory_space=pltpu.SMEM),
                  pl.BlockSpec((tm, dw), lambda i, pe, pd: (i, 0))],
        out_specs=pl.BlockSpec(memory_space=pl.ANY),
        scratch_shapes=[pltpu.VMEM((EXPERT_BLK, dw), I32), pltpu.SemaphoreType.DMA(())],
    )
    return pl.pallas_call(
        functools.partial(_dispatch_kernel, n_e=n_e, tm=tm, blk=EXPERT_BLK),
        grid_spec=grid_spec,
        out_shape=jax.ShapeDtypeStruct((slots, dw), I32),
        compiler_params=_cparams(),
        name="moe_dispatch",
    )(pad_end, padded, dest_flat, hp)


def _swiglu_packed(xp, wg, wu, wd):
    half = xp.shape[-1]
    lo, hi = _unpack_halves(xp)
    lo, hi = lo.astype(BF16), hi.astype(BF16)
    a = (jnp.dot(lo, wg[:half], preferred_element_type=F32) + jnp.dot(hi, wg[half:], preferred_element_type=F32))
    u = (jnp.dot(lo, wu[:half], preferred_element_type=F32) + jnp.dot(hi, wu[half:], preferred_element_type=F32))
    hid = (jax.nn.silu(a) * u).astype(BF16)
    return jnp.dot(hid, wd, preferred_element_type=F32)


def _expert_kernel(be_ref, nu_ref, x_ref, wg_ref, wu_ref, wd_ref, o_ref):
    i = pl.program_id(0)
    half = x_ref.shape[-1]

    @pl.when(i < nu_ref[0])
    def _():
        y = _swiglu_packed(x_ref[...], wg_ref[0], wu_ref[0], wd_ref[0])
        o_ref[...] = _pack_halves(y[:, :half], y[:, half:])

    @pl.when(i >= nu_ref[0])
    def _():
        o_ref[...] = jnp.zeros_like(o_ref)


def expert_ffn(xs, block_e, n_used, wg, wu, wd, blk):
    slots, dw = xs.shape
    d, de = wg.shape[1], wg.shape[2]
    n_blocks = slots // blk
    grid_spec = pltpu.PrefetchScalarGridSpec(
        num_scalar_prefetch=2,
        grid=(n_blocks,),
        in_specs=[pl.BlockSpec((blk, dw), lambda i, be, nu: (jnp.minimum(i, nu[0] - 1), 0)),
                  pl.BlockSpec((1, d, de), lambda i, be, nu: (be[i], 0, 0)),
                  pl.BlockSpec((1, d, de), lambda i, be, nu: (be[i], 0, 0)),
                  pl.BlockSpec((1, de, d), lambda i, be, nu: (be[i], 0, 0))],
        out_specs=pl.BlockSpec((blk, dw), lambda i, be, nu: (i, 0)),
    )
    return pl.pallas_call(
        _expert_kernel,
        grid_spec=grid_spec,
        out_shape=jax.ShapeDtypeStruct((slots, dw), I32),
        compiler_params=_cparams(),
        name="expert_ffn",
    )(block_e, n_used, xs, wg, wu, wd)


def _final_kernel(dc_ref, dn_ref, x1_ref, hp_ref, gate_ref, ys_ref, wsg_ref, wsu_ref, wsd_ref, gpost_ref, gt_ref,
                  o_ref, buf, sem, *, tm, n_tiles):
    i = pl.program_id(0)
    slot = i % 2

    def row_copy(dref, s, t, k):
        return pltpu.make_async_copy(ys_ref.at[pl.ds(dref[t * TOP_K + k], 1), :],
                                     buf.at[s, pl.ds(k * tm + t, 1), :], sem.at[s])

    def issue(dref, s):
        def body(t, _):
            for k in range(TOP_K):
                row_copy(dref, s, t, k).start()
            return 0
        lax.fori_loop(0, tm, body, 0)

    @pl.when(i == 0)
    def _():
        issue(dc_ref, 0)

    @pl.when(i + 1 < n_tiles)
    def _():
        issue(dn_ref, 1 - slot)

    pltpu.make_async_copy(ys_ref.at[pl.ds(0, TOP_K * tm), :], buf.at[slot], sem.at[slot]).wait()

    g = gate_ref[...]
    half = hp_ref.shape[-1]
    acc_lo = jnp.zeros((tm, half), F32)
    acc_hi = jnp.zeros((tm, half), F32)
    for k in range(TOP_K):
        lo, hi = _unpack_halves(buf[slot, pl.ds(k * tm, tm), :])
        gk = g[:, k:k + 1]
        acc_lo = acc_lo + gk * lo
        acc_hi = acc_hi + gk * hi
    shared = _swiglu_packed(hp_ref[...], wsg_ref[...], wsu_ref[...], wsd_ref[...])
    ffn = jnp.concatenate([acc_lo, acc_hi], axis=-1) + shared
    o_ref[...] = x1_ref[...] + gt_ref[0] * _rms(ffn, gpost_ref[...])


def final_mix(x1, hp, gates, ys, dest_flat, wsg, wsu, wsd, g_post, gt, tiles_per_batch, tm=COMBINE_TM):
    n, d = x1.shape
    dw = hp.shape[1]
    n_e = gates.shape[1]
    ds = wsg.shape[1]
    n_tiles = n // tm
    const = lambda shape: pl.BlockSpec(shape, lambda i: (0,) * len(shape))
    return pl.pallas_call(
        functools.partial(_final_kernel, tm=tm, n_tiles=n_tiles),
        grid=(n_tiles,),
        in_specs=[pl.BlockSpec((tm * TOP_K,), lambda i: (i,), memory_space=pltpu.SMEM),
                  pl.BlockSpec((tm * TOP_K,), lambda i: (jnp.minimum(i + 1, n_tiles - 1),), memory_space=pltpu.SMEM),
                  pl.BlockSpec((tm, d), lambda i: (i, 0)),
                  pl.BlockSpec((tm, dw), lambda i: (i, 0)),
                  pl.BlockSpec((tm, n_e), lambda i: (i, 0)),
                  pl.BlockSpec(memory_space=pl.ANY),
                  const((d, ds)), const((d, ds)), const((ds, d)), const((1, d)),
                  pl.BlockSpec((1, 1, d), lambda i: (i // tiles_per_batch, 0, 0))],
        out_specs=pl.BlockSpec((tm, d), lambda i: (i, 0)),
        out_shape=jax.ShapeDtypeStruct((n, d), F32),
        scratch_shapes=[pltpu.VMEM((2, TOP_K * tm, dw), I32), pltpu.SemaphoreType.DMA((2,))],
        compiler_params=_cparams(),
        name="final_mix",
    )(dest_flat, dest_flat, x1, hp, gates, ys, wsg, wsu, wsd, g_post.reshape(1, d), gt)


def kernel(x, c, ctx, c_ctx, w_ada, b_ada, g_pre_mix, g_post_mix, g_pre_ffn, g_post_ffn, w_in, rpb, ssm_a_re, ssm_a_im, ssm_log_dt, ssm_b_re, ssm_b_im, ssm_c_re, ssm_c_im, ssm_d, w_glu, b_glu, g_attn_out, g_ssm_out, w_out, w_router, b_router, w_exp_gate, w_exp_up, w_exp_down, w_sh_gate, w_sh_up, w_sh_down):
    b, l, d = x.shape
    lc = ctx.shape[1]
    assert w_ada.shape[0] == 1 and b + 1 <= 8
    n_in = w_in.shape[-1]
    d_ssm = w_glu.shape[-1]
    d_attn = d - d_ssm
    n_heads = d_attn // HEAD_DIM
    n = b * l

    c8 = jnp.concatenate([c, c_ctx[None], jnp.zeros((8 - b - 1, d), F32)], axis=0)
    mod = ada_mod(c8, w_ada[0], b_ada[0]).reshape(8, 6, 1, d)
    sh_m, sc_m, gt_m, sh_f, sc_f, gt_f = [mod[:b, j] for j in range(6)]
    csh_m, csc_m = mod[b:b + 1, 0], mod[b:b + 1, 1]

    w_in_b = w_in[0].astype(BF16)
    tn = PROJ_TN
    p_lat = mod_proj(x, g_pre_mix[0], sc_m, sh_m, w_in_b, 0, n_in, 512, tn)
    p_ctx = mod_proj(ctx.reshape(1, b * lc, d), g_pre_mix[0], csc_m, csh_m, w_in_b,
                     d_attn // tn, n_in - d_attn, 512, tn).reshape(b, lc, n_in - d_attn)

    attn = neighborhood_attention(p_lat, p_ctx, rpb[0], n_heads)

    mats = _ssm_mats(ssm_a_re[0], ssm_a_im[0], ssm_log_dt[0], ssm_b_re[0], ssm_b_im[0],
                     ssm_c_re[0], ssm_c_im[0], ssm_d[0])
    y = s5_mixer(p_lat[..., 3 * d_attn:], p_ctx[..., 2 * d_attn:], mats)
    ssm = s5_glu(y.reshape(n, d_ssm), w_glu[0].astype(BF16), b_glu[0])

    mix = merge_proj(attn.reshape(n, d_attn), ssm, g_attn_out[0], g_ssm_out[0], w_out[0].astype(BF16))
    x1, hp, idx, gates, mask = post_mix(x, mix.reshape(b, l, d), g_post_mix[0], gt_m, g_pre_ffn[0], sc_f, sh_f,
                                        w_router[0].astype(BF16), b_router[0])

    n_e = w_router.shape[-1]
    m = n * TOP_K
    idx = idx.reshape(n, n_e)
    rank8, counts = route_ranks(mask.reshape(n, n_e), idx)
    padded = (counts + EXPERT_BLK - 1) // EXPERT_BLK * EXPERT_BLK
    pad_end = jnp.cumsum(padded).astype(I32)
    pad_start = pad_end - padded
    dest = (pad_start[idx[:, :TOP_K]] + rank8).astype(I32).reshape(m)
    n_blocks = m // EXPERT_BLK + n_e
    slots = n_blocks * EXPERT_BLK
    block_e = jnp.minimum(jnp.searchsorted(pad_end, jnp.arange(n_blocks) * EXPERT_BLK, side='right'),
                          n_e - 1).astype(I32)
    n_used = (pad_end[-1] // EXPERT_BLK).astype(I32).reshape(1)

    hp2 = hp.reshape(n, d // 2)
    xs = moe_dispatch(hp2, dest, pad_end, padded.astype(I32), slots)
    ys = expert_ffn(xs, block_e, n_used, w_exp_gate[0].astype(BF16), w_exp_up[0].astype(BF16),
                    w_exp_down[0].astype(BF16), EXPERT_BLK)
    out = final_mix(x1.reshape(n, d), hp2, gates.reshape(n, n_e), ys, dest,
                    w_sh_gate[0].astype(BF16), w_sh_up[0].astype(BF16), w_sh_down[0].astype(BF16),
                    g_post_ffn[0], gt_f, l // COMBINE_TM)
    return out.reshape(b, l, d)
```

```python
import functools

import numpy as np
import jax
import jax.numpy as jnp
from jax import lax
from jax.experimental import pallas as pl
from jax.experimental.pallas import tpu as pltpu

F32 = jnp.float32
BF16 = jnp.bfloat16
I32 = jnp.int32

GRID_W = 64
HEAD_DIM = 128
WIN_ROWS = 8
WIN_COLS = 16
ROPE_THETA = 10000.0
SSM_GROUP_CH = 16
SSM_STATE = 64
TOP_K = 8
ROUTED_SCALE = 2.5
EPS = 1e-6
NEG_INF = -1e30

Q_ROWS = 4
BAND_ROWS = Q_ROWS + WIN_ROWS - 1
VMEM_LIMIT = 56 * 1024 * 1024
EXPERT_BLK = 256
PROJ_TN = 1024
DISPATCH_TM = 256
COMBINE_TM = 128


def _cparams():
    return pltpu.CompilerParams(vmem_limit_bytes=VMEM_LIMIT)


def _rms(x, g):
    return x * lax.rsqrt(jnp.mean(x * x, axis=-1, keepdims=True) + EPS) * g


def _pack_halves(lo, hi):
    lo_bits = lax.bitcast_convert_type(lo.astype(BF16).astype(F32), I32)
    hi_bits = lax.bitcast_convert_type(hi.astype(BF16).astype(F32), I32)
    return (hi_bits & jnp.int32(-65536)) | lax.shift_right_logical(lo_bits, jnp.int32(16))


def _unpack_halves(w):
    lo = lax.bitcast_convert_type(lax.shift_left(w, jnp.int32(16)), F32)
    hi = lax.bitcast_convert_type(w & jnp.int32(-65536), F32)
    return lo, hi


def _ada_kernel(c_ref, w_ref, b_ref, o_ref):
    a = jax.nn.silu(c_ref[...]).astype(BF16)
    o_ref[...] = jnp.dot(a, w_ref[...].astype(BF16), preferred_element_type=F32) + b_ref[...]


def ada_mod(c8, w_ada, b_ada):
    d, n = w_ada.shape
    tn = 512
    return pl.pallas_call(
        _ada_kernel,
        grid=(n // tn,),
        in_specs=[pl.BlockSpec((8, d), lambda j: (0, 0)),
                  pl.BlockSpec((d, tn), lambda j: (0, j)),
                  pl.BlockSpec((1, tn), lambda j: (0, j))],
        out_specs=pl.BlockSpec((8, tn), lambda j: (0, j)),
        out_shape=jax.ShapeDtypeStruct((8, n), F32),
        compiler_params=_cparams(),
        name="ada_mod",
    )(c8, w_ada, b_ada.reshape(1, n))


def _modproj_kernel(x_ref, g_ref, sc_ref, sh_ref, w_ref, o_ref, h_scr):
    @pl.when(pl.program_id(2) == 0)
    def _():
        h = _rms(x_ref[0], g_ref[...]) * (1.0 + sc_ref[0]) + sh_ref[0]
        h_scr[...] = h.astype(BF16)

    o_ref[0] = jnp.dot(h_scr[...], w_ref[...], preferred_element_type=F32).astype(o_ref.dtype)


def mod_proj(x, g, scale, shift, w_bf16, col_blk_off, n_out, tm, tn):
    b, l, d = x.shape
    return pl.pallas_call(
        _modproj_kernel,
        grid=(b, l // tm, n_out // tn),
        in_specs=[pl.BlockSpec((1, tm, d), lambda bi, i, j: (bi, i, 0)),
                  pl.BlockSpec((1, d), lambda bi, i, j: (0, 0)),
                  pl.BlockSpec((1, 1, d), lambda bi, i, j: (bi, 0, 0)),
                  pl.BlockSpec((1, 1, d), lambda bi, i, j: (bi, 0, 0)),
                  pl.BlockSpec((d, tn), lambda bi, i, j: (0, j + col_blk_off))],
        out_specs=pl.BlockSpec((1, tm, tn), lambda bi, i, j: (bi, i, j)),
        out_shape=jax.ShapeDtypeStruct((b, l, n_out), BF16),
        scratch_shapes=[pltpu.VMEM((tm, d), BF16)],
        compiler_params=_cparams(),
        name="mod_proj",
    )(x, g.reshape(1, d), scale, shift, w_bf16)


def _rope_tables(l):
    half = HEAD_DIM // 2
    quarter = half // 2
    inv_freq = 1.0 / (ROPE_THETA ** (jnp.arange(0, half, 2, dtype=F32) / half))
    rows = (jnp.arange(l) // GRID_W).astype(F32)
    cols = (jnp.arange(l) % GRID_W).astype(F32)

    def cs(pos):
        ang = pos[:, None] * inv_freq[None, :]
        return jnp.cos(ang), jnp.sin(ang)

    cr, sr = cs(rows)
    cc, sc = cs(cols)
    zero = jnp.zeros((l, quarter), F32)
    cos = jnp.concatenate([cr, cr, cc, cc], axis=-1)
    s_lo = jnp.concatenate([-sr, zero, -sc, zero], axis=-1)
    s_hi = jnp.concatenate([zero, sr, zero, sc], axis=-1)
    return cos, s_lo, s_hi


def _attn_bias(rpb, rows_n):
    kh = min(WIN_ROWS, rows_n)
    cases = [(0, 0), (Q_ROWS, 0), (rows_n - Q_ROWS, rows_n - BAND_ROWS)]
    col = np.arange(GRID_W)
    cstart = np.clip(col - WIN_COLS // 2, 0, GRID_W - WIN_COLS)
    ok_c = (col[None, :] >= cstart[:, None]) & (col[None, :] < cstart[:, None] + WIN_COLS)
    co = np.clip(col[None, :] - col[:, None], -(WIN_COLS - 1), WIN_COLS - 1) + (WIN_COLS - 1)
    ro_all, ok_all = [], []
    for r0, sb in cases:
        qrow = r0 + np.arange(Q_ROWS)
        krow = sb + np.arange(BAND_ROWS)
        rs = np.clip(qrow - kh // 2, 0, rows_n - kh)
        ok_all.append((krow[None, :] >= rs[:, None]) & (krow[None, :] < rs[:, None] + kh))
        ro_all.append(np.clip(krow[None, :] - qrow[:, None] + (WIN_ROWS - 1), 0, 2 * WIN_ROWS - 2))
    ro_all, ok_all = np.stack(ro_all), np.stack(ok_all)
    onehot = (co[None] == np.arange(2 * WIN_COLS - 1)[:, None, None]).astype(np.float32)
    toe = jnp.einsum('hrc,cqk->hrqk', rpb.astype(F32), onehot, precision=lax.Precision.HIGHEST)
    blocks = toe[:, ro_all]
    ok = ok_all[None, :, :, :, None, None] & ok_c[None, None, None, None]
    bias = jnp.transpose(jnp.where(ok, blocks, NEG_INF), (0, 1, 2, 4, 3, 5))
    return bias.reshape(rpb.shape[0], 3, Q_ROWS * GRID_W, BAND_ROWS * GRID_W)


def _attn_kernel(q_ref, k_ref, v_ref, kc_ref, vc_ref, bias_ref, cos_ref, slo_ref, shi_ref, o_ref,
                 qr_scr, qs_scr, kr_scr, *, n_blk, rows_n):
    quarter = HEAD_DIM // 4
    qn = Q_ROWS * GRID_W
    kn = BAND_ROWS * GRID_W
    scale = HEAD_DIM ** -0.5

    def rope(x, sl):
        return (x * cos_ref[sl, :] + pltpu.roll(x, HEAD_DIM - quarter, 1) * slo_ref[sl, :]
                + pltpu.roll(x, quarter, 1) * shi_ref[sl, :])

    def rope_body(i, _):
        sl = pl.ds(pl.multiple_of(i * qn, qn), qn)
        q = q_ref[0, sl, :].astype(F32) * scale
        qs_scr[sl, :] = q.astype(BF16)
        qr_scr[sl, :] = rope(q, sl).astype(BF16)
        kr_scr[sl, :] = rope(k_ref[0, sl, :].astype(F32), sl).astype(BF16)
        return 0

    lax.fori_loop(0, n_blk, rope_body, 0)

    kc = kc_ref[0]
    vc = vc_ref[0]
    nt = (((1,), (1,)), ((), ()))

    def body(i, _):
        sb = jnp.clip(i * Q_ROWS - WIN_ROWS // 2, 0, rows_n - BAND_ROWS)
        ks = pl.ds(pl.multiple_of(sb * GRID_W, GRID_W), kn)
        qs = pl.ds(pl.multiple_of(i * qn, qn), qn)
        case = jnp.where(i == 0, 0, jnp.where(i == n_blk - 1, 2, 1))
        s = lax.dot_general(qr_scr[qs, :], kr_scr[ks, :], nt, preferred_element_type=F32) + bias_ref[0, case]
        sc = lax.dot_general(qs_scr[qs, :], kc, nt, preferred_element_type=F32)
        m = jnp.maximum(jnp.max(s, axis=-1, keepdims=True), jnp.max(sc, axis=-1, keepdims=True))
        p = jnp.exp(s - m)
        pc = jnp.exp(sc - m)
        den = jnp.sum(p, axis=-1, keepdims=True) + jnp.sum(pc, axis=-1, keepdims=True)
        o = (jnp.dot(p.astype(BF16), v_ref[0, ks, :], preferred_element_type=F32)
             + jnp.dot(pc.astype(BF16), vc, preferred_element_type=F32))
        o_ref[0, qs, :] = (o / den).astype(o_ref.dtype)
        return 0

    lax.fori_loop(0, n_blk, body, 0, unroll=2)


def neighborhood_attention(p_lat, p_ctx, rpb, n_heads):
    b, l, _ = p_lat.shape
    lc = p_ctx.shape[1]
    rows_n = l // GRID_W
    assert rows_n % Q_ROWS == 0 and rows_n >= BAND_ROWS + Q_ROWS
    n_blk = rows_n // Q_ROWS
    bias = _attn_bias(rpb, rows_n)
    cos, s_lo, s_hi = _rope_tables(l)
    qn, kn = Q_ROWS * GRID_W, BAND_ROWS * GRID_W
    h = n_heads
    tab = pl.BlockSpec((l, HEAD_DIM), lambda bi, hi: (0, 0))
    return pl.pallas_call(
        functools.partial(_attn_kernel, n_blk=n_blk, rows_n=rows_n),
        grid=(b, h),
        in_specs=[pl.BlockSpec((1, l, HEAD_DIM), lambda bi, hi: (bi, 0, hi)),
                  pl.BlockSpec((1, l, HEAD_DIM), lambda bi, hi: (bi, 0, hi + h)),
                  pl.BlockSpec((1, l, HEAD_DIM), lambda bi, hi: (bi, 0, hi + 2 * h)),
                  pl.BlockSpec((1, lc, HEAD_DIM), lambda bi, hi: (bi, 0, hi)),
                  pl.BlockSpec((1, lc, HEAD_DIM), lambda bi, hi: (bi, 0, hi + h)),
                  pl.BlockSpec((1, 3, qn, kn), lambda bi, hi: (hi, 0, 0, 0)),
                  tab, tab, tab],
        out_specs=pl.BlockSpec((1, l, HEAD_DIM), lambda bi, hi: (bi, 0, hi)),
        out_shape=jax.ShapeDtypeStruct((b, l, h * HEAD_DIM), BF16),
        scratch_shapes=[pltpu.VMEM((l, HEAD_DIM), BF16)] * 3,
        compiler_params=_cparams(),
        name="nbr_attn",
    )(p_lat, p_lat, p_lat, p_ctx, p_ctx, bias, cos, s_lo, s_hi)


SSM_CHUNK = 16
SSM_ROWS = 8


def _ssm_mats(a_re, a_im, log_dt, b_re, b_im, c_re, c_im, d_skip):
    t = SSM_CHUNK
    hi = lax.Precision.HIGHEST
    a = lax.complex(a_re.astype(F32), a_im.astype(F32))
    dta = jnp.exp(log_dt.astype(F32))[..., None] * a
    a_bar = jnp.exp(dta)
    b_bar = ((a_bar - 1.0) / a)[..., None] * lax.complex(b_re.astype(F32), b_im.astype(F32))
    cm = lax.complex(c_re.astype(F32), c_im.astype(F32))
    k = jnp.arange(t + 1, dtype=F32)
    ap = jnp.exp(dta[..., None] * k)
    g, p, n = a.shape[1], a.shape[2], b_re.shape[-1]
    kern = jnp.einsum('dgnp,dgpl,dgpm->dglnm', cm, ap[..., :t], b_bar, precision=hi).real
    s_i = np.arange(t)[:, None]
    t_i = np.arange(t)[None, :]
    kf = kern[0][:, np.clip(t_i - s_i, 0, t - 1)]
    kb = kern[1][:, np.clip(s_i - t_i, 0, t - 1)]
    eye = jnp.eye(n, dtype=F32)
    m_mat = (jnp.where((s_i <= t_i)[None, :, :, None, None], kf, 0.0)
             + jnp.where((s_i >= t_i)[None, :, :, None, None], kb, 0.0)
             + (s_i == t_i)[None, :, :, None, None] * (d_skip.astype(F32)[:, None, None, :, None] * eye))
    m_mat = jnp.transpose(m_mat, (0, 1, 4, 2, 3)).reshape(g, t * n, t * n)
    wf = ap[0][:, :, t - 1::-1][..., :t, None] * b_bar[0][:, :, None, :]
    wb = ap[1][:, :, :t, None] * b_bar[1][:, :, None, :]
    to_rows = lambda z: jnp.transpose(z, (0, 2, 3, 1)).reshape(g, t * n, p)
    w_mat = jnp.concatenate([to_rows(wf.real), to_rows(wb.real), to_rows(wf.imag), to_rows(wb.imag)], axis=-1)
    zf = jnp.transpose(cm[0], (0, 2, 1))[:, :, None, :] * ap[0][:, :, 1:t + 1, None]
    zb = jnp.transpose(cm[1], (0, 2, 1))[:, :, None, :] * ap[1][:, :, t:0:-1, None]
    flat = lambda z: z.reshape(g, p, t * n)
    r_mat = jnp.concatenate([flat(zf.real), flat(zb.real), -flat(zf.imag), -flat(zb.imag)], axis=1)
    a_t = ap[..., t]
    a_vec = jnp.stack([jnp.concatenate([a_t[0].real, a_t[1].real], -1),
                       jnp.concatenate([a_t[0].imag, a_t[1].imag], -1)], axis=1)
    a_vec = jnp.pad(a_vec, ((0, 0), (0, SSM_ROWS - 2), (0, 0)))
    return m_mat.astype(BF16), w_mat.astype(BF16), r_mat.astype(BF16), a_vec


def _s5_kernel(u_ref, m_ref, w_ref, r_ref, a_ref, y_ref, v_scr, ent_scr, *, n_ctx, n_chunks):
    u = u_ref[0]
    p2 = a_ref.shape[-1]
    p = p2 // 2
    v_scr[...] = jnp.dot(u, w_ref[0], preferred_element_type=F32)
    a_re = jnp.broadcast_to(a_ref[0, 0:1, :], (SSM_ROWS, p2))
    a_im = jnp.broadcast_to(a_ref[0, 1:2, :], (SSM_ROWS, p2))
    is_fwd = lax.broadcasted_iota(I32, (SSM_ROWS, p2), 1) < p

    def step(j, carry):
        s_re, s_im = carry
        cb = jnp.where(j < n_ctx, n_ctx - 1 - j, n_chunks + n_ctx - 1 - j)
        rf = pl.ds(pl.multiple_of(j * SSM_ROWS, SSM_ROWS), SSM_ROWS)
        rb = pl.ds(pl.multiple_of(cb * SSM_ROWS, SSM_ROWS), SSM_ROWS)
        ent_scr[rf, 0:p] = s_re[:, :p]
        ent_scr[rb, p:p2] = s_re[:, p:]
        ent_scr[rf, p2:p2 + p] = s_im[:, :p]
        ent_scr[rb, p2 + p:] = s_im[:, p:]
        in_re = jnp.where(is_fwd, v_scr[rf, :p2], v_scr[rb, :p2])
        in_im = jnp.where(is_fwd, v_scr[rf, p2:], v_scr[rb, p2:])
        return a_re * s_re - a_im * s_im + in_re, a_re * s_im + a_im * s_re + in_im

    zero = jnp.zeros((SSM_ROWS, p2), F32)
    lax.fori_loop(0, n_chunks, step, (zero, zero))
    y_ref[0] = (jnp.dot(u, m_ref[0], preferred_element_type=F32)
                + jnp.dot(ent_scr[...].astype(BF16), r_ref[0], preferred_element_type=F32)).astype(y_ref.dtype)


def s5_mixer(u_lat, u_ctx, mats):
    m_mat, w_mat, r_mat, a_vec = mats
    b, l, dch = u_lat.shape
    lc = u_ctx.shape[1]
    g = m_mat.shape[0]
    n = dch // g
    t = SSM_CHUNK
    tn = t * n
    n_ctx, n_chunks = lc // t, (lc + l) // t
    assert b <= SSM_ROWS and lc % t == 0 and l % t == 0
    z = jnp.concatenate([u_ctx, u_lat], axis=1).reshape(b, n_chunks, t, g, n)
    z = jnp.pad(jnp.transpose(z, (3, 1, 0, 2, 4)), ((0, 0), (0, 0), (0, SSM_ROWS - b), (0, 0), (0, 0)))
    rows = n_chunks * SSM_ROWS
    z = z.reshape(g, rows, tn)
    mat = lambda k: pl.BlockSpec((1, k, tn), lambda gi: (gi, 0, 0))
    y = pl.pallas_call(
        functools.partial(_s5_kernel, n_ctx=n_ctx, n_chunks=n_chunks),
        grid=(g,),
        in_specs=[mat(rows), mat(tn), mat(tn), mat(r_mat.shape[1]),
                  pl.BlockSpec((1, SSM_ROWS, a_vec.shape[-1]), lambda gi: (gi, 0, 0))],
        out_specs=mat(rows),
        out_shape=jax.ShapeDtypeStruct((g, rows, tn), BF16),
        scratch_shapes=[pltpu.VMEM((rows, w_mat.shape[-1]), F32), pltpu.VMEM((rows, r_mat.shape[1]), F32)],
        compiler_params=_cparams(),
        name="s5_mixer",
    )(z, m_mat, w_mat, r_mat, a_vec)
    y = y.reshape(g, n_chunks, SSM_ROWS, t, n)[:, n_ctx:, :b]
    return jnp.transpose(y, (2, 1, 3, 0, 4)).reshape(b, l, dch)


def _glu_kernel(y_ref, w_ref, b_ref, o_ref):
    z = jax.nn.gelu(y_ref[...].astype(F32))
    gate = jax.nn.sigmoid(jnp.dot(z.astype(BF16), w_ref[...], preferred_element_type=F32) + b_ref[...])
    o_ref[...] = (z * gate).astype(o_ref.dtype)


def s5_glu(y, w_bf16, b_glu, tm=512):
    n, d = y.shape
    return pl.pallas_call(
        _glu_kernel,
        grid=(n // tm,),
        in_specs=[pl.BlockSpec((tm, d), lambda i: (i, 0)),
                  pl.BlockSpec((d, d), lambda i: (0, 0)),
                  pl.BlockSpec((1, d), lambda i: (0, 0))],
        out_specs=pl.BlockSpec((tm, d), lambda i: (i, 0)),
        out_shape=jax.ShapeDtypeStruct((n, d), BF16),
        compiler_params=_cparams(),
        name="s5_glu",
    )(y, w_bf16, b_glu.reshape(1, d))


def _merge_kernel(a_ref, s_ref, ga_ref, gs_ref, w_ref, o_ref, h_scr):
    da = a_ref.shape[-1]

    @pl.when(pl.program_id(1) == 0)
    def _():
        h_scr[:, :da] = _rms(a_ref[...].astype(F32), ga_ref[...]).astype(BF16)
        h_scr[:, da:] = _rms(s_ref[...].astype(F32), gs_ref[...]).astype(BF16)

    o_ref[...] = jnp.dot(h_scr[...], w_ref[...], preferred_element_type=F32)


def merge_proj(attn, ssm, g_attn, g_ssm, w_bf16, tm=512, tn=1024):
    n, da = attn.shape
    ds = ssm.shape[1]
    d, dout = w_bf16.shape
    return pl.pallas_call(
        _merge_kernel,
        grid=(n // tm, dout // tn),
        in_specs=[pl.BlockSpec((tm, da), lambda i, j: (i, 0)),
                  pl.BlockSpec((tm, ds), lambda i, j: (i, 0)),
                  pl.BlockSpec((1, da), lambda i, j: (0, 0)),
                  pl.BlockSpec((1, ds), lambda i, j: (0, 0)),
                  pl.BlockSpec((d, tn), lambda i, j: (0, j))],
        out_specs=pl.BlockSpec((tm, tn), lambda i, j: (i, j)),
        out_shape=jax.ShapeDtypeStruct((n, dout), F32),
        scratch_shapes=[pltpu.VMEM((tm, d), BF16)],
        compiler_params=_cparams(),
        name="merge_proj",
    )(attn, ssm, g_attn.reshape(1, da), g_ssm.reshape(1, ds), w_bf16)


def _post_mix_kernel(x_ref, mix_ref, gpost_ref, gt_ref, gpre_ref, sc_ref, sh_ref, wr_ref, br_ref,
                     x1_ref, hp_ref, idx_ref, gate_ref, mask_ref):
    x1 = x_ref[0] + gt_ref[0] * _rms(mix_ref[0], gpost_ref[...])
    x1_ref[0] = x1
    h = _rms(x1, gpre_ref[...]) * (1.0 + sc_ref[0]) + sh_ref[0]
    half = h.shape[-1] // 2
    hp_ref[0] = _pack_halves(h[:, :half], h[:, half:])
    scores = jax.nn.sigmoid(jnp.dot(h.astype(BF16), wr_ref[...], preferred_element_type=F32))
    n_e = scores.shape[-1]
    lane = lax.broadcasted_iota(I32, scores.shape, 1)
    biased = scores + br_ref[...]
    idx_out = jnp.zeros(scores.shape, I32)
    sel_out = jnp.zeros(scores.shape, F32)
    mask = jnp.zeros(scores.shape, jnp.bool_)
    for k in range(TOP_K):
        m = jnp.max(biased, axis=-1, keepdims=True)
        ik = jnp.min(jnp.where(biased == m, lane, n_e), axis=-1, keepdims=True)
        hit = lane == ik
        sel_k = jnp.sum(jnp.where(hit, scores, 0.0), axis=-1, keepdims=True)
        idx_out = jnp.where(lane == k, ik, idx_out)
        sel_out = jnp.where(lane == k, sel_k, sel_out)
        mask = jnp.logical_or(mask, hit)
        biased = jnp.where(hit, -jnp.inf, biased)
    idx_ref[0] = idx_out
    gate_ref[0] = sel_out / jnp.sum(sel_out, axis=-1, keepdims=True) * ROUTED_SCALE
    mask_ref[0] = mask.astype(BF16)


def post_mix(x, mix, g_post, gt, g_pre, scale, shift, wr_bf16, b_router, tm=256):
    b, l, d = x.shape
    n_e = wr_bf16.shape[1]
    row = pl.BlockSpec((1, tm, d), lambda bi, i: (bi, i, 0))
    prow = pl.BlockSpec((1, tm, d // 2), lambda bi, i: (bi, i, 0))
    vec = pl.BlockSpec((1, d), lambda bi, i: (0, 0))
    bvec = pl.BlockSpec((1, 1, d), lambda bi, i: (bi, 0, 0))
    small = pl.BlockSpec((1, tm, n_e), lambda bi, i: (bi, i, 0))
    return pl.pallas_call(
        _post_mix_kernel,
        grid=(b, l // tm),
        in_specs=[row, row, vec, bvec, vec, bvec, bvec,
                  pl.BlockSpec((d, n_e), lambda bi, i: (0, 0)),
                  pl.BlockSpec((1, n_e), lambda bi, i: (0, 0))],
        out_specs=[row, prow, small, small, small],
        out_shape=[jax.ShapeDtypeStruct((b, l, d), F32),
                   jax.ShapeDtypeStruct((b, l, d // 2), I32),
                   jax.ShapeDtypeStruct((b, l, n_e), I32),
                   jax.ShapeDtypeStruct((b, l, n_e), F32),
                   jax.ShapeDtypeStruct((b, l, n_e), BF16)],
        compiler_params=_cparams(),
        name="post_mix",
    )(x, mix, g_post.reshape(1, d), gt, g_pre.reshape(1, d), scale, shift, wr_bf16, b_router.reshape(1, n_e))


def _rank_kernel(mask_ref, idx_ref, rank_ref, cnt_ref, carry):
    @pl.when(pl.program_id(0) == 0)
    def _():
        carry[...] = jnp.zeros_like(carry)

    m = mask_ref[...]
    tm = m.shape[0]
    earlier = (lax.broadcasted_iota(I32, (tm, tm), 1) < lax.broadcasted_iota(I32, (tm, tm), 0)).astype(BF16)
    excl = jnp.dot(earlier, m, preferred_element_type=F32) + carry[0:1, :]
    lane = lax.broadcasted_iota(I32, m.shape, 1)
    idx = idx_ref[...]
    out = jnp.zeros(m.shape, F32)
    for k in range(TOP_K):
        rk = jnp.sum(jnp.where(lane == idx[:, k:k + 1], excl, 0.0), axis=-1, keepdims=True)
        out = jnp.where(lane == k, rk, out)
    rank_ref[...] = out.astype(I32)
    carry[0:1, :] = carry[0:1, :] + jnp.sum(m.astype(F32), axis=0, keepdims=True)
    cnt_ref[...] = carry[...]


def route_ranks(mask, idx, tm=512):
    n, n_e = mask.shape
    blk = pl.BlockSpec((tm, n_e), lambda i: (i, 0))
    rank, cnt = pl.pallas_call(
        _rank_kernel,
        grid=(n // tm,),
        in_specs=[blk, blk],
        out_specs=[blk, pl.BlockSpec((8, n_e), lambda i: (0, 0))],
        out_shape=[jax.ShapeDtypeStruct((n, n_e), I32), jax.ShapeDtypeStruct((8, n_e), F32)],
        scratch_shapes=[pltpu.VMEM((8, n_e), F32)],
        compiler_params=_cparams(),
        name="route_ranks",
    )(mask, idx)
    return rank, cnt[0].astype(I32)


def _dest_kernel(rank_ref, idx_ref, ps_ref, dest_ref):
    idx = idx_ref[...]
    lane = lax.broadcasted_iota(I32, idx.shape, 1)
    start = jnp.zeros(idx.shape, F32)
    for k in range(TOP_K):
        sk = jnp.sum(jnp.where(lane == idx[:, k:k + 1], ps_ref[...], 0.0), axis=-1, keepdims=True)
        start = jnp.where(lane == k, sk, start)
    dest_ref[...] = rank_ref[...] + start.astype(I32)


def route_dest(rank, idx, pad_start, tm=512):
    n, n_e = rank.shape
    blk = pl.BlockSpec((tm, n_e), lambda i: (i, 0))
    dest = pl.pallas_call(
        _dest_kernel,
        grid=(n // tm,),
        in_specs=[blk, blk, pl.BlockSpec((1, n_e), lambda i: (0, 0))],
        out_specs=blk,
        out_shape=jax.ShapeDtypeStruct((n, n_e), I32),
        compiler_params=_cparams(),
        name="route_dest",
    )(rank, idx, pad_start.astype(F32).reshape(1, n_e))
    return dest[:, :TOP_K].reshape(n * TOP_K)


def _dispatch_kernel(pe_ref, pd_ref, dest_ref, h_ref, xs_ref, z_scr, sem, *, n_e, tm, blk):
    def zero_copy(e):
        start = pl.multiple_of(pe_ref[e] - blk, blk)
        return pltpu.make_async_copy(z_scr, xs_ref.at[pl.ds(start, blk), :], sem)

    @pl.when(pl.program_id(0) == 0)
    def _():
        z_scr[...] = jnp.zeros_like(z_scr)

        def start(e, _):
            @pl.when(pd_ref[e] > 0)
            def _():
                zero_copy(e).start()
            return 0

        def wait(e, _):
            @pl.when(pd_ref[e] > 0)
            def _():
                zero_copy(e).wait()
            return 0

        lax.fori_loop(0, n_e, start, 0)
        lax.fori_loop(0, n_e, wait, 0)

    def row_copy(t, k):
        return pltpu.make_async_copy(h_ref.at[pl.ds(t, 1), :], xs_ref.at[pl.ds(dest_ref[t * TOP_K + k], 1), :], sem)

    def issue(t, _):
        for k in range(TOP_K):
            row_copy(t, k).start(priority=k % 2)
        return 0

    lax.fori_loop(0, tm, issue, 0)
    rows = tm * TOP_K
    pltpu.make_async_copy(xs_ref.at[pl.ds(0, rows), :], xs_ref.at[pl.ds(0, rows), :], sem).wait()


def moe_dispatch(hp, dest_flat, pad_end, padded, slots, tm=DISPATCH_TM):
    n, dw = hp.shape
    n_e = pad_end.shape[0]
    grid_spec = pltpu.PrefetchScalarGridSpec(
        num_scalar_prefetch=2,
        grid=(n // tm,),
        in_specs=[pl.BlockSpec((tm * TOP_K,), lambda i, pe, pd: (i,), memory_space=pltpu.SMEM),
                  pl.BlockSpec((tm, dw), lambda i, pe, pd: (i, 0))],
        out_specs=pl.BlockSpec(memory_space=pl.ANY),
        scratch_shapes=[pltpu.VMEM((EXPERT_BLK, dw), I32), pltpu.SemaphoreType.DMA(())],
    )
    return pl.pallas_call(
        functools.partial(_dispatch_kernel, n_e=n_e, tm=tm, blk=EXPERT_BLK),
        grid_spec=grid_spec,
        out_shape=jax.ShapeDtypeStruct((slots, dw), I32),
        compiler_params=_cparams(),
        name="moe_dispatch",
    )(pad_end, padded, dest_flat, hp)


PACK_CHUNK = 512


def _swiglu_hidden(xp_ref, wgu):
    half = xp_ref.shape[-1]
    acc = None
    for c in range(half // PACK_CHUNK):
        cl = slice(c * PACK_CHUNK, (c + 1) * PACK_CHUNK)
        ch = slice(half + c * PACK_CHUNK, half + (c + 1) * PACK_CHUNK)
        lo, hi = _unpack_halves(xp_ref[:, cl])
        part = (jnp.dot(lo.astype(BF16), wgu(cl), preferred_element_type=F32)
                + jnp.dot(hi.astype(BF16), wgu(ch), preferred_element_type=F32))
        acc = part if acc is None else acc + part
    de = acc.shape[-1] // 2
    return (jax.nn.silu(acc[:, :de]) * acc[:, de:]).astype(BF16)


def _expert_kernel(be_ref, nu_ref, x_ref, wgu_ref, wd_ref, o_ref, hid_scr):
    i = pl.program_id(0)
    half = x_ref.shape[-1]

    @pl.when(i == 0)
    def _():
        hid_scr[...] = jnp.zeros_like(hid_scr)

    @pl.when(i <= nu_ref[0])
    def _():
        hid_prev = hid_scr[...]
        for c in range(half // PACK_CHUNK):
            cl = slice(c * PACK_CHUNK, (c + 1) * PACK_CHUNK)
            ch = slice(half + c * PACK_CHUNK, half + (c + 1) * PACK_CHUNK)
            o_ref[:, cl] = _pack_halves(jnp.dot(hid_prev, wd_ref[0, :, cl], preferred_element_type=F32),
                                        jnp.dot(hid_prev, wd_ref[0, :, ch], preferred_element_type=F32))
        hid_scr[...] = _swiglu_hidden(x_ref, lambda sl: wgu_ref[0, sl, :])

    @pl.when(i > nu_ref[0])
    def _():
        o_ref[...] = jnp.zeros_like(o_ref)


def expert_ffn(xs, block_e, n_used, wgu, wd, blk):
    slots, dw = xs.shape
    de, d = wd.shape[1], wd.shape[2]
    n_blocks = slots // blk
    cur = lambda i, nu: jnp.minimum(i, nu[0] - 1)
    prev = lambda i: jnp.maximum(i - 1, 0)
    grid_spec = pltpu.PrefetchScalarGridSpec(
        num_scalar_prefetch=2,
        grid=(n_blocks + 1,),
        in_specs=[pl.BlockSpec((blk, dw), lambda i, be, nu: (cur(i, nu), 0)),
                  pl.BlockSpec((1, d, 2 * de), lambda i, be, nu: (be[cur(i, nu)], 0, 0)),
                  pl.BlockSpec((1, de, d), lambda i, be, nu: (be[prev(i)], 0, 0))],
        out_specs=pl.BlockSpec((blk, dw), lambda i, be, nu: (prev(i), 0)),
        scratch_shapes=[pltpu.VMEM((blk, de), BF16)],
    )
    return pl.pallas_call(
        _expert_kernel,
        grid_spec=grid_spec,
        out_shape=jax.ShapeDtypeStruct((slots, dw), I32),
        compiler_params=_cparams(),
        name="expert_ffn",
    )(block_e, n_used, xs, wgu, wd)


def _final_kernel(dc_ref, dn_ref, x1_ref, hp_ref, gate_ref, ys_ref, wsgu_ref, wsd_ref, gpost_ref, gt_ref,
                  o_ref, buf, sem, *, tm, n_tiles):
    i = pl.program_id(0)
    slot = i % 2

    def row_copy(dref, s, t, k):
        return pltpu.make_async_copy(ys_ref.at[pl.ds(dref[t * TOP_K + k], 1), :],
                                     buf.at[s, pl.ds(k * tm + t, 1), :], sem.at[s])

    def issue(dref, s):
        def body(t, _):
            for k in range(TOP_K):
                row_copy(dref, s, t, k).start(priority=k % 2)
            return 0
        lax.fori_loop(0, tm, body, 0)

    @pl.when(i == 0)
    def _():
        issue(dc_ref, 0)

    @pl.when(i + 1 < n_tiles)
    def _():
        issue(dn_ref, 1 - slot)

    pltpu.make_async_copy(ys_ref.at[pl.ds(0, TOP_K * tm), :], buf.at[slot], sem.at[slot]).wait()

    g = gate_ref[...]
    half = hp_ref.shape[-1]
    acc_lo = jnp.zeros((tm, half), F32)
    acc_hi = jnp.zeros((tm, half), F32)
    for k in range(TOP_K):
        lo, hi = _unpack_halves(buf[slot, pl.ds(k * tm, tm), :])
        gk = g[:, k:k + 1]
        acc_lo = acc_lo + gk * lo
        acc_hi = acc_hi + gk * hi
    hid = _swiglu_hidden(hp_ref, lambda sl: wsgu_ref[sl, :])
    shared = jnp.dot(hid, wsd_ref[...], preferred_element_type=F32)
    ffn = jnp.concatenate([acc_lo, acc_hi], axis=-1) + shared
    o_ref[...] = x1_ref[...] + gt_ref[0] * _rms(ffn, gpost_ref[...])


def final_mix(x1, hp, gates, ys, dest_flat, wsgu, wsd, g_post, gt, tiles_per_batch, tm=COMBINE_TM):
    n, d = x1.shape
    dw = hp.shape[1]
    n_e = gates.shape[1]
    ds = wsd.shape[0]
    n_tiles = n // tm
    const = lambda shape: pl.BlockSpec(shape, lambda i: (0,) * len(shape))
    return pl.pallas_call(
        functools.partial(_final_kernel, tm=tm, n_tiles=n_tiles),
        grid=(n_tiles,),
        in_specs=[pl.BlockSpec((tm * TOP_K,), lambda i: (i,), memory_space=pltpu.SMEM),
                  pl.BlockSpec((tm * TOP_K,), lambda i: (jnp.minimum(i + 1, n_tiles - 1),), memory_space=pltpu.SMEM),
                  pl.BlockSpec((tm, d), lambda i: (i, 0)),
                  pl.BlockSpec((tm, dw), lambda i: (i, 0)),
                  pl.BlockSpec((tm, n_e), lambda i: (i, 0)),
                  pl.BlockSpec(memory_space=pl.ANY),
                  const((d, 2 * ds)), const((ds, d)), const((1, d)),
                  pl.BlockSpec((1, 1, d), lambda i: (i // tiles_per_batch, 0, 0))],
        out_specs=pl.BlockSpec((tm, d), lambda i: (i, 0)),
        out_shape=jax.ShapeDtypeStruct((n, d), F32),
        scratch_shapes=[pltpu.VMEM((2, TOP_K * tm, dw), I32), pltpu.SemaphoreType.DMA((2,))],
        compiler_params=_cparams(),
        name="final_mix",
    )(dest_flat, dest_flat, x1, hp, gates, ys, wsgu, wsd, g_post.reshape(1, d), gt)


def kernel(x, c, ctx, c_ctx, w_ada, b_ada, g_pre_mix, g_post_mix, g_pre_ffn, g_post_ffn, w_in, rpb, ssm_a_re, ssm_a_im, ssm_log_dt, ssm_b_re, ssm_b_im, ssm_c_re, ssm_c_im, ssm_d, w_glu, b_glu, g_attn_out, g_ssm_out, w_out, w_router, b_router, w_exp_gate, w_exp_up, w_exp_down, w_sh_gate, w_sh_up, w_sh_down):
    b, l, d = x.shape
    lc = ctx.shape[1]
    assert w_ada.shape[0] == 1 and b + 1 <= 8
    n_in = w_in.shape[-1]
    d_ssm = w_glu.shape[-1]
    d_attn = d - d_ssm
    n_heads = d_attn // HEAD_DIM
    n = b * l

    c8 = jnp.concatenate([c, c_ctx[None], jnp.zeros((8 - b - 1, d), F32)], axis=0)
    mod = ada_mod(c8, w_ada[0], b_ada[0]).reshape(8, 6, 1, d)
    sh_m, sc_m, gt_m, sh_f, sc_f, gt_f = [mod[:b, j] for j in range(6)]
    csh_m, csc_m = mod[b:b + 1, 0], mod[b:b + 1, 1]

    w_in_b = w_in[0].astype(BF16)
    tn = PROJ_TN
    p_lat = mod_proj(x, g_pre_mix[0], sc_m, sh_m, w_in_b, 0, n_in, 512, tn)
    p_ctx = mod_proj(ctx.reshape(1, b * lc, d), g_pre_mix[0], csc_m, csh_m, w_in_b,
                     d_attn // tn, n_in - d_attn, 512, tn).reshape(b, lc, n_in - d_attn)

    attn = neighborhood_attention(p_lat, p_ctx, rpb[0], n_heads)

    mats = _ssm_mats(ssm_a_re[0], ssm_a_im[0], ssm_log_dt[0], ssm_b_re[0], ssm_b_im[0],
                     ssm_c_re[0], ssm_c_im[0], ssm_d[0])
    y = s5_mixer(p_lat[..., 3 * d_attn:], p_ctx[..., 2 * d_attn:], mats)
    ssm = s5_glu(y.reshape(n, d_ssm), w_glu[0].astype(BF16), b_glu[0])

    mix = merge_proj(attn.reshape(n, d_attn), ssm, g_attn_out[0], g_ssm_out[0], w_out[0].astype(BF16))
    x1, hp, idx, gates, mask = post_mix(x, mix.reshape(b, l, d), g_post_mix[0], gt_m, g_pre_ffn[0], sc_f, sh_f,
                                        w_router[0].astype(BF16), b_router[0])

    n_e = w_router.shape[-1]
    m = n * TOP_K
    idx = idx.reshape(n, n_e)
    rank, counts = route_ranks(mask.reshape(n, n_e), idx)
    padded = (counts + EXPERT_BLK - 1) // EXPERT_BLK * EXPERT_BLK
    pad_end = jnp.cumsum(padded).astype(I32)
    dest = route_dest(rank, idx, pad_end - padded)
    n_blocks = m // EXPERT_BLK + n_e
    slots = n_blocks * EXPERT_BLK
    block_e = jnp.minimum(jnp.searchsorted(pad_end, jnp.arange(n_blocks) * EXPERT_BLK, side='right'),
                          n_e - 1).astype(I32)
    n_used = (pad_end[-1] // EXPERT_BLK).astype(I32).reshape(1)

    hp2 = hp.reshape(n, d // 2)
    xs = moe_dispatch(hp2, dest, pad_end, padded.astype(I32), slots)
    wgu = jnp.concatenate([w_exp_gate[0], w_exp_up[0]], axis=-1).astype(BF16)
    ys = expert_ffn(xs, block_e, n_used, wgu, w_exp_down[0].astype(BF16), EXPERT_BLK)
    out = final_mix(x1.reshape(n, d), hp2, gates.reshape(n, n_e), ys, dest,
                    jnp.concatenate([w_sh_gate[0], w_sh_up[0]], axis=-1).astype(BF16), w_sh_down[0].astype(BF16),
                    g_post_ffn[0], gt_f, l // COMBINE_TM)
    return out.reshape(b, l, d)
```

```python
import functools

import numpy as np
import jax
import jax.numpy as jnp
from jax import lax
from jax.experimental import pallas as pl
from jax.experimental.pallas import tpu as pltpu

F32 = jnp.float32
BF16 = jnp.bfloat16
I32 = jnp.int32

GRID_W = 64
HEAD_DIM = 128
WIN_ROWS = 8
WIN_COLS = 16
ROPE_THETA = 10000.0
SSM_GROUP_CH = 16
SSM_STATE = 64
TOP_K = 8
ROUTED_SCALE = 2.5
EPS = 1e-6
NEG_INF = -1e30

Q_ROWS = 4
BAND_ROWS = Q_ROWS + WIN_ROWS - 1
VMEM_LIMIT = 56 * 1024 * 1024
EXPERT_BLK = 256
PROJ_TN = 1024
DISPATCH_TM = 256
COMBINE_TM = 128


def _cparams():
    return pltpu.CompilerParams(vmem_limit_bytes=VMEM_LIMIT)


def _rms(x, g):
    return x * lax.rsqrt(jnp.mean(x * x, axis=-1, keepdims=True) + EPS) * g


def _pack_halves(lo, hi):
    lo_bits = lax.bitcast_convert_type(lo.astype(BF16).astype(F32), I32)
    hi_bits = lax.bitcast_convert_type(hi.astype(BF16).astype(F32), I32)
    return (hi_bits & jnp.int32(-65536)) | lax.shift_right_logical(lo_bits, jnp.int32(16))


def _unpack_halves(w):
    lo = lax.bitcast_convert_type(lax.shift_left(w, jnp.int32(16)), F32)
    hi = lax.bitcast_convert_type(w & jnp.int32(-65536), F32)
    return lo, hi


def _ada_kernel(c_ref, w_ref, b_ref, o_ref):
    a = jax.nn.silu(c_ref[...]).astype(BF16)
    o_ref[...] = jnp.dot(a, w_ref[...].astype(BF16), preferred_element_type=F32) + b_ref[...]


def ada_mod(c8, w_ada, b_ada):
    d, n = w_ada.shape
    tn = 512
    return pl.pallas_call(
        _ada_kernel,
        grid=(n // tn,),
        in_specs=[pl.BlockSpec((8, d), lambda j: (0, 0)),
                  pl.BlockSpec((d, tn), lambda j: (0, j)),
                  pl.BlockSpec((1, tn), lambda j: (0, j))],
        out_specs=pl.BlockSpec((8, tn), lambda j: (0, j)),
        out_shape=jax.ShapeDtypeStruct((8, n), F32),
        compiler_params=_cparams(),
        name="ada_mod",
    )(c8, w_ada, b_ada.reshape(1, n))


def _modproj_kernel(x_ref, g_ref, sc_ref, sh_ref, w_ref, o_ref, h_scr):
    @pl.when(pl.program_id(2) == 0)
    def _():
        h = _rms(x_ref[0], g_ref[...]) * (1.0 + sc_ref[0]) + sh_ref[0]
        h_scr[...] = h.astype(BF16)

    o_ref[0] = jnp.dot(h_scr[...], w_ref[...], preferred_element_type=F32).astype(o_ref.dtype)


def mod_proj(x, g, scale, shift, w_bf16, col_blk_off, n_out, tm, tn):
    b, l, d = x.shape
    return pl.pallas_call(
        _modproj_kernel,
        grid=(b, l // tm, n_out // tn),
        in_specs=[pl.BlockSpec((1, tm, d), lambda bi, i, j: (bi, i, 0)),
                  pl.BlockSpec((1, d), lambda bi, i, j: (0, 0)),
                  pl.BlockSpec((1, 1, d), lambda bi, i, j: (bi, 0, 0)),
                  pl.BlockSpec((1, 1, d), lambda bi, i, j: (bi, 0, 0)),
                  pl.BlockSpec((d, tn), lambda bi, i, j: (0, j + col_blk_off))],
        out_specs=pl.BlockSpec((1, tm, tn), lambda bi, i, j: (bi, i, j)),
        out_shape=jax.ShapeDtypeStruct((b, l, n_out), BF16),
        scratch_shapes=[pltpu.VMEM((tm, d), BF16)],
        compiler_params=_cparams(),
        name="mod_proj",
    )(x, g.reshape(1, d), scale, shift, w_bf16)


def _rope_tables(l):
    half = HEAD_DIM // 2
    quarter = half // 2
    inv_freq = 1.0 / (ROPE_THETA ** (jnp.arange(0, half, 2, dtype=F32) / half))
    rows = (jnp.arange(l) // GRID_W).astype(F32)
    cols = (jnp.arange(l) % GRID_W).astype(F32)

    def cs(pos):
        ang = pos[:, None] * inv_freq[None, :]
        return jnp.cos(ang), jnp.sin(ang)

    cr, sr = cs(rows)
    cc, sc = cs(cols)
    zero = jnp.zeros((l, quarter), F32)
    cos = jnp.concatenate([cr, cr, cc, cc], axis=-1)
    s_lo = jnp.concatenate([-sr, zero, -sc, zero], axis=-1)
    s_hi = jnp.concatenate([zero, sr, zero, sc], axis=-1)
    return cos, s_lo, s_hi


def _attn_bias(rpb, rows_n):
    kh = min(WIN_ROWS, rows_n)
    cases = [(0, 0), (Q_ROWS, 0), (rows_n - Q_ROWS, rows_n - BAND_ROWS)]
    col = np.arange(GRID_W)
    cstart = np.clip(col - WIN_COLS // 2, 0, GRID_W - WIN_COLS)
    ok_c = (col[None, :] >= cstart[:, None]) & (col[None, :] < cstart[:, None] + WIN_COLS)
    co = np.clip(col[None, :] - col[:, None], -(WIN_COLS - 1), WIN_COLS - 1) + (WIN_COLS - 1)
    ro_all, ok_all = [], []
    for r0, sb in cases:
        qrow = r0 + np.arange(Q_ROWS)
        krow = sb + np.arange(BAND_ROWS)
        rs = np.clip(qrow - kh // 2, 0, rows_n - kh)
        ok_all.append((krow[None, :] >= rs[:, None]) & (krow[None, :] < rs[:, None] + kh))
        ro_all.append(np.clip(krow[None, :] - qrow[:, None] + (WIN_ROWS - 1), 0, 2 * WIN_ROWS - 2))
    ro_all, ok_all = np.stack(ro_all), np.stack(ok_all)
    onehot = (co[None] == np.arange(2 * WIN_COLS - 1)[:, None, None]).astype(np.float32)
    toe = jnp.einsum('hrc,cqk->hrqk', rpb.astype(F32), onehot, precision=lax.Precision.HIGHEST)
    blocks = toe[:, ro_all]
    ok = ok_all[None, :, :, :, None, None] & ok_c[None, None, None, None]
    bias = jnp.transpose(jnp.where(ok, blocks, NEG_INF), (0, 1, 2, 4, 3, 5))
    return bias.reshape(rpb.shape[0], 3, Q_ROWS * GRID_W, BAND_ROWS * GRID_W)


def _attn_kernel(q_ref, k_ref, v_ref, kc_ref, vc_ref, bias_ref, cos_ref, slo_ref, shi_ref, o_ref,
                 qr_scr, qs_scr, kr_scr, *, n_blk, rows_n):
    quarter = HEAD_DIM // 4
    qn = Q_ROWS * GRID_W
    kn = BAND_ROWS * GRID_W
    scale = HEAD_DIM ** -0.5

    def rope(x, sl):
        return (x * cos_ref[sl, :] + pltpu.roll(x, HEAD_DIM - quarter, 1) * slo_ref[sl, :]
                + pltpu.roll(x, quarter, 1) * shi_ref[sl, :])

    def rope_body(i, _):
        sl = pl.ds(pl.multiple_of(i * qn, qn), qn)
        q = q_ref[0, sl, :].astype(F32) * scale
        qs_scr[sl, :] = q.astype(BF16)
        qr_scr[sl, :] = rope(q, sl).astype(BF16)
        kr_scr[sl, :] = rope(k_ref[0, sl, :].astype(F32), sl).astype(BF16)
        return 0

    lax.fori_loop(0, n_blk, rope_body, 0)

    kc = kc_ref[0]
    vc = vc_ref[0]
    nt = (((1,), (1,)), ((), ()))

    def body(i, _):
        sb = jnp.clip(i * Q_ROWS - WIN_ROWS // 2, 0, rows_n - BAND_ROWS)
        ks = pl.ds(pl.multiple_of(sb * GRID_W, GRID_W), kn)
        qs = pl.ds(pl.multiple_of(i * qn, qn), qn)
        case = jnp.where(i == 0, 0, jnp.where(i == n_blk - 1, 2, 1))
        s = lax.dot_general(qr_scr[qs, :], kr_scr[ks, :], nt, preferred_element_type=F32) + bias_ref[0, case]
        sc = lax.dot_general(qs_scr[qs, :], kc, nt, preferred_element_type=F32)
        m = jnp.maximum(jnp.max(s, axis=-1, keepdims=True), jnp.max(sc, axis=-1, keepdims=True))
        p = jnp.exp(s - m)
        pc = jnp.exp(sc - m)
        den = jnp.sum(p, axis=-1, keepdims=True) + jnp.sum(pc, axis=-1, keepdims=True)
        o = (jnp.dot(p.astype(BF16), v_ref[0, ks, :], preferred_element_type=F32)
             + jnp.dot(pc.astype(BF16), vc, preferred_element_type=F32))
        o_ref[0, qs, :] = (o / den).astype(o_ref.dtype)
        return 0

    lax.fori_loop(0, n_blk, body, 0, unroll=2)


def neighborhood_attention(p_lat, p_ctx, rpb, n_heads):
    b, l, _ = p_lat.shape
    lc = p_ctx.shape[1]
    rows_n = l // GRID_W
    assert rows_n % Q_ROWS == 0 and rows_n >= BAND_ROWS + Q_ROWS
    n_blk = rows_n // Q_ROWS
    bias = _attn_bias(rpb, rows_n)
    cos, s_lo, s_hi = _rope_tables(l)
    qn, kn = Q_ROWS * GRID_W, BAND_ROWS * GRID_W
    h = n_heads
    tab = pl.BlockSpec((l, HEAD_DIM), lambda bi, hi: (0, 0))
    return pl.pallas_call(
        functools.partial(_attn_kernel, n_blk=n_blk, rows_n=rows_n),
        grid=(b, h),
        in_specs=[pl.BlockSpec((1, l, HEAD_DIM), lambda bi, hi: (bi, 0, hi)),
                  pl.BlockSpec((1, l, HEAD_DIM), lambda bi, hi: (bi, 0, hi + h)),
                  pl.BlockSpec((1, l, HEAD_DIM), lambda bi, hi: (bi, 0, hi + 2 * h)),
                  pl.BlockSpec((1, lc, HEAD_DIM), lambda bi, hi: (bi, 0, hi)),
                  pl.BlockSpec((1, lc, HEAD_DIM), lambda bi, hi: (bi, 0, hi + h)),
                  pl.BlockSpec((1, 3, qn, kn), lambda bi, hi: (hi, 0, 0, 0)),
                  tab, tab, tab],
        out_specs=pl.BlockSpec((1, l, HEAD_DIM), lambda bi, hi: (bi, 0, hi)),
        out_shape=jax.ShapeDtypeStruct((b, l, h * HEAD_DIM), BF16),
        scratch_shapes=[pltpu.VMEM((l, HEAD_DIM), BF16)] * 3,
        compiler_params=_cparams(),
        name="nbr_attn",
    )(p_lat, p_lat, p_lat, p_ctx, p_ctx, bias, cos, s_lo, s_hi)


SSM_CHUNK = 16
SSM_ROWS = 8


def _ssm_mats(a_re, a_im, log_dt, b_re, b_im, c_re, c_im, d_skip):
    t = SSM_CHUNK
    hi = lax.Precision.HIGHEST
    a = lax.complex(a_re.astype(F32), a_im.astype(F32))
    dta = jnp.exp(log_dt.astype(F32))[..., None] * a
    a_bar = jnp.exp(dta)
    b_bar = ((a_bar - 1.0) / a)[..., None] * lax.complex(b_re.astype(F32), b_im.astype(F32))
    cm = lax.complex(c_re.astype(F32), c_im.astype(F32))
    k = jnp.arange(t + 1, dtype=F32)
    ap = jnp.exp(dta[..., None] * k)
    g, p, n = a.shape[1], a.shape[2], b_re.shape[-1]
    kern = jnp.einsum('dgnp,dgpl,dgpm->dglnm', cm, ap[..., :t], b_bar, precision=hi).real
    s_i = np.arange(t)[:, None]
    t_i = np.arange(t)[None, :]
    kf = kern[0][:, np.clip(t_i - s_i, 0, t - 1)]
    kb = kern[1][:, np.clip(s_i - t_i, 0, t - 1)]
    eye = jnp.eye(n, dtype=F32)
    m_mat = (jnp.where((s_i <= t_i)[None, :, :, None, None], kf, 0.0)
             + jnp.where((s_i >= t_i)[None, :, :, None, None], kb, 0.0)
             + (s_i == t_i)[None, :, :, None, None] * (d_skip.astype(F32)[:, None, None, :, None] * eye))
    m_mat = jnp.transpose(m_mat, (0, 1, 4, 2, 3)).reshape(g, t * n, t * n)
    wf = ap[0][:, :, t - 1::-1][..., :t, None] * b_bar[0][:, :, None, :]
    wb = ap[1][:, :, :t, None] * b_bar[1][:, :, None, :]
    to_rows = lambda z: jnp.transpose(z, (0, 2, 3, 1)).reshape(g, t * n, p)
    w_mat = jnp.concatenate([to_rows(wf.real), to_rows(wb.real), to_rows(wf.imag), to_rows(wb.imag)], axis=-1)
    zf = jnp.transpose(cm[0], (0, 2, 1))[:, :, None, :] * ap[0][:, :, 1:t + 1, None]
    zb = jnp.transpose(cm[1], (0, 2, 1))[:, :, None, :] * ap[1][:, :, t:0:-1, None]
    flat = lambda z: z.reshape(g, p, t * n)
    r_mat = jnp.concatenate([flat(zf.real), flat(zb.real), -flat(zf.imag), -flat(zb.imag)], axis=1)
    a_t = ap[..., t]
    a_vec = jnp.stack([jnp.concatenate([a_t[0].real, a_t[1].real], -1),
                       jnp.concatenate([a_t[0].imag, a_t[1].imag], -1)], axis=1)
    a_vec = jnp.pad(a_vec, ((0, 0), (0, SSM_ROWS - 2), (0, 0)))
    return m_mat.astype(BF16), w_mat.astype(BF16), r_mat.astype(BF16), a_vec


def _s5_kernel(u_ref, m_ref, w_ref, r_ref, a_ref, y_ref, v_scr, ent_scr, *, n_ctx, n_chunks):
    u = u_ref[0]
    p2 = a_ref.shape[-1]
    p = p2 // 2
    v_scr[...] = jnp.dot(u, w_ref[0], preferred_element_type=F32)
    a_re = jnp.broadcast_to(a_ref[0, 0:1, :], (SSM_ROWS, p2))
    a_im = jnp.broadcast_to(a_ref[0, 1:2, :], (SSM_ROWS, p2))
    is_fwd = lax.broadcasted_iota(I32, (SSM_ROWS, p2), 1) < p

    def step(j, carry):
        s_re, s_im = carry
        cb = jnp.where(j < n_ctx, n_ctx - 1 - j, n_chunks + n_ctx - 1 - j)
        rf = pl.ds(pl.multiple_of(j * SSM_ROWS, SSM_ROWS), SSM_ROWS)
        rb = pl.ds(pl.multiple_of(cb * SSM_ROWS, SSM_ROWS), SSM_ROWS)
        ent_scr[rf, 0:p] = s_re[:, :p]
        ent_scr[rb, p:p2] = s_re[:, p:]
        ent_scr[rf, p2:p2 + p] = s_im[:, :p]
        ent_scr[rb, p2 + p:] = s_im[:, p:]
        in_re = jnp.where(is_fwd, v_scr[rf, :p2], v_scr[rb, :p2])
        in_im = jnp.where(is_fwd, v_scr[rf, p2:], v_scr[rb, p2:])
        return a_re * s_re - a_im * s_im + in_re, a_re * s_im + a_im * s_re + in_im

    zero = jnp.zeros((SSM_ROWS, p2), F32)
    lax.fori_loop(0, n_chunks, step, (zero, zero))
    y_ref[0] = (jnp.dot(u, m_ref[0], preferred_element_type=F32)
                + jnp.dot(ent_scr[...].astype(BF16), r_ref[0], preferred_element_type=F32)).astype(y_ref.dtype)


def s5_mixer(u_lat, u_ctx, mats):
    m_mat, w_mat, r_mat, a_vec = mats
    b, l, dch = u_lat.shape
    lc = u_ctx.shape[1]
    g = m_mat.shape[0]
    n = dch // g
    t = SSM_CHUNK
    tn = t * n
    n_ctx, n_chunks = lc // t, (lc + l) // t
    assert b <= SSM_ROWS and lc % t == 0 and l % t == 0
    z = jnp.concatenate([u_ctx, u_lat], axis=1).reshape(b, n_chunks, t, g, n)
    z = jnp.pad(jnp.transpose(z, (3, 1, 0, 2, 4)), ((0, 0), (0, 0), (0, SSM_ROWS - b), (0, 0), (0, 0)))
    rows = n_chunks * SSM_ROWS
    z = z.reshape(g, rows, tn)
    mat = lambda k: pl.BlockSpec((1, k, tn), lambda gi: (gi, 0, 0))
    y = pl.pallas_call(
        functools.partial(_s5_kernel, n_ctx=n_ctx, n_chunks=n_chunks),
        grid=(g,),
        in_specs=[mat(rows), mat(tn), mat(tn), mat(r_mat.shape[1]),
                  pl.BlockSpec((1, SSM_ROWS, a_vec.shape[-1]), lambda gi: (gi, 0, 0))],
        out_specs=mat(rows),
        out_shape=jax.ShapeDtypeStruct((g, rows, tn), BF16),
        scratch_shapes=[pltpu.VMEM((rows, w_mat.shape[-1]), F32), pltpu.VMEM((rows, r_mat.shape[1]), F32)],
        compiler_params=_cparams(),
        name="s5_mixer",
    )(z, m_mat, w_mat, r_mat, a_vec)
    y = y.reshape(g, n_chunks, SSM_ROWS, t, n)[:, n_ctx:, :b]
    return jnp.transpose(y, (2, 1, 3, 0, 4)).reshape(b, l, dch)


def _glu_kernel(y_ref, w_ref, b_ref, o_ref):
    z = jax.nn.gelu(y_ref[...].astype(F32))
    gate = jax.nn.sigmoid(jnp.dot(z.astype(BF16), w_ref[...], preferred_element_type=F32) + b_ref[...])
    o_ref[...] = (z * gate).astype(o_ref.dtype)


def s5_glu(y, w_bf16, b_glu, tm=512):
    n, d = y.shape
    return pl.pallas_call(
        _glu_kernel,
        grid=(n // tm,),
        in_specs=[pl.BlockSpec((tm, d), lambda i: (i, 0)),
                  pl.BlockSpec((d, d), lambda i: (0, 0)),
                  pl.BlockSpec((1, d), lambda i: (0, 0))],
        out_specs=pl.BlockSpec((tm, d), lambda i: (i, 0)),
        out_shape=jax.ShapeDtypeStruct((n, d), BF16),
        compiler_params=_cparams(),
        name="s5_glu",
    )(y, w_bf16, b_glu.reshape(1, d))


def _merge_kernel(a_ref, s_ref, ga_ref, gs_ref, w_ref, o_ref, h_scr):
    da = a_ref.shape[-1]

    @pl.when(pl.program_id(1) == 0)
    def _():
        h_scr[:, :da] = _rms(a_ref[...].astype(F32), ga_ref[...]).astype(BF16)
        h_scr[:, da:] = _rms(s_ref[...].astype(F32), gs_ref[...]).astype(BF16)

    o_ref[...] = jnp.dot(h_scr[...], w_ref[...], preferred_element_type=F32)


def merge_proj(attn, ssm, g_attn, g_ssm, w_bf16, tm=512, tn=1024):
    n, da = attn.shape
    ds = ssm.shape[1]
    d, dout = w_bf16.shape
    return pl.pallas_call(
        _merge_kernel,
        grid=(n // tm, dout // tn),
        in_specs=[pl.BlockSpec((tm, da), lambda i, j: (i, 0)),
                  pl.BlockSpec((tm, ds), lambda i, j: (i, 0)),
                  pl.BlockSpec((1, da), lambda i, j: (0, 0)),
                  pl.BlockSpec((1, ds), lambda i, j: (0, 0)),
                  pl.BlockSpec((d, tn), lambda i, j: (0, j))],
        out_specs=pl.BlockSpec((tm, tn), lambda i, j: (i, j)),
        out_shape=jax.ShapeDtypeStruct((n, dout), F32),
        scratch_shapes=[pltpu.VMEM((tm, d), BF16)],
        compiler_params=_cparams(),
        name="merge_proj",
    )(attn, ssm, g_attn.reshape(1, da), g_ssm.reshape(1, ds), w_bf16)


def _post_mix_kernel(x_ref, mix_ref, gpost_ref, gt_ref, gpre_ref, sc_ref, sh_ref, wr_ref, br_ref,
                     x1_ref, hp_ref, idx_ref, gate_ref, mask_ref):
    x1 = x_ref[0] + gt_ref[0] * _rms(mix_ref[0], gpost_ref[...])
    x1_ref[0] = x1
    h = _rms(x1, gpre_ref[...]) * (1.0 + sc_ref[0]) + sh_ref[0]
    half = h.shape[-1] // 2
    hp_ref[0] = _pack_halves(h[:, :half], h[:, half:])
    scores = jax.nn.sigmoid(jnp.dot(h.astype(BF16), wr_ref[...], preferred_element_type=F32))
    n_e = scores.shape[-1]
    lane = lax.broadcasted_iota(I32, scores.shape, 1)
    biased = scores + br_ref[...]
    idx_out = jnp.zeros(scores.shape, I32)
    sel_out = jnp.zeros(scores.shape, F32)
    mask = jnp.zeros(scores.shape, jnp.bool_)
    for k in range(TOP_K):
        m = jnp.max(biased, axis=-1, keepdims=True)
        ik = jnp.min(jnp.where(biased == m, lane, n_e), axis=-1, keepdims=True)
        hit = lane == ik
        sel_k = jnp.sum(jnp.where(hit, scores, 0.0), axis=-1, keepdims=True)
        idx_out = jnp.where(lane == k, ik, idx_out)
        sel_out = jnp.where(lane == k, sel_k, sel_out)
        mask = jnp.logical_or(mask, hit)
        biased = jnp.where(hit, -jnp.inf, biased)
    idx_ref[0] = idx_out
    gate_ref[0] = sel_out / jnp.sum(sel_out, axis=-1, keepdims=True) * ROUTED_SCALE
    mask_ref[0] = mask.astype(BF16)


def post_mix(x, mix, g_post, gt, g_pre, scale, shift, wr_bf16, b_router, tm=256):
    b, l, d = x.shape
    n_e = wr_bf16.shape[1]
    row = pl.BlockSpec((1, tm, d), lambda bi, i: (bi, i, 0))
    prow = pl.BlockSpec((1, tm, d // 2), lambda bi, i: (bi, i, 0))
    vec = pl.BlockSpec((1, d), lambda bi, i: (0, 0))
    bvec = pl.BlockSpec((1, 1, d), lambda bi, i: (bi, 0, 0))
    small = pl.BlockSpec((1, tm, n_e), lambda bi, i: (bi, i, 0))
    return pl.pallas_call(
        _post_mix_kernel,
        grid=(b, l // tm),
        in_specs=[row, row, vec, bvec, vec, bvec, bvec,
                  pl.BlockSpec((d, n_e), lambda bi, i: (0, 0)),
                  pl.BlockSpec((1, n_e), lambda bi, i: (0, 0))],
        out_specs=[row, prow, small, small, small],
        out_shape=[jax.ShapeDtypeStruct((b, l, d), F32),
                   jax.ShapeDtypeStruct((b, l, d // 2), I32),
                   jax.ShapeDtypeStruct((b, l, n_e), I32),
                   jax.ShapeDtypeStruct((b, l, n_e), F32),
                   jax.ShapeDtypeStruct((b, l, n_e), BF16)],
        compiler_params=_cparams(),
        name="post_mix",
    )(x, mix, g_post.reshape(1, d), gt, g_pre.reshape(1, d), scale, shift, wr_bf16, b_router.reshape(1, n_e))


def _rank_kernel(mask_ref, idx_ref, rank_ref, cnt_ref, carry):
    @pl.when(pl.program_id(0) == 0)
    def _():
        carry[...] = jnp.zeros_like(carry)

    m = mask_ref[...]
    tm = m.shape[0]
    earlier = (lax.broadcasted_iota(I32, (tm, tm), 1) < lax.broadcasted_iota(I32, (tm, tm), 0)).astype(BF16)
    excl = jnp.dot(earlier, m, preferred_element_type=F32) + carry[0:1, :]
    lane = lax.broadcasted_iota(I32, m.shape, 1)
    idx = idx_ref[...]
    out = jnp.zeros(m.shape, F32)
    for k in range(TOP_K):
        rk = jnp.sum(jnp.where(lane == idx[:, k:k + 1], excl, 0.0), axis=-1, keepdims=True)
        out = jnp.where(lane == k, rk, out)
    rank_ref[...] = out.astype(I32)
    carry[0:1, :] = carry[0:1, :] + jnp.sum(m.astype(F32), axis=0, keepdims=True)
    cnt_ref[...] = carry[...]


def route_ranks(mask, idx, tm=512):
    n, n_e = mask.shape
    blk = pl.BlockSpec((tm, n_e), lambda i: (i, 0))
    rank, cnt = pl.pallas_call(
        _rank_kernel,
        grid=(n // tm,),
        in_specs=[blk, blk],
        out_specs=[blk, pl.BlockSpec((8, n_e), lambda i: (0, 0))],
        out_shape=[jax.ShapeDtypeStruct((n, n_e), I32), jax.ShapeDtypeStruct((8, n_e), F32)],
        scratch_shapes=[pltpu.VMEM((8, n_e), F32)],
        compiler_params=_cparams(),
        name="route_ranks",
    )(mask, idx)
    return rank, cnt[0].astype(I32)


def _dest_kernel(rank_ref, idx_ref, ps_ref, dest_ref):
    idx = idx_ref[...]
    lane = lax.broadcasted_iota(I32, idx.shape, 1)
    start = jnp.zeros(idx.shape, F32)
    for k in range(TOP_K):
        sk = jnp.sum(jnp.where(lane == idx[:, k:k + 1], ps_ref[...], 0.0), axis=-1, keepdims=True)
        start = jnp.where(lane == k, sk, start)
    dest_ref[...] = rank_ref[...] + start.astype(I32)


def route_dest(rank, idx, pad_start, tm=512):
    n, n_e = rank.shape
    blk = pl.BlockSpec((tm, n_e), lambda i: (i, 0))
    dest = pl.pallas_call(
        _dest_kernel,
        grid=(n // tm,),
        in_specs=[blk, blk, pl.BlockSpec((1, n_e), lambda i: (0, 0))],
        out_specs=blk,
        out_shape=jax.ShapeDtypeStruct((n, n_e), I32),
        compiler_params=_cparams(),
        name="route_dest",
    )(rank, idx, pad_start.astype(F32).reshape(1, n_e))
    return dest[:, :TOP_K].reshape(n * TOP_K)


def _dispatch_kernel(pe_ref, pd_ref, dest_ref, h_ref, xs_ref, z_scr, sem, *, n_e, tm, blk):
    def zero_copy(e):
        start = pl.multiple_of(pe_ref[e] - blk, blk)
        return pltpu.make_async_copy(z_scr, xs_ref.at[pl.ds(start, blk), :], sem)

    @pl.when(pl.program_id(0) == 0)
    def _():
        z_scr[...] = jnp.zeros_like(z_scr)

        def start(e, _):
            @pl.when(pd_ref[e] > 0)
            def _():
                zero_copy(e).start()
            return 0

        def wait(e, _):
            @pl.when(pd_ref[e] > 0)
            def _():
                zero_copy(e).wait()
            return 0

        lax.fori_loop(0, n_e, start, 0)
        lax.fori_loop(0, n_e, wait, 0)

    def row_copy(t, k):
        return pltpu.make_async_copy(h_ref.at[pl.ds(t, 1), :], xs_ref.at[pl.ds(dest_ref[t * TOP_K + k], 1), :], sem)

    def issue(t, _):
        for k in range(TOP_K):
            row_copy(t, k).start(priority=k % 2)
        return 0

    lax.fori_loop(0, tm, issue, 0)
    rows = tm * TOP_K
    pltpu.make_async_copy(xs_ref.at[pl.ds(0, rows), :], xs_ref.at[pl.ds(0, rows), :], sem).wait()


def moe_dispatch(hp, dest_flat, pad_end, padded, slots, tm=DISPATCH_TM):
    n, dw = hp.shape
    n_e = pad_end.shape[0]
    grid_spec = pltpu.PrefetchScalarGridSpec(
        num_scalar_prefetch=2,
        grid=(n // tm,),
        in_specs=[pl.BlockSpec((tm * TOP_K,), lambda i, pe, pd: (i,), memory_space=pltpu.SMEM),
                  pl.BlockSpec((tm, dw), lambda i, pe, pd: (i, 0))],
        out_specs=pl.BlockSpec(memory_space=pl.ANY),
        scratch_shapes=[pltpu.VMEM((EXPERT_BLK, dw), I32), pltpu.SemaphoreType.DMA(())],
    )
    return pl.pallas_call(
        functools.partial(_dispatch_kernel, n_e=n_e, tm=tm, blk=EXPERT_BLK),
        grid_spec=grid_spec,
        out_shape=jax.ShapeDtypeStruct((slots, dw), I32),
        compiler_params=_cparams(),
        name="moe_dispatch",
    )(pad_end, padded, dest_flat, hp)


PACK_CHUNK = 512


def _swiglu_hidden(xp_ref, wgu):
    half = xp_ref.shape[-1]
    acc = None
    for c in range(half // PACK_CHUNK):
        cl = slice(c * PACK_CHUNK, (c + 1) * PACK_CHUNK)
        ch = slice(half + c * PACK_CHUNK, half + (c + 1) * PACK_CHUNK)
        lo, hi = _unpack_halves(xp_ref[:, cl])
        part = (jnp.dot(lo.astype(BF16), wgu(cl), preferred_element_type=F32)
                + jnp.dot(hi.astype(BF16), wgu(ch), preferred_element_type=F32))
        acc = part if acc is None else acc + part
    de = acc.shape[-1] // 2
    return (jax.nn.silu(acc[:, :de]) * acc[:, de:]).astype(BF16)


def _down_packed(hid, wd, o_ref):
    half = o_ref.shape[-1]
    for c in range(half // PACK_CHUNK):
        cl = slice(c * PACK_CHUNK, (c + 1) * PACK_CHUNK)
        ch = slice(half + c * PACK_CHUNK, half + (c + 1) * PACK_CHUNK)
        o_ref[:, cl] = _pack_halves(jnp.dot(hid, wd(cl), preferred_element_type=F32),
                                    jnp.dot(hid, wd(ch), preferred_element_type=F32))


def _expert_changed(be_ref, i):
    return jnp.logical_or(i == 0, be_ref[i] != be_ref[jnp.maximum(i - 1, 0)])


def _expert_up_kernel(be_ref, nu_ref, x_ref, wg_ref, wu_ref, hid_ref, wgu_scr):
    i = pl.program_id(0)
    d, de = wg_ref.shape[1], wg_ref.shape[2]
    used = i < nu_ref[0]

    @pl.when(jnp.logical_and(used, _expert_changed(be_ref, i)))
    def _():
        for r in range(0, d, PACK_CHUNK):
            wgu_scr[r:r + PACK_CHUNK, :de] = wg_ref[0, r:r + PACK_CHUNK, :].astype(BF16)
            wgu_scr[r:r + PACK_CHUNK, de:] = wu_ref[0, r:r + PACK_CHUNK, :].astype(BF16)

    @pl.when(used)
    def _():
        hid_ref[...] = _swiglu_hidden(x_ref, lambda sl: wgu_scr[sl, :])

    @pl.when(jnp.logical_not(used))
    def _():
        hid_ref[...] = jnp.zeros_like(hid_ref)


def _expert_down_kernel(be_ref, nu_ref, hid_ref, wd_ref, o_ref, wd_scr):
    i = pl.program_id(0)
    d = wd_ref.shape[2]
    used = i < nu_ref[0]

    @pl.when(jnp.logical_and(used, _expert_changed(be_ref, i)))
    def _():
        for c in range(0, d, PACK_CHUNK):
            wd_scr[:, c:c + PACK_CHUNK] = wd_ref[0, :, c:c + PACK_CHUNK].astype(BF16)

    @pl.when(used)
    def _():
        _down_packed(hid_ref[...], lambda sl: wd_scr[:, sl], o_ref)

    @pl.when(jnp.logical_not(used))
    def _():
        o_ref[...] = jnp.zeros_like(o_ref)


def expert_ffn(xs, block_e, n_used, wg, wu, wd, blk):
    slots, dw = xs.shape
    d, de = wg.shape[1], wg.shape[2]
    n_blocks = slots // blk
    cur = lambda i, nu: jnp.minimum(i, nu[0] - 1)
    up_spec = pltpu.PrefetchScalarGridSpec(
        num_scalar_prefetch=2,
        grid=(n_blocks,),
        in_specs=[pl.BlockSpec((blk, dw), lambda i, be, nu: (cur(i, nu), 0)),
                  pl.BlockSpec((1, d, de), lambda i, be, nu: (be[cur(i, nu)], 0, 0)),
                  pl.BlockSpec((1, d, de), lambda i, be, nu: (be[cur(i, nu)], 0, 0))],
        out_specs=pl.BlockSpec((blk, de), lambda i, be, nu: (i, 0)),
        scratch_shapes=[pltpu.VMEM((d, 2 * de), BF16)],
    )
    hid = pl.pallas_call(
        _expert_up_kernel,
        grid_spec=up_spec,
        out_shape=jax.ShapeDtypeStruct((slots, de), BF16),
        compiler_params=_cparams(),
        name="expert_up",
    )(block_e, n_used, xs, wg, wu)
    down_spec = pltpu.PrefetchScalarGridSpec(
        num_scalar_prefetch=2,
        grid=(n_blocks,),
        in_specs=[pl.BlockSpec((blk, de), lambda i, be, nu: (cur(i, nu), 0)),
                  pl.BlockSpec((1, de, d), lambda i, be, nu: (be[cur(i, nu)], 0, 0))],
        out_specs=pl.BlockSpec((blk, dw), lambda i, be, nu: (i, 0)),
        scratch_shapes=[pltpu.VMEM((de, d), BF16)],
    )
    return pl.pallas_call(
        _expert_down_kernel,
        grid_spec=down_spec,
        out_shape=jax.ShapeDtypeStruct((slots, dw), I32),
        compiler_params=_cparams(),
        name="expert_down",
    )(block_e, n_used, hid, wd)


def _shared_kernel(hp_ref, wgu_ref, wd_ref, o_ref):
    hid = _swiglu_hidden(hp_ref, lambda sl: wgu_ref[sl, :])
    _down_packed(hid, lambda sl: wd_ref[:, sl], o_ref)


def shared_ffn(hp, wgu, wd, tm=512):
    n, dw = hp.shape
    d, de2 = wgu.shape
    row = pl.BlockSpec((tm, dw), lambda i: (i, 0))
    return pl.pallas_call(
        _shared_kernel,
        grid=(n // tm,),
        in_specs=[row, pl.BlockSpec((d, de2), lambda i: (0, 0)), pl.BlockSpec((de2 // 2, d), lambda i: (0, 0))],
        out_specs=row,
        out_shape=jax.ShapeDtypeStruct((n, dw), I32),
        compiler_params=_cparams(),
        name="shared_ffn",
    )(hp, wgu, wd)


def _final_kernel(dc_ref, dn_ref, x1_ref, sp_ref, gate_ref, ys_ref, gpost_ref, gt_ref, o_ref, buf, sem, *, tm, n_tiles):
    i = pl.program_id(0)
    slot = i % 2

    def issue(dref, s):
        def body(t, _):
            for k in range(TOP_K):
                pltpu.make_async_copy(ys_ref.at[pl.ds(dref[t * TOP_K + k], 1), :],
                                      buf.at[s, pl.ds(k * tm + t, 1), :],
                                      sem.at[s]).start(priority=k % 2)
            return 0
        lax.fori_loop(0, tm, body, 0)

    @pl.when(i == 0)
    def _():
        issue(dc_ref, 0)

    for s in range(2):
        @pl.when(jnp.logical_and(i + 1 < n_tiles, slot == 1 - s))
        def _():
            issue(dn_ref, s)

    pltpu.make_async_copy(ys_ref.at[pl.ds(0, TOP_K * tm), :], buf.at[slot], sem.at[slot]).wait()

    g = gate_ref[...]
    acc_lo, acc_hi = _unpack_halves(sp_ref[...])
    for k in range(TOP_K):
        lo, hi = _unpack_halves(buf[slot, pl.ds(k * tm, tm), :])
        gk = g[:, k:k + 1]
        acc_lo = acc_lo + gk * lo
        acc_hi = acc_hi + gk * hi
    ffn = jnp.concatenate([acc_lo, acc_hi], axis=-1)
    o_ref[...] = x1_ref[...] + gt_ref[0] * _rms(ffn, gpost_ref[...])


def final_mix(x1, sp, gates, ys, dest_flat, g_post, gt, tiles_per_batch, tm=COMBINE_TM):
    n, d = x1.shape
    dw = sp.shape[1]
    n_e = gates.shape[1]
    n_tiles = n // tm
    return pl.pallas_call(
        functools.partial(_final_kernel, tm=tm, n_tiles=n_tiles),
        grid=(n_tiles,),
        in_specs=[pl.BlockSpec((tm * TOP_K,), lambda i: (i,), memory_space=pltpu.SMEM),
                  pl.BlockSpec((tm * TOP_K,), lambda i: (jnp.minimum(i + 1, n_tiles - 1),), memory_space=pltpu.SMEM),
                  pl.BlockSpec((tm, d), lambda i: (i, 0)),
                  pl.BlockSpec((tm, dw), lambda i: (i, 0)),
                  pl.BlockSpec((tm, n_e), lambda i: (i, 0)),
                  pl.BlockSpec(memory_space=pl.ANY),
                  pl.BlockSpec((1, d), lambda i: (0, 0)),
                  pl.BlockSpec((1, 1, d), lambda i: (i // tiles_per_batch, 0, 0))],
        out_specs=pl.BlockSpec((tm, d), lambda i: (i, 0)),
        out_shape=jax.ShapeDtypeStruct((n, d), F32),
        scratch_shapes=[pltpu.VMEM((2, TOP_K * tm, dw), I32), pltpu.SemaphoreType.DMA((2,))],
        compiler_params=_cparams(),
        name="final_mix",
    )(dest_flat, dest_flat, x1, sp, gates, ys, g_post.reshape(1, d), gt)


def kernel(x, c, ctx, c_ctx, w_ada, b_ada, g_pre_mix, g_post_mix, g_pre_ffn, g_post_ffn, w_in, rpb, ssm_a_re, ssm_a_im, ssm_log_dt, ssm_b_re, ssm_b_im, ssm_c_re, ssm_c_im, ssm_d, w_glu, b_glu, g_attn_out, g_ssm_out, w_out, w_router, b_router, w_exp_gate, w_exp_up, w_exp_down, w_sh_gate, w_sh_up, w_sh_down):
    b, l, d = x.shape
    lc = ctx.shape[1]
    assert w_ada.shape[0] == 1 and b + 1 <= 8
    n_in = w_in.shape[-1]
    d_ssm = w_glu.shape[-1]
    d_attn = d - d_ssm
    n_heads = d_attn // HEAD_DIM
    n = b * l

    c8 = jnp.concatenate([c, c_ctx[None], jnp.zeros((8 - b - 1, d), F32)], axis=0)
    mod = ada_mod(c8, w_ada[0], b_ada[0]).reshape(8, 6, 1, d)
    sh_m, sc_m, gt_m, sh_f, sc_f, gt_f = [mod[:b, j] for j in range(6)]
    csh_m, csc_m = mod[b:b + 1, 0], mod[b:b + 1, 1]

    w_in_b = w_in[0].astype(BF16)
    tn = PROJ_TN
    p_lat = mod_proj(x, g_pre_mix[0], sc_m, sh_m, w_in_b, 0, n_in, 512, tn)
    p_ctx = mod_proj(ctx.reshape(1, b * lc, d), g_pre_mix[0], csc_m, csh_m, w_in_b,
                     d_attn // tn, n_in - d_attn, 512, tn).reshape(b, lc, n_in - d_attn)

    attn = neighborhood_attention(p_lat, p_ctx, rpb[0], n_heads)

    mats = _ssm_mats(ssm_a_re[0], ssm_a_im[0], ssm_log_dt[0], ssm_b_re[0], ssm_b_im[0],
                     ssm_c_re[0], ssm_c_im[0], ssm_d[0])
    y = s5_mixer(p_lat[..., 3 * d_attn:], p_ctx[..., 2 * d_attn:], mats)
    ssm = s5_glu(y.reshape(n, d_ssm), w_glu[0].astype(BF16), b_glu[0])

    mix = merge_proj(attn.reshape(n, d_attn), ssm, g_attn_out[0], g_ssm_out[0], w_out[0].astype(BF16))
    x1, hp, idx, gates, mask = post_mix(x, mix.reshape(b, l, d), g_post_mix[0], gt_m, g_pre_ffn[0], sc_f, sh_f,
                                        w_router[0].astype(BF16), b_router[0])

    n_e = w_router.shape[-1]
    m = n * TOP_K
    idx = idx.reshape(n, n_e)
    rank, counts = route_ranks(mask.reshape(n, n_e), idx)
    padded = (counts + EXPERT_BLK - 1) // EXPERT_BLK * EXPERT_BLK
    pad_end = jnp.cumsum(padded).astype(I32)
    dest = route_dest(rank, idx, pad_end - padded)
    n_blocks = m // EXPERT_BLK + n_e
    slots = n_blocks * EXPERT_BLK
    block_e = jnp.minimum(jnp.searchsorted(pad_end, jnp.arange(n_blocks) * EXPERT_BLK, side='right'),
                          n_e - 1).astype(I32)
    n_used = (pad_end[-1] // EXPERT_BLK).astype(I32).reshape(1)

    hp2 = hp.reshape(n, d // 2)
    xs = moe_dispatch(hp2, dest, pad_end, padded.astype(I32), slots)
    ys = expert_ffn(xs, block_e, n_used, w_exp_gate[0], w_exp_up[0], w_exp_down[0], EXPERT_BLK)
    sp = shared_ffn(hp2, jnp.concatenate([w_sh_gate[0], w_sh_up[0]], axis=-1).astype(BF16), w_sh_down[0].astype(BF16))
    out = final_mix(x1.reshape(n, d), sp, gates.reshape(n, n_e), ys, dest, g_post_ffn[0], gt_f, l // COMBINE_TM)
    return out.reshape(b, l, d)
```

```python
import functools

import numpy as np
import jax
import jax.numpy as jnp
from jax import lax
from jax.experimental import pallas as pl
from jax.experimental.pallas import tpu as pltpu

F32 = jnp.float32
BF16 = jnp.bfloat16
I32 = jnp.int32

GRID_W = 64
HEAD_DIM = 128
WIN_ROWS = 8
WIN_COLS = 16
ROPE_THETA = 10000.0
SSM_GROUP_CH = 16
SSM_STATE = 64
TOP_K = 8
ROUTED_SCALE = 2.5
EPS = 1e-6
NEG_INF = -1e30

Q_ROWS = 4
BAND_ROWS = Q_ROWS + WIN_ROWS - 1
VMEM_LIMIT = 56 * 1024 * 1024
EXPERT_BLK = 256
PROJ_TN = 1024
DISPATCH_TM = 256
COMBINE_TM = 128


def _cparams():
    return pltpu.CompilerParams(vmem_limit_bytes=VMEM_LIMIT)


def _rms(x, g):
    return x * lax.rsqrt(jnp.mean(x * x, axis=-1, keepdims=True) + EPS) * g


def _pack_halves(lo, hi):
    lo_bits = lax.bitcast_convert_type(lo.astype(BF16).astype(F32), I32)
    hi_bits = lax.bitcast_convert_type(hi.astype(BF16).astype(F32), I32)
    return (hi_bits & jnp.int32(-65536)) | lax.shift_right_logical(lo_bits, jnp.int32(16))


def _unpack_halves(w):
    lo = lax.bitcast_convert_type(lax.shift_left(w, jnp.int32(16)), F32)
    hi = lax.bitcast_convert_type(w & jnp.int32(-65536), F32)
    return lo, hi


def _ada_kernel(c_ref, w_ref, b_ref, o_ref):
    a = jax.nn.silu(c_ref[...]).astype(BF16)
    o_ref[...] = jnp.dot(a, w_ref[...].astype(BF16), preferred_element_type=F32) + b_ref[...]


def ada_mod(c8, w_ada, b_ada):
    d, n = w_ada.shape
    tn = 512
    return pl.pallas_call(
        _ada_kernel,
        grid=(n // tn,),
        in_specs=[pl.BlockSpec((8, d), lambda j: (0, 0)),
                  pl.BlockSpec((d, tn), lambda j: (0, j)),
                  pl.BlockSpec((1, tn), lambda j: (0, j))],
        out_specs=pl.BlockSpec((8, tn), lambda j: (0, j)),
        out_shape=jax.ShapeDtypeStruct((8, n), F32),
        compiler_params=_cparams(),
        name="ada_mod",
    )(c8, w_ada, b_ada.reshape(1, n))


def _modproj_kernel(x_ref, g_ref, sc_ref, sh_ref, w_ref, o_ref, h_scr):
    @pl.when(pl.program_id(2) == 0)
    def _():
        h = _rms(x_ref[0], g_ref[...]) * (1.0 + sc_ref[0]) + sh_ref[0]
        h_scr[...] = h.astype(BF16)

    o_ref[0] = jnp.dot(h_scr[...], w_ref[...], preferred_element_type=F32).astype(o_ref.dtype)


def mod_proj(x, g, scale, shift, w_bf16, col_blk_off, n_out, tm, tn):
    b, l, d = x.shape
    return pl.pallas_call(
        _modproj_kernel,
        grid=(b, l // tm, n_out // tn),
        in_specs=[pl.BlockSpec((1, tm, d), lambda bi, i, j: (bi, i, 0)),
                  pl.BlockSpec((1, d), lambda bi, i, j: (0, 0)),
                  pl.BlockSpec((1, 1, d), lambda bi, i, j: (bi, 0, 0)),
                  pl.BlockSpec((1, 1, d), lambda bi, i, j: (bi, 0, 0)),
                  pl.BlockSpec((d, tn), lambda bi, i, j: (0, j + col_blk_off))],
        out_specs=pl.BlockSpec((1, tm, tn), lambda bi, i, j: (bi, i, j)),
        out_shape=jax.ShapeDtypeStruct((b, l, n_out), BF16),
        scratch_shapes=[pltpu.VMEM((tm, d), BF16)],
        compiler_params=_cparams(),
        name="mod_proj",
    )(x, g.reshape(1, d), scale, shift, w_bf16)


def _rope_tables(l):
    half = HEAD_DIM // 2
    quarter = half // 2
    inv_freq = 1.0 / (ROPE_THETA ** (jnp.arange(0, half, 2, dtype=F32) / half))
    rows = (jnp.arange(l) // GRID_W).astype(F32)
    cols = (jnp.arange(l) % GRID_W).astype(F32)

    def cs(pos):
        ang = pos[:, None] * inv_freq[None, :]
        return jnp.cos(ang), jnp.sin(ang)

    cr, sr = cs(rows)
    cc, sc = cs(cols)
    zero = jnp.zeros((l, quarter), F32)
    cos = jnp.concatenate([cr, cr, cc, cc], axis=-1)
    s_lo = jnp.concatenate([-sr, zero, -sc, zero], axis=-1)
    s_hi = jnp.concatenate([zero, sr, zero, sc], axis=-1)
    return cos, s_lo, s_hi


def _attn_bias(rpb, rows_n):
    kh = min(WIN_ROWS, rows_n)
    cases = [(0, 0), (Q_ROWS, 0), (rows_n - Q_ROWS, rows_n - BAND_ROWS)]
    col = np.arange(GRID_W)
    cstart = np.clip(col - WIN_COLS // 2, 0, GRID_W - WIN_COLS)
    ok_c = (col[None, :] >= cstart[:, None]) & (col[None, :] < cstart[:, None] + WIN_COLS)
    co = np.clip(col[None, :] - col[:, None], -(WIN_COLS - 1), WIN_COLS - 1) + (WIN_COLS - 1)
    ro_all, ok_all = [], []
    for r0, sb in cases:
        qrow = r0 + np.arange(Q_ROWS)
        krow = sb + np.arange(BAND_ROWS)
        rs = np.clip(qrow - kh // 2, 0, rows_n - kh)
        ok_all.append((krow[None, :] >= rs[:, None]) & (krow[None, :] < rs[:, None] + kh))
        ro_all.append(np.clip(krow[None, :] - qrow[:, None] + (WIN_ROWS - 1), 0, 2 * WIN_ROWS - 2))
    ro_all, ok_all = np.stack(ro_all), np.stack(ok_all)
    onehot = (co[None] == np.arange(2 * WIN_COLS - 1)[:, None, None]).astype(np.float32)
    toe = jnp.einsum('hrc,cqk->hrqk', rpb.astype(F32), onehot, precision=lax.Precision.HIGHEST)
    blocks = toe[:, ro_all]
    ok = ok_all[None, :, :, :, None, None] & ok_c[None, None, None, None]
    bias = jnp.transpose(jnp.where(ok, blocks, NEG_INF), (0, 1, 2, 4, 3, 5))
    return bias.reshape(rpb.shape[0], 3, Q_ROWS * GRID_W, BAND_ROWS * GRID_W)


def _attn_kernel(q_ref, k_ref, v_ref, kc_ref, vc_ref, bias_ref, cos_ref, slo_ref, shi_ref, o_ref,
                 qr_scr, qs_scr, kr_scr, *, n_blk, rows_n):
    quarter = HEAD_DIM // 4
    qn = Q_ROWS * GRID_W
    kn = BAND_ROWS * GRID_W
    scale = HEAD_DIM ** -0.5

    def rope(x, sl):
        return (x * cos_ref[sl, :] + pltpu.roll(x, HEAD_DIM - quarter, 1) * slo_ref[sl, :]
                + pltpu.roll(x, quarter, 1) * shi_ref[sl, :])

    def rope_body(i, _):
        sl = pl.ds(pl.multiple_of(i * qn, qn), qn)
        q = q_ref[0, sl, :].astype(F32) * scale
        qs_scr[sl, :] = q.astype(BF16)
        qr_scr[sl, :] = rope(q, sl).astype(BF16)
        kr_scr[sl, :] = rope(k_ref[0, sl, :].astype(F32), sl).astype(BF16)
        return 0

    lax.fori_loop(0, n_blk, rope_body, 0)

    kc = kc_ref[0]
    vc = vc_ref[0]
    nt = (((1,), (1,)), ((), ()))

    def body(i, _):
        sb = jnp.clip(i * Q_ROWS - WIN_ROWS // 2, 0, rows_n - BAND_ROWS)
        ks = pl.ds(pl.multiple_of(sb * GRID_W, GRID_W), kn)
        qs = pl.ds(pl.multiple_of(i * qn, qn), qn)
        case = jnp.where(i == 0, 0, jnp.where(i == n_blk - 1, 2, 1))
        s = lax.dot_general(qr_scr[qs, :], kr_scr[ks, :], nt, preferred_element_type=F32) + bias_ref[0, case]
        sc = lax.dot_general(qs_scr[qs, :], kc, nt, preferred_element_type=F32)
        m = jnp.maximum(jnp.max(s, axis=-1, keepdims=True), jnp.max(sc, axis=-1, keepdims=True))
        p = jnp.exp(s - m)
        pc = jnp.exp(sc - m)
        den = jnp.sum(p, axis=-1, keepdims=True) + jnp.sum(pc, axis=-1, keepdims=True)
        o = (jnp.dot(p.astype(BF16), v_ref[0, ks, :], preferred_element_type=F32)
             + jnp.dot(pc.astype(BF16), vc, preferred_element_type=F32))
        o_ref[0, qs, :] = (o / den).astype(o_ref.dtype)
        return 0

    lax.fori_loop(0, n_blk, body, 0, unroll=2)


def neighborhood_attention(p_lat, p_ctx, rpb, n_heads):
    b, l, _ = p_lat.shape
    lc = p_ctx.shape[1]
    rows_n = l // GRID_W
    assert rows_n % Q_ROWS == 0 and rows_n >= BAND_ROWS + Q_ROWS
    n_blk = rows_n // Q_ROWS
    bias = _attn_bias(rpb, rows_n)
    cos, s_lo, s_hi = _rope_tables(l)
    qn, kn = Q_ROWS * GRID_W, BAND_ROWS * GRID_W
    h = n_heads
    tab = pl.BlockSpec((l, HEAD_DIM), lambda bi, hi: (0, 0))
    return pl.pallas_call(
        functools.partial(_attn_kernel, n_blk=n_blk, rows_n=rows_n),
        grid=(b, h),
        in_specs=[pl.BlockSpec((1, l, HEAD_DIM), lambda bi, hi: (bi, 0, hi)),
                  pl.BlockSpec((1, l, HEAD_DIM), lambda bi, hi: (bi, 0, hi + h)),
                  pl.BlockSpec((1, l, HEAD_DIM), lambda bi, hi: (bi, 0, hi + 2 * h)),
                  pl.BlockSpec((1, lc, HEAD_DIM), lambda bi, hi: (bi, 0, hi)),
                  pl.BlockSpec((1, lc, HEAD_DIM), lambda bi, hi: (bi, 0, hi + h)),
                  pl.BlockSpec((1, 3, qn, kn), lambda bi, hi: (hi, 0, 0, 0)),
                  tab, tab, tab],
        out_specs=pl.BlockSpec((1, l, HEAD_DIM), lambda bi, hi: (bi, 0, hi)),
        out_shape=jax.ShapeDtypeStruct((b, l, h * HEAD_DIM), BF16),
        scratch_shapes=[pltpu.VMEM((l, HEAD_DIM), BF16)] * 3,
        compiler_params=_cparams(),
        name="nbr_attn",
    )(p_lat, p_lat, p_lat, p_ctx, p_ctx, bias, cos, s_lo, s_hi)


SSM_CHUNK = 16
SSM_ROWS = 8


def _ssm_mats(a_re, a_im, log_dt, b_re, b_im, c_re, c_im, d_skip):
    t = SSM_CHUNK
    hi = lax.Precision.HIGHEST
    a = lax.complex(a_re.astype(F32), a_im.astype(F32))
    dta = jnp.exp(log_dt.astype(F32))[..., None] * a
    a_bar = jnp.exp(dta)
    b_bar = ((a_bar - 1.0) / a)[..., None] * lax.complex(b_re.astype(F32), b_im.astype(F32))
    cm = lax.complex(c_re.astype(F32), c_im.astype(F32))
    k = jnp.arange(t + 1, dtype=F32)
    ap = jnp.exp(dta[..., None] * k)
    g, p, n = a.shape[1], a.shape[2], b_re.shape[-1]
    kern = jnp.einsum('dgnp,dgpl,dgpm->dglnm', cm, ap[..., :t], b_bar, precision=hi).real
    s_i = np.arange(t)[:, None]
    t_i = np.arange(t)[None, :]
    kf = kern[0][:, np.clip(t_i - s_i, 0, t - 1)]
    kb = kern[1][:, np.clip(s_i - t_i, 0, t - 1)]
    eye = jnp.eye(n, dtype=F32)
    m_mat = (jnp.where((s_i <= t_i)[None, :, :, None, None], kf, 0.0)
             + jnp.where((s_i >= t_i)[None, :, :, None, None], kb, 0.0)
             + (s_i == t_i)[None, :, :, None, None] * (d_skip.astype(F32)[:, None, None, :, None] * eye))
    m_mat = jnp.transpose(m_mat, (0, 1, 4, 2, 3)).reshape(g, t * n, t * n)
    wf = ap[0][:, :, t - 1::-1][..., :t, None] * b_bar[0][:, :, None, :]
    wb = ap[1][:, :, :t, None] * b_bar[1][:, :, None, :]
    to_rows = lambda z: jnp.transpose(z, (0, 2, 3, 1)).reshape(g, t * n, p)
    w_mat = jnp.concatenate([to_rows(wf.real), to_rows(wb.real), to_rows(wf.imag), to_rows(wb.imag)], axis=-1)
    zf = jnp.transpose(cm[0], (0, 2, 1))[:, :, None, :] * ap[0][:, :, 1:t + 1, None]
    zb = jnp.transpose(cm[1], (0, 2, 1))[:, :, None, :] * ap[1][:, :, t:0:-1, None]
    flat = lambda z: z.reshape(g, p, t * n)
    r_mat = jnp.concatenate([flat(zf.real), flat(zb.real), -flat(zf.imag), -flat(zb.imag)], axis=1)
    a_t = ap[..., t]
    a_vec = jnp.stack([jnp.concatenate([a_t[0].real, a_t[1].real], -1),
                       jnp.concatenate([a_t[0].imag, a_t[1].imag], -1)], axis=1)
    a_vec = jnp.pad(a_vec, ((0, 0), (0, SSM_ROWS - 2), (0, 0)))
    return m_mat.astype(BF16), w_mat.astype(BF16), r_mat.astype(BF16), a_vec


def _s5_kernel(u_ref, m_ref, w_ref, r_ref, a_ref, y_ref, v_scr, ent_scr, *, n_ctx, n_chunks):
    u = u_ref[0]
    p2 = a_ref.shape[-1]
    p = p2 // 2
    v_scr[...] = jnp.dot(u, w_ref[0], preferred_element_type=F32)
    a_re = jnp.broadcast_to(a_ref[0, 0:1, :], (SSM_ROWS, p2))
    a_im = jnp.broadcast_to(a_ref[0, 1:2, :], (SSM_ROWS, p2))
    is_fwd = lax.broadcasted_iota(I32, (SSM_ROWS, p2), 1) < p

    def step(j, carry):
        s_re, s_im = carry
        cb = jnp.where(j < n_ctx, n_ctx - 1 - j, n_chunks + n_ctx - 1 - j)
        rf = pl.ds(pl.multiple_of(j * SSM_ROWS, SSM_ROWS), SSM_ROWS)
        rb = pl.ds(pl.multiple_of(cb * SSM_ROWS, SSM_ROWS), SSM_ROWS)
        ent_scr[rf, 0:p] = s_re[:, :p]
        ent_scr[rb, p:p2] = s_re[:, p:]
        ent_scr[rf, p2:p2 + p] = s_im[:, :p]
        ent_scr[rb, p2 + p:] = s_im[:, p:]
        in_re = jnp.where(is_fwd, v_scr[rf, :p2], v_scr[rb, :p2])
        in_im = jnp.where(is_fwd, v_scr[rf, p2:], v_scr[rb, p2:])
        return a_re * s_re - a_im * s_im + in_re, a_re * s_im + a_im * s_re + in_im

    zero = jnp.zeros((SSM_ROWS, p2), F32)
    lax.fori_loop(0, n_chunks, step, (zero, zero))
    y_ref[0] = (jnp.dot(u, m_ref[0], preferred_element_type=F32)
                + jnp.dot(ent_scr[...].astype(BF16), r_ref[0], preferred_element_type=F32)).astype(y_ref.dtype)


def s5_mixer(u_lat, u_ctx, mats):
    m_mat, w_mat, r_mat, a_vec = mats
    b, l, dch = u_lat.shape
    lc = u_ctx.shape[1]
    g = m_mat.shape[0]
    n = dch // g
    t = SSM_CHUNK
    tn = t * n
    n_ctx, n_chunks = lc // t, (lc + l) // t
    assert b <= SSM_ROWS and lc % t == 0 and l % t == 0
    z = jnp.concatenate([u_ctx, u_lat], axis=1).reshape(b, n_chunks, t, g, n)
    z = jnp.pad(jnp.transpose(z, (3, 1, 0, 2, 4)), ((0, 0), (0, 0), (0, SSM_ROWS - b), (0, 0), (0, 0)))
    rows = n_chunks * SSM_ROWS
    z = z.reshape(g, rows, tn)
    mat = lambda k: pl.BlockSpec((1, k, tn), lambda gi: (gi, 0, 0))
    y = pl.pallas_call(
        functools.partial(_s5_kernel, n_ctx=n_ctx, n_chunks=n_chunks),
        grid=(g,),
        in_specs=[mat(rows), mat(tn), mat(tn), mat(r_mat.shape[1]),
                  pl.BlockSpec((1, SSM_ROWS, a_vec.shape[-1]), lambda gi: (gi, 0, 0))],
        out_specs=mat(rows),
        out_shape=jax.ShapeDtypeStruct((g, rows, tn), BF16),
        scratch_shapes=[pltpu.VMEM((rows, w_mat.shape[-1]), F32), pltpu.VMEM((rows, r_mat.shape[1]), F32)],
        compiler_params=_cparams(),
        name="s5_mixer",
    )(z, m_mat, w_mat, r_mat, a_vec)
    y = y.reshape(g, n_chunks, SSM_ROWS, t, n)[:, n_ctx:, :b]
    return jnp.transpose(y, (2, 1, 3, 0, 4)).reshape(b, l, dch)


def _glu_kernel(y_ref, w_ref, b_ref, o_ref):
    z = jax.nn.gelu(y_ref[...].astype(F32))
    gate = jax.nn.sigmoid(jnp.dot(z.astype(BF16), w_ref[...], preferred_element_type=F32) + b_ref[...])
    o_ref[...] = (z * gate).astype(o_ref.dtype)


def s5_glu(y, w_bf16, b_glu, tm=512):
    n, d = y.shape
    return pl.pallas_call(
        _glu_kernel,
        grid=(n // tm,),
        in_specs=[pl.BlockSpec((tm, d), lambda i: (i, 0)),
                  pl.BlockSpec((d, d), lambda i: (0, 0)),
                  pl.BlockSpec((1, d), lambda i: (0, 0))],
        out_specs=pl.BlockSpec((tm, d), lambda i: (i, 0)),
        out_shape=jax.ShapeDtypeStruct((n, d), BF16),
        compiler_params=_cparams(),
        name="s5_glu",
    )(y, w_bf16, b_glu.reshape(1, d))


def _merge_kernel(a_ref, s_ref, ga_ref, gs_ref, w_ref, o_ref, h_scr):
    da = a_ref.shape[-1]

    @pl.when(pl.program_id(1) == 0)
    def _():
        h_scr[:, :da] = _rms(a_ref[...].astype(F32), ga_ref[...]).astype(BF16)
        h_scr[:, da:] = _rms(s_ref[...].astype(F32), gs_ref[...]).astype(BF16)

    o_ref[...] = jnp.dot(h_scr[...], w_ref[...], preferred_element_type=F32)


def merge_proj(attn, ssm, g_attn, g_ssm, w_bf16, tm=512, tn=1024):
    n, da = attn.shape
    ds = ssm.shape[1]
    d, dout = w_bf16.shape
    return pl.pallas_call(
        _merge_kernel,
        grid=(n // tm, dout // tn),
        in_specs=[pl.BlockSpec((tm, da), lambda i, j: (i, 0)),
                  pl.BlockSpec((tm, ds), lambda i, j: (i, 0)),
                  pl.BlockSpec((1, da), lambda i, j: (0, 0)),
                  pl.BlockSpec((1, ds), lambda i, j: (0, 0)),
                  pl.BlockSpec((d, tn), lambda i, j: (0, j))],
        out_specs=pl.BlockSpec((tm, tn), lambda i, j: (i, j)),
        out_shape=jax.ShapeDtypeStruct((n, dout), F32),
        scratch_shapes=[pltpu.VMEM((tm, d), BF16)],
        compiler_params=_cparams(),
        name="merge_proj",
    )(attn, ssm, g_attn.reshape(1, da), g_ssm.reshape(1, ds), w_bf16)


def _post_mix_kernel(x_ref, mix_ref, gpost_ref, gt_ref, gpre_ref, sc_ref, sh_ref, wr_ref, br_ref,
                     x1_ref, hp_ref, idx_ref, gate_ref, mask_ref):
    x1 = x_ref[0] + gt_ref[0] * _rms(mix_ref[0], gpost_ref[...])
    x1_ref[0] = x1
    h = _rms(x1, gpre_ref[...]) * (1.0 + sc_ref[0]) + sh_ref[0]
    half = h.shape[-1] // 2
    hp_ref[0] = _pack_halves(h[:, :half], h[:, half:])
    scores = jax.nn.sigmoid(jnp.dot(h.astype(BF16), wr_ref[...], preferred_element_type=F32))
    n_e = scores.shape[-1]
    lane = lax.broadcasted_iota(I32, scores.shape, 1)
    biased = scores + br_ref[...]
    idx_out = jnp.zeros(scores.shape, I32)
    sel_out = jnp.zeros(scores.shape, F32)
    mask = jnp.zeros(scores.shape, jnp.bool_)
    for k in range(TOP_K):
        m = jnp.max(biased, axis=-1, keepdims=True)
        ik = jnp.min(jnp.where(biased == m, lane, n_e), axis=-1, keepdims=True)
        hit = lane == ik
        sel_k = jnp.sum(jnp.where(hit, scores, 0.0), axis=-1, keepdims=True)
        idx_out = jnp.where(lane == k, ik, idx_out)
        sel_out = jnp.where(lane == k, sel_k, sel_out)
        mask = jnp.logical_or(mask, hit)
        biased = jnp.where(hit, -jnp.inf, biased)
    idx_ref[0] = idx_out
    gate_ref[0] = sel_out / jnp.sum(sel_out, axis=-1, keepdims=True) * ROUTED_SCALE
    mask_ref[0] = mask.astype(BF16)


def post_mix(x, mix, g_post, gt, g_pre, scale, shift, wr_bf16, b_router, tm=256):
    b, l, d = x.shape
    n_e = wr_bf16.shape[1]
    row = pl.BlockSpec((1, tm, d), lambda bi, i: (bi, i, 0))
    prow = pl.BlockSpec((1, tm, d // 2), lambda bi, i: (bi, i, 0))
    vec = pl.BlockSpec((1, d), lambda bi, i: (0, 0))
    bvec = pl.BlockSpec((1, 1, d), lambda bi, i: (bi, 0, 0))
    small = pl.BlockSpec((1, tm, n_e), lambda bi, i: (bi, i, 0))
    return pl.pallas_call(
        _post_mix_kernel,
        grid=(b, l // tm),
        in_specs=[row, row, vec, bvec, vec, bvec, bvec,
                  pl.BlockSpec((d, n_e), lambda bi, i: (0, 0)),
                  pl.BlockSpec((1, n_e), lambda bi, i: (0, 0))],
        out_specs=[row, prow, small, small, small],
        out_shape=[jax.ShapeDtypeStruct((b, l, d), F32),
                   jax.ShapeDtypeStruct((b, l, d // 2), I32),
                   jax.ShapeDtypeStruct((b, l, n_e), I32),
                   jax.ShapeDtypeStruct((b, l, n_e), F32),
                   jax.ShapeDtypeStruct((b, l, n_e), BF16)],
        compiler_params=_cparams(),
        name="post_mix",
    )(x, mix, g_post.reshape(1, d), gt, g_pre.reshape(1, d), scale, shift, wr_bf16, b_router.reshape(1, n_e))


def _rank_kernel(mask_ref, idx_ref, rank_ref, cnt_ref, carry):
    @pl.when(pl.program_id(0) == 0)
    def _():
        carry[...] = jnp.zeros_like(carry)

    m = mask_ref[...]
    tm = m.shape[0]
    earlier = (lax.broadcasted_iota(I32, (tm, tm), 1) < lax.broadcasted_iota(I32, (tm, tm), 0)).astype(BF16)
    excl = jnp.dot(earlier, m, preferred_element_type=F32) + carry[0:1, :]
    lane = lax.broadcasted_iota(I32, m.shape, 1)
    idx = idx_ref[...]
    out = jnp.zeros(m.shape, F32)
    for k in range(TOP_K):
        rk = jnp.sum(jnp.where(lane == idx[:, k:k + 1], excl, 0.0), axis=-1, keepdims=True)
        out = jnp.where(lane == k, rk, out)
    rank_ref[...] = out.astype(I32)
    carry[0:1, :] = carry[0:1, :] + jnp.sum(m.astype(F32), axis=0, keepdims=True)
    cnt_ref[...] = carry[...]


def route_ranks(mask, idx, tm=512):
    n, n_e = mask.shape
    blk = pl.BlockSpec((tm, n_e), lambda i: (i, 0))
    rank, cnt = pl.pallas_call(
        _rank_kernel,
        grid=(n // tm,),
        in_specs=[blk, blk],
        out_specs=[blk, pl.BlockSpec((8, n_e), lambda i: (0, 0))],
        out_shape=[jax.ShapeDtypeStruct((n, n_e), I32), jax.ShapeDtypeStruct((8, n_e), F32)],
        scratch_shapes=[pltpu.VMEM((8, n_e), F32)],
        compiler_params=_cparams(),
        name="route_ranks",
    )(mask, idx)
    return rank, cnt[0].astype(I32)


def _dest_kernel(rank_ref, idx_ref, ps_ref, dest_ref):
    idx = idx_ref[...]
    lane = lax.broadcasted_iota(I32, idx.shape, 1)
    start = jnp.zeros(idx.shape, F32)
    for k in range(TOP_K):
        sk = jnp.sum(jnp.where(lane == idx[:, k:k + 1], ps_ref[...], 0.0), axis=-1, keepdims=True)
        start = jnp.where(lane == k, sk, start)
    dest_ref[...] = rank_ref[...] + start.astype(I32)


def route_dest(rank, idx, pad_start, tm=512):
    n, n_e = rank.shape
    blk = pl.BlockSpec((tm, n_e), lambda i: (i, 0))
    dest = pl.pallas_call(
        _dest_kernel,
        grid=(n // tm,),
        in_specs=[blk, blk, pl.BlockSpec((1, n_e), lambda i: (0, 0))],
        out_specs=blk,
        out_shape=jax.ShapeDtypeStruct((n, n_e), I32),
        compiler_params=_cparams(),
        name="route_dest",
    )(rank, idx, pad_start.astype(F32).reshape(1, n_e))
    return dest[:, :TOP_K].reshape(n * TOP_K)


def _dispatch_kernel(pe_ref, pd_ref, dest_ref, h_ref, xs_ref, z_scr, sem, *, n_e, tm, blk):
    def zero_copy(e):
        start = pl.multiple_of(pe_ref[e] - blk, blk)
        return pltpu.make_async_copy(z_scr, xs_ref.at[pl.ds(start, blk), :], sem)

    @pl.when(pl.program_id(0) == 0)
    def _():
        z_scr[...] = jnp.zeros_like(z_scr)

        def start(e, _):
            @pl.when(pd_ref[e] > 0)
            def _():
                zero_copy(e).start()
            return 0

        def wait(e, _):
            @pl.when(pd_ref[e] > 0)
            def _():
                zero_copy(e).wait()
            return 0

        lax.fori_loop(0, n_e, start, 0)
        lax.fori_loop(0, n_e, wait, 0)

    def row_copy(t, k):
        return pltpu.make_async_copy(h_ref.at[pl.ds(t, 1), :], xs_ref.at[pl.ds(dest_ref[t * TOP_K + k], 1), :], sem)

    def issue(t, _):
        for k in range(TOP_K):
            row_copy(t, k).start(priority=k % 2)
        return 0

    lax.fori_loop(0, tm, issue, 0)
    rows = tm * TOP_K
    pltpu.make_async_copy(xs_ref.at[pl.ds(0, rows), :], xs_ref.at[pl.ds(0, rows), :], sem).wait()


def moe_dispatch(hp, dest_flat, pad_end, padded, slots, tm=DISPATCH_TM):
    n, dw = hp.shape
    n_e = pad_end.shape[0]
    grid_spec = pltpu.PrefetchScalarGridSpec(
        num_scalar_prefetch=2,
        grid=(n // tm,),
        in_specs=[pl.BlockSpec((tm * TOP_K,), lambda i, pe, pd: (i,), memory_space=pltpu.SMEM),
                  pl.BlockSpec((tm, dw), lambda i, pe, pd: (i, 0))],
        out_specs=pl.BlockSpec(memory_space=pl.ANY),
        scratch_shapes=[pltpu.VMEM((EXPERT_BLK, dw), I32), pltpu.SemaphoreType.DMA(())],
    )
    return pl.pallas_call(
        functools.partial(_dispatch_kernel, n_e=n_e, tm=tm, blk=EXPERT_BLK),
        grid_spec=grid_spec,
        out_shape=jax.ShapeDtypeStruct((slots, dw), I32),
        compiler_params=_cparams(),
        name="moe_dispatch",
    )(pad_end, padded, dest_flat, hp)


PACK_CHUNK = 512


def _swiglu_hidden(xp_ref, wgu):
    half = xp_ref.shape[-1]
    acc = None
    for c in range(half // PACK_CHUNK):
        cl = slice(c * PACK_CHUNK, (c + 1) * PACK_CHUNK)
        ch = slice(half + c * PACK_CHUNK, half + (c + 1) * PACK_CHUNK)
        lo, hi = _unpack_halves(xp_ref[:, cl])
        part = (jnp.dot(lo.astype(BF16), wgu(cl), preferred_element_type=F32)
                + jnp.dot(hi.astype(BF16), wgu(ch), preferred_element_type=F32))
        acc = part if acc is None else acc + part
    de = acc.shape[-1] // 2
    return (jax.nn.silu(acc[:, :de]) * acc[:, de:]).astype(BF16)


def _down_packed(hid, wd, o_ref):
    half = o_ref.shape[-1]
    for c in range(half // PACK_CHUNK):
        cl = slice(c * PACK_CHUNK, (c + 1) * PACK_CHUNK)
        ch = slice(half + c * PACK_CHUNK, half + (c + 1) * PACK_CHUNK)
        o_ref[:, cl] = _pack_halves(jnp.dot(hid, wd(cl), preferred_element_type=F32),
                                    jnp.dot(hid, wd(ch), preferred_element_type=F32))


def _expert_up_kernel(be_ref, nu_ref, x_ref, wg_ref, wu_ref, hid_ref):
    used = pl.program_id(0) < nu_ref[0]

    def wgu(sl):
        return jnp.concatenate([wg_ref[0, sl, :].astype(BF16), wu_ref[0, sl, :].astype(BF16)], axis=-1)

    @pl.when(used)
    def _():
        hid_ref[...] = _swiglu_hidden(x_ref, wgu)

    @pl.when(jnp.logical_not(used))
    def _():
        hid_ref[...] = jnp.zeros_like(hid_ref)


def _expert_down_kernel(be_ref, nu_ref, hid_ref, wd_ref, o_ref):
    used = pl.program_id(0) < nu_ref[0]

    @pl.when(used)
    def _():
        _down_packed(hid_ref[...], lambda sl: wd_ref[0, :, sl].astype(BF16), o_ref)

    @pl.when(jnp.logical_not(used))
    def _():
        o_ref[...] = jnp.zeros_like(o_ref)


def expert_ffn(xs, block_e, n_used, wg, wu, wd, blk):
    slots, dw = xs.shape
    d, de = wg.shape[1], wg.shape[2]
    n_blocks = slots // blk
    cur = lambda i, nu: jnp.minimum(i, nu[0] - 1)
    up_spec = pltpu.PrefetchScalarGridSpec(
        num_scalar_prefetch=2,
        grid=(n_blocks,),
        in_specs=[pl.BlockSpec((blk, dw), lambda i, be, nu: (cur(i, nu), 0)),
                  pl.BlockSpec((1, d, de), lambda i, be, nu: (be[cur(i, nu)], 0, 0)),
                  pl.BlockSpec((1, d, de), lambda i, be, nu: (be[cur(i, nu)], 0, 0))],
        out_specs=pl.BlockSpec((blk, de), lambda i, be, nu: (i, 0)),
    )
    hid = pl.pallas_call(
        _expert_up_kernel,
        grid_spec=up_spec,
        out_shape=jax.ShapeDtypeStruct((slots, de), BF16),
        compiler_params=_cparams(),
        name="expert_up",
    )(block_e, n_used, xs, wg, wu)
    down_spec = pltpu.PrefetchScalarGridSpec(
        num_scalar_prefetch=2,
        grid=(n_blocks,),
        in_specs=[pl.BlockSpec((blk, de), lambda i, be, nu: (cur(i, nu), 0)),
                  pl.BlockSpec((1, de, d), lambda i, be, nu: (be[cur(i, nu)], 0, 0))],
        out_specs=pl.BlockSpec((blk, dw), lambda i, be, nu: (i, 0)),
    )
    return pl.pallas_call(
        _expert_down_kernel,
        grid_spec=down_spec,
        out_shape=jax.ShapeDtypeStruct((slots, dw), I32),
        compiler_params=_cparams(),
        name="expert_down",
    )(block_e, n_used, hid, wd)


def _shared_kernel(hp_ref, wgu_ref, wd_ref, o_ref):
    hid = _swiglu_hidden(hp_ref, lambda sl: wgu_ref[sl, :])
    _down_packed(hid, lambda sl: wd_ref[:, sl], o_ref)


def shared_ffn(hp, wgu, wd, tm=512):
    n, dw = hp.shape
    d, de2 = wgu.shape
    row = pl.BlockSpec((tm, dw), lambda i: (i, 0))
    return pl.pallas_call(
        _shared_kernel,
        grid=(n // tm,),
        in_specs=[row, pl.BlockSpec((d, de2), lambda i: (0, 0)), pl.BlockSpec((de2 // 2, d), lambda i: (0, 0))],
        out_specs=row,
        out_shape=jax.ShapeDtypeStruct((n, dw), I32),
        compiler_params=_cparams(),
        name="shared_ffn",
    )(hp, wgu, wd)


def _final_kernel(dc_ref, dn_ref, x1_ref, sp_ref, gate_ref, ys_ref, gpost_ref, gt_ref, o_ref, buf, sem, *, tm, n_tiles):
    i = pl.program_id(0)
    slot = i % 2
    sub = 8

    def issue_rows(dref, s, j):
        base = pl.multiple_of(j * sub, sub)
        for tt in range(sub):
            for k in range(TOP_K):
                pltpu.make_async_copy(ys_ref.at[pl.ds(dref[(base + tt) * TOP_K + k], 1), :],
                                      buf.at[s, pl.ds(k * tm + base + tt, 1), :],
                                      sem.at[s]).start(priority=k % 2)

    def reduce_rows(s, j):
        rs = pl.ds(pl.multiple_of(j * sub, sub), sub)
        g = gate_ref[rs, :]
        acc_lo, acc_hi = _unpack_halves(sp_ref[rs, :])
        for k in range(TOP_K):
            lo, hi = _unpack_halves(buf[s, pl.ds(pl.multiple_of(k * tm + j * sub, sub), sub), :])
            gk = g[:, k:k + 1]
            acc_lo = acc_lo + gk * lo
            acc_hi = acc_hi + gk * hi
        ffn = jnp.concatenate([acc_lo, acc_hi], axis=-1)
        o_ref[rs, :] = x1_ref[rs, :] + gt_ref[0] * _rms(ffn, gpost_ref[...])

    @pl.when(i == 0)
    def _():
        def first(j, c):
            issue_rows(dc_ref, 0, j)
            return c
        lax.fori_loop(0, tm // sub, first, 0)

    def wait_slot(s):
        pltpu.make_async_copy(ys_ref.at[pl.ds(0, TOP_K * tm), :], buf.at[s], sem.at[s]).wait()

    wait_slot(slot)

    for s in range(2):
        @pl.when(slot == s)
        def _():
            def body(j, c):
                issue_rows(dn_ref, 1 - s, j)
                reduce_rows(s, j)
                return c
            lax.fori_loop(0, tm // sub, body, 0, unroll=4)

    @pl.when(i == n_tiles - 1)
    def _():
        wait_slot(1 - slot)


def final_mix(x1, sp, gates, ys, dest_flat, g_post, gt, tiles_per_batch, tm=COMBINE_TM):
    n, d = x1.shape
    dw = sp.shape[1]
    n_e = gates.shape[1]
    n_tiles = n // tm
    return pl.pallas_call(
        functools.partial(_final_kernel, tm=tm, n_tiles=n_tiles),
        grid=(n_tiles,),
        in_specs=[pl.BlockSpec((tm * TOP_K,), lambda i: (i,), memory_space=pltpu.SMEM),
                  pl.BlockSpec((tm * TOP_K,), lambda i: (jnp.minimum(i + 1, n_tiles - 1),), memory_space=pltpu.SMEM),
                  pl.BlockSpec((tm, d), lambda i: (i, 0)),
                  pl.BlockSpec((tm, dw), lambda i: (i, 0)),
                  pl.BlockSpec((tm, n_e), lambda i: (i, 0)),
                  pl.BlockSpec(memory_space=pl.ANY),
                  pl.BlockSpec((1, d), lambda i: (0, 0)),
                  pl.BlockSpec((1, 1, d), lambda i: (i // tiles_per_batch, 0, 0))],
        out_specs=pl.BlockSpec((tm, d), lambda i: (i, 0)),
        out_shape=jax.ShapeDtypeStruct((n, d), F32),
        scratch_shapes=[pltpu.VMEM((2, TOP_K * tm, dw), I32), pltpu.SemaphoreType.DMA((2,))],
        compiler_params=_cparams(),
        name="final_mix",
    )(dest_flat, dest_flat, x1, sp, gates, ys, g_post.reshape(1, d), gt)


def kernel(x, c, ctx, c_ctx, w_ada, b_ada, g_pre_mix, g_post_mix, g_pre_ffn, g_post_ffn, w_in, rpb, ssm_a_re, ssm_a_im, ssm_log_dt, ssm_b_re, ssm_b_im, ssm_c_re, ssm_c_im, ssm_d, w_glu, b_glu, g_attn_out, g_ssm_out, w_out, w_router, b_router, w_exp_gate, w_exp_up, w_exp_down, w_sh_gate, w_sh_up, w_sh_down):
    b, l, d = x.shape
    lc = ctx.shape[1]
    assert w_ada.shape[0] == 1 and b + 1 <= 8
    n_in = w_in.shape[-1]
    d_ssm = w_glu.shape[-1]
    d_attn = d - d_ssm
    n_heads = d_attn // HEAD_DIM
    n = b * l

    c8 = jnp.concatenate([c, c_ctx[None], jnp.zeros((8 - b - 1, d), F32)], axis=0)
    mod = ada_mod(c8, w_ada[0], b_ada[0]).reshape(8, 6, 1, d)
    sh_m, sc_m, gt_m, sh_f, sc_f, gt_f = [mod[:b, j] for j in range(6)]
    csh_m, csc_m = mod[b:b + 1, 0], mod[b:b + 1, 1]

    w_in_b = w_in[0].astype(BF16)
    tn = PROJ_TN
    p_lat = mod_proj(x, g_pre_mix[0], sc_m, sh_m, w_in_b, 0, n_in, 512, tn)
    p_ctx = mod_proj(ctx.reshape(1, b * lc, d), g_pre_mix[0], csc_m, csh_m, w_in_b,
                     d_attn // tn, n_in - d_attn, 512, tn).reshape(b, lc, n_in - d_attn)

    attn = neighborhood_attention(p_lat, p_ctx, rpb[0], n_heads)

    mats = _ssm_mats(ssm_a_re[0], ssm_a_im[0], ssm_log_dt[0], ssm_b_re[0], ssm_b_im[0],
                     ssm_c_re[0], ssm_c_im[0], ssm_d[0])
    y = s5_mixer(p_lat[..., 3 * d_attn:], p_ctx[..., 2 * d_attn:], mats)
    ssm = s5_glu(y.reshape(n, d_ssm), w_glu[0].astype(BF16), b_glu[0])

    mix = merge_proj(attn.reshape(n, d_attn), ssm, g_attn_out[0], g_ssm_out[0], w_out[0].astype(BF16))
    x1, hp, idx, gates, mask = post_mix(x, mix.reshape(b, l, d), g_post_mix[0], gt_m, g_pre_ffn[0], sc_f, sh_f,
                                        w_router[0].astype(BF16), b_router[0])

    n_e = w_router.shape[-1]
    m = n * TOP_K
    idx = idx.reshape(n, n_e)
    rank, counts = route_ranks(mask.reshape(n, n_e), idx)
    padded = (counts + EXPERT_BLK - 1) // EXPERT_BLK * EXPERT_BLK
    pad_end = jnp.cumsum(padded).astype(I32)
    dest = route_dest(rank, idx, pad_end - padded)
    n_blocks = m // EXPERT_BLK + n_e
    slots = n_blocks * EXPERT_BLK
    block_e = jnp.minimum(jnp.searchsorted(pad_end, jnp.arange(n_blocks) * EXPERT_BLK, side='right'),
                          n_e - 1).astype(I32)
    n_used = (pad_end[-1] // EXPERT_BLK).astype(I32).reshape(1)

    hp2 = hp.reshape(n, d // 2)
    xs = moe_dispatch(hp2, dest, pad_end, padded.astype(I32), slots)
    ys = expert_ffn(xs, block_e, n_used, w_exp_gate[0], w_exp_up[0], w_exp_down[0], EXPERT_BLK)
    sp = shared_ffn(hp2, jnp.concatenate([w_sh_gate[0], w_sh_up[0]], axis=-1).astype(BF16), w_sh_down[0].astype(BF16))
    out = final_mix(x1.reshape(n, d), sp, gates.reshape(n, n_e), ys, dest, g_post_ffn[0], gt_f, l // COMBINE_TM)
    return out.reshape(b, l, d)
```

```python
import functools

import numpy as np
import jax
import jax.numpy as jnp
from jax import lax
from jax.experimental import pallas as pl
from jax.experimental.pallas import tpu as pltpu

F32 = jnp.float32
BF16 = jnp.bfloat16
I32 = jnp.int32

GRID_W = 64
HEAD_DIM = 128
WIN_ROWS = 8
WIN_COLS = 16
ROPE_THETA = 10000.0
SSM_GROUP_CH = 16
SSM_STATE = 64
TOP_K = 8
ROUTED_SCALE = 2.5
EPS = 1e-6
NEG_INF = -1e30

Q_ROWS = 4
BAND_ROWS = Q_ROWS + WIN_ROWS - 1
VMEM_LIMIT = 56 * 1024 * 1024
EXPERT_BLK = 512
PROJ_TN = 1024
DISPATCH_TM = 256
COMBINE_TM = 128


def _cparams():
    return pltpu.CompilerParams(vmem_limit_bytes=VMEM_LIMIT)


def _rms(x, g):
    return x * lax.rsqrt(jnp.mean(x * x, axis=-1, keepdims=True) + EPS) * g


def _pack_halves(lo, hi):
    lo_bits = lax.bitcast_convert_type(lo.astype(BF16).astype(F32), I32)
    hi_bits = lax.bitcast_convert_type(hi.astype(BF16).astype(F32), I32)
    return (hi_bits & jnp.int32(-65536)) | lax.shift_right_logical(lo_bits, jnp.int32(16))


def _unpack_halves(w):
    lo = lax.bitcast_convert_type(lax.shift_left(w, jnp.int32(16)), F32)
    hi = lax.bitcast_convert_type(w & jnp.int32(-65536), F32)
    return lo, hi


def _ada_kernel(c_ref, w_ref, b_ref, o_ref):
    a = jax.nn.silu(c_ref[...]).astype(BF16)
    o_ref[...] = jnp.dot(a, w_ref[...].astype(BF16), preferred_element_type=F32) + b_ref[...]


def ada_mod(c8, w_ada, b_ada):
    d, n = w_ada.shape
    tn = 512
    return pl.pallas_call(
        _ada_kernel,
        grid=(n // tn,),
        in_specs=[pl.BlockSpec((8, d), lambda j: (0, 0)),
                  pl.BlockSpec((d, tn), lambda j: (0, j)),
                  pl.BlockSpec((1, tn), lambda j: (0, j))],
        out_specs=pl.BlockSpec((8, tn), lambda j: (0, j)),
        out_shape=jax.ShapeDtypeStruct((8, n), F32),
        compiler_params=_cparams(),
        name="ada_mod",
    )(c8, w_ada, b_ada.reshape(1, n))


def _modproj_kernel(x_ref, g_ref, sc_ref, sh_ref, w_ref, o_ref, h_scr):
    @pl.when(pl.program_id(2) == 0)
    def _():
        h = _rms(x_ref[0], g_ref[...]) * (1.0 + sc_ref[0]) + sh_ref[0]
        h_scr[...] = h.astype(BF16)

    o_ref[0] = jnp.dot(h_scr[...], w_ref[...], preferred_element_type=F32).astype(o_ref.dtype)


def mod_proj(x, g, scale, shift, w_bf16, col_blk_off, n_out, tm, tn):
    b, l, d = x.shape
    return pl.pallas_call(
        _modproj_kernel,
        grid=(b, l // tm, n_out // tn),
        in_specs=[pl.BlockSpec((1, tm, d), lambda bi, i, j: (bi, i, 0)),
                  pl.BlockSpec((1, d), lambda bi, i, j: (0, 0)),
                  pl.BlockSpec((1, 1, d), lambda bi, i, j: (bi, 0, 0)),
                  pl.BlockSpec((1, 1, d), lambda bi, i, j: (bi, 0, 0)),
                  pl.BlockSpec((d, tn), lambda bi, i, j: (0, j + col_blk_off))],
        out_specs=pl.BlockSpec((1, tm, tn), lambda bi, i, j: (bi, i, j)),
        out_shape=jax.ShapeDtypeStruct((b, l, n_out), BF16),
        scratch_shapes=[pltpu.VMEM((tm, d), BF16)],
        compiler_params=_cparams(),
        name="mod_proj",
    )(x, g.reshape(1, d), scale, shift, w_bf16)


def _rope_tables(l):
    half = HEAD_DIM // 2
    quarter = half // 2
    inv_freq = 1.0 / (ROPE_THETA ** (jnp.arange(0, half, 2, dtype=F32) / half))
    rows = (jnp.arange(l) // GRID_W).astype(F32)
    cols = (jnp.arange(l) % GRID_W).astype(F32)

    def cs(pos):
        ang = pos[:, None] * inv_freq[None, :]
        return jnp.cos(ang), jnp.sin(ang)

    cr, sr = cs(rows)
    cc, sc = cs(cols)
    zero = jnp.zeros((l, quarter), F32)
    cos = jnp.concatenate([cr, cr, cc, cc], axis=-1)
    s_lo = jnp.concatenate([-sr, zero, -sc, zero], axis=-1)
    s_hi = jnp.concatenate([zero, sr, zero, sc], axis=-1)
    return cos, s_lo, s_hi


def _attn_bias(rpb, rows_n):
    kh = min(WIN_ROWS, rows_n)
    cases = [(0, 0), (Q_ROWS, 0), (rows_n - Q_ROWS, rows_n - BAND_ROWS)]
    col = np.arange(GRID_W)
    cstart = np.clip(col - WIN_COLS // 2, 0, GRID_W - WIN_COLS)
    ok_c = (col[None, :] >= cstart[:, None]) & (col[None, :] < cstart[:, None] + WIN_COLS)
    co = np.clip(col[None, :] - col[:, None], -(WIN_COLS - 1), WIN_COLS - 1) + (WIN_COLS - 1)
    ro_all, ok_all = [], []
    for r0, sb in cases:
        qrow = r0 + np.arange(Q_ROWS)
        krow = sb + np.arange(BAND_ROWS)
        rs = np.clip(qrow - kh // 2, 0, rows_n - kh)
        ok_all.append((krow[None, :] >= rs[:, None]) & (krow[None, :] < rs[:, None] + kh))
        ro_all.append(np.clip(krow[None, :] - qrow[:, None] + (WIN_ROWS - 1), 0, 2 * WIN_ROWS - 2))
    ro_all, ok_all = np.stack(ro_all), np.stack(ok_all)
    onehot = (co[None] == np.arange(2 * WIN_COLS - 1)[:, None, None]).astype(np.float32)
    toe = jnp.einsum('hrc,cqk->hrqk', rpb.astype(F32), onehot, precision=lax.Precision.HIGHEST)
    blocks = toe[:, ro_all]
    ok = ok_all[None, :, :, :, None, None] & ok_c[None, None, None, None]
    bias = jnp.transpose(jnp.where(ok, blocks, NEG_INF), (0, 1, 2, 4, 3, 5))
    return bias.reshape(rpb.shape[0], 3, Q_ROWS * GRID_W, BAND_ROWS * GRID_W)


def _attn_kernel(q_ref, k_ref, v_ref, kc_ref, vc_ref, bias_ref, cos_ref, slo_ref, shi_ref, o_ref,
                 qr_scr, qs_scr, kr_scr, *, n_blk, rows_n):
    quarter = HEAD_DIM // 4
    qn = Q_ROWS * GRID_W
    kn = BAND_ROWS * GRID_W
    scale = HEAD_DIM ** -0.5

    def rope(x, sl):
        return (x * cos_ref[sl, :] + pltpu.roll(x, HEAD_DIM - quarter, 1) * slo_ref[sl, :]
                + pltpu.roll(x, quarter, 1) * shi_ref[sl, :])

    def rope_body(i, _):
        sl = pl.ds(pl.multiple_of(i * qn, qn), qn)
        q = q_ref[0, sl, :].astype(F32) * scale
        qs_scr[sl, :] = q.astype(BF16)
        qr_scr[sl, :] = rope(q, sl).astype(BF16)
        kr_scr[sl, :] = rope(k_ref[0, sl, :].astype(F32), sl).astype(BF16)
        return 0

    lax.fori_loop(0, n_blk, rope_body, 0)

    kc = kc_ref[0]
    vc = vc_ref[0]
    nt = (((1,), (1,)), ((), ()))

    def body(i, _):
        sb = jnp.clip(i * Q_ROWS - WIN_ROWS // 2, 0, rows_n - BAND_ROWS)
        ks = pl.ds(pl.multiple_of(sb * GRID_W, GRID_W), kn)
        qs = pl.ds(pl.multiple_of(i * qn, qn), qn)
        case = jnp.where(i == 0, 0, jnp.where(i == n_blk - 1, 2, 1))
        s = lax.dot_general(qr_scr[qs, :], kr_scr[ks, :], nt, preferred_element_type=F32) + bias_ref[0, case]
        sc = lax.dot_general(qs_scr[qs, :], kc, nt, preferred_element_type=F32)
        m = jnp.maximum(jnp.max(s, axis=-1, keepdims=True), jnp.max(sc, axis=-1, keepdims=True))
        p = jnp.exp(s - m)
        pc = jnp.exp(sc - m)
        den = jnp.sum(p, axis=-1, keepdims=True) + jnp.sum(pc, axis=-1, keepdims=True)
        o = (jnp.dot(p.astype(BF16), v_ref[0, ks, :], preferred_element_type=F32)
             + jnp.dot(pc.astype(BF16), vc, preferred_element_type=F32))
        o_ref[0, qs, :] = (o / den).astype(o_ref.dtype)
        return 0

    lax.fori_loop(0, n_blk, body, 0, unroll=2)


def neighborhood_attention(p_lat, p_ctx, rpb, n_heads):
    b, l, _ = p_lat.shape
    lc = p_ctx.shape[1]
    rows_n = l // GRID_W
    assert rows_n % Q_ROWS == 0 and rows_n >= BAND_ROWS + Q_ROWS
    n_blk = rows_n // Q_ROWS
    bias = _attn_bias(rpb, rows_n)
    cos, s_lo, s_hi = _rope_tables(l)
    qn, kn = Q_ROWS * GRID_W, BAND_ROWS * GRID_W
    h = n_heads
    tab = pl.BlockSpec((l, HEAD_DIM), lambda bi, hi: (0, 0))
    return pl.pallas_call(
        functools.partial(_attn_kernel, n_blk=n_blk, rows_n=rows_n),
        grid=(b, h),
        in_specs=[pl.BlockSpec((1, l, HEAD_DIM), lambda bi, hi: (bi, 0, hi)),
                  pl.BlockSpec((1, l, HEAD_DIM), lambda bi, hi: (bi, 0, hi + h)),
                  pl.BlockSpec((1, l, HEAD_DIM), lambda bi, hi: (bi, 0, hi + 2 * h)),
                  pl.BlockSpec((1, lc, HEAD_DIM), lambda bi, hi: (bi, 0, hi)),
                  pl.BlockSpec((1, lc, HEAD_DIM), lambda bi, hi: (bi, 0, hi + h)),
                  pl.BlockSpec((1, 3, qn, kn), lambda bi, hi: (hi, 0, 0, 0)),
                  tab, tab, tab],
        out_specs=pl.BlockSpec((1, l, HEAD_DIM), lambda bi, hi: (bi, 0, hi)),
        out_shape=jax.ShapeDtypeStruct((b, l, h * HEAD_DIM), BF16),
        scratch_shapes=[pltpu.VMEM((l, HEAD_DIM), BF16)] * 3,
        compiler_params=_cparams(),
        name="nbr_attn",
    )(p_lat, p_lat, p_lat, p_ctx, p_ctx, bias, cos, s_lo, s_hi)


SSM_CHUNK = 16
SSM_ROWS = 8
SSM_PACK = 8


def _ssm_mats(a_re, a_im, log_dt, b_re, b_im, c_re, c_im, d_skip):
    t = SSM_CHUNK
    hi = lax.Precision.HIGHEST
    a = lax.complex(a_re.astype(F32), a_im.astype(F32))
    dta = jnp.exp(log_dt.astype(F32))[..., None] * a
    a_bar = jnp.exp(dta)
    b_bar = ((a_bar - 1.0) / a)[..., None] * lax.complex(b_re.astype(F32), b_im.astype(F32))
    cm = lax.complex(c_re.astype(F32), c_im.astype(F32))
    k = jnp.arange(t + 1, dtype=F32)
    ap = jnp.exp(dta[..., None] * k)
    g, p, n = a.shape[1], a.shape[2], b_re.shape[-1]
    kern = jnp.einsum('dgnp,dgpl,dgpm->dglnm', cm, ap[..., :t], b_bar, precision=hi).real
    s_i = np.arange(t)[:, None]
    t_i = np.arange(t)[None, :]
    kf = kern[0][:, np.clip(t_i - s_i, 0, t - 1)]
    kb = kern[1][:, np.clip(s_i - t_i, 0, t - 1)]
    eye = jnp.eye(n, dtype=F32)
    m_mat = (jnp.where((s_i <= t_i)[None, :, :, None, None], kf, 0.0)
             + jnp.where((s_i >= t_i)[None, :, :, None, None], kb, 0.0)
             + (s_i == t_i)[None, :, :, None, None] * (d_skip.astype(F32)[:, None, None, :, None] * eye))
    m_mat = jnp.transpose(m_mat, (0, 1, 4, 2, 3)).reshape(g, t * n, t * n)
    wf = ap[0][:, :, t - 1::-1][..., :t, None] * b_bar[0][:, :, None, :]
    wb = ap[1][:, :, :t, None] * b_bar[1][:, :, None, :]
    to_rows = lambda z: jnp.transpose(z, (0, 2, 3, 1)).reshape(g, t * n, p)
    w_mat = jnp.concatenate([to_rows(wf.real), to_rows(wb.real), to_rows(wf.imag), to_rows(wb.imag)], axis=-1)
    zf = jnp.transpose(cm[0], (0, 2, 1))[:, :, None, :] * ap[0][:, :, 1:t + 1, None]
    zb = jnp.transpose(cm[1], (0, 2, 1))[:, :, None, :] * ap[1][:, :, t:0:-1, None]
    flat = lambda z: z.reshape(g, p, t * n)
    r_mat = jnp.concatenate([flat(zf.real), flat(zb.real), -flat(zf.imag), -flat(zb.imag)], axis=1)
    a_t = ap[..., t]
    a_vec = jnp.stack([jnp.concatenate([a_t[0].real, a_t[1].real], -1),
                       jnp.concatenate([a_t[0].imag, a_t[1].imag], -1)], axis=1)
    a_vec = jnp.pad(a_vec, ((0, 0), (0, SSM_ROWS - 2), (0, 0)))
    return m_mat.astype(BF16), w_mat.astype(BF16), r_mat.astype(BF16), a_vec


def _s5_kernel(u_ref, sel_ref, selt_ref, m_ref, w_ref, r_ref, a_ref, y_ref, v_scr, ent_scr, *, n_ctx, n_chunks):
    u = jnp.dot(u_ref[0], sel_ref[0], preferred_element_type=F32).astype(BF16)
    p2 = a_ref.shape[-1]
    p = p2 // 2
    v_scr[...] = jnp.dot(u, w_ref[0], preferred_element_type=F32)
    a_re = jnp.broadcast_to(a_ref[0, 0:1, :], (SSM_ROWS, p2))
    a_im = jnp.broadcast_to(a_ref[0, 1:2, :], (SSM_ROWS, p2))
    is_fwd = lax.broadcasted_iota(I32, (SSM_ROWS, p2), 1) < p

    def step(j, carry):
        s_re, s_im = carry
        cb = jnp.where(j < n_ctx, n_ctx - 1 - j, n_chunks + n_ctx - 1 - j)
        rf = pl.ds(pl.multiple_of(j * SSM_ROWS, SSM_ROWS), SSM_ROWS)
        rb = pl.ds(pl.multiple_of(cb * SSM_ROWS, SSM_ROWS), SSM_ROWS)
        ent_scr[rf, 0:p] = s_re[:, :p]
        ent_scr[rb, p:p2] = s_re[:, p:]
        ent_scr[rf, p2:p2 + p] = s_im[:, :p]
        ent_scr[rb, p2 + p:] = s_im[:, p:]
        in_re = jnp.where(is_fwd, v_scr[rf, :p2], v_scr[rb, :p2])
        in_im = jnp.where(is_fwd, v_scr[rf, p2:], v_scr[rb, p2:])
        return a_re * s_re - a_im * s_im + in_re, a_re * s_im + a_im * s_re + in_im

    zero = jnp.zeros((SSM_ROWS, p2), F32)
    lax.fori_loop(0, n_chunks, step, (zero, zero))
    y = (jnp.dot(u, m_ref[0], preferred_element_type=F32)
         + jnp.dot(ent_scr[...].astype(BF16), r_ref[0], preferred_element_type=F32)).astype(BF16)
    def part(c):
        return jnp.dot(y, selt_ref[0, :, c:c + PACK_CHUNK], preferred_element_type=F32).astype(y_ref.dtype)

    @pl.when(pl.program_id(1) == 0)
    def _():
        for c in range(0, y_ref.shape[-1], PACK_CHUNK):
            y_ref[0, :, c:c + PACK_CHUNK] = part(c)

    @pl.when(pl.program_id(1) > 0)
    def _():
        for c in range(0, y_ref.shape[-1], PACK_CHUNK):
            y_ref[0, :, c:c + PACK_CHUNK] = y_ref[0, :, c:c + PACK_CHUNK] + part(c)


def s5_mixer(u_lat, u_ctx, mats):
    m_mat, w_mat, r_mat, a_vec = mats
    b, l, dch = u_lat.shape
    lc = u_ctx.shape[1]
    g = m_mat.shape[0]
    n = dch // g
    t = SSM_CHUNK
    tn = t * n
    gp = SSM_PACK
    packs, lanes = g // gp, gp * n
    n_ctx, n_chunks = lc // t, (lc + l) // t
    assert b <= SSM_ROWS and lc % t == 0 and l % t == 0 and g % gp == 0
    z = jnp.concatenate([u_ctx, u_lat], axis=1).reshape(b, n_chunks, t, packs, lanes)
    z = jnp.pad(jnp.transpose(z, (3, 1, 0, 2, 4)), ((0, 0), (0, 0), (0, SSM_ROWS - b), (0, 0), (0, 0)))
    rows = n_chunks * SSM_ROWS
    z = z.reshape(packs, rows, t * lanes)
    ri = jnp.arange(t * lanes)
    ci = jnp.arange(tn)
    same = (ri[:, None] // lanes == ci[None, :] // n) & (ri[:, None] % n == ci[None, :] % n)
    sel = (same[None] & ((ri[None, :, None] % lanes) // n == jnp.arange(gp)[:, None, None])).astype(BF16)
    selt = jnp.transpose(sel, (0, 2, 1))
    grp = lambda k: pl.BlockSpec((1, k, tn), lambda pi, qi: (pi * gp + qi, 0, 0))
    pack_blk = pl.BlockSpec((1, rows, t * lanes), lambda pi, qi: (pi, 0, 0))
    y = pl.pallas_call(
        functools.partial(_s5_kernel, n_ctx=n_ctx, n_chunks=n_chunks),
        grid=(packs, gp),
        in_specs=[pack_blk,
                  pl.BlockSpec((1, t * lanes, tn), lambda pi, qi: (qi, 0, 0)),
                  pl.BlockSpec((1, tn, t * lanes), lambda pi, qi: (qi, 0, 0)),
                  grp(tn), grp(tn), grp(r_mat.shape[1]),
                  pl.BlockSpec((1, SSM_ROWS, a_vec.shape[-1]), lambda pi, qi: (pi * gp + qi, 0, 0))],
        out_specs=pack_blk,
        out_shape=jax.ShapeDtypeStruct((packs, rows, t * lanes), BF16),
        scratch_shapes=[pltpu.VMEM((rows, w_mat.shape[-1]), F32), pltpu.VMEM((rows, r_mat.shape[1]), F32)],
        compiler_params=_cparams(),
        name="s5_mixer",
    )(z, sel, selt, m_mat, w_mat, r_mat, a_vec)
    y = y.reshape(packs, n_chunks, SSM_ROWS, t, lanes)[:, n_ctx:, :b]
    return jnp.transpose(y, (2, 1, 3, 0, 4)).reshape(b, l, dch)


def _glu_kernel(y_ref, w_ref, b_ref, o_ref):
    z = jax.nn.gelu(y_ref[...].astype(F32))
    gate = jax.nn.sigmoid(jnp.dot(z.astype(BF16), w_ref[...], preferred_element_type=F32) + b_ref[...])
    o_ref[...] = (z * gate).astype(o_ref.dtype)


def s5_glu(y, w_bf16, b_glu, tm=512):
    n, d = y.shape
    return pl.pallas_call(
        _glu_kernel,
        grid=(n // tm,),
        in_specs=[pl.BlockSpec((tm, d), lambda i: (i, 0)),
                  pl.BlockSpec((d, d), lambda i: (0, 0)),
                  pl.BlockSpec((1, d), lambda i: (0, 0))],
        out_specs=pl.BlockSpec((tm, d), lambda i: (i, 0)),
        out_shape=jax.ShapeDtypeStruct((n, d), BF16),
        compiler_params=_cparams(),
        name="s5_glu",
    )(y, w_bf16, b_glu.reshape(1, d))


def _merge_kernel(a_ref, s_ref, ga_ref, gs_ref, w_ref, o_ref, h_scr):
    da = a_ref.shape[-1]

    @pl.when(pl.program_id(1) == 0)
    def _():
        h_scr[:, :da] = _rms(a_ref[...].astype(F32), ga_ref[...]).astype(BF16)
        h_scr[:, da:] = _rms(s_ref[...].astype(F32), gs_ref[...]).astype(BF16)

    o_ref[...] = jnp.dot(h_scr[...], w_ref[...], preferred_element_type=F32)


def merge_proj(attn, ssm, g_attn, g_ssm, w_bf16, tm=512, tn=1024):
    n, da = attn.shape
    ds = ssm.shape[1]
    d, dout = w_bf16.shape
    return pl.pallas_call(
        _merge_kernel,
        grid=(n // tm, dout // tn),
        in_specs=[pl.BlockSpec((tm, da), lambda i, j: (i, 0)),
                  pl.BlockSpec((tm, ds), lambda i, j: (i, 0)),
                  pl.BlockSpec((1, da), lambda i, j: (0, 0)),
                  pl.BlockSpec((1, ds), lambda i, j: (0, 0)),
                  pl.BlockSpec((d, tn), lambda i, j: (0, j))],
        out_specs=pl.BlockSpec((tm, tn), lambda i, j: (i, j)),
        out_shape=jax.ShapeDtypeStruct((n, dout), F32),
        scratch_shapes=[pltpu.VMEM((tm, d), BF16)],
        compiler_params=_cparams(),
        name="merge_proj",
    )(attn, ssm, g_attn.reshape(1, da), g_ssm.reshape(1, ds), w_bf16)


def _post_mix_kernel(x_ref, mix_ref, gpost_ref, gt_ref, gpre_ref, sc_ref, sh_ref, wr_ref, br_ref,
                     x1_ref, hp_ref, idx_ref, gate_ref, mask_ref):
    x1 = x_ref[0] + gt_ref[0] * _rms(mix_ref[0], gpost_ref[...])
    x1_ref[0] = x1
    h = _rms(x1, gpre_ref[...]) * (1.0 + sc_ref[0]) + sh_ref[0]
    half = h.shape[-1] // 2
    hp_ref[0] = _pack_halves(h[:, :half], h[:, half:])
    scores = jax.nn.sigmoid(jnp.dot(h.astype(BF16), wr_ref[...], preferred_element_type=F32))
    n_e = scores.shape[-1]
    lane = lax.broadcasted_iota(I32, scores.shape, 1)
    biased = scores + br_ref[...]
    idx_out = jnp.zeros(scores.shape, I32)
    sel_out = jnp.zeros(scores.shape, F32)
    mask = jnp.zeros(scores.shape, jnp.bool_)
    for k in range(TOP_K):
        m = jnp.max(biased, axis=-1, keepdims=True)
        ik = jnp.min(jnp.where(biased == m, lane, n_e), axis=-1, keepdims=True)
        hit = lane == ik
        sel_k = jnp.sum(jnp.where(hit, scores, 0.0), axis=-1, keepdims=True)
        idx_out = jnp.where(lane == k, ik, idx_out)
        sel_out = jnp.where(lane == k, sel_k, sel_out)
        mask = jnp.logical_or(mask, hit)
        biased = jnp.where(hit, -jnp.inf, biased)
    idx_ref[0] = idx_out
    gate_ref[0] = sel_out / jnp.sum(sel_out, axis=-1, keepdims=True) * ROUTED_SCALE
    mask_ref[0] = mask.astype(BF16)


def post_mix(x, mix, g_post, gt, g_pre, scale, shift, wr_bf16, b_router, tm=256):
    b, l, d = x.shape
    n_e = wr_bf16.shape[1]
    row = pl.BlockSpec((1, tm, d), lambda bi, i: (bi, i, 0))
    prow = pl.BlockSpec((1, tm, d // 2), lambda bi, i: (bi, i, 0))
    vec = pl.BlockSpec((1, d), lambda bi, i: (0, 0))
    bvec = pl.BlockSpec((1, 1, d), lambda bi, i: (bi, 0, 0))
    small = pl.BlockSpec((1, tm, n_e), lambda bi, i: (bi, i, 0))
    return pl.pallas_call(
        _post_mix_kernel,
        grid=(b, l // tm),
        in_specs=[row, row, vec, bvec, vec, bvec, bvec,
                  pl.BlockSpec((d, n_e), lambda bi, i: (0, 0)),
                  pl.BlockSpec((1, n_e), lambda bi, i: (0, 0))],
        out_specs=[row, prow, small, small, small],
        out_shape=[jax.ShapeDtypeStruct((b, l, d), F32),
                   jax.ShapeDtypeStruct((b, l, d // 2), I32),
                   jax.ShapeDtypeStruct((b, l, n_e), I32),
                   jax.ShapeDtypeStruct((b, l, n_e), F32),
                   jax.ShapeDtypeStruct((b, l, n_e), BF16)],
        compiler_params=_cparams(),
        name="post_mix",
    )(x, mix, g_post.reshape(1, d), gt, g_pre.reshape(1, d), scale, shift, wr_bf16, b_router.reshape(1, n_e))


def _rank_kernel(mask_ref, idx_ref, rank_ref, cnt_ref, carry):
    @pl.when(pl.program_id(0) == 0)
    def _():
        carry[...] = jnp.zeros_like(carry)

    m = mask_ref[...]
    tm = m.shape[0]
    earlier = (lax.broadcasted_iota(I32, (tm, tm), 1) < lax.broadcasted_iota(I32, (tm, tm), 0)).astype(BF16)
    excl = jnp.dot(earlier, m, preferred_element_type=F32) + carry[0:1, :]
    lane = lax.broadcasted_iota(I32, m.shape, 1)
    idx = idx_ref[...]
    out = jnp.zeros(m.shape, F32)
    for k in range(TOP_K):
        rk = jnp.sum(jnp.where(lane == idx[:, k:k + 1], excl, 0.0), axis=-1, keepdims=True)
        out = jnp.where(lane == k, rk, out)
    rank_ref[...] = out.astype(I32)
    carry[0:1, :] = carry[0:1, :] + jnp.sum(m.astype(F32), axis=0, keepdims=True)
    cnt_ref[...] = carry[...]


def route_ranks(mask, idx, tm=512):
    n, n_e = mask.shape
    blk = pl.BlockSpec((tm, n_e), lambda i: (i, 0))
    rank, cnt = pl.pallas_call(
        _rank_kernel,
        grid=(n // tm,),
        in_specs=[blk, blk],
        out_specs=[blk, pl.BlockSpec((8, n_e), lambda i: (0, 0))],
        out_shape=[jax.ShapeDtypeStruct((n, n_e), I32), jax.ShapeDtypeStruct((8, n_e), F32)],
        scratch_shapes=[pltpu.VMEM((8, n_e), F32)],
        compiler_params=_cparams(),
        name="route_ranks",
    )(mask, idx)
    return rank, cnt[0].astype(I32)


def _dest_kernel(rank_ref, idx_ref, ps_ref, dest_ref):
    idx = idx_ref[...]
    lane = lax.broadcasted_iota(I32, idx.shape, 1)
    start = jnp.zeros(idx.shape, F32)
    for k in range(TOP_K):
        sk = jnp.sum(jnp.where(lane == idx[:, k:k + 1], ps_ref[...], 0.0), axis=-1, keepdims=True)
        start = jnp.where(lane == k, sk, start)
    dest_ref[...] = rank_ref[...] + start.astype(I32)


def route_dest(rank, idx, pad_start, tm=512):
    n, n_e = rank.shape
    blk = pl.BlockSpec((tm, n_e), lambda i: (i, 0))
    dest = pl.pallas_call(
        _dest_kernel,
        grid=(n // tm,),
        in_specs=[blk, blk, pl.BlockSpec((1, n_e), lambda i: (0, 0))],
        out_specs=blk,
        out_shape=jax.ShapeDtypeStruct((n, n_e), I32),
        compiler_params=_cparams(),
        name="route_dest",
    )(rank, idx, pad_start.astype(F32).reshape(1, n_e))
    return dest[:, :TOP_K].reshape(n * TOP_K)


def _dispatch_kernel(pe_ref, pd_ref, dest_ref, h_ref, xs_ref, z_scr, sem, *, n_e, tm, blk):
    def zero_copy(e):
        start = pl.multiple_of(pe_ref[e] - blk, blk)
        return pltpu.make_async_copy(z_scr, xs_ref.at[pl.ds(start, blk), :], sem)

    @pl.when(pl.program_id(0) == 0)
    def _():
        z_scr[...] = jnp.zeros_like(z_scr)

        def start(e, _):
            @pl.when(pd_ref[e] > 0)
            def _():
                zero_copy(e).start()
            return 0

        def wait(e, _):
            @pl.when(pd_ref[e] > 0)
            def _():
                zero_copy(e).wait()
            return 0

        lax.fori_loop(0, n_e, start, 0)
        lax.fori_loop(0, n_e, wait, 0)

    def row_copy(t, k):
        return pltpu.make_async_copy(h_ref.at[pl.ds(t, 1), :], xs_ref.at[pl.ds(dest_ref[t * TOP_K + k], 1), :], sem)

    def issue(t, _):
        for k in range(TOP_K):
            row_copy(t, k).start(priority=k % 2)
        return 0

    lax.fori_loop(0, tm, issue, 0)
    rows = tm * TOP_K
    pltpu.make_async_copy(xs_ref.at[pl.ds(0, rows), :], xs_ref.at[pl.ds(0, rows), :], sem).wait()


def moe_dispatch(hp, dest_flat, pad_end, padded, slots, tm=DISPATCH_TM):
    n, dw = hp.shape
    n_e = pad_end.shape[0]
    grid_spec = pltpu.PrefetchScalarGridSpec(
        num_scalar_prefetch=2,
        grid=(n // tm,),
        in_specs=[pl.BlockSpec((tm * TOP_K,), lambda i, pe, pd: (i,), memory_space=pltpu.SMEM),
                  pl.BlockSpec((tm, dw), lambda i, pe, pd: (i, 0))],
        out_specs=pl.BlockSpec(memory_space=pl.ANY),
        scratch_shapes=[pltpu.VMEM((EXPERT_BLK, dw), I32), pltpu.SemaphoreType.DMA(())],
    )
    return pl.pallas_call(
        functools.partial(_dispatch_kernel, n_e=n_e, tm=tm, blk=EXPERT_BLK),
        grid_spec=grid_spec,
        out_shape=jax.ShapeDtypeStruct((slots, dw), I32),
        compiler_params=_cparams(),
        name="moe_dispatch",
    )(pad_end, padded, dest_flat, hp)


PACK_CHUNK = 512


def _swiglu_hidden(xp_ref, wgu):
    half = xp_ref.shape[-1]
    acc = None
    for c in range(half // PACK_CHUNK):
        cl = slice(c * PACK_CHUNK, (c + 1) * PACK_CHUNK)
        ch = slice(half + c * PACK_CHUNK, half + (c + 1) * PACK_CHUNK)
        lo, hi = _unpack_halves(xp_ref[:, cl])
        part = (jnp.dot(lo.astype(BF16), wgu(cl), preferred_element_type=F32)
                + jnp.dot(hi.astype(BF16), wgu(ch), preferred_element_type=F32))
        acc = part if acc is None else acc + part
    de = acc.shape[-1] // 2
    return (jax.nn.silu(acc[:, :de]) * acc[:, de:]).astype(BF16)


def _down_packed(hid, wd, o_ref):
    half = o_ref.shape[-1]
    for c in range(half // PACK_CHUNK):
        cl = slice(c * PACK_CHUNK, (c + 1) * PACK_CHUNK)
        ch = slice(half + c * PACK_CHUNK, half + (c + 1) * PACK_CHUNK)
        o_ref[:, cl] = _pack_halves(jnp.dot(hid, wd(cl), preferred_element_type=F32),
                                    jnp.dot(hid, wd(ch), preferred_element_type=F32))


def _expert_up_kernel(be_ref, nu_ref, x_ref, wg_ref, wu_ref, hid_ref):
    used = pl.program_id(0) < nu_ref[0]

    def wgu(sl):
        return jnp.concatenate([wg_ref[0, sl, :].astype(BF16), wu_ref[0, sl, :].astype(BF16)], axis=-1)

    @pl.when(used)
    def _():
        hid_ref[...] = _swiglu_hidden(x_ref, wgu)

    @pl.when(jnp.logical_not(used))
    def _():
        hid_ref[...] = jnp.zeros_like(hid_ref)


def _expert_down_kernel(be_ref, nu_ref, hid_ref, wd_ref, o_ref):
    used = pl.program_id(0) < nu_ref[0]

    @pl.when(used)
    def _():
        _down_packed(hid_ref[...], lambda sl: wd_ref[0, :, sl].astype(BF16), o_ref)

    @pl.when(jnp.logical_not(used))
    def _():
        o_ref[...] = jnp.zeros_like(o_ref)


def expert_ffn(xs, block_e, n_used, wg, wu, wd, blk):
    slots, dw = xs.shape
    d, de = wg.shape[1], wg.shape[2]
    n_blocks = slots // blk
    cur = lambda i, nu: jnp.minimum(i, nu[0] - 1)
    up_spec = pltpu.PrefetchScalarGridSpec(
        num_scalar_prefetch=2,
        grid=(n_blocks,),
        in_specs=[pl.BlockSpec((blk, dw), lambda i, be, nu: (cur(i, nu), 0)),
                  pl.BlockSpec((1, d, de), lambda i, be, nu: (be[cur(i, nu)], 0, 0)),
                  pl.BlockSpec((1, d, de), lambda i, be, nu: (be[cur(i, nu)], 0, 0))],
        out_specs=pl.BlockSpec((blk, de), lambda i, be, nu: (i, 0)),
    )
    hid = pl.pallas_call(
        _expert_up_kernel,
        grid_spec=up_spec,
        out_shape=jax.ShapeDtypeStruct((slots, de), BF16),
        compiler_params=_cparams(),
        name="expert_up",
    )(block_e, n_used, xs, wg, wu)
    down_spec = pltpu.PrefetchScalarGridSpec(
        num_scalar_prefetch=2,
        grid=(n_blocks,),
        in_specs=[pl.BlockSpec((blk, de), lambda i, be, nu: (cur(i, nu), 0)),
                  pl.BlockSpec((1, de, d), lambda i, be, nu: (be[cur(i, nu)], 0, 0))],
        out_specs=pl.BlockSpec((blk, dw), lambda i, be, nu: (i, 0)),
    )
    return pl.pallas_call(
        _expert_down_kernel,
        grid_spec=down_spec,
        out_shape=jax.ShapeDtypeStruct((slots, dw), I32),
        compiler_params=_cparams(),
        name="expert_down",
    )(block_e, n_used, hid, wd)


def _shared_kernel(hp_ref, wgu_ref, wd_ref, o_ref):
    hid = _swiglu_hidden(hp_ref, lambda sl: wgu_ref[sl, :])
    _down_packed(hid, lambda sl: wd_ref[:, sl], o_ref)


def shared_ffn(hp, wgu, wd, tm=512):
    n, dw = hp.shape
    d, de2 = wgu.shape
    row = pl.BlockSpec((tm, dw), lambda i: (i, 0))
    return pl.pallas_call(
        _shared_kernel,
        grid=(n // tm,),
        in_specs=[row, pl.BlockSpec((d, de2), lambda i: (0, 0)), pl.BlockSpec((de2 // 2, d), lambda i: (0, 0))],
        out_specs=row,
        out_shape=jax.ShapeDtypeStruct((n, dw), I32),
        compiler_params=_cparams(),
        name="shared_ffn",
    )(hp, wgu, wd)


def _final_kernel(dc_ref, dn_ref, x1_ref, sp_ref, gate_ref, ys_ref, gpost_ref, gt_ref, o_ref, buf, sem, *, tm, n_tiles):
    i = pl.program_id(0)
    slot = i % 2
    sub = 8

    def issue_rows(dref, s, j):
        base = pl.multiple_of(j * sub, sub)
        for tt in range(sub):
            for k in range(TOP_K):
                pltpu.make_async_copy(ys_ref.at[pl.ds(dref[(base + tt) * TOP_K + k], 1), :],
                                      buf.at[s, pl.ds(k * tm + base + tt, 1), :],
                                      sem.at[s]).start(priority=k % 2)

    def reduce_rows(s, j):
        rs = pl.ds(pl.multiple_of(j * sub, sub), sub)
        g = gate_ref[rs, :]
        acc_lo, acc_hi = _unpack_halves(sp_ref[rs, :])
        for k in range(TOP_K):
            lo, hi = _unpack_halves(buf[s, pl.ds(pl.multiple_of(k * tm + j * sub, sub), sub), :])
            gk = g[:, k:k + 1]
            acc_lo = acc_lo + gk * lo
            acc_hi = acc_hi + gk * hi
        ffn = jnp.concatenate([acc_lo, acc_hi], axis=-1)
        o_ref[rs, :] = x1_ref[rs, :] + gt_ref[0] * _rms(ffn, gpost_ref[...])

    @pl.when(i == 0)
    def _():
        def first(j, c):
            issue_rows(dc_ref, 0, j)
            return c
        lax.fori_loop(0, tm // sub, first, 0)

    def wait_slot(s):
        pltpu.make_async_copy(ys_ref.at[pl.ds(0, TOP_K * tm), :], buf.at[s], sem.at[s]).wait()

    wait_slot(slot)

    for s in range(2):
        @pl.when(slot == s)
        def _():
            def body(j, c):
                issue_rows(dn_ref, 1 - s, j)
                reduce_rows(s, j)
                return c
            lax.fori_loop(0, tm // sub, body, 0, unroll=4)

    @pl.when(i == n_tiles - 1)
    def _():
        wait_slot(1 - slot)


def final_mix(x1, sp, gates, ys, dest_flat, g_post, gt, tiles_per_batch, tm=COMBINE_TM):
    n, d = x1.shape
    dw = sp.shape[1]
    n_e = gates.shape[1]
    n_tiles = n // tm
    return pl.pallas_call(
        functools.partial(_final_kernel, tm=tm, n_tiles=n_tiles),
        grid=(n_tiles,),
        in_specs=[pl.BlockSpec((tm * TOP_K,), lambda i: (i,), memory_space=pltpu.SMEM),
                  pl.BlockSpec((tm * TOP_K,), lambda i: (jnp.minimum(i + 1, n_tiles - 1),), memory_space=pltpu.SMEM),
                  pl.BlockSpec((tm, d), lambda i: (i, 0)),
                  pl.BlockSpec((tm, dw), lambda i: (i, 0)),
                  pl.BlockSpec((tm, n_e), lambda i: (i, 0)),
                  pl.BlockSpec(memory_space=pl.ANY),
                  pl.BlockSpec((1, d), lambda i: (0, 0)),
                  pl.BlockSpec((1, 1, d), lambda i: (i // tiles_per_batch, 0, 0))],
        out_specs=pl.BlockSpec((tm, d), lambda i: (i, 0)),
        out_shape=jax.ShapeDtypeStruct((n, d), F32),
        scratch_shapes=[pltpu.VMEM((2, TOP_K * tm, dw), I32), pltpu.SemaphoreType.DMA((2,))],
        compiler_params=_cparams(),
        name="final_mix",
    )(dest_flat, dest_flat, x1, sp, gates, ys, g_post.reshape(1, d), gt)


def kernel(x, c, ctx, c_ctx, w_ada, b_ada, g_pre_mix, g_post_mix, g_pre_ffn, g_post_ffn, w_in, rpb, ssm_a_re, ssm_a_im, ssm_log_dt, ssm_b_re, ssm_b_im, ssm_c_re, ssm_c_im, ssm_d, w_glu, b_glu, g_attn_out, g_ssm_out, w_out, w_router, b_router, w_exp_gate, w_exp_up, w_exp_down, w_sh_gate, w_sh_up, w_sh_down):
    b, l, d = x.shape
    lc = ctx.shape[1]
    assert w_ada.shape[0] == 1 and b + 1 <= 8
    n_in = w_in.shape[-1]
    d_ssm = w_glu.shape[-1]
    d_attn = d - d_ssm
    n_heads = d_attn // HEAD_DIM
    n = b * l

    c8 = jnp.concatenate([c, c_ctx[None], jnp.zeros((8 - b - 1, d), F32)], axis=0)
    mod = ada_mod(c8, w_ada[0], b_ada[0]).reshape(8, 6, 1, d)
    sh_m, sc_m, gt_m, sh_f, sc_f, gt_f = [mod[:b, j] for j in range(6)]
    csh_m, csc_m = mod[b:b + 1, 0], mod[b:b + 1, 1]

    w_in_b = w_in[0].astype(BF16)
    tn = PROJ_TN
    p_lat = mod_proj(x, g_pre_mix[0], sc_m, sh_m, w_in_b, 0, n_in, 512, tn)
    p_ctx = mod_proj(ctx.reshape(1, b * lc, d), g_pre_mix[0], csc_m, csh_m, w_in_b,
                     d_attn // tn, n_in - d_attn, 512, tn).reshape(b, lc, n_in - d_attn)

    attn = neighborhood_attention(p_lat, p_ctx, rpb[0], n_heads)

    mats = _ssm_mats(ssm_a_re[0], ssm_a_im[0], ssm_log_dt[0], ssm_b_re[0], ssm_b_im[0],
                     ssm_c_re[0], ssm_c_im[0], ssm_d[0])
    y = s5_mixer(p_lat[..., 3 * d_attn:], p_ctx[..., 2 * d_attn:], mats)
    ssm = s5_glu(y.reshape(n, d_ssm), w_glu[0].astype(BF16), b_glu[0])

    mix = merge_proj(attn.reshape(n, d_attn), ssm, g_attn_out[0], g_ssm_out[0], w_out[0].astype(BF16))
    x1, hp, idx, gates, mask = post_mix(x, mix.reshape(b, l, d), g_post_mix[0], gt_m, g_pre_ffn[0], sc_f, sh_f,
                                        w_router[0].astype(BF16), b_router[0])

    n_e = w_router.shape[-1]
    m = n * TOP_K
    idx = idx.reshape(n, n_e)
    rank, counts = route_ranks(mask.reshape(n, n_e), idx)
    padded = (counts + EXPERT_BLK - 1) // EXPERT_BLK * EXPERT_BLK
    pad_end = jnp.cumsum(padded).astype(I32)
    dest = route_dest(rank, idx, pad_end - padded)
    n_blocks = m // EXPERT_BLK + n_e
    slots = n_blocks * EXPERT_BLK
    block_e = jnp.minimum(jnp.searchsorted(pad_end, jnp.arange(n_blocks) * EXPERT_BLK, side='right'),
                          n_e - 1).astype(I32)
    n_used = (pad_end[-1] // EXPERT_BLK).astype(I32).reshape(1)

    hp2 = hp.reshape(n, d // 2)
    xs = moe_dispatch(hp2, dest, pad_end, padded.astype(I32), slots)
    ys = expert_ffn(xs, block_e, n_used, w_exp_gate[0], w_exp_up[0], w_exp_down[0], EXPERT_BLK)
    sp = shared_ffn(hp2, jnp.concatenate([w_sh_gate[0], w_sh_up[0]], axis=-1).astype(BF16), w_sh_down[0].astype(BF16))
    out = final_mix(x1.reshape(n, d), sp, gates.reshape(n, n_e), ys, dest, g_post_ffn[0], gt_f, l // COMBINE_TM)
    return out.reshape(b, l, d)
```

```python
import functools

import numpy as np
import jax
import jax.numpy as jnp
from jax import lax
from jax.experimental import pallas as pl
from jax.experimental.pallas import tpu as pltpu

F32 = jnp.float32
BF16 = jnp.bfloat16
I32 = jnp.int32

GRID_W = 64
HEAD_DIM = 128
WIN_ROWS = 8
WIN_COLS = 16
ROPE_THETA = 10000.0
SSM_GROUP_CH = 16
SSM_STATE = 64
TOP_K = 8
ROUTED_SCALE = 2.5
EPS = 1e-6
NEG_INF = -1e30

Q_ROWS = 4
BAND_ROWS = Q_ROWS + WIN_ROWS - 1
VMEM_LIMIT = 56 * 1024 * 1024
EXPERT_BLK = 512
PROJ_TN = 1024
DISPATCH_TM = 256
COMBINE_TM = 128


def _cparams():
    return pltpu.CompilerParams(vmem_limit_bytes=VMEM_LIMIT)


def _rms(x, g):
    return x * lax.rsqrt(jnp.mean(x * x, axis=-1, keepdims=True) + EPS) * g


def _pack_halves(lo, hi):
    lo_bits = lax.bitcast_convert_type(lo.astype(BF16).astype(F32), I32)
    hi_bits = lax.bitcast_convert_type(hi.astype(BF16).astype(F32), I32)
    return (hi_bits & jnp.int32(-65536)) | lax.shift_right_logical(lo_bits, jnp.int32(16))


def _unpack_halves(w):
    lo = lax.bitcast_convert_type(lax.shift_left(w, jnp.int32(16)), F32)
    hi = lax.bitcast_convert_type(w & jnp.int32(-65536), F32)
    return lo, hi


def _ada_kernel(c_ref, w_ref, b_ref, o_ref):
    a = jax.nn.silu(c_ref[...]).astype(BF16)
    o_ref[...] = jnp.dot(a, w_ref[...].astype(BF16), preferred_element_type=F32) + b_ref[...]


def ada_mod(c8, w_ada, b_ada):
    d, n = w_ada.shape
    tn = 512
    return pl.pallas_call(
        _ada_kernel,
        grid=(n // tn,),
        in_specs=[pl.BlockSpec((8, d), lambda j: (0, 0)),
                  pl.BlockSpec((d, tn), lambda j: (0, j)),
                  pl.BlockSpec((1, tn), lambda j: (0, j))],
        out_specs=pl.BlockSpec((8, tn), lambda j: (0, j)),
        out_shape=jax.ShapeDtypeStruct((8, n), F32),
        compiler_params=_cparams(),
        name="ada_mod",
    )(c8, w_ada, b_ada.reshape(1, n))


def _modproj_kernel(x_ref, g_ref, sc_ref, sh_ref, w_ref, o_ref, h_scr):
    @pl.when(pl.program_id(2) == 0)
    def _():
        h = _rms(x_ref[0], g_ref[...]) * (1.0 + sc_ref[0]) + sh_ref[0]
        h_scr[...] = h.astype(BF16)

    o_ref[0] = jnp.dot(h_scr[...], w_ref[...], preferred_element_type=F32).astype(o_ref.dtype)


def mod_proj(x, g, scale, shift, w_bf16, col_blk_off, n_out, tm, tn):
    b, l, d = x.shape
    return pl.pallas_call(
        _modproj_kernel,
        grid=(b, l // tm, n_out // tn),
        in_specs=[pl.BlockSpec((1, tm, d), lambda bi, i, j: (bi, i, 0)),
                  pl.BlockSpec((1, d), lambda bi, i, j: (0, 0)),
                  pl.BlockSpec((1, 1, d), lambda bi, i, j: (bi, 0, 0)),
                  pl.BlockSpec((1, 1, d), lambda bi, i, j: (bi, 0, 0)),
                  pl.BlockSpec((d, tn), lambda bi, i, j: (0, j + col_blk_off))],
        out_specs=pl.BlockSpec((1, tm, tn), lambda bi, i, j: (bi, i, j)),
        out_shape=jax.ShapeDtypeStruct((b, l, n_out), BF16),
        scratch_shapes=[pltpu.VMEM((tm, d), BF16)],
        compiler_params=_cparams(),
        name="mod_proj",
    )(x, g.reshape(1, d), scale, shift, w_bf16)


def _rope_tables(l):
    half = HEAD_DIM // 2
    quarter = half // 2
    inv_freq = 1.0 / (ROPE_THETA ** (jnp.arange(0, half, 2, dtype=F32) / half))
    rows = (jnp.arange(l) // GRID_W).astype(F32)
    cols = (jnp.arange(l) % GRID_W).astype(F32)

    def cs(pos):
        ang = pos[:, None] * inv_freq[None, :]
        return jnp.cos(ang), jnp.sin(ang)

    cr, sr = cs(rows)
    cc, sc = cs(cols)
    zero = jnp.zeros((l, quarter), F32)
    cos = jnp.concatenate([cr, cr, cc, cc], axis=-1)
    s_lo = jnp.concatenate([-sr, zero, -sc, zero], axis=-1)
    s_hi = jnp.concatenate([zero, sr, zero, sc], axis=-1)
    return cos, s_lo, s_hi


def _attn_bias(rpb, rows_n):
    kh = min(WIN_ROWS, rows_n)
    cases = [(0, 0), (Q_ROWS, 0), (rows_n - Q_ROWS, rows_n - BAND_ROWS)]
    col = np.arange(GRID_W)
    cstart = np.clip(col - WIN_COLS // 2, 0, GRID_W - WIN_COLS)
    ok_c = (col[None, :] >= cstart[:, None]) & (col[None, :] < cstart[:, None] + WIN_COLS)
    co = np.clip(col[None, :] - col[:, None], -(WIN_COLS - 1), WIN_COLS - 1) + (WIN_COLS - 1)
    ro_all, ok_all = [], []
    for r0, sb in cases:
        qrow = r0 + np.arange(Q_ROWS)
        krow = sb + np.arange(BAND_ROWS)
        rs = np.clip(qrow - kh // 2, 0, rows_n - kh)
        ok_all.append((krow[None, :] >= rs[:, None]) & (krow[None, :] < rs[:, None] + kh))
        ro_all.append(np.clip(krow[None, :] - qrow[:, None] + (WIN_ROWS - 1), 0, 2 * WIN_ROWS - 2))
    ro_all, ok_all = np.stack(ro_all), np.stack(ok_all)
    onehot = (co[None] == np.arange(2 * WIN_COLS - 1)[:, None, None]).astype(np.float32)
    toe = jnp.einsum('hrc,cqk->hrqk', rpb.astype(F32), onehot, precision=lax.Precision.HIGHEST)
    blocks = toe[:, ro_all]
    ok = ok_all[None, :, :, :, None, None] & ok_c[None, None, None, None]
    bias = jnp.transpose(jnp.where(ok, blocks, NEG_INF), (0, 1, 2, 4, 3, 5))
    return bias.reshape(rpb.shape[0], 3, Q_ROWS * GRID_W, BAND_ROWS * GRID_W)


def _attn_kernel(q_ref, k_ref, v_ref, kc_ref, vc_ref, bias_ref, cos_ref, slo_ref, shi_ref, o_ref,
                 qr_scr, qs_scr, kr_scr, *, n_blk, rows_n):
    quarter = HEAD_DIM // 4
    qn = Q_ROWS * GRID_W
    kn = BAND_ROWS * GRID_W
    scale = HEAD_DIM ** -0.5

    def rope(x, sl):
        return (x * cos_ref[sl, :] + pltpu.roll(x, HEAD_DIM - quarter, 1) * slo_ref[sl, :]
                + pltpu.roll(x, quarter, 1) * shi_ref[sl, :])

    def rope_body(i, _):
        sl = pl.ds(pl.multiple_of(i * qn, qn), qn)
        q = q_ref[0, sl, :].astype(F32) * scale
        qs_scr[sl, :] = q.astype(BF16)
        qr_scr[sl, :] = rope(q, sl).astype(BF16)
        kr_scr[sl, :] = rope(k_ref[0, sl, :].astype(F32), sl).astype(BF16)
        return 0

    lax.fori_loop(0, n_blk, rope_body, 0)

    kc = kc_ref[0]
    vc = vc_ref[0]
    nt = (((1,), (1,)), ((), ()))

    def body(i, _):
        sb = jnp.clip(i * Q_ROWS - WIN_ROWS // 2, 0, rows_n - BAND_ROWS)
        ks = pl.ds(pl.multiple_of(sb * GRID_W, GRID_W), kn)
        qs = pl.ds(pl.multiple_of(i * qn, qn), qn)
        case = jnp.where(i == 0, 0, jnp.where(i == n_blk - 1, 2, 1))
        s = lax.dot_general(qr_scr[qs, :], kr_scr[ks, :], nt, preferred_element_type=F32) + bias_ref[0, case]
        sc = lax.dot_general(qs_scr[qs, :], kc, nt, preferred_element_type=F32)
        m = jnp.maximum(jnp.max(s, axis=-1, keepdims=True), jnp.max(sc, axis=-1, keepdims=True))
        p = jnp.exp(s - m)
        pc = jnp.exp(sc - m)
        den = jnp.sum(p, axis=-1, keepdims=True) + jnp.sum(pc, axis=-1, keepdims=True)
        o = (jnp.dot(p.astype(BF16), v_ref[0, ks, :], preferred_element_type=F32)
             + jnp.dot(pc.astype(BF16), vc, preferred_element_type=F32))
        o_ref[0, qs, :] = (o / den).astype(o_ref.dtype)
        return 0

    lax.fori_loop(0, n_blk, body, 0, unroll=2)


def neighborhood_attention(p_lat, p_ctx, rpb, n_heads):
    b, l, _ = p_lat.shape
    lc = p_ctx.shape[1]
    rows_n = l // GRID_W
    assert rows_n % Q_ROWS == 0 and rows_n >= BAND_ROWS + Q_ROWS
    n_blk = rows_n // Q_ROWS
    bias = _attn_bias(rpb, rows_n)
    cos, s_lo, s_hi = _rope_tables(l)
    qn, kn = Q_ROWS * GRID_W, BAND_ROWS * GRID_W
    h = n_heads
    tab = pl.BlockSpec((l, HEAD_DIM), lambda bi, hi: (0, 0))
    return pl.pallas_call(
        functools.partial(_attn_kernel, n_blk=n_blk, rows_n=rows_n),
        grid=(b, h),
        in_specs=[pl.BlockSpec((1, l, HEAD_DIM), lambda bi, hi: (bi, 0, hi)),
                  pl.BlockSpec((1, l, HEAD_DIM), lambda bi, hi: (bi, 0, hi + h)),
                  pl.BlockSpec((1, l, HEAD_DIM), lambda bi, hi: (bi, 0, hi + 2 * h)),
                  pl.BlockSpec((1, lc, HEAD_DIM), lambda bi, hi: (bi, 0, hi)),
                  pl.BlockSpec((1, lc, HEAD_DIM), lambda bi, hi: (bi, 0, hi + h)),
                  pl.BlockSpec((1, 3, qn, kn), lambda bi, hi: (hi, 0, 0, 0)),
                  tab, tab, tab],
        out_specs=pl.BlockSpec((1, l, HEAD_DIM), lambda bi, hi: (bi, 0, hi)),
        out_shape=jax.ShapeDtypeStruct((b, l, h * HEAD_DIM), BF16),
        scratch_shapes=[pltpu.VMEM((l, HEAD_DIM), BF16)] * 3,
        compiler_params=_cparams(),
        name="nbr_attn",
    )(p_lat, p_lat, p_lat, p_ctx, p_ctx, bias, cos, s_lo, s_hi)


SSM_CHUNK = 16
SSM_ROWS = 8
SSM_PACK = 8


def _ssm_mats(a_re, a_im, log_dt, b_re, b_im, c_re, c_im, d_skip):
    t = SSM_CHUNK
    hi = lax.Precision.HIGHEST
    a = lax.complex(a_re.astype(F32), a_im.astype(F32))
    dta = jnp.exp(log_dt.astype(F32))[..., None] * a
    a_bar = jnp.exp(dta)
    b_bar = ((a_bar - 1.0) / a)[..., None] * lax.complex(b_re.astype(F32), b_im.astype(F32))
    cm = lax.complex(c_re.astype(F32), c_im.astype(F32))
    k = jnp.arange(t + 1, dtype=F32)
    ap = jnp.exp(dta[..., None] * k)
    g, p, n = a.shape[1], a.shape[2], b_re.shape[-1]
    kern = jnp.einsum('dgnp,dgpl,dgpm->dglnm', cm, ap[..., :t], b_bar, precision=hi).real
    s_i = np.arange(t)[:, None]
    t_i = np.arange(t)[None, :]
    kf = kern[0][:, np.clip(t_i - s_i, 0, t - 1)]
    kb = kern[1][:, np.clip(s_i - t_i, 0, t - 1)]
    eye = jnp.eye(n, dtype=F32)
    m_mat = (jnp.where((s_i <= t_i)[None, :, :, None, None], kf, 0.0)
             + jnp.where((s_i >= t_i)[None, :, :, None, None], kb, 0.0)
             + (s_i == t_i)[None, :, :, None, None] * (d_skip.astype(F32)[:, None, None, :, None] * eye))
    m_mat = jnp.transpose(m_mat, (0, 1, 4, 2, 3)).reshape(g, t * n, t * n)
    wf = ap[0][:, :, t - 1::-1][..., :t, None] * b_bar[0][:, :, None, :]
    wb = ap[1][:, :, :t, None] * b_bar[1][:, :, None, :]
    to_rows = lambda z: jnp.transpose(z, (0, 2, 3, 1)).reshape(g, t * n, p)
    w_mat = jnp.concatenate([to_rows(wf.real), to_rows(wb.real), to_rows(wf.imag), to_rows(wb.imag)], axis=-1)
    zf = jnp.transpose(cm[0], (0, 2, 1))[:, :, None, :] * ap[0][:, :, 1:t + 1, None]
    zb = jnp.transpose(cm[1], (0, 2, 1))[:, :, None, :] * ap[1][:, :, t:0:-1, None]
    flat = lambda z: z.reshape(g, p, t * n)
    r_mat = jnp.concatenate([flat(zf.real), flat(zb.real), -flat(zf.imag), -flat(zb.imag)], axis=1)
    a_t = ap[..., t]
    a_vec = jnp.stack([jnp.concatenate([a_t[0].real, a_t[1].real], -1),
                       jnp.concatenate([a_t[0].imag, a_t[1].imag], -1)], axis=1)
    a_vec = jnp.pad(a_vec, ((0, 0), (0, SSM_ROWS - 2), (0, 0)))
    return m_mat.astype(BF16), w_mat.astype(BF16), r_mat.astype(BF16), a_vec


def _s5_kernel(u_ref, sel_ref, selt_ref, m_ref, w_ref, r_ref, a_ref, y_ref, v_scr, ent_scr, *, n_ctx, n_chunks):
    u = jnp.dot(u_ref[0], sel_ref[0], preferred_element_type=F32).astype(BF16)
    p2 = a_ref.shape[-1]
    p = p2 // 2
    v_scr[...] = jnp.dot(u, w_ref[0], preferred_element_type=F32)
    a_re = jnp.broadcast_to(a_ref[0, 0:1, :], (SSM_ROWS, p2))
    a_im = jnp.broadcast_to(a_ref[0, 1:2, :], (SSM_ROWS, p2))
    is_fwd = lax.broadcasted_iota(I32, (SSM_ROWS, p2), 1) < p

    def step(j, carry):
        s_re, s_im = carry
        cb = jnp.where(j < n_ctx, n_ctx - 1 - j, n_chunks + n_ctx - 1 - j)
        rf = pl.ds(pl.multiple_of(j * SSM_ROWS, SSM_ROWS), SSM_ROWS)
        rb = pl.ds(pl.multiple_of(cb * SSM_ROWS, SSM_ROWS), SSM_ROWS)
        ent_scr[rf, 0:p] = s_re[:, :p]
        ent_scr[rb, p:p2] = s_re[:, p:]
        ent_scr[rf, p2:p2 + p] = s_im[:, :p]
        ent_scr[rb, p2 + p:] = s_im[:, p:]
        in_re = jnp.where(is_fwd, v_scr[rf, :p2], v_scr[rb, :p2])
        in_im = jnp.where(is_fwd, v_scr[rf, p2:], v_scr[rb, p2:])
        return a_re * s_re - a_im * s_im + in_re, a_re * s_im + a_im * s_re + in_im

    zero = jnp.zeros((SSM_ROWS, p2), F32)
    lax.fori_loop(0, n_chunks, step, (zero, zero))
    y = (jnp.dot(u, m_ref[0], preferred_element_type=F32)
         + jnp.dot(ent_scr[...].astype(BF16), r_ref[0], preferred_element_type=F32)).astype(BF16)
    def part(c):
        return jnp.dot(y, selt_ref[0, :, c:c + PACK_CHUNK], preferred_element_type=F32).astype(y_ref.dtype)

    @pl.when(pl.program_id(1) == 0)
    def _():
        for c in range(0, y_ref.shape[-1], PACK_CHUNK):
            y_ref[0, :, c:c + PACK_CHUNK] = part(c)

    @pl.when(pl.program_id(1) > 0)
    def _():
        for c in range(0, y_ref.shape[-1], PACK_CHUNK):
            y_ref[0, :, c:c + PACK_CHUNK] = y_ref[0, :, c:c + PACK_CHUNK] + part(c)


def s5_mixer(u_lat, u_ctx, mats):
    m_mat, w_mat, r_mat, a_vec = mats
    b, l, dch = u_lat.shape
    lc = u_ctx.shape[1]
    g = m_mat.shape[0]
    n = dch // g
    t = SSM_CHUNK
    tn = t * n
    gp = SSM_PACK
    packs, lanes = g // gp, gp * n
    n_ctx, n_chunks = lc // t, (lc + l) // t
    assert b <= SSM_ROWS and lc % t == 0 and l % t == 0 and g % gp == 0
    z = jnp.concatenate([u_ctx, u_lat], axis=1).reshape(b, n_chunks, t, packs, lanes)
    z = jnp.pad(jnp.transpose(z, (3, 1, 0, 2, 4)), ((0, 0), (0, 0), (0, SSM_ROWS - b), (0, 0), (0, 0)))
    rows = n_chunks * SSM_ROWS
    z = z.reshape(packs, rows, t * lanes)
    ri = jnp.arange(t * lanes)
    ci = jnp.arange(tn)
    same = (ri[:, None] // lanes == ci[None, :] // n) & (ri[:, None] % n == ci[None, :] % n)
    sel = (same[None] & ((ri[None, :, None] % lanes) // n == jnp.arange(gp)[:, None, None])).astype(BF16)
    selt = jnp.transpose(sel, (0, 2, 1))
    grp = lambda k: pl.BlockSpec((1, k, tn), lambda pi, qi: (pi * gp + qi, 0, 0))
    pack_blk = pl.BlockSpec((1, rows, t * lanes), lambda pi, qi: (pi, 0, 0))
    y = pl.pallas_call(
        functools.partial(_s5_kernel, n_ctx=n_ctx, n_chunks=n_chunks),
        grid=(packs, gp),
        in_specs=[pack_blk,
                  pl.BlockSpec((1, t * lanes, tn), lambda pi, qi: (qi, 0, 0)),
                  pl.BlockSpec((1, tn, t * lanes), lambda pi, qi: (qi, 0, 0)),
                  grp(tn), grp(tn), grp(r_mat.shape[1]),
                  pl.BlockSpec((1, SSM_ROWS, a_vec.shape[-1]), lambda pi, qi: (pi * gp + qi, 0, 0))],
        out_specs=pack_blk,
        out_shape=jax.ShapeDtypeStruct((packs, rows, t * lanes), BF16),
        scratch_shapes=[pltpu.VMEM((rows, w_mat.shape[-1]), F32), pltpu.VMEM((rows, r_mat.shape[1]), F32)],
        compiler_params=_cparams(),
        name="s5_mixer",
    )(z, sel, selt, m_mat, w_mat, r_mat, a_vec)
    y = y.reshape(packs, n_chunks, SSM_ROWS, t, lanes)[:, n_ctx:, :b]
    return jnp.transpose(y, (2, 1, 3, 0, 4)).reshape(b, l, dch)


def _glu_kernel(y_ref, w_ref, b_ref, o_ref):
    z = jax.nn.gelu(y_ref[...].astype(F32))
    gate = jax.nn.sigmoid(jnp.dot(z.astype(BF16), w_ref[...], preferred_element_type=F32) + b_ref[...])
    o_ref[...] = (z * gate).astype(o_ref.dtype)


def s5_glu(y, w_bf16, b_glu, tm=512):
    n, d = y.shape
    return pl.pallas_call(
        _glu_kernel,
        grid=(n // tm,),
        in_specs=[pl.BlockSpec((tm, d), lambda i: (i, 0)),
                  pl.BlockSpec((d, d), lambda i: (0, 0)),
                  pl.BlockSpec((1, d), lambda i: (0, 0))],
        out_specs=pl.BlockSpec((tm, d), lambda i: (i, 0)),
        out_shape=jax.ShapeDtypeStruct((n, d), BF16),
        compiler_params=_cparams(),
        name="s5_glu",
    )(y, w_bf16, b_glu.reshape(1, d))


def _merge_kernel(a_ref, s_ref, ga_ref, gs_ref, w_ref, o_ref, h_scr):
    da = a_ref.shape[-1]

    @pl.when(pl.program_id(1) == 0)
    def _():
        h_scr[:, :da] = _rms(a_ref[...].astype(F32), ga_ref[...]).astype(BF16)
        h_scr[:, da:] = _rms(s_ref[...].astype(F32), gs_ref[...]).astype(BF16)

    o_ref[...] = jnp.dot(h_scr[...], w_ref[...], preferred_element_type=F32)


def merge_proj(attn, ssm, g_attn, g_ssm, w_bf16, tm=512, tn=1024):
    n, da = attn.shape
    ds = ssm.shape[1]
    d, dout = w_bf16.shape
    return pl.pallas_call(
        _merge_kernel,
        grid=(n // tm, dout // tn),
        in_specs=[pl.BlockSpec((tm, da), lambda i, j: (i, 0)),
                  pl.BlockSpec((tm, ds), lambda i, j: (i, 0)),
                  pl.BlockSpec((1, da), lambda i, j: (0, 0)),
                  pl.BlockSpec((1, ds), lambda i, j: (0, 0)),
                  pl.BlockSpec((d, tn), lambda i, j: (0, j))],
        out_specs=pl.BlockSpec((tm, tn), lambda i, j: (i, j)),
        out_shape=jax.ShapeDtypeStruct((n, dout), F32),
        scratch_shapes=[pltpu.VMEM((tm, d), BF16)],
        compiler_params=_cparams(),
        name="merge_proj",
    )(attn, ssm, g_attn.reshape(1, da), g_ssm.reshape(1, ds), w_bf16)


def _post_mix_kernel(x_ref, mix_ref, gpost_ref, gt_ref, gpre_ref, sc_ref, sh_ref, wr_ref, br_ref,
                     x1_ref, hp_ref, idx_ref, gate_ref, mask_ref):
    x1 = x_ref[0] + gt_ref[0] * _rms(mix_ref[0], gpost_ref[...])
    x1_ref[0] = x1
    h = _rms(x1, gpre_ref[...]) * (1.0 + sc_ref[0]) + sh_ref[0]
    half = h.shape[-1] // 2
    hp_ref[0] = _pack_halves(h[:, :half], h[:, half:])
    scores = jax.nn.sigmoid(jnp.dot(h.astype(BF16), wr_ref[...], preferred_element_type=F32))
    n_e = scores.shape[-1]
    lane = lax.broadcasted_iota(I32, scores.shape, 1)
    biased = scores + br_ref[...]
    idx_out = jnp.zeros(scores.shape, I32)
    sel_out = jnp.zeros(scores.shape, F32)
    mask = jnp.zeros(scores.shape, jnp.bool_)
    for k in range(TOP_K):
        m = jnp.max(biased, axis=-1, keepdims=True)
        ik = jnp.min(jnp.where(biased == m, lane, n_e), axis=-1, keepdims=True)
        hit = lane == ik
        sel_k = jnp.sum(jnp.where(hit, scores, 0.0), axis=-1, keepdims=True)
        idx_out = jnp.where(lane == k, ik, idx_out)
        sel_out = jnp.where(lane == k, sel_k, sel_out)
        mask = jnp.logical_or(mask, hit)
        biased = jnp.where(hit, -jnp.inf, biased)
    idx_ref[0] = idx_out
    gate_ref[0] = sel_out / jnp.sum(sel_out, axis=-1, keepdims=True) * ROUTED_SCALE
    mask_ref[0] = mask.astype(BF16)


def post_mix(x, mix, g_post, gt, g_pre, scale, shift, wr_bf16, b_router, tm=256):
    b, l, d = x.shape
    n_e = wr_bf16.shape[1]
    row = pl.BlockSpec((1, tm, d), lambda bi, i: (bi, i, 0))
    prow = pl.BlockSpec((1, tm, d // 2), lambda bi, i: (bi, i, 0))
    vec = pl.BlockSpec((1, d), lambda bi, i: (0, 0))
    bvec = pl.BlockSpec((1, 1, d), lambda bi, i: (bi, 0, 0))
    small = pl.BlockSpec((1, tm, n_e), lambda bi, i: (bi, i, 0))
    return pl.pallas_call(
        _post_mix_kernel,
        grid=(b, l // tm),
        in_specs=[row, row, vec, bvec, vec, bvec, bvec,
                  pl.BlockSpec((d, n_e), lambda bi, i: (0, 0)),
                  pl.BlockSpec((1, n_e), lambda bi, i: (0, 0))],
        out_specs=[row, prow, small, small, small],
        out_shape=[jax.ShapeDtypeStruct((b, l, d), F32),
                   jax.ShapeDtypeStruct((b, l, d // 2), I32),
                   jax.ShapeDtypeStruct((b, l, n_e), I32),
                   jax.ShapeDtypeStruct((b, l, n_e), F32),
                   jax.ShapeDtypeStruct((b, l, n_e), BF16)],
        compiler_params=_cparams(),
        name="post_mix",
    )(x, mix, g_post.reshape(1, d), gt, g_pre.reshape(1, d), scale, shift, wr_bf16, b_router.reshape(1, n_e))


def _rank_kernel(mask_ref, idx_ref, rank_ref, cnt_ref, carry):
    @pl.when(pl.program_id(0) == 0)
    def _():
        carry[...] = jnp.zeros_like(carry)

    m = mask_ref[...]
    tm = m.shape[0]
    earlier = (lax.broadcasted_iota(I32, (tm, tm), 1) < lax.broadcasted_iota(I32, (tm, tm), 0)).astype(BF16)
    excl = jnp.dot(earlier, m, preferred_element_type=F32) + carry[0:1, :]
    lane = lax.broadcasted_iota(I32, m.shape, 1)
    idx = idx_ref[...]
    out = jnp.zeros(m.shape, F32)
    for k in range(TOP_K):
        rk = jnp.sum(jnp.where(lane == idx[:, k:k + 1], excl, 0.0), axis=-1, keepdims=True)
        out = jnp.where(lane == k, rk, out)
    rank_ref[...] = out.astype(I32)
    carry[0:1, :] = carry[0:1, :] + jnp.sum(m.astype(F32), axis=0, keepdims=True)
    cnt_ref[...] = carry[...]


def route_ranks(mask, idx, tm=512):
    n, n_e = mask.shape
    blk = pl.BlockSpec((tm, n_e), lambda i: (i, 0))
    rank, cnt = pl.pallas_call(
        _rank_kernel,
        grid=(n // tm,),
        in_specs=[blk, blk],
        out_specs=[blk, pl.BlockSpec((8, n_e), lambda i: (0, 0))],
        out_shape=[jax.ShapeDtypeStruct((n, n_e), I32), jax.ShapeDtypeStruct((8, n_e), F32)],
        scratch_shapes=[pltpu.VMEM((8, n_e), F32)],
        compiler_params=_cparams(),
        name="route_ranks",
    )(mask, idx)
    return rank, cnt[0].astype(I32)


def _dest_kernel(rank_ref, idx_ref, ps_ref, dest_ref):
    idx = idx_ref[...]
    lane = lax.broadcasted_iota(I32, idx.shape, 1)
    start = jnp.zeros(idx.shape, F32)
    for k in range(TOP_K):
        sk = jnp.sum(jnp.where(lane == idx[:, k:k + 1], ps_ref[...], 0.0), axis=-1, keepdims=True)
        start = jnp.where(lane == k, sk, start)
    dest_ref[...] = rank_ref[...] + start.astype(I32)


def route_dest(rank, idx, pad_start, tm=512):
    n, n_e = rank.shape
    blk = pl.BlockSpec((tm, n_e), lambda i: (i, 0))
    dest = pl.pallas_call(
        _dest_kernel,
        grid=(n // tm,),
        in_specs=[blk, blk, pl.BlockSpec((1, n_e), lambda i: (0, 0))],
        out_specs=blk,
        out_shape=jax.ShapeDtypeStruct((n, n_e), I32),
        compiler_params=_cparams(),
        name="route_dest",
    )(rank, idx, pad_start.astype(F32).reshape(1, n_e))
    return dest[:, :TOP_K].reshape(n * TOP_K)


def _dispatch_kernel(pe_ref, pd_ref, dest_ref, h_ref, xs_ref, z_scr, sem, *, n_e, tm, blk):
    def zero_copy(e):
        start = pl.multiple_of(pe_ref[e] - blk, blk)
        return pltpu.make_async_copy(z_scr, xs_ref.at[pl.ds(start, blk), :], sem)

    @pl.when(pl.program_id(0) == 0)
    def _():
        z_scr[...] = jnp.zeros_like(z_scr)

        def start(e, _):
            @pl.when(pd_ref[e] > 0)
            def _():
                zero_copy(e).start()
            return 0

        def wait(e, _):
            @pl.when(pd_ref[e] > 0)
            def _():
                zero_copy(e).wait()
            return 0

        lax.fori_loop(0, n_e, start, 0)
        lax.fori_loop(0, n_e, wait, 0)

    def row_copy(t, k):
        return pltpu.make_async_copy(h_ref.at[pl.ds(t, 1), :], xs_ref.at[pl.ds(dest_ref[t * TOP_K + k], 1), :], sem)

    def issue(t, _):
        for k in range(TOP_K):
            row_copy(t, k).start(priority=k % 2)
        return 0

    lax.fori_loop(0, tm, issue, 0)
    rows = tm * TOP_K
    pltpu.make_async_copy(xs_ref.at[pl.ds(0, rows), :], xs_ref.at[pl.ds(0, rows), :], sem).wait()


def moe_dispatch(hp, dest_flat, pad_end, padded, slots, tm=DISPATCH_TM):
    n, dw = hp.shape
    n_e = pad_end.shape[0]
    grid_spec = pltpu.PrefetchScalarGridSpec(
        num_scalar_prefetch=2,
        grid=(n // tm,),
        in_specs=[pl.BlockSpec((tm * TOP_K,), lambda i, pe, pd: (i,), memory_space=pltpu.SMEM),
                  pl.BlockSpec((tm, dw), lambda i, pe, pd: (i, 0))],
        out_specs=pl.BlockSpec(memory_space=pl.ANY),
        scratch_shapes=[pltpu.VMEM((EXPERT_BLK, dw), I32), pltpu.SemaphoreType.DMA(())],
    )
    return pl.pallas_call(
        functools.partial(_dispatch_kernel, n_e=n_e, tm=tm, blk=EXPERT_BLK),
        grid_spec=grid_spec,
        out_shape=jax.ShapeDtypeStruct((slots, dw), I32),
        compiler_params=_cparams(),
        name="moe_dispatch",
    )(pad_end, padded, dest_flat, hp)


PACK_CHUNK = 512


def _swiglu_hidden(xp_ref, wgu):
    half = xp_ref.shape[-1]
    acc = None
    for c in range(half // PACK_CHUNK):
        cl = slice(c * PACK_CHUNK, (c + 1) * PACK_CHUNK)
        ch = slice(half + c * PACK_CHUNK, half + (c + 1) * PACK_CHUNK)
        lo, hi = _unpack_halves(xp_ref[:, cl])
        part = (jnp.dot(lo.astype(BF16), wgu(cl), preferred_element_type=F32)
                + jnp.dot(hi.astype(BF16), wgu(ch), preferred_element_type=F32))
        acc = part if acc is None else acc + part
    de = acc.shape[-1] // 2
    return (jax.nn.silu(acc[:, :de]) * acc[:, de:]).astype(BF16)


def _down_packed(hid, wd, o_ref):
    half = o_ref.shape[-1]
    for c in range(half // PACK_CHUNK):
        cl = slice(c * PACK_CHUNK, (c + 1) * PACK_CHUNK)
        ch = slice(half + c * PACK_CHUNK, half + (c + 1) * PACK_CHUNK)
        o_ref[:, cl] = _pack_halves(jnp.dot(hid, wd(cl), preferred_element_type=F32),
                                    jnp.dot(hid, wd(ch), preferred_element_type=F32))


def _expert_weights(sched, i, w_hbm, w_buf, sem):
    be_ref, first_ref, slot_ref, nxt_ref = sched
    s = slot_ref[i]

    def copies(e, sl):
        return [pltpu.make_async_copy(h.at[e], v.at[sl], sem.at[j, sl]) for j, (h, v) in enumerate(zip(w_hbm, w_buf))]

    @pl.when(first_ref[i] == 1)
    def _():
        @pl.when(i == 0)
        def _():
            for c in copies(be_ref[0], 0):
                c.start()

        for c in copies(be_ref[i], s):
            c.wait()

        @pl.when(nxt_ref[i] >= 0)
        def _():
            for c in copies(nxt_ref[i], 1 - s):
                c.start()

    return s


def _expert_up_kernel(be_ref, nu_ref, first_ref, slot_ref, nxt_ref, x_ref, wg_hbm, wu_hbm, hid_ref,
                      wg_buf, wu_buf, sem):
    i = pl.program_id(0)
    used = i < nu_ref[0]
    s = _expert_weights((be_ref, first_ref, slot_ref, nxt_ref), i, (wg_hbm, wu_hbm), (wg_buf, wu_buf), sem)

    def wgu(sl):
        return jnp.concatenate([wg_buf[s, sl, :].astype(BF16), wu_buf[s, sl, :].astype(BF16)], axis=-1)

    @pl.when(used)
    def _():
        hid_ref[...] = _swiglu_hidden(x_ref, wgu)

    @pl.when(jnp.logical_not(used))
    def _():
        hid_ref[...] = jnp.zeros_like(hid_ref)


def _expert_down_kernel(be_ref, nu_ref, first_ref, slot_ref, nxt_ref, hid_ref, wd_hbm, o_ref, wd_buf, sem):
    i = pl.program_id(0)
    used = i < nu_ref[0]
    s = _expert_weights((be_ref, first_ref, slot_ref, nxt_ref), i, (wd_hbm,), (wd_buf,), sem)

    @pl.when(used)
    def _():
        _down_packed(hid_ref[...], lambda sl: wd_buf[s, :, sl].astype(BF16), o_ref)

    @pl.when(jnp.logical_not(used))
    def _():
        o_ref[...] = jnp.zeros_like(o_ref)


def _expert_schedule(block_e, n_used):
    nb = block_e.shape[0]
    ar = jnp.arange(nb, dtype=I32)
    used = ar < n_used[0]
    first = used & ((ar == 0) | (block_e != jnp.roll(block_e, 1)))
    slot = jnp.where(used, (jnp.cumsum(first.astype(I32)) - 1) % 2, 0).astype(I32)
    nf = jnp.where(first, ar, nb)
    nxt_idx = jnp.concatenate([lax.cummin(nf, axis=0, reverse=True)[1:], jnp.full((1,), nb, I32)])
    nxt = jnp.where(nxt_idx < nb, block_e[jnp.minimum(nxt_idx, nb - 1)], -1).astype(I32)
    return first.astype(I32), slot, nxt


def expert_ffn(xs, block_e, n_used, wg, wu, wd, blk):
    slots, dw = xs.shape
    d, de = wg.shape[1], wg.shape[2]
    n_blocks = slots // blk
    sched = (block_e, n_used) + _expert_schedule(block_e, n_used)
    cur = lambda i, nu: jnp.minimum(i, nu[0] - 1)
    hbm = pl.BlockSpec(memory_space=pl.ANY)
    up_spec = pltpu.PrefetchScalarGridSpec(
        num_scalar_prefetch=5,
        grid=(n_blocks,),
        in_specs=[pl.BlockSpec((blk, dw), lambda i, be, nu, *_: (cur(i, nu), 0)), hbm, hbm],
        out_specs=pl.BlockSpec((blk, de), lambda i, *_: (i, 0)),
        scratch_shapes=[pltpu.VMEM((2, d, de), F32), pltpu.VMEM((2, d, de), F32), pltpu.SemaphoreType.DMA((2, 2))],
    )
    hid = pl.pallas_call(
        _expert_up_kernel,
        grid_spec=up_spec,
        out_shape=jax.ShapeDtypeStruct((slots, de), BF16),
        compiler_params=_cparams(),
        name="expert_up",
    )(*sched, xs, wg, wu)
    down_spec = pltpu.PrefetchScalarGridSpec(
        num_scalar_prefetch=5,
        grid=(n_blocks,),
        in_specs=[pl.BlockSpec((blk, de), lambda i, be, nu, *_: (cur(i, nu), 0)), hbm],
        out_specs=pl.BlockSpec((blk, dw), lambda i, *_: (i, 0)),
        scratch_shapes=[pltpu.VMEM((2, de, d), F32), pltpu.SemaphoreType.DMA((1, 2))],
    )
    return pl.pallas_call(
        _expert_down_kernel,
        grid_spec=down_spec,
        out_shape=jax.ShapeDtypeStruct((slots, dw), I32),
        compiler_params=_cparams(),
        name="expert_down",
    )(*sched, hid, wd)


def _shared_kernel(hp_ref, wgu_ref, wd_ref, o_ref):
    hid = _swiglu_hidden(hp_ref, lambda sl: wgu_ref[sl, :])
    _down_packed(hid, lambda sl: wd_ref[:, sl], o_ref)


def shared_ffn(hp, wgu, wd, tm=512):
    n, dw = hp.shape
    d, de2 = wgu.shape
    row = pl.BlockSpec((tm, dw), lambda i: (i, 0))
    return pl.pallas_call(
        _shared_kernel,
        grid=(n // tm,),
        in_specs=[row, pl.BlockSpec((d, de2), lambda i: (0, 0)), pl.BlockSpec((de2 // 2, d), lambda i: (0, 0))],
        out_specs=row,
        out_shape=jax.ShapeDtypeStruct((n, dw), I32),
        compiler_params=_cparams(),
        name="shared_ffn",
    )(hp, wgu, wd)


def _final_kernel(dc_ref, dn_ref, x1_ref, sp_ref, gate_ref, ys_ref, gpost_ref, gt_ref, o_ref, buf, sem, *, tm, n_tiles):
    i = pl.program_id(0)
    slot = i % 2
    sub = 8

    def issue_rows(dref, s, j):
        base = pl.multiple_of(j * sub, sub)
        for tt in range(sub):
            for k in range(TOP_K):
                pltpu.make_async_copy(ys_ref.at[pl.ds(dref[(base + tt) * TOP_K + k], 1), :],
                                      buf.at[s, pl.ds(k * tm + base + tt, 1), :],
                                      sem.at[s]).start(priority=k % 2)

    def reduce_rows(s, j):
        rs = pl.ds(pl.multiple_of(j * sub, sub), sub)
        g = gate_ref[rs, :]
        acc_lo, acc_hi = _unpack_halves(sp_ref[rs, :])
        for k in range(TOP_K):
            lo, hi = _unpack_halves(buf[s, pl.ds(pl.multiple_of(k * tm + j * sub, sub), sub), :])
            gk = g[:, k:k + 1]
            acc_lo = acc_lo + gk * lo
            acc_hi = acc_hi + gk * hi
        ffn = jnp.concatenate([acc_lo, acc_hi], axis=-1)
        o_ref[rs, :] = x1_ref[rs, :] + gt_ref[0] * _rms(ffn, gpost_ref[...])

    @pl.when(i == 0)
    def _():
        def first(j, c):
            issue_rows(dc_ref, 0, j)
            return c
        lax.fori_loop(0, tm // sub, first, 0)

    def wait_slot(s):
        pltpu.make_async_copy(ys_ref.at[pl.ds(0, TOP_K * tm), :], buf.at[s], sem.at[s]).wait()

    wait_slot(slot)

    for s in range(2):
        @pl.when(slot == s)
        def _():
            def body(j, c):
                issue_rows(dn_ref, 1 - s, j)
                reduce_rows(s, j)
                return c
            lax.fori_loop(0, tm // sub, body, 0, unroll=4)

    @pl.when(i == n_tiles - 1)
    def _():
        wait_slot(1 - slot)


def final_mix(x1, sp, gates, ys, dest_flat, g_post, gt, tiles_per_batch, tm=COMBINE_TM):
    n, d = x1.shape
    dw = sp.shape[1]
    n_e = gates.shape[1]
    n_tiles = n // tm
    return pl.pallas_call(
        functools.partial(_final_kernel, tm=tm, n_tiles=n_tiles),
        grid=(n_tiles,),
        in_specs=[pl.BlockSpec((tm * TOP_K,), lambda i: (i,), memory_space=pltpu.SMEM),
                  pl.BlockSpec((tm * TOP_K,), lambda i: (jnp.minimum(i + 1, n_tiles - 1),), memory_space=pltpu.SMEM),
                  pl.BlockSpec((tm, d), lambda i: (i, 0)),
                  pl.BlockSpec((tm, dw), lambda i: (i, 0)),
                  pl.BlockSpec((tm, n_e), lambda i: (i, 0)),
                  pl.BlockSpec(memory_space=pl.ANY),
                  pl.BlockSpec((1, d), lambda i: (0, 0)),
                  pl.BlockSpec((1, 1, d), lambda i: (i // tiles_per_batch, 0, 0))],
        out_specs=pl.BlockSpec((tm, d), lambda i: (i, 0)),
        out_shape=jax.ShapeDtypeStruct((n, d), F32),
        scratch_shapes=[pltpu.VMEM((2, TOP_K * tm, dw), I32), pltpu.SemaphoreType.DMA((2,))],
        compiler_params=_cparams(),
        name="final_mix",
    )(dest_flat, dest_flat, x1, sp, gates, ys, g_post.reshape(1, d), gt)


def kernel(x, c, ctx, c_ctx, w_ada, b_ada, g_pre_mix, g_post_mix, g_pre_ffn, g_post_ffn, w_in, rpb, ssm_a_re, ssm_a_im, ssm_log_dt, ssm_b_re, ssm_b_im, ssm_c_re, ssm_c_im, ssm_d, w_glu, b_glu, g_attn_out, g_ssm_out, w_out, w_router, b_router, w_exp_gate, w_exp_up, w_exp_down, w_sh_gate, w_sh_up, w_sh_down):
    b, l, d = x.shape
    lc = ctx.shape[1]
    assert w_ada.shape[0] == 1 and b + 1 <= 8
    n_in = w_in.shape[-1]
    d_ssm = w_glu.shape[-1]
    d_attn = d - d_ssm
    n_heads = d_attn // HEAD_DIM
    n = b * l

    c8 = jnp.concatenate([c, c_ctx[None], jnp.zeros((8 - b - 1, d), F32)], axis=0)
    mod = ada_mod(c8, w_ada[0], b_ada[0]).reshape(8, 6, 1, d)
    sh_m, sc_m, gt_m, sh_f, sc_f, gt_f = [mod[:b, j] for j in range(6)]
    csh_m, csc_m = mod[b:b + 1, 0], mod[b:b + 1, 1]

    w_in_b = w_in[0].astype(BF16)
    tn = PROJ_TN
    p_lat = mod_proj(x, g_pre_mix[0], sc_m, sh_m, w_in_b, 0, n_in, 512, tn)
    p_ctx = mod_proj(ctx.reshape(1, b * lc, d), g_pre_mix[0], csc_m, csh_m, w_in_b,
                     d_attn // tn, n_in - d_attn, 512, tn).reshape(b, lc, n_in - d_attn)

    attn = neighborhood_attention(p_lat, p_ctx, rpb[0], n_heads)

    mats = _ssm_mats(ssm_a_re[0], ssm_a_im[0], ssm_log_dt[0], ssm_b_re[0], ssm_b_im[0],
                     ssm_c_re[0], ssm_c_im[0], ssm_d[0])
    y = s5_mixer(p_lat[..., 3 * d_attn:], p_ctx[..., 2 * d_attn:], mats)
    ssm = s5_glu(y.reshape(n, d_ssm), w_glu[0].astype(BF16), b_glu[0])

    mix = merge_proj(attn.reshape(n, d_attn), ssm, g_attn_out[0], g_ssm_out[0], w_out[0].astype(BF16))
    x1, hp, idx, gates, mask = post_mix(x, mix.reshape(b, l, d), g_post_mix[0], gt_m, g_pre_ffn[0], sc_f, sh_f,
                                        w_router[0].astype(BF16), b_router[0])

    n_e = w_router.shape[-1]
    m = n * TOP_K
    idx = idx.reshape(n, n_e)
    rank, counts = route_ranks(mask.reshape(n, n_e), idx)
    padded = (counts + EXPERT_BLK - 1) // EXPERT_BLK * EXPERT_BLK
    pad_end = jnp.cumsum(padded).astype(I32)
    dest = route_dest(rank, idx, pad_end - padded)
    n_blocks = m // EXPERT_BLK + n_e
    slots = n_blocks * EXPERT_BLK
    block_e = jnp.minimum(jnp.searchsorted(pad_end, jnp.arange(n_blocks) * EXPERT_BLK, side='right'),
                          n_e - 1).astype(I32)
    n_used = (pad_end[-1] // EXPERT_BLK).astype(I32).reshape(1)

    hp2 = hp.reshape(n, d // 2)
    xs = moe_dispatch(hp2, dest, pad_end, padded.astype(I32), slots)
    ys = expert_ffn(xs, block_e, n_used, w_exp_gate[0], w_exp_up[0], w_exp_down[0], EXPERT_BLK)
    sp = shared_ffn(hp2, jnp.concatenate([w_sh_gate[0], w_sh_up[0]], axis=-1).astype(BF16), w_sh_down[0].astype(BF16))
    out = final_mix(x1.reshape(n, d), sp, gates.reshape(n, n_e), ys, dest, g_post_ffn[0], gt_f, l // COMBINE_TM)
    return out.reshape(b, l, d)
```

```python
import functools

import numpy as np
import jax
import jax.numpy as jnp
from jax import lax
from jax.experimental import pallas as pl
from jax.experimental.pallas import tpu as pltpu

F32 = jnp.float32
BF16 = jnp.bfloat16
I32 = jnp.int32

GRID_W = 64
HEAD_DIM = 128
WIN_ROWS = 8
WIN_COLS = 16
ROPE_THETA = 10000.0
SSM_GROUP_CH = 16
SSM_STATE = 64
TOP_K = 8
ROUTED_SCALE = 2.5
EPS = 1e-6
NEG_INF = -1e30

Q_ROWS = 4
BAND_ROWS = Q_ROWS + WIN_ROWS - 1
VMEM_LIMIT = 56 * 1024 * 1024
EXPERT_BLK = 512
PROJ_TN = 1024
DISPATCH_TM = 512
COMBINE_TM = 128


def _cparams():
    return pltpu.CompilerParams(vmem_limit_bytes=VMEM_LIMIT)


def _rms(x, g):
    return x * lax.rsqrt(jnp.mean(x * x, axis=-1, keepdims=True) + EPS) * g


def _pack_halves(lo, hi):
    lo_bits = lax.bitcast_convert_type(lo.astype(BF16).astype(F32), I32)
    hi_bits = lax.bitcast_convert_type(hi.astype(BF16).astype(F32), I32)
    return (hi_bits & jnp.int32(-65536)) | lax.shift_right_logical(lo_bits, jnp.int32(16))


def _unpack_halves(w):
    lo = lax.bitcast_convert_type(lax.shift_left(w, jnp.int32(16)), F32)
    hi = lax.bitcast_convert_type(w & jnp.int32(-65536), F32)
    return lo, hi


def _ada_kernel(c_ref, w_ref, b_ref, o_ref):
    a = jax.nn.silu(c_ref[...]).astype(BF16)
    o_ref[...] = jnp.dot(a, w_ref[...].astype(BF16), preferred_element_type=F32) + b_ref[...]


def ada_mod(c8, w_ada, b_ada):
    d, n = w_ada.shape
    tn = 512
    return pl.pallas_call(
        _ada_kernel,
        grid=(n // tn,),
        in_specs=[pl.BlockSpec((8, d), lambda j: (0, 0)),
                  pl.BlockSpec((d, tn), lambda j: (0, j)),
                  pl.BlockSpec((1, tn), lambda j: (0, j))],
        out_specs=pl.BlockSpec((8, tn), lambda j: (0, j)),
        out_shape=jax.ShapeDtypeStruct((8, n), F32),
        compiler_params=_cparams(),
        name="ada_mod",
    )(c8, w_ada, b_ada.reshape(1, n))


def _modproj_kernel(x_ref, g_ref, sc_ref, sh_ref, w_ref, o_ref, h_scr):
    @pl.when(pl.program_id(2) == 0)
    def _():
        h = _rms(x_ref[0], g_ref[...]) * (1.0 + sc_ref[0]) + sh_ref[0]
        h_scr[...] = h.astype(BF16)

    o_ref[0] = jnp.dot(h_scr[...], w_ref[...], preferred_element_type=F32).astype(o_ref.dtype)


def mod_proj(x, g, scale, shift, w_bf16, col_blk_off, n_out, tm, tn):
    b, l, d = x.shape
    return pl.pallas_call(
        _modproj_kernel,
        grid=(b, l // tm, n_out // tn),
        in_specs=[pl.BlockSpec((1, tm, d), lambda bi, i, j: (bi, i, 0)),
                  pl.BlockSpec((1, d), lambda bi, i, j: (0, 0)),
                  pl.BlockSpec((1, 1, d), lambda bi, i, j: (bi, 0, 0)),
                  pl.BlockSpec((1, 1, d), lambda bi, i, j: (bi, 0, 0)),
                  pl.BlockSpec((d, tn), lambda bi, i, j: (0, j + col_blk_off))],
        out_specs=pl.BlockSpec((1, tm, tn), lambda bi, i, j: (bi, i, j)),
        out_shape=jax.ShapeDtypeStruct((b, l, n_out), BF16),
        scratch_shapes=[pltpu.VMEM((tm, d), BF16)],
        compiler_params=_cparams(),
        name="mod_proj",
    )(x, g.reshape(1, d), scale, shift, w_bf16)


def _rope_tables(l):
    half = HEAD_DIM // 2
    quarter = half // 2
    inv_freq = 1.0 / (ROPE_THETA ** (jnp.arange(0, half, 2, dtype=F32) / half))
    rows = (jnp.arange(l) // GRID_W).astype(F32)
    cols = (jnp.arange(l) % GRID_W).astype(F32)

    def cs(pos):
        ang = pos[:, None] * inv_freq[None, :]
        return jnp.cos(ang), jnp.sin(ang)

    cr, sr = cs(rows)
    cc, sc = cs(cols)
    zero = jnp.zeros((l, quarter), F32)
    cos = jnp.concatenate([cr, cr, cc, cc], axis=-1)
    s_lo = jnp.concatenate([-sr, zero, -sc, zero], axis=-1)
    s_hi = jnp.concatenate([zero, sr, zero, sc], axis=-1)
    return cos, s_lo, s_hi


def _attn_bias(rpb, rows_n):
    kh = min(WIN_ROWS, rows_n)
    cases = [(0, 0), (Q_ROWS, 0), (rows_n - Q_ROWS, rows_n - BAND_ROWS)]
    col = np.arange(GRID_W)
    cstart = np.clip(col - WIN_COLS // 2, 0, GRID_W - WIN_COLS)
    ok_c = (col[None, :] >= cstart[:, None]) & (col[None, :] < cstart[:, None] + WIN_COLS)
    co = np.clip(col[None, :] - col[:, None], -(WIN_COLS - 1), WIN_COLS - 1) + (WIN_COLS - 1)
    ro_all, ok_all = [], []
    for r0, sb in cases:
        qrow = r0 + np.arange(Q_ROWS)
        krow = sb + np.arange(BAND_ROWS)
        rs = np.clip(qrow - kh // 2, 0, rows_n - kh)
        ok_all.append((krow[None, :] >= rs[:, None]) & (krow[None, :] < rs[:, None] + kh))
        ro_all.append(np.clip(krow[None, :] - qrow[:, None] + (WIN_ROWS - 1), 0, 2 * WIN_ROWS - 2))
    ro_all, ok_all = np.stack(ro_all), np.stack(ok_all)
    onehot = (co[None] == np.arange(2 * WIN_COLS - 1)[:, None, None]).astype(np.float32)
    toe = jnp.einsum('hrc,cqk->hrqk', rpb.astype(F32), onehot, precision=lax.Precision.HIGHEST)
    blocks = toe[:, ro_all]
    ok = ok_all[None, :, :, :, None, None] & ok_c[None, None, None, None]
    bias = jnp.transpose(jnp.where(ok, blocks, NEG_INF), (0, 1, 2, 4, 3, 5))
    return bias.reshape(rpb.shape[0], 3, Q_ROWS * GRID_W, BAND_ROWS * GRID_W)


def _attn_kernel(q_ref, k_ref, v_ref, kc_ref, vc_ref, bias_ref, cos_ref, slo_ref, shi_ref, o_ref,
                 qr_scr, qs_scr, kr_scr, *, n_blk, rows_n):
    quarter = HEAD_DIM // 4
    qn = Q_ROWS * GRID_W
    kn = BAND_ROWS * GRID_W
    scale = HEAD_DIM ** -0.5

    def rope(x, sl):
        return (x * cos_ref[sl, :] + pltpu.roll(x, HEAD_DIM - quarter, 1) * slo_ref[sl, :]
                + pltpu.roll(x, quarter, 1) * shi_ref[sl, :])

    def rope_body(i, _):
        sl = pl.ds(pl.multiple_of(i * qn, qn), qn)
        q = q_ref[0, sl, :].astype(F32) * scale
        qs_scr[sl, :] = q.astype(BF16)
        qr_scr[sl, :] = rope(q, sl).astype(BF16)
        kr_scr[sl, :] = rope(k_ref[0, sl, :].astype(F32), sl).astype(BF16)
        return 0

    lax.fori_loop(0, n_blk, rope_body, 0)

    kc = kc_ref[0]
    vc = vc_ref[0]
    nt = (((1,), (1,)), ((), ()))

    def body(i, _):
        sb = jnp.clip(i * Q_ROWS - WIN_ROWS // 2, 0, rows_n - BAND_ROWS)
        ks = pl.ds(pl.multiple_of(sb * GRID_W, GRID_W), kn)
        qs = pl.ds(pl.multiple_of(i * qn, qn), qn)
        case = jnp.where(i == 0, 0, jnp.where(i == n_blk - 1, 2, 1))
        s = lax.dot_general(qr_scr[qs, :], kr_scr[ks, :], nt, preferred_element_type=F32) + bias_ref[0, case]
        sc = lax.dot_general(qs_scr[qs, :], kc, nt, preferred_element_type=F32)
        m = jnp.maximum(jnp.max(s, axis=-1, keepdims=True), jnp.max(sc, axis=-1, keepdims=True))
        p = jnp.exp(s - m)
        pc = jnp.exp(sc - m)
        den = jnp.sum(p, axis=-1, keepdims=True) + jnp.sum(pc, axis=-1, keepdims=True)
        o = (jnp.dot(p.astype(BF16), v_ref[0, ks, :], preferred_element_type=F32)
             + jnp.dot(pc.astype(BF16), vc, preferred_element_type=F32))
        o_ref[0, qs, :] = (o / den).astype(o_ref.dtype)
        return 0

    lax.fori_loop(0, n_blk, body, 0, unroll=4)


def neighborhood_attention(p_lat, p_ctx, rpb, n_heads):
    b, l, _ = p_lat.shape
    lc = p_ctx.shape[1]
    rows_n = l // GRID_W
    assert rows_n % Q_ROWS == 0 and rows_n >= BAND_ROWS + Q_ROWS
    n_blk = rows_n // Q_ROWS
    bias = _attn_bias(rpb, rows_n)
    cos, s_lo, s_hi = _rope_tables(l)
    qn, kn = Q_ROWS * GRID_W, BAND_ROWS * GRID_W
    h = n_heads
    tab = pl.BlockSpec((l, HEAD_DIM), lambda bi, hi: (0, 0))
    return pl.pallas_call(
        functools.partial(_attn_kernel, n_blk=n_blk, rows_n=rows_n),
        grid=(b, h),
        in_specs=[pl.BlockSpec((1, l, HEAD_DIM), lambda bi, hi: (bi, 0, hi)),
                  pl.BlockSpec((1, l, HEAD_DIM), lambda bi, hi: (bi, 0, hi + h)),
                  pl.BlockSpec((1, l, HEAD_DIM), lambda bi, hi: (bi, 0, hi + 2 * h)),
                  pl.BlockSpec((1, lc, HEAD_DIM), lambda bi, hi: (bi, 0, hi)),
                  pl.BlockSpec((1, lc, HEAD_DIM), lambda bi, hi: (bi, 0, hi + h)),
                  pl.BlockSpec((1, 3, qn, kn), lambda bi, hi: (hi, 0, 0, 0)),
                  tab, tab, tab],
        out_specs=pl.BlockSpec((1, l, HEAD_DIM), lambda bi, hi: (bi, 0, hi)),
        out_shape=jax.ShapeDtypeStruct((b, l, h * HEAD_DIM), BF16),
        scratch_shapes=[pltpu.VMEM((l, HEAD_DIM), BF16)] * 3,
        compiler_params=_cparams(),
        name="nbr_attn",
    )(p_lat, p_lat, p_lat, p_ctx, p_ctx, bias, cos, s_lo, s_hi)


SSM_CHUNK = 16
SSM_ROWS = 8
SSM_PACK = 8


def _ssm_mats(a_re, a_im, log_dt, b_re, b_im, c_re, c_im, d_skip):
    t = SSM_CHUNK
    hi = lax.Precision.HIGHEST
    a = lax.complex(a_re.astype(F32), a_im.astype(F32))
    dta = jnp.exp(log_dt.astype(F32))[..., None] * a
    a_bar = jnp.exp(dta)
    b_bar = ((a_bar - 1.0) / a)[..., None] * lax.complex(b_re.astype(F32), b_im.astype(F32))
    cm = lax.complex(c_re.astype(F32), c_im.astype(F32))
    k = jnp.arange(t + 1, dtype=F32)
    ap = jnp.exp(dta[..., None] * k)
    g, p, n = a.shape[1], a.shape[2], b_re.shape[-1]
    kern = jnp.einsum('dgnp,dgpl,dgpm->dglnm', cm, ap[..., :t], b_bar, precision=hi).real
    s_i = np.arange(t)[:, None]
    t_i = np.arange(t)[None, :]
    kf = kern[0][:, np.clip(t_i - s_i, 0, t - 1)]
    kb = kern[1][:, np.clip(s_i - t_i, 0, t - 1)]
    eye = jnp.eye(n, dtype=F32)
    m_mat = (jnp.where((s_i <= t_i)[None, :, :, None, None], kf, 0.0)
             + jnp.where((s_i >= t_i)[None, :, :, None, None], kb, 0.0)
             + (s_i == t_i)[None, :, :, None, None] * (d_skip.astype(F32)[:, None, None, :, None] * eye))
    m_mat = jnp.transpose(m_mat, (0, 1, 4, 2, 3)).reshape(g, t * n, t * n)
    wf = ap[0][:, :, t - 1::-1][..., :t, None] * b_bar[0][:, :, None, :]
    wb = ap[1][:, :, :t, None] * b_bar[1][:, :, None, :]
    to_rows = lambda z: jnp.transpose(z, (0, 2, 3, 1)).reshape(g, t * n, p)
    w_mat = jnp.concatenate([to_rows(wf.real), to_rows(wb.real), to_rows(wf.imag), to_rows(wb.imag)], axis=-1)
    zf = jnp.transpose(cm[0], (0, 2, 1))[:, :, None, :] * ap[0][:, :, 1:t + 1, None]
    zb = jnp.transpose(cm[1], (0, 2, 1))[:, :, None, :] * ap[1][:, :, t:0:-1, None]
    flat = lambda z: z.reshape(g, p, t * n)
    r_mat = jnp.concatenate([flat(zf.real), flat(zb.real), -flat(zf.imag), -flat(zb.imag)], axis=1)
    a_t = ap[..., t]
    a_vec = jnp.stack([jnp.concatenate([a_t[0].real, a_t[1].real], -1),
                       jnp.concatenate([a_t[0].imag, a_t[1].imag], -1)], axis=1)
    a_vec = jnp.pad(a_vec, ((0, 0), (0, SSM_ROWS - 2), (0, 0)))
    return m_mat.astype(BF16), w_mat.astype(BF16), r_mat.astype(BF16), a_vec


def _s5_kernel(u_ref, sel_ref, selt_ref, m_ref, w_ref, r_ref, a_ref, y_ref, v_scr, ent_scr, *, n_ctx, n_chunks):
    u = jnp.dot(u_ref[0], sel_ref[0], preferred_element_type=F32).astype(BF16)
    p2 = a_ref.shape[-1]
    p = p2 // 2
    v_scr[...] = jnp.dot(u, w_ref[0], preferred_element_type=F32)
    a_re = jnp.broadcast_to(a_ref[0, 0:1, :], (SSM_ROWS, p2))
    a_im = jnp.broadcast_to(a_ref[0, 1:2, :], (SSM_ROWS, p2))
    is_fwd = lax.broadcasted_iota(I32, (SSM_ROWS, p2), 1) < p

    def step(j, carry):
        s_re, s_im = carry
        cb = jnp.where(j < n_ctx, n_ctx - 1 - j, n_chunks + n_ctx - 1 - j)
        rf = pl.ds(pl.multiple_of(j * SSM_ROWS, SSM_ROWS), SSM_ROWS)
        rb = pl.ds(pl.multiple_of(cb * SSM_ROWS, SSM_ROWS), SSM_ROWS)
        ent_scr[rf, 0:p] = s_re[:, :p]
        ent_scr[rb, p:p2] = s_re[:, p:]
        ent_scr[rf, p2:p2 + p] = s_im[:, :p]
        ent_scr[rb, p2 + p:] = s_im[:, p:]
        in_re = jnp.where(is_fwd, v_scr[rf, :p2], v_scr[rb, :p2])
        in_im = jnp.where(is_fwd, v_scr[rf, p2:], v_scr[rb, p2:])
        return a_re * s_re - a_im * s_im + in_re, a_re * s_im + a_im * s_re + in_im

    zero = jnp.zeros((SSM_ROWS, p2), F32)
    lax.fori_loop(0, n_chunks, step, (zero, zero))
    y = (jnp.dot(u, m_ref[0], preferred_element_type=F32)
         + jnp.dot(ent_scr[...].astype(BF16), r_ref[0], preferred_element_type=F32)).astype(BF16)
    def part(c):
        return jnp.dot(y, selt_ref[0, :, c:c + PACK_CHUNK], preferred_element_type=F32).astype(y_ref.dtype)

    @pl.when(pl.program_id(1) == 0)
    def _():
        for c in range(0, y_ref.shape[-1], PACK_CHUNK):
            y_ref[0, :, c:c + PACK_CHUNK] = part(c)

    @pl.when(pl.program_id(1) > 0)
    def _():
        for c in range(0, y_ref.shape[-1], PACK_CHUNK):
            y_ref[0, :, c:c + PACK_CHUNK] = y_ref[0, :, c:c + PACK_CHUNK] + part(c)


def s5_mixer(u_lat, u_ctx, mats):
    m_mat, w_mat, r_mat, a_vec = mats
    b, l, dch = u_lat.shape
    lc = u_ctx.shape[1]
    g = m_mat.shape[0]
    n = dch // g
    t = SSM_CHUNK
    tn = t * n
    gp = SSM_PACK
    packs, lanes = g // gp, gp * n
    n_ctx, n_chunks = lc // t, (lc + l) // t
    assert b <= SSM_ROWS and lc % t == 0 and l % t == 0 and g % gp == 0
    z = jnp.concatenate([u_ctx, u_lat], axis=1).reshape(b, n_chunks, t, packs, lanes)
    z = jnp.pad(jnp.transpose(z, (3, 1, 0, 2, 4)), ((0, 0), (0, 0), (0, SSM_ROWS - b), (0, 0), (0, 0)))
    rows = n_chunks * SSM_ROWS
    z = z.reshape(packs, rows, t * lanes)
    ri = jnp.arange(t * lanes)
    ci = jnp.arange(tn)
    same = (ri[:, None] // lanes == ci[None, :] // n) & (ri[:, None] % n == ci[None, :] % n)
    sel = (same[None] & ((ri[None, :, None] % lanes) // n == jnp.arange(gp)[:, None, None])).astype(BF16)
    selt = jnp.transpose(sel, (0, 2, 1))
    grp = lambda k: pl.BlockSpec((1, k, tn), lambda pi, qi: (pi * gp + qi, 0, 0))
    pack_blk = pl.BlockSpec((1, rows, t * lanes), lambda pi, qi: (pi, 0, 0))
    y = pl.pallas_call(
        functools.partial(_s5_kernel, n_ctx=n_ctx, n_chunks=n_chunks),
        grid=(packs, gp),
        in_specs=[pack_blk,
                  pl.BlockSpec((1, t * lanes, tn), lambda pi, qi: (qi, 0, 0)),
                  pl.BlockSpec((1, tn, t * lanes), lambda pi, qi: (qi, 0, 0)),
                  grp(tn), grp(tn), grp(r_mat.shape[1]),
                  pl.BlockSpec((1, SSM_ROWS, a_vec.shape[-1]), lambda pi, qi: (pi * gp + qi, 0, 0))],
        out_specs=pack_blk,
        out_shape=jax.ShapeDtypeStruct((packs, rows, t * lanes), BF16),
        scratch_shapes=[pltpu.VMEM((rows, w_mat.shape[-1]), F32), pltpu.VMEM((rows, r_mat.shape[1]), F32)],
        compiler_params=_cparams(),
        name="s5_mixer",
    )(z, sel, selt, m_mat, w_mat, r_mat, a_vec)
    y = y.reshape(packs, n_chunks, SSM_ROWS, t, lanes)[:, n_ctx:, :b]
    return jnp.transpose(y, (2, 1, 3, 0, 4)).reshape(b, l, dch)


def _glu_kernel(y_ref, w_ref, b_ref, o_ref):
    z = jax.nn.gelu(y_ref[...].astype(F32))
    gate = jax.nn.sigmoid(jnp.dot(z.astype(BF16), w_ref[...], preferred_element_type=F32) + b_ref[...])
    o_ref[...] = (z * gate).astype(o_ref.dtype)


def s5_glu(y, w_bf16, b_glu, tm=512):
    n, d = y.shape
    return pl.pallas_call(
        _glu_kernel,
        grid=(n // tm,),
        in_specs=[pl.BlockSpec((tm, d), lambda i: (i, 0)),
                  pl.BlockSpec((d, d), lambda i: (0, 0)),
                  pl.BlockSpec((1, d), lambda i: (0, 0))],
        out_specs=pl.BlockSpec((tm, d), lambda i: (i, 0)),
        out_shape=jax.ShapeDtypeStruct((n, d), BF16),
        compiler_params=_cparams(),
        name="s5_glu",
    )(y, w_bf16, b_glu.reshape(1, d))


def _merge_kernel(a_ref, s_ref, ga_ref, gs_ref, w_ref, o_ref, h_scr):
    da = a_ref.shape[-1]

    @pl.when(pl.program_id(1) == 0)
    def _():
        h_scr[:, :da] = _rms(a_ref[...].astype(F32), ga_ref[...]).astype(BF16)
        h_scr[:, da:] = _rms(s_ref[...].astype(F32), gs_ref[...]).astype(BF16)

    o_ref[...] = jnp.dot(h_scr[...], w_ref[...], preferred_element_type=F32)


def merge_proj(attn, ssm, g_attn, g_ssm, w_bf16, tm=512, tn=1024):
    n, da = attn.shape
    ds = ssm.shape[1]
    d, dout = w_bf16.shape
    return pl.pallas_call(
        _merge_kernel,
        grid=(n // tm, dout // tn),
        in_specs=[pl.BlockSpec((tm, da), lambda i, j: (i, 0)),
                  pl.BlockSpec((tm, ds), lambda i, j: (i, 0)),
                  pl.BlockSpec((1, da), lambda i, j: (0, 0)),
                  pl.BlockSpec((1, ds), lambda i, j: (0, 0)),
                  pl.BlockSpec((d, tn), lambda i, j: (0, j))],
        out_specs=pl.BlockSpec((tm, tn), lambda i, j: (i, j)),
        out_shape=jax.ShapeDtypeStruct((n, dout), F32),
        scratch_shapes=[pltpu.VMEM((tm, d), BF16)],
        compiler_params=_cparams(),
        name="merge_proj",
    )(attn, ssm, g_attn.reshape(1, da), g_ssm.reshape(1, ds), w_bf16)


def _post_mix_kernel(x_ref, mix_ref, gpost_ref, gt_ref, gpre_ref, sc_ref, sh_ref, wr_ref, br_ref,
                     x1_ref, hp_ref, idx_ref, gate_ref, mask_ref):
    x1 = x_ref[0] + gt_ref[0] * _rms(mix_ref[0], gpost_ref[...])
    x1_ref[0] = x1
    h = _rms(x1, gpre_ref[...]) * (1.0 + sc_ref[0]) + sh_ref[0]
    half = h.shape[-1] // 2
    hp_ref[0] = _pack_halves(h[:, :half], h[:, half:])
    scores = jax.nn.sigmoid(jnp.dot(h.astype(BF16), wr_ref[...], preferred_element_type=F32))
    n_e = scores.shape[-1]
    lane = lax.broadcasted_iota(I32, scores.shape, 1)
    biased = scores + br_ref[...]
    idx_out = jnp.zeros(scores.shape, I32)
    sel_out = jnp.zeros(scores.shape, F32)
    mask = jnp.zeros(scores.shape, jnp.bool_)
    for k in range(TOP_K):
        m = jnp.max(biased, axis=-1, keepdims=True)
        ik = jnp.min(jnp.where(biased == m, lane, n_e), axis=-1, keepdims=True)
        hit = lane == ik
        sel_k = jnp.sum(jnp.where(hit, scores, 0.0), axis=-1, keepdims=True)
        idx_out = jnp.where(lane == k, ik, idx_out)
        sel_out = jnp.where(lane == k, sel_k, sel_out)
        mask = jnp.logical_or(mask, hit)
        biased = jnp.where(hit, -jnp.inf, biased)
    idx_ref[0] = idx_out
    gate_ref[0] = sel_out / jnp.sum(sel_out, axis=-1, keepdims=True) * ROUTED_SCALE
    mask_ref[0] = mask.astype(BF16)


def post_mix(x, mix, g_post, gt, g_pre, scale, shift, wr_bf16, b_router, tm=256):
    b, l, d = x.shape
    n_e = wr_bf16.shape[1]
    row = pl.BlockSpec((1, tm, d), lambda bi, i: (bi, i, 0))
    prow = pl.BlockSpec((1, tm, d // 2), lambda bi, i: (bi, i, 0))
    vec = pl.BlockSpec((1, d), lambda bi, i: (0, 0))
    bvec = pl.BlockSpec((1, 1, d), lambda bi, i: (bi, 0, 0))
    small = pl.BlockSpec((1, tm, n_e), lambda bi, i: (bi, i, 0))
    return pl.pallas_call(
        _post_mix_kernel,
        grid=(b, l // tm),
        in_specs=[row, row, vec, bvec, vec, bvec, bvec,
                  pl.BlockSpec((d, n_e), lambda bi, i: (0, 0)),
                  pl.BlockSpec((1, n_e), lambda bi, i: (0, 0))],
        out_specs=[row, prow, small, small, small],
        out_shape=[jax.ShapeDtypeStruct((b, l, d), F32),
                   jax.ShapeDtypeStruct((b, l, d // 2), I32),
                   jax.ShapeDtypeStruct((b, l, n_e), I32),
                   jax.ShapeDtypeStruct((b, l, n_e), F32),
                   jax.ShapeDtypeStruct((b, l, n_e), BF16)],
        compiler_params=_cparams(),
        name="post_mix",
    )(x, mix, g_post.reshape(1, d), gt, g_pre.reshape(1, d), scale, shift, wr_bf16, b_router.reshape(1, n_e))


def _rank_kernel(mask_ref, idx_ref, rank_ref, cnt_ref, carry):
    @pl.when(pl.program_id(0) == 0)
    def _():
        carry[...] = jnp.zeros_like(carry)

    m = mask_ref[...]
    tm = m.shape[0]
    earlier = (lax.broadcasted_iota(I32, (tm, tm), 1) < lax.broadcasted_iota(I32, (tm, tm), 0)).astype(BF16)
    excl = jnp.dot(earlier, m, preferred_element_type=F32) + carry[0:1, :]
    lane = lax.broadcasted_iota(I32, m.shape, 1)
    idx = idx_ref[...]
    out = jnp.zeros(m.shape, F32)
    for k in range(TOP_K):
        rk = jnp.sum(jnp.where(lane == idx[:, k:k + 1], excl, 0.0), axis=-1, keepdims=True)
        out = jnp.where(lane == k, rk, out)
    rank_ref[...] = out.astype(I32)
    carry[0:1, :] = carry[0:1, :] + jnp.sum(m.astype(F32), axis=0, keepdims=True)
    cnt_ref[...] = carry[...]


def route_ranks(mask, idx, tm=512):
    n, n_e = mask.shape
    blk = pl.BlockSpec((tm, n_e), lambda i: (i, 0))
    rank, cnt = pl.pallas_call(
        _rank_kernel,
        grid=(n // tm,),
        in_specs=[blk, blk],
        out_specs=[blk, pl.BlockSpec((8, n_e), lambda i: (0, 0))],
        out_shape=[jax.ShapeDtypeStruct((n, n_e), I32), jax.ShapeDtypeStruct((8, n_e), F32)],
        scratch_shapes=[pltpu.VMEM((8, n_e), F32)],
        compiler_params=_cparams(),
        name="route_ranks",
    )(mask, idx)
    return rank, cnt[0].astype(I32)


def _dest_kernel(rank_ref, idx_ref, ps_ref, dest_ref):
    idx = idx_ref[...]
    lane = lax.broadcasted_iota(I32, idx.shape, 1)
    start = jnp.zeros(idx.shape, F32)
    for k in range(TOP_K):
        sk = jnp.sum(jnp.where(lane == idx[:, k:k + 1], ps_ref[...], 0.0), axis=-1, keepdims=True)
        start = jnp.where(lane == k, sk, start)
    dest_ref[...] = rank_ref[...] + start.astype(I32)


def route_dest(rank, idx, pad_start, tm=512):
    n, n_e = rank.shape
    blk = pl.BlockSpec((tm, n_e), lambda i: (i, 0))
    dest = pl.pallas_call(
        _dest_kernel,
        grid=(n // tm,),
        in_specs=[blk, blk, pl.BlockSpec((1, n_e), lambda i: (0, 0))],
        out_specs=blk,
        out_shape=jax.ShapeDtypeStruct((n, n_e), I32),
        compiler_params=_cparams(),
        name="route_dest",
    )(rank, idx, pad_start.astype(F32).reshape(1, n_e))
    return dest[:, :TOP_K].reshape(n * TOP_K)


def _dispatch_kernel(pe_ref, pd_ref, dest_ref, h_ref, wgu_ref, wd_ref, xs_ref, sp_ref, z_scr, sem, *, n_e, tm, blk):
    def zero_copy(e):
        start = pl.multiple_of(pe_ref[e] - blk, blk)
        return pltpu.make_async_copy(z_scr, xs_ref.at[pl.ds(start, blk), :], sem)

    @pl.when(pl.program_id(0) == 0)
    def _():
        z_scr[...] = jnp.zeros_like(z_scr)

        def start(e, _):
            @pl.when(pd_ref[e] > 0)
            def _():
                zero_copy(e).start()
            return 0

        def wait(e, _):
            @pl.when(pd_ref[e] > 0)
            def _():
                zero_copy(e).wait()
            return 0

        lax.fori_loop(0, n_e, start, 0)
        lax.fori_loop(0, n_e, wait, 0)

    def row_copy(t, k):
        return pltpu.make_async_copy(h_ref.at[pl.ds(t, 1), :], xs_ref.at[pl.ds(dest_ref[t * TOP_K + k], 1), :], sem)

    def issue(t, _):
        for k in range(TOP_K):
            row_copy(t, k).start(priority=k % 2)
        return 0

    lax.fori_loop(0, tm, issue, 0)
    hid = _swiglu_hidden(h_ref, lambda sl: wgu_ref[sl, :])
    _down_packed(hid, lambda sl: wd_ref[:, sl], sp_ref)
    rows = tm * TOP_K
    pltpu.make_async_copy(xs_ref.at[pl.ds(0, rows), :], xs_ref.at[pl.ds(0, rows), :], sem).wait()


def moe_dispatch(hp, dest_flat, pad_end, padded, slots, wsgu, wsd, tm=DISPATCH_TM):
    n, dw = hp.shape
    n_e = pad_end.shape[0]
    d, de2 = wsgu.shape
    row = pl.BlockSpec((tm, dw), lambda i, pe, pd: (i, 0))
    grid_spec = pltpu.PrefetchScalarGridSpec(
        num_scalar_prefetch=2,
        grid=(n // tm,),
        in_specs=[pl.BlockSpec((tm * TOP_K,), lambda i, pe, pd: (i,), memory_space=pltpu.SMEM),
                  row,
                  pl.BlockSpec((d, de2), lambda i, pe, pd: (0, 0)),
                  pl.BlockSpec((de2 // 2, d), lambda i, pe, pd: (0, 0))],
        out_specs=[pl.BlockSpec(memory_space=pl.ANY), row],
        scratch_shapes=[pltpu.VMEM((EXPERT_BLK, dw), I32), pltpu.SemaphoreType.DMA(())],
    )
    return pl.pallas_call(
        functools.partial(_dispatch_kernel, n_e=n_e, tm=tm, blk=EXPERT_BLK),
        grid_spec=grid_spec,
        out_shape=[jax.ShapeDtypeStruct((slots, dw), I32), jax.ShapeDtypeStruct((n, dw), I32)],
        compiler_params=_cparams(),
        name="moe_dispatch",
    )(pad_end, padded, dest_flat, hp, wsgu, wsd)


PACK_CHUNK = 512


def _swiglu_hidden(xp_ref, wgu):
    half = xp_ref.shape[-1]
    acc = None
    for c in range(half // PACK_CHUNK):
        cl = slice(c * PACK_CHUNK, (c + 1) * PACK_CHUNK)
        ch = slice(half + c * PACK_CHUNK, half + (c + 1) * PACK_CHUNK)
        lo, hi = _unpack_halves(xp_ref[:, cl])
        part = (jnp.dot(lo.astype(BF16), wgu(cl), preferred_element_type=F32)
                + jnp.dot(hi.astype(BF16), wgu(ch), preferred_element_type=F32))
        acc = part if acc is None else acc + part
    de = acc.shape[-1] // 2
    return (jax.nn.silu(acc[:, :de]) * acc[:, de:]).astype(BF16)


def _down_packed(hid, wd, o_ref):
    half = o_ref.shape[-1]
    for c in range(half // PACK_CHUNK):
        cl = slice(c * PACK_CHUNK, (c + 1) * PACK_CHUNK)
        ch = slice(half + c * PACK_CHUNK, half + (c + 1) * PACK_CHUNK)
        o_ref[:, cl] = _pack_halves(jnp.dot(hid, wd(cl), preferred_element_type=F32),
                                    jnp.dot(hid, wd(ch), preferred_element_type=F32))


def _expert_weights(sched, i, w_hbm, w_buf, sem):
    be_ref, first_ref, slot_ref, nxt_ref = sched
    s = slot_ref[i]

    def copies(e, sl):
        return [pltpu.make_async_copy(h.at[e], v.at[sl], sem.at[j, sl]) for j, (h, v) in enumerate(zip(w_hbm, w_buf))]

    @pl.when(first_ref[i] == 1)
    def _():
        @pl.when(i == 0)
        def _():
            for c in copies(be_ref[0], 0):
                c.start()

        for c in copies(be_ref[i], s):
            c.wait()

        @pl.when(nxt_ref[i] >= 0)
        def _():
            for c in copies(nxt_ref[i], 1 - s):
                c.start()

    return s


def _expert_up_kernel(be_ref, nu_ref, first_ref, slot_ref, nxt_ref, x_ref, wg_hbm, wu_hbm, hid_ref,
                      wg_buf, wu_buf, sem):
    i = pl.program_id(0)
    used = i < nu_ref[0]
    s = _expert_weights((be_ref, first_ref, slot_ref, nxt_ref), i, (wg_hbm, wu_hbm), (wg_buf, wu_buf), sem)

    def wgu(sl):
        return jnp.concatenate([wg_buf[s, sl, :].astype(BF16), wu_buf[s, sl, :].astype(BF16)], axis=-1)

    @pl.when(used)
    def _():
        hid_ref[...] = _swiglu_hidden(x_ref, wgu)

    @pl.when(jnp.logical_not(used))
    def _():
        hid_ref[...] = jnp.zeros_like(hid_ref)


def _expert_down_kernel(be_ref, nu_ref, first_ref, slot_ref, nxt_ref, hid_ref, wd_hbm, o_ref, wd_buf, sem):
    i = pl.program_id(0)
    used = i < nu_ref[0]
    s = _expert_weights((be_ref, first_ref, slot_ref, nxt_ref), i, (wd_hbm,), (wd_buf,), sem)

    @pl.when(used)
    def _():
        _down_packed(hid_ref[...], lambda sl: wd_buf[s, :, sl].astype(BF16), o_ref)

    @pl.when(jnp.logical_not(used))
    def _():
        o_ref[...] = jnp.zeros_like(o_ref)


def _expert_schedule(block_e, n_used):
    nb = block_e.shape[0]
    ar = jnp.arange(nb, dtype=I32)
    used = ar < n_used[0]
    first = used & ((ar == 0) | (block_e != jnp.roll(block_e, 1)))
    slot = jnp.where(used, (jnp.cumsum(first.astype(I32)) - 1) % 2, 0).astype(I32)
    nf = jnp.where(first, ar, nb)
    nxt_idx = jnp.concatenate([lax.cummin(nf, axis=0, reverse=True)[1:], jnp.full((1,), nb, I32)])
    nxt = jnp.where(nxt_idx < nb, block_e[jnp.minimum(nxt_idx, nb - 1)], -1).astype(I32)
    return first.astype(I32), slot, nxt


def expert_ffn(xs, block_e, n_used, wg, wu, wd, blk):
    slots, dw = xs.shape
    d, de = wg.shape[1], wg.shape[2]
    n_blocks = slots // blk
    sched = (block_e, n_used) + _expert_schedule(block_e, n_used)
    cur = lambda i, nu: jnp.minimum(i, nu[0] - 1)
    hbm = pl.BlockSpec(memory_space=pl.ANY)
    up_spec = pltpu.PrefetchScalarGridSpec(
        num_scalar_prefetch=5,
        grid=(n_blocks,),
        in_specs=[pl.BlockSpec((blk, dw), lambda i, be, nu, *_: (cur(i, nu), 0)), hbm, hbm],
        out_specs=pl.BlockSpec((blk, de), lambda i, *_: (i, 0)),
        scratch_shapes=[pltpu.VMEM((2, d, de), F32), pltpu.VMEM((2, d, de), F32), pltpu.SemaphoreType.DMA((2, 2))],
    )
    hid = pl.pallas_call(
        _expert_up_kernel,
        grid_spec=up_spec,
        out_shape=jax.ShapeDtypeStruct((slots, de), BF16),
        compiler_params=_cparams(),
        name="expert_up",
    )(*sched, xs, wg, wu)
    down_spec = pltpu.PrefetchScalarGridSpec(
        num_scalar_prefetch=5,
        grid=(n_blocks,),
        in_specs=[pl.BlockSpec((blk, de), lambda i, be, nu, *_: (cur(i, nu), 0)), hbm],
        out_specs=pl.BlockSpec((blk, dw), lambda i, *_: (i, 0)),
        scratch_shapes=[pltpu.VMEM((2, de, d), F32), pltpu.SemaphoreType.DMA((1, 2))],
    )
    return pl.pallas_call(
        _expert_down_kernel,
        grid_spec=down_spec,
        out_shape=jax.ShapeDtypeStruct((slots, dw), I32),
        compiler_params=_cparams(),
        name="expert_down",
    )(*sched, hid, wd)


def _final_kernel(dc_ref, dn_ref, x1_ref, sp_ref, gate_ref, ys_ref, gpost_ref, gt_ref, o_ref, buf, sem, *, tm, n_tiles):
    i = pl.program_id(0)
    slot = i % 2
    sub = 8

    def issue_rows(dref, s, j):
        base = pl.multiple_of(j * sub, sub)
        for tt in range(sub):
            for k in range(TOP_K):
                pltpu.make_async_copy(ys_ref.at[pl.ds(dref[(base + tt) * TOP_K + k], 1), :],
                                      buf.at[s, pl.ds(k * tm + base + tt, 1), :],
                                      sem.at[s]).start(priority=k % 2)

    def reduce_rows(s, j):
        rs = pl.ds(pl.multiple_of(j * sub, sub), sub)
        g = gate_ref[rs, :]
        acc_lo, acc_hi = _unpack_halves(sp_ref[rs, :])
        for k in range(TOP_K):
            lo, hi = _unpack_halves(buf[s, pl.ds(pl.multiple_of(k * tm + j * sub, sub), sub), :])
            gk = g[:, k:k + 1]
            acc_lo = acc_lo + gk * lo
            acc_hi = acc_hi + gk * hi
        ffn = jnp.concatenate([acc_lo, acc_hi], axis=-1)
        o_ref[rs, :] = x1_ref[rs, :] + gt_ref[0] * _rms(ffn, gpost_ref[...])

    @pl.when(i == 0)
    def _():
        def first(j, c):
            issue_rows(dc_ref, 0, j)
            return c
        lax.fori_loop(0, tm // sub, first, 0)

    def wait_slot(s):
        pltpu.make_async_copy(ys_ref.at[pl.ds(0, TOP_K * tm), :], buf.at[s], sem.at[s]).wait()

    wait_slot(slot)

    for s in range(2):
        @pl.when(slot == s)
        def _():
            def body(j, c):
                issue_rows(dn_ref, 1 - s, j)
                reduce_rows(s, j)
                return c
            lax.fori_loop(0, tm // sub, body, 0, unroll=4)

    @pl.when(i == n_tiles - 1)
    def _():
        wait_slot(1 - slot)


def final_mix(x1, sp, gates, ys, dest_flat, g_post, gt, tiles_per_batch, tm=COMBINE_TM):
    n, d = x1.shape
    dw = sp.shape[1]
    n_e = gates.shape[1]
    n_tiles = n // tm
    return pl.pallas_call(
        functools.partial(_final_kernel, tm=tm, n_tiles=n_tiles),
        grid=(n_tiles,),
        in_specs=[pl.BlockSpec((tm * TOP_K,), lambda i: (i,), memory_space=pltpu.SMEM),
                  pl.BlockSpec((tm * TOP_K,), lambda i: (jnp.minimum(i + 1, n_tiles - 1),), memory_space=pltpu.SMEM),
                  pl.BlockSpec((tm, d), lambda i: (i, 0)),
                  pl.BlockSpec((tm, dw), lambda i: (i, 0)),
                  pl.BlockSpec((tm, n_e), lambda i: (i, 0)),
                  pl.BlockSpec(memory_space=pl.ANY),
                  pl.BlockSpec((1, d), lambda i: (0, 0)),
                  pl.BlockSpec((1, 1, d), lambda i: (i // tiles_per_batch, 0, 0))],
        out_specs=pl.BlockSpec((tm, d), lambda i: (i, 0)),
        out_shape=jax.ShapeDtypeStruct((n, d), F32),
        scratch_shapes=[pltpu.VMEM((2, TOP_K * tm, dw), I32), pltpu.SemaphoreType.DMA((2,))],
        compiler_params=_cparams(),
        name="final_mix",
    )(dest_flat, dest_flat, x1, sp, gates, ys, g_post.reshape(1, d), gt)


def kernel(x, c, ctx, c_ctx, w_ada, b_ada, g_pre_mix, g_post_mix, g_pre_ffn, g_post_ffn, w_in, rpb, ssm_a_re, ssm_a_im, ssm_log_dt, ssm_b_re, ssm_b_im, ssm_c_re, ssm_c_im, ssm_d, w_glu, b_glu, g_attn_out, g_ssm_out, w_out, w_router, b_router, w_exp_gate, w_exp_up, w_exp_down, w_sh_gate, w_sh_up, w_sh_down):
    b, l, d = x.shape
    lc = ctx.shape[1]
    assert w_ada.shape[0] == 1 and b + 1 <= 8
    n_in = w_in.shape[-1]
    d_ssm = w_glu.shape[-1]
    d_attn = d - d_ssm
    n_heads = d_attn // HEAD_DIM
    n = b * l

    c8 = jnp.concatenate([c, c_ctx[None], jnp.zeros((8 - b - 1, d), F32)], axis=0)
    mod = ada_mod(c8, w_ada[0], b_ada[0]).reshape(8, 6, 1, d)
    sh_m, sc_m, gt_m, sh_f, sc_f, gt_f = [mod[:b, j] for j in range(6)]
    csh_m, csc_m = mod[b:b + 1, 0], mod[b:b + 1, 1]

    w_in_b = w_in[0].astype(BF16)
    tn = PROJ_TN
    p_lat = mod_proj(x, g_pre_mix[0], sc_m, sh_m, w_in_b, 0, n_in, 512, tn)
    p_ctx = mod_proj(ctx.reshape(1, b * lc, d), g_pre_mix[0], csc_m, csh_m, w_in_b,
                     d_attn // tn, n_in - d_attn, 512, tn).reshape(b, lc, n_in - d_attn)

    attn = neighborhood_attention(p_lat, p_ctx, rpb[0], n_heads)

    mats = _ssm_mats(ssm_a_re[0], ssm_a_im[0], ssm_log_dt[0], ssm_b_re[0], ssm_b_im[0],
                     ssm_c_re[0], ssm_c_im[0], ssm_d[0])
    y = s5_mixer(p_lat[..., 3 * d_attn:], p_ctx[..., 2 * d_attn:], mats)
    ssm = s5_glu(y.reshape(n, d_ssm), w_glu[0].astype(BF16), b_glu[0])

    mix = merge_proj(attn.reshape(n, d_attn), ssm, g_attn_out[0], g_ssm_out[0], w_out[0].astype(BF16))
    x1, hp, idx, gates, mask = post_mix(x, mix.reshape(b, l, d), g_post_mix[0], gt_m, g_pre_ffn[0], sc_f, sh_f,
                                        w_router[0].astype(BF16), b_router[0])

    n_e = w_router.shape[-1]
    m = n * TOP_K
    idx = idx.reshape(n, n_e)
    rank, counts = route_ranks(mask.reshape(n, n_e), idx)
    padded = (counts + EXPERT_BLK - 1) // EXPERT_BLK * EXPERT_BLK
    pad_end = jnp.cumsum(padded).astype(I32)
    dest = route_dest(rank, idx, pad_end - padded)
    n_blocks = m // EXPERT_BLK + n_e
    slots = n_blocks * EXPERT_BLK
    block_e = jnp.minimum(jnp.searchsorted(pad_end, jnp.arange(n_blocks) * EXPERT_BLK, side='right'),
                          n_e - 1).astype(I32)
    n_used = (pad_end[-1] // EXPERT_BLK).astype(I32).reshape(1)

    hp2 = hp.reshape(n, d // 2)
    xs, sp = moe_dispatch(hp2, dest, pad_end, padded.astype(I32), slots,
                          jnp.concatenate([w_sh_gate[0], w_sh_up[0]], axis=-1).astype(BF16), w_sh_down[0].astype(BF16))
    ys = expert_ffn(xs, block_e, n_used, w_exp_gate[0], w_exp_up[0], w_exp_down[0], EXPERT_BLK)
    out = final_mix(x1.reshape(n, d), sp, gates.reshape(n, n_e), ys, dest, g_post_ffn[0], gt_f, l // COMBINE_TM)
    return out.reshape(b, l, d)
```

```python
import functools

import numpy as np
import jax
import jax.numpy as jnp
from jax import lax
from jax.experimental import pallas as pl
from jax.experimental.pallas import tpu as pltpu

F32 = jnp.float32
BF16 = jnp.bfloat16
I32 = jnp.int32

GRID_W = 64
HEAD_DIM = 128
WIN_ROWS = 8
WIN_COLS = 16
ROPE_THETA = 10000.0
SSM_GROUP_CH = 16
SSM_STATE = 64
TOP_K = 8
ROUTED_SCALE = 2.5
EPS = 1e-6
NEG_INF = -1e30

Q_ROWS = 4
BAND_ROWS = Q_ROWS + WIN_ROWS - 1
VMEM_LIMIT = 56 * 1024 * 1024
EXPERT_BLK = 512
PROJ_TN = 1024
DISPATCH_TM = 512
COMBINE_TM = 128


def _cparams():
    return pltpu.CompilerParams(vmem_limit_bytes=VMEM_LIMIT)


def _rms(x, g):
    return x * lax.rsqrt(jnp.mean(x * x, axis=-1, keepdims=True) + EPS) * g


def _pack_halves(lo, hi):
    lo_bits = lax.bitcast_convert_type(lo.astype(BF16).astype(F32), I32)
    hi_bits = lax.bitcast_convert_type(hi.astype(BF16).astype(F32), I32)
    return (hi_bits & jnp.int32(-65536)) | lax.shift_right_logical(lo_bits, jnp.int32(16))


def _unpack_halves(w):
    lo = lax.bitcast_convert_type(lax.shift_left(w, jnp.int32(16)), F32)
    hi = lax.bitcast_convert_type(w & jnp.int32(-65536), F32)
    return lo, hi


def _ada_kernel(c_ref, w_ref, b_ref, o_ref):
    a = jax.nn.silu(c_ref[...]).astype(BF16)
    o_ref[...] = jnp.dot(a, w_ref[...].astype(BF16), preferred_element_type=F32) + b_ref[...]


def ada_mod(c8, w_ada, b_ada):
    d, n = w_ada.shape
    tn = 512
    return pl.pallas_call(
        _ada_kernel,
        grid=(n // tn,),
        in_specs=[pl.BlockSpec((8, d), lambda j: (0, 0)),
                  pl.BlockSpec((d, tn), lambda j: (0, j)),
                  pl.BlockSpec((1, tn), lambda j: (0, j))],
        out_specs=pl.BlockSpec((8, tn), lambda j: (0, j)),
        out_shape=jax.ShapeDtypeStruct((8, n), F32),
        compiler_params=_cparams(),
        name="ada_mod",
    )(c8, w_ada, b_ada.reshape(1, n))


def _modproj_kernel(x_ref, g_ref, sc_ref, sh_ref, w_ref, o_ref, h_scr):
    @pl.when(pl.program_id(2) == 0)
    def _():
        h = _rms(x_ref[0], g_ref[...]) * (1.0 + sc_ref[0]) + sh_ref[0]
        h_scr[...] = h.astype(BF16)

    o_ref[0] = jnp.dot(h_scr[...], w_ref[...], preferred_element_type=F32).astype(o_ref.dtype)


def mod_proj(x, g, scale, shift, w_bf16, col_blk_off, n_out, tm, tn):
    b, l, d = x.shape
    return pl.pallas_call(
        _modproj_kernel,
        grid=(b, l // tm, n_out // tn),
        in_specs=[pl.BlockSpec((1, tm, d), lambda bi, i, j: (bi, i, 0)),
                  pl.BlockSpec((1, d), lambda bi, i, j: (0, 0)),
                  pl.BlockSpec((1, 1, d), lambda bi, i, j: (bi, 0, 0)),
                  pl.BlockSpec((1, 1, d), lambda bi, i, j: (bi, 0, 0)),
                  pl.BlockSpec((d, tn), lambda bi, i, j: (0, j + col_blk_off))],
        out_specs=pl.BlockSpec((1, tm, tn), lambda bi, i, j: (bi, i, j)),
        out_shape=jax.ShapeDtypeStruct((b, l, n_out), BF16),
        scratch_shapes=[pltpu.VMEM((tm, d), BF16)],
        compiler_params=_cparams(),
        name="mod_proj",
    )(x, g.reshape(1, d), scale, shift, w_bf16)


def _rope_tables(l):
    half = HEAD_DIM // 2
    quarter = half // 2
    inv_freq = 1.0 / (ROPE_THETA ** (jnp.arange(0, half, 2, dtype=F32) / half))
    rows = (jnp.arange(l) // GRID_W).astype(F32)
    cols = (jnp.arange(l) % GRID_W).astype(F32)

    def cs(pos):
        ang = pos[:, None] * inv_freq[None, :]
        return jnp.cos(ang), jnp.sin(ang)

    cr, sr = cs(rows)
    cc, sc = cs(cols)
    zero = jnp.zeros((l, quarter), F32)
    cos = jnp.concatenate([cr, cr, cc, cc], axis=-1)
    s_lo = jnp.concatenate([-sr, zero, -sc, zero], axis=-1)
    s_hi = jnp.concatenate([zero, sr, zero, sc], axis=-1)
    return cos, s_lo, s_hi


def _attn_bias_plan(rows_n):
    kh = min(WIN_ROWS, rows_n)
    plan = []
    for r0, sb in [(0, 0), (Q_ROWS, 0), (rows_n - Q_ROWS, rows_n - BAND_ROWS)]:
        qrow = r0 + np.arange(Q_ROWS)
        krow = sb + np.arange(BAND_ROWS)
        rs = np.clip(qrow - kh // 2, 0, rows_n - kh)
        ok = (krow[None, :] >= rs[:, None]) & (krow[None, :] < rs[:, None] + kh)
        ro = krow[None, :] - qrow[:, None] + (WIN_ROWS - 1)
        plan.append(np.where(ok, ro, 2 * WIN_ROWS - 1))
    return np.stack(plan)


def _attn_toe(rpb):
    col = np.arange(GRID_W)
    cstart = np.clip(col - WIN_COLS // 2, 0, GRID_W - WIN_COLS)
    ok_c = (col[None, :] >= cstart[:, None]) & (col[None, :] < cstart[:, None] + WIN_COLS)
    co = np.clip(col[None, :] - col[:, None], -(WIN_COLS - 1), WIN_COLS - 1) + (WIN_COLS - 1)
    onehot = (co[None] == np.arange(2 * WIN_COLS - 1)[:, None, None]).astype(np.float32)
    toe = jnp.einsum('hrc,cqk->hrqk', rpb.astype(F32), onehot, precision=lax.Precision.HIGHEST)
    toe = jnp.where(ok_c[None, None], toe, NEG_INF)
    return jnp.concatenate([toe, jnp.full((rpb.shape[0], 1, GRID_W, GRID_W), NEG_INF, F32)], axis=1)


def _attn_kernel(q_ref, k_ref, v_ref, kc_ref, vc_ref, toe_ref, cos_ref, slo_ref, shi_ref, o_ref,
                 qr_scr, qs_scr, kr_scr, bias_scr, *, n_blk, rows_n, plan):
    quarter = HEAD_DIM // 4
    qn = Q_ROWS * GRID_W
    kn = BAND_ROWS * GRID_W
    scale = HEAD_DIM ** -0.5

    def rope(x, sl):
        return (x * cos_ref[sl, :] + pltpu.roll(x, HEAD_DIM - quarter, 1) * slo_ref[sl, :]
                + pltpu.roll(x, quarter, 1) * shi_ref[sl, :])

    def rope_body(i, _):
        sl = pl.ds(pl.multiple_of(i * qn, qn), qn)
        q = q_ref[0, sl, :].astype(F32) * scale
        qs_scr[sl, :] = q.astype(BF16)
        qr_scr[sl, :] = rope(q, sl).astype(BF16)
        kr_scr[sl, :] = rope(k_ref[0, sl, :].astype(F32), sl).astype(BF16)
        return 0

    lax.fori_loop(0, n_blk, rope_body, 0)

    @pl.when(pl.program_id(1) == 0)
    def _():
        for case in range(plan.shape[0]):
            for qi in range(Q_ROWS):
                for kj in range(BAND_ROWS):
                    bias_scr[case, qi * GRID_W:(qi + 1) * GRID_W, kj * GRID_W:(kj + 1) * GRID_W] = (
                        toe_ref[0, int(plan[case, qi, kj])])

    kc = kc_ref[0]
    vc = vc_ref[0]
    nt = (((1,), (1,)), ((), ()))

    def body(i, _):
        sb = jnp.clip(i * Q_ROWS - WIN_ROWS // 2, 0, rows_n - BAND_ROWS)
        ks = pl.ds(pl.multiple_of(sb * GRID_W, GRID_W), kn)
        qs = pl.ds(pl.multiple_of(i * qn, qn), qn)
        case = jnp.where(i == 0, 0, jnp.where(i == n_blk - 1, 2, 1))
        s = lax.dot_general(qr_scr[qs, :], kr_scr[ks, :], nt, preferred_element_type=F32) + bias_scr[case]
        sc = lax.dot_general(qs_scr[qs, :], kc, nt, preferred_element_type=F32)
        m = jnp.maximum(jnp.max(s, axis=-1, keepdims=True), jnp.max(sc, axis=-1, keepdims=True))
        p = jnp.exp(s - m)
        pc = jnp.exp(sc - m)
        den = jnp.sum(p, axis=-1, keepdims=True) + jnp.sum(pc, axis=-1, keepdims=True)
        o = (jnp.dot(p.astype(BF16), v_ref[0, ks, :], preferred_element_type=F32)
             + jnp.dot(pc.astype(BF16), vc, preferred_element_type=F32))
        o_ref[0, qs, :] = (o / den).astype(o_ref.dtype)
        return 0

    lax.fori_loop(0, n_blk, body, 0, unroll=4)


def neighborhood_attention(p_lat, p_ctx, rpb, n_heads):
    b, l, _ = p_lat.shape
    lc = p_ctx.shape[1]
    rows_n = l // GRID_W
    assert rows_n % Q_ROWS == 0 and rows_n >= BAND_ROWS + Q_ROWS
    n_blk = rows_n // Q_ROWS
    plan = _attn_bias_plan(rows_n)
    cos, s_lo, s_hi = _rope_tables(l)
    qn, kn = Q_ROWS * GRID_W, BAND_ROWS * GRID_W
    h = n_heads
    tab = pl.BlockSpec((l, HEAD_DIM), lambda hi, bi: (0, 0))
    return pl.pallas_call(
        functools.partial(_attn_kernel, n_blk=n_blk, rows_n=rows_n, plan=plan),
        grid=(h, b),
        in_specs=[pl.BlockSpec((1, l, HEAD_DIM), lambda hi, bi: (bi, 0, hi)),
                  pl.BlockSpec((1, l, HEAD_DIM), lambda hi, bi: (bi, 0, hi + h)),
                  pl.BlockSpec((1, l, HEAD_DIM), lambda hi, bi: (bi, 0, hi + 2 * h)),
                  pl.BlockSpec((1, lc, HEAD_DIM), lambda hi, bi: (bi, 0, hi)),
                  pl.BlockSpec((1, lc, HEAD_DIM), lambda hi, bi: (bi, 0, hi + h)),
                  pl.BlockSpec((1, 2 * WIN_ROWS, GRID_W, GRID_W), lambda hi, bi: (hi, 0, 0, 0)),
                  tab, tab, tab],
        out_specs=pl.BlockSpec((1, l, HEAD_DIM), lambda hi, bi: (bi, 0, hi)),
        out_shape=jax.ShapeDtypeStruct((b, l, h * HEAD_DIM), BF16),
        scratch_shapes=[pltpu.VMEM((l, HEAD_DIM), BF16)] * 3 + [pltpu.VMEM((plan.shape[0], qn, kn), F32)],
        compiler_params=_cparams(),
        name="nbr_attn",
    )(p_lat, p_lat, p_lat, p_ctx, p_ctx, _attn_toe(rpb), cos, s_lo, s_hi)


SSM_CHUNK = 16
SSM_ROWS = 8
SSM_PACK = 8


def _ssm_mats(a_re, a_im, log_dt, b_re, b_im, c_re, c_im, d_skip):
    t = SSM_CHUNK
    hi = lax.Precision.HIGHEST
    a = lax.complex(a_re.astype(F32), a_im.astype(F32))
    dta = jnp.exp(log_dt.astype(F32))[..., None] * a
    a_bar = jnp.exp(dta)
    b_bar = ((a_bar - 1.0) / a)[..., None] * lax.complex(b_re.astype(F32), b_im.astype(F32))
    cm = lax.complex(c_re.astype(F32), c_im.astype(F32))
    k = jnp.arange(t + 1, dtype=F32)
    ap = jnp.exp(dta[..., None] * k)
    g, p, n = a.shape[1], a.shape[2], b_re.shape[-1]
    kern = jnp.einsum('dgnp,dgpl,dgpm->dglnm', cm, ap[..., :t], b_bar, precision=hi).real
    s_i = np.arange(t)[:, None]
    t_i = np.arange(t)[None, :]
    kf = kern[0][:, np.clip(t_i - s_i, 0, t - 1)]
    kb = kern[1][:, np.clip(s_i - t_i, 0, t - 1)]
    eye = jnp.eye(n, dtype=F32)
    m_mat = (jnp.where((s_i <= t_i)[None, :, :, None, None], kf, 0.0)
             + jnp.where((s_i >= t_i)[None, :, :, None, None], kb, 0.0)
             + (s_i == t_i)[None, :, :, None, None] * (d_skip.astype(F32)[:, None, None, :, None] * eye))
    m_mat = jnp.transpose(m_mat, (0, 1, 4, 2, 3)).reshape(g, t * n, t * n)
    wf = ap[0][:, :, t - 1::-1][..., :t, None] * b_bar[0][:, :, None, :]
    wb = ap[1][:, :, :t, None] * b_bar[1][:, :, None, :]
    to_rows = lambda z: jnp.transpose(z, (0, 2, 3, 1)).reshape(g, t * n, p)
    w_mat = jnp.concatenate([to_rows(wf.real), to_rows(wb.real), to_rows(wf.imag), to_rows(wb.imag)], axis=-1)
    zf = jnp.transpose(cm[0], (0, 2, 1))[:, :, None, :] * ap[0][:, :, 1:t + 1, None]
    zb = jnp.transpose(cm[1], (0, 2, 1))[:, :, None, :] * ap[1][:, :, t:0:-1, None]
    flat = lambda z: z.reshape(g, p, t * n)
    r_mat = jnp.concatenate([flat(zf.real), flat(zb.real), -flat(zf.imag), -flat(zb.imag)], axis=1)
    a_t = ap[..., t]
    a_vec = jnp.stack([jnp.concatenate([a_t[0].real, a_t[1].real], -1),
                       jnp.concatenate([a_t[0].imag, a_t[1].imag], -1)], axis=1)
    a_vec = jnp.pad(a_vec, ((0, 0), (0, SSM_ROWS - 2), (0, 0)))
    return m_mat.astype(BF16), w_mat.astype(BF16), r_mat.astype(BF16), a_vec


def _s5_kernel(u_ref, sel_ref, selt_ref, m_ref, w_ref, r_ref, a_ref, y_ref, v_scr, ent_scr, *, n_ctx, n_chunks):
    u = jnp.dot(u_ref[0], sel_ref[0], preferred_element_type=F32).astype(BF16)
    p2 = a_ref.shape[-1]
    p = p2 // 2
    v_scr[...] = jnp.dot(u, w_ref[0], preferred_element_type=F32)
    a_re = jnp.broadcast_to(a_ref[0, 0:1, :], (SSM_ROWS, p2))
    a_im = jnp.broadcast_to(a_ref[0, 1:2, :], (SSM_ROWS, p2))
    is_fwd = lax.broadcasted_iota(I32, (SSM_ROWS, p2), 1) < p

    def step(j, carry):
        s_re, s_im = carry
        cb = jnp.where(j < n_ctx, n_ctx - 1 - j, n_chunks + n_ctx - 1 - j)
        rf = pl.ds(pl.multiple_of(j * SSM_ROWS, SSM_ROWS), SSM_ROWS)
        rb = pl.ds(pl.multiple_of(cb * SSM_ROWS, SSM_ROWS), SSM_ROWS)
        ent_scr[rf, 0:p] = s_re[:, :p]
        ent_scr[rb, p:p2] = s_re[:, p:]
        ent_scr[rf, p2:p2 + p] = s_im[:, :p]
        ent_scr[rb, p2 + p:] = s_im[:, p:]
        in_re = jnp.where(is_fwd, v_scr[rf, :p2], v_scr[rb, :p2])
        in_im = jnp.where(is_fwd, v_scr[rf, p2:], v_scr[rb, p2:])
        return a_re * s_re - a_im * s_im + in_re, a_re * s_im + a_im * s_re + in_im

    zero = jnp.zeros((SSM_ROWS, p2), F32)
    lax.fori_loop(0, n_chunks, step, (zero, zero))
    y = (jnp.dot(u, m_ref[0], preferred_element_type=F32)
         + jnp.dot(ent_scr[...].astype(BF16), r_ref[0], preferred_element_type=F32)).astype(BF16)
    def part(c):
        return jnp.dot(y, selt_ref[0, :, c:c + PACK_CHUNK], preferred_element_type=F32).astype(y_ref.dtype)

    @pl.when(pl.program_id(1) == 0)
    def _():
        for c in range(0, y_ref.shape[-1], PACK_CHUNK):
            y_ref[0, :, c:c + PACK_CHUNK] = part(c)

    @pl.when(pl.program_id(1) > 0)
    def _():
        for c in range(0, y_ref.shape[-1], PACK_CHUNK):
            y_ref[0, :, c:c + PACK_CHUNK] = y_ref[0, :, c:c + PACK_CHUNK] + part(c)


def s5_mixer(u_lat, u_ctx, mats):
    m_mat, w_mat, r_mat, a_vec = mats
    b, l, dch = u_lat.shape
    lc = u_ctx.shape[1]
    g = m_mat.shape[0]
    n = dch // g
    t = SSM_CHUNK
    tn = t * n
    gp = SSM_PACK
    packs, lanes = g // gp, gp * n
    n_ctx, n_chunks = lc // t, (lc + l) // t
    assert b <= SSM_ROWS and lc % t == 0 and l % t == 0 and g % gp == 0
    z = jnp.concatenate([u_ctx, u_lat], axis=1).reshape(b, n_chunks, t, packs, lanes)
    z = jnp.pad(jnp.transpose(z, (3, 1, 0, 2, 4)), ((0, 0), (0, 0), (0, SSM_ROWS - b), (0, 0), (0, 0)))
    rows = n_chunks * SSM_ROWS
    z = z.reshape(packs, rows, t * lanes)
    ri = jnp.arange(t * lanes)
    ci = jnp.arange(tn)
    same = (ri[:, None] // lanes == ci[None, :] // n) & (ri[:, None] % n == ci[None, :] % n)
    sel = (same[None] & ((ri[None, :, None] % lanes) // n == jnp.arange(gp)[:, None, None])).astype(BF16)
    selt = jnp.transpose(sel, (0, 2, 1))
    grp = lambda k: pl.BlockSpec((1, k, tn), lambda pi, qi: (pi * gp + qi, 0, 0))
    pack_blk = pl.BlockSpec((1, rows, t * lanes), lambda pi, qi: (pi, 0, 0))
    y = pl.pallas_call(
        functools.partial(_s5_kernel, n_ctx=n_ctx, n_chunks=n_chunks),
        grid=(packs, gp),
        in_specs=[pack_blk,
                  pl.BlockSpec((1, t * lanes, tn), lambda pi, qi: (qi, 0, 0)),
                  pl.BlockSpec((1, tn, t * lanes), lambda pi, qi: (qi, 0, 0)),
                  grp(tn), grp(tn), grp(r_mat.shape[1]),
                  pl.BlockSpec((1, SSM_ROWS, a_vec.shape[-1]), lambda pi, qi: (pi * gp + qi, 0, 0))],
        out_specs=pack_blk,
        out_shape=jax.ShapeDtypeStruct((packs, rows, t * lanes), BF16),
        scratch_shapes=[pltpu.VMEM((rows, w_mat.shape[-1]), F32), pltpu.VMEM((rows, r_mat.shape[1]), F32)],
        compiler_params=_cparams(),
        name="s5_mixer",
    )(z, sel, selt, m_mat, w_mat, r_mat, a_vec)
    y = y.reshape(packs, n_chunks, SSM_ROWS, t, lanes)[:, n_ctx:, :b]
    return jnp.transpose(y, (2, 1, 3, 0, 4)).reshape(b, l, dch)


def _glu_kernel(y_ref, w_ref, b_ref, o_ref):
    z = jax.nn.gelu(y_ref[...].astype(F32))
    gate = jax.nn.sigmoid(jnp.dot(z.astype(BF16), w_ref[...], preferred_element_type=F32) + b_ref[...])
    o_ref[...] = (z * gate).astype(o_ref.dtype)


def s5_glu(y, w_bf16, b_glu, tm=512):
    n, d = y.shape
    return pl.pallas_call(
        _glu_kernel,
        grid=(n // tm,),
        in_specs=[pl.BlockSpec((tm, d), lambda i: (i, 0)),
                  pl.BlockSpec((d, d), lambda i: (0, 0)),
                  pl.BlockSpec((1, d), lambda i: (0, 0))],
        out_specs=pl.BlockSpec((tm, d), lambda i: (i, 0)),
        out_shape=jax.ShapeDtypeStruct((n, d), BF16),
        compiler_params=_cparams(),
        name="s5_glu",
    )(y, w_bf16, b_glu.reshape(1, d))


def _merge_kernel(a_ref, s_ref, ga_ref, gs_ref, w_ref, o_ref, h_scr):
    da = a_ref.shape[-1]

    @pl.when(pl.program_id(1) == 0)
    def _():
        h_scr[:, :da] = _rms(a_ref[...].astype(F32), ga_ref[...]).astype(BF16)
        h_scr[:, da:] = _rms(s_ref[...].astype(F32), gs_ref[...]).astype(BF16)

    o_ref[...] = jnp.dot(h_scr[...], w_ref[...], preferred_element_type=F32)


def merge_proj(attn, ssm, g_attn, g_ssm, w_bf16, tm=512, tn=1024):
    n, da = attn.shape
    ds = ssm.shape[1]
    d, dout = w_bf16.shape
    return pl.pallas_call(
        _merge_kernel,
        grid=(n // tm, dout // tn),
        in_specs=[pl.BlockSpec((tm, da), lambda i, j: (i, 0)),
                  pl.BlockSpec((tm, ds), lambda i, j: (i, 0)),
                  pl.BlockSpec((1, da), lambda i, j: (0, 0)),
                  pl.BlockSpec((1, ds), lambda i, j: (0, 0)),
                  pl.BlockSpec((d, tn), lambda i, j: (0, j))],
        out_specs=pl.BlockSpec((tm, tn), lambda i, j: (i, j)),
        out_shape=jax.ShapeDtypeStruct((n, dout), F32),
        scratch_shapes=[pltpu.VMEM((tm, d), BF16)],
        compiler_params=_cparams(),
        name="merge_proj",
    )(attn, ssm, g_attn.reshape(1, da), g_ssm.reshape(1, ds), w_bf16)


def _post_mix_kernel(x_ref, mix_ref, gpost_ref, gt_ref, gpre_ref, sc_ref, sh_ref, wr_ref, br_ref,
                     x1_ref, hp_ref, idx_ref, gate_ref, mask_ref):
    x1 = x_ref[0] + gt_ref[0] * _rms(mix_ref[0], gpost_ref[...])
    x1_ref[0] = x1
    h = _rms(x1, gpre_ref[...]) * (1.0 + sc_ref[0]) + sh_ref[0]
    half = h.shape[-1] // 2
    hp_ref[0] = _pack_halves(h[:, :half], h[:, half:])
    scores = jax.nn.sigmoid(jnp.dot(h.astype(BF16), wr_ref[...], preferred_element_type=F32))
    n_e = scores.shape[-1]
    lane = lax.broadcasted_iota(I32, scores.shape, 1)
    biased = scores + br_ref[...]
    idx_out = jnp.zeros(scores.shape, I32)
    sel_out = jnp.zeros(scores.shape, F32)
    mask = jnp.zeros(scores.shape, jnp.bool_)
    for k in range(TOP_K):
        m = jnp.max(biased, axis=-1, keepdims=True)
        ik = jnp.min(jnp.where(biased == m, lane, n_e), axis=-1, keepdims=True)
        hit = lane == ik
        sel_k = jnp.sum(jnp.where(hit, scores, 0.0), axis=-1, keepdims=True)
        idx_out = jnp.where(lane == k, ik, idx_out)
        sel_out = jnp.where(lane == k, sel_k, sel_out)
        mask = jnp.logical_or(mask, hit)
        biased = jnp.where(hit, -jnp.inf, biased)
    idx_ref[0] = idx_out
    gate_ref[0] = sel_out / jnp.sum(sel_out, axis=-1, keepdims=True) * ROUTED_SCALE
    mask_ref[0] = mask.astype(BF16)


def post_mix(x, mix, g_post, gt, g_pre, scale, shift, wr_bf16, b_router, tm=256):
    b, l, d = x.shape
    n_e = wr_bf16.shape[1]
    row = pl.BlockSpec((1, tm, d), lambda bi, i: (bi, i, 0))
    prow = pl.BlockSpec((1, tm, d // 2), lambda bi, i: (bi, i, 0))
    vec = pl.BlockSpec((1, d), lambda bi, i: (0, 0))
    bvec = pl.BlockSpec((1, 1, d), lambda bi, i: (bi, 0, 0))
    small = pl.BlockSpec((1, tm, n_e), lambda bi, i: (bi, i, 0))
    return pl.pallas_call(
        _post_mix_kernel,
        grid=(b, l // tm),
        in_specs=[row, row, vec, bvec, vec, bvec, bvec,
                  pl.BlockSpec((d, n_e), lambda bi, i: (0, 0)),
                  pl.BlockSpec((1, n_e), lambda bi, i: (0, 0))],
        out_specs=[row, prow, small, small, small],
        out_shape=[jax.ShapeDtypeStruct((b, l, d), F32),
                   jax.ShapeDtypeStruct((b, l, d // 2), I32),
                   jax.ShapeDtypeStruct((b, l, n_e), I32),
                   jax.ShapeDtypeStruct((b, l, n_e), F32),
                   jax.ShapeDtypeStruct((b, l, n_e), BF16)],
        compiler_params=_cparams(),
        name="post_mix",
    )(x, mix, g_post.reshape(1, d), gt, g_pre.reshape(1, d), scale, shift, wr_bf16, b_router.reshape(1, n_e))


def _rank_kernel(mask_ref, idx_ref, rank_ref, cnt_ref, carry):
    @pl.when(pl.program_id(0) == 0)
    def _():
        carry[...] = jnp.zeros_like(carry)

    m = mask_ref[...]
    tm = m.shape[0]
    earlier = (lax.broadcasted_iota(I32, (tm, tm), 1) < lax.broadcasted_iota(I32, (tm, tm), 0)).astype(BF16)
    excl = jnp.dot(earlier, m, preferred_element_type=F32) + carry[0:1, :]
    lane = lax.broadcasted_iota(I32, m.shape, 1)
    idx = idx_ref[...]
    out = jnp.zeros(m.shape, F32)
    for k in range(TOP_K):
        rk = jnp.sum(jnp.where(lane == idx[:, k:k + 1], excl, 0.0), axis=-1, keepdims=True)
        out = jnp.where(lane == k, rk, out)
    rank_ref[...] = out.astype(I32)
    carry[0:1, :] = carry[0:1, :] + jnp.sum(m.astype(F32), axis=0, keepdims=True)
    cnt_ref[...] = carry[...]


def route_ranks(mask, idx, tm=512):
    n, n_e = mask.shape
    blk = pl.BlockSpec((tm, n_e), lambda i: (i, 0))
    rank, cnt = pl.pallas_call(
        _rank_kernel,
        grid=(n // tm,),
        in_specs=[blk, blk],
        out_specs=[blk, pl.BlockSpec((8, n_e), lambda i: (0, 0))],
        out_shape=[jax.ShapeDtypeStruct((n, n_e), I32), jax.ShapeDtypeStruct((8, n_e), F32)],
        scratch_shapes=[pltpu.VMEM((8, n_e), F32)],
        compiler_params=_cparams(),
        name="route_ranks",
    )(mask, idx)
    return rank, cnt[0].astype(I32)


def _dest_kernel(rank_ref, idx_ref, ps_ref, dest_ref):
    idx = idx_ref[...]
    lane = lax.broadcasted_iota(I32, idx.shape, 1)
    start = jnp.zeros(idx.shape, F32)
    for k in range(TOP_K):
        sk = jnp.sum(jnp.where(lane == idx[:, k:k + 1], ps_ref[...], 0.0), axis=-1, keepdims=True)
        start = jnp.where(lane == k, sk, start)
    dest_ref[...] = rank_ref[...] + start.astype(I32)


def route_dest(rank, idx, pad_start, tm=512):
    n, n_e = rank.shape
    blk = pl.BlockSpec((tm, n_e), lambda i: (i, 0))
    dest = pl.pallas_call(
        _dest_kernel,
        grid=(n // tm,),
        in_specs=[blk, blk, pl.BlockSpec((1, n_e), lambda i: (0, 0))],
        out_specs=blk,
        out_shape=jax.ShapeDtypeStruct((n, n_e), I32),
        compiler_params=_cparams(),
        name="route_dest",
    )(rank, idx, pad_start.astype(F32).reshape(1, n_e))
    return dest[:, :TOP_K].reshape(n * TOP_K)


def _dispatch_kernel(pe_ref, pd_ref, dest_ref, h_ref, wgu_ref, wd_ref, xs_ref, sp_ref, z_scr, sem, *, n_e, tm, blk):
    def zero_copy(e):
        start = pl.multiple_of(pe_ref[e] - blk, blk)
        return pltpu.make_async_copy(z_scr, xs_ref.at[pl.ds(start, blk), :], sem)

    @pl.when(pl.program_id(0) == 0)
    def _():
        z_scr[...] = jnp.zeros_like(z_scr)

        def start(e, _):
            @pl.when(pd_ref[e] > 0)
            def _():
                zero_copy(e).start()
            return 0

        def wait(e, _):
            @pl.when(pd_ref[e] > 0)
            def _():
                zero_copy(e).wait()
            return 0

        lax.fori_loop(0, n_e, start, 0)
        lax.fori_loop(0, n_e, wait, 0)

    sub = 8

    def issue(j, _):
        base = pl.multiple_of(j * sub, sub)
        tile = h_ref.at[pl.ds(base, sub), :]
        for tt in range(sub):
            for k in range(TOP_K):
                pltpu.make_async_copy(tile.at[pl.ds(tt, 1), :],
                                      xs_ref.at[pl.ds(dest_ref[(base + tt) * TOP_K + k], 1), :],
                                      sem).start(priority=k % 2)
        return 0

    lax.fori_loop(0, tm // sub, issue, 0)
    hid = _swiglu_hidden(h_ref, lambda sl: wgu_ref[sl, :])
    _down_packed(hid, lambda sl: wd_ref[:, sl], sp_ref)
    rows = tm * TOP_K
    pltpu.make_async_copy(xs_ref.at[pl.ds(0, rows), :], xs_ref.at[pl.ds(0, rows), :], sem).wait()


def moe_dispatch(hp, dest_flat, pad_end, padded, slots, wsgu, wsd, tm=DISPATCH_TM):
    n, dw = hp.shape
    n_e = pad_end.shape[0]
    d, de2 = wsgu.shape
    row = pl.BlockSpec((tm, dw), lambda i, pe, pd: (i, 0))
    grid_spec = pltpu.PrefetchScalarGridSpec(
        num_scalar_prefetch=2,
        grid=(n // tm,),
        in_specs=[pl.BlockSpec((tm * TOP_K,), lambda i, pe, pd: (i,), memory_space=pltpu.SMEM),
                  row,
                  pl.BlockSpec((d, de2), lambda i, pe, pd: (0, 0)),
                  pl.BlockSpec((de2 // 2, d), lambda i, pe, pd: (0, 0))],
        out_specs=[pl.BlockSpec(memory_space=pl.ANY), row],
        scratch_shapes=[pltpu.VMEM((EXPERT_BLK, dw), I32), pltpu.SemaphoreType.DMA(())],
    )
    return pl.pallas_call(
        functools.partial(_dispatch_kernel, n_e=n_e, tm=tm, blk=EXPERT_BLK),
        grid_spec=grid_spec,
        out_shape=[jax.ShapeDtypeStruct((slots, dw), I32), jax.ShapeDtypeStruct((n, dw), I32)],
        compiler_params=_cparams(),
        name="moe_dispatch",
    )(pad_end, padded, dest_flat, hp, wsgu, wsd)


PACK_CHUNK = 512


def _swiglu_hidden(xp_ref, wgu):
    half = xp_ref.shape[-1]
    acc = None
    for c in range(half // PACK_CHUNK):
        cl = slice(c * PACK_CHUNK, (c + 1) * PACK_CHUNK)
        ch = slice(half + c * PACK_CHUNK, half + (c + 1) * PACK_CHUNK)
        lo, hi = _unpack_halves(xp_ref[:, cl])
        part = (jnp.dot(lo.astype(BF16), wgu(cl), preferred_element_type=F32)
                + jnp.dot(hi.astype(BF16), wgu(ch), preferred_element_type=F32))
        acc = part if acc is None else acc + part
    de = acc.shape[-1] // 2
    return (jax.nn.silu(acc[:, :de]) * acc[:, de:]).astype(BF16)


def _down_packed(hid, wd, o_ref):
    half = o_ref.shape[-1]
    for c in range(half // PACK_CHUNK):
        cl = slice(c * PACK_CHUNK, (c + 1) * PACK_CHUNK)
        ch = slice(half + c * PACK_CHUNK, half + (c + 1) * PACK_CHUNK)
        o_ref[:, cl] = _pack_halves(jnp.dot(hid, wd(cl), preferred_element_type=F32),
                                    jnp.dot(hid, wd(ch), preferred_element_type=F32))


def _expert_weights(sched, i, w_hbm, w_buf, sem):
    be_ref, first_ref, slot_ref, nxt_ref = sched
    s = slot_ref[i]

    def copies(e, sl):
        return [pltpu.make_async_copy(h.at[e], v.at[sl], sem.at[j, sl]) for j, (h, v) in enumerate(zip(w_hbm, w_buf))]

    @pl.when(first_ref[i] == 1)
    def _():
        @pl.when(i == 0)
        def _():
            for c in copies(be_ref[0], 0):
                c.start()

        for c in copies(be_ref[i], s):
            c.wait()

        @pl.when(nxt_ref[i] >= 0)
        def _():
            for c in copies(nxt_ref[i], 1 - s):
                c.start()

    return s


def _expert_up_kernel(be_ref, nu_ref, first_ref, slot_ref, nxt_ref, x_ref, wg_hbm, wu_hbm, hid_ref,
                      wg_buf, wu_buf, sem):
    i = pl.program_id(0)
    used = i < nu_ref[0]
    s = _expert_weights((be_ref, first_ref, slot_ref, nxt_ref), i, (wg_hbm, wu_hbm), (wg_buf, wu_buf), sem)

    def wgu(sl):
        return jnp.concatenate([wg_buf[s, sl, :].astype(BF16), wu_buf[s, sl, :].astype(BF16)], axis=-1)

    @pl.when(used)
    def _():
        hid_ref[...] = _swiglu_hidden(x_ref, wgu)

    @pl.when(jnp.logical_not(used))
    def _():
        hid_ref[...] = jnp.zeros_like(hid_ref)


def _expert_down_kernel(be_ref, nu_ref, first_ref, slot_ref, nxt_ref, hid_ref, wd_hbm, o_ref, wd_buf, sem):
    i = pl.program_id(0)
    used = i < nu_ref[0]
    s = _expert_weights((be_ref, first_ref, slot_ref, nxt_ref), i, (wd_hbm,), (wd_buf,), sem)

    @pl.when(used)
    def _():
        _down_packed(hid_ref[...], lambda sl: wd_buf[s, :, sl].astype(BF16), o_ref)

    @pl.when(jnp.logical_not(used))
    def _():
        o_ref[...] = jnp.zeros_like(o_ref)


def _expert_schedule(block_e, n_used):
    nb = block_e.shape[0]
    ar = jnp.arange(nb, dtype=I32)
    used = ar < n_used[0]
    first = used & ((ar == 0) | (block_e != jnp.roll(block_e, 1)))
    slot = jnp.where(used, (jnp.cumsum(first.astype(I32)) - 1) % 2, 0).astype(I32)
    nf = jnp.where(first, ar, nb)
    nxt_idx = jnp.concatenate([lax.cummin(nf, axis=0, reverse=True)[1:], jnp.full((1,), nb, I32)])
    nxt = jnp.where(nxt_idx < nb, block_e[jnp.minimum(nxt_idx, nb - 1)], -1).astype(I32)
    return first.astype(I32), slot, nxt


def expert_ffn(xs, block_e, n_used, wg, wu, wd, blk):
    slots, dw = xs.shape
    d, de = wg.shape[1], wg.shape[2]
    n_blocks = slots // blk
    sched = (block_e, n_used) + _expert_schedule(block_e, n_used)
    cur = lambda i, nu: jnp.minimum(i, nu[0] - 1)
    hbm = pl.BlockSpec(memory_space=pl.ANY)
    up_spec = pltpu.PrefetchScalarGridSpec(
        num_scalar_prefetch=5,
        grid=(n_blocks,),
        in_specs=[pl.BlockSpec((blk, dw), lambda i, be, nu, *_: (cur(i, nu), 0)), hbm, hbm],
        out_specs=pl.BlockSpec((blk, de), lambda i, *_: (i, 0)),
        scratch_shapes=[pltpu.VMEM((2, d, de), F32), pltpu.VMEM((2, d, de), F32), pltpu.SemaphoreType.DMA((2, 2))],
    )
    hid = pl.pallas_call(
        _expert_up_kernel,
        grid_spec=up_spec,
        out_shape=jax.ShapeDtypeStruct((slots, de), BF16),
        compiler_params=_cparams(),
        name="expert_up",
    )(*sched, xs, wg, wu)
    down_spec = pltpu.PrefetchScalarGridSpec(
        num_scalar_prefetch=5,
        grid=(n_blocks,),
        in_specs=[pl.BlockSpec((blk, de), lambda i, be, nu, *_: (cur(i, nu), 0)), hbm],
        out_specs=pl.BlockSpec((blk, dw), lambda i, *_: (i, 0)),
        scratch_shapes=[pltpu.VMEM((2, de, d), F32), pltpu.SemaphoreType.DMA((1, 2))],
    )
    return pl.pallas_call(
        _expert_down_kernel,
        grid_spec=down_spec,
        out_shape=jax.ShapeDtypeStruct((slots, dw), I32),
        compiler_params=_cparams(),
        name="expert_down",
    )(*sched, hid, wd)


def _final_kernel(dc_ref, dn_ref, x1_ref, sp_ref, gate_ref, ys_ref, gpost_ref, gt_ref, o_ref, buf, sem, *, tm, n_tiles):
    i = pl.program_id(0)
    slot = i % 2
    sub = 8

    def issue_rows(dref, s, j):
        base = pl.multiple_of(j * sub, sub)
        for tt in range(sub):
            for k in range(TOP_K):
                tile = buf.at[s, pl.ds(pl.multiple_of(k * tm + base, sub), sub), :]
                pltpu.make_async_copy(ys_ref.at[pl.ds(dref[(base + tt) * TOP_K + k], 1), :],
                                      tile.at[pl.ds(tt, 1), :],
                                      sem.at[s]).start(priority=k % 2)

    def reduce_rows(s, j):
        rs = pl.ds(pl.multiple_of(j * sub, sub), sub)
        g = gate_ref[rs, :]
        acc_lo, acc_hi = _unpack_halves(sp_ref[rs, :])
        for k in range(TOP_K):
            lo, hi = _unpack_halves(buf[s, pl.ds(pl.multiple_of(k * tm + j * sub, sub), sub), :])
            gk = g[:, k:k + 1]
            acc_lo = acc_lo + gk * lo
            acc_hi = acc_hi + gk * hi
        ffn = jnp.concatenate([acc_lo, acc_hi], axis=-1)
        o_ref[rs, :] = x1_ref[rs, :] + gt_ref[0] * _rms(ffn, gpost_ref[...])

    @pl.when(i == 0)
    def _():
        def first(j, c):
            issue_rows(dc_ref, 0, j)
            return c
        lax.fori_loop(0, tm // sub, first, 0)

    def wait_slot(s):
        pltpu.make_async_copy(ys_ref.at[pl.ds(0, TOP_K * tm), :], buf.at[s], sem.at[s]).wait()

    wait_slot(slot)

    for s in range(2):
        @pl.when(slot == s)
        def _():
            def body(j, c):
                issue_rows(dn_ref, 1 - s, j)
                reduce_rows(s, j)
                return c
            lax.fori_loop(0, tm // sub, body, 0, unroll=4)

    @pl.when(i == n_tiles - 1)
    def _():
        wait_slot(1 - slot)


def final_mix(x1, sp, gates, ys, dest_flat, g_post, gt, tiles_per_batch, tm=COMBINE_TM):
    n, d = x1.shape
    dw = sp.shape[1]
    n_e = gates.shape[1]
    n_tiles = n // tm
    return pl.pallas_call(
        functools.partial(_final_kernel, tm=tm, n_tiles=n_tiles),
        grid=(n_tiles,),
        in_specs=[pl.BlockSpec((tm * TOP_K,), lambda i: (i,), memory_space=pltpu.SMEM),
                  pl.BlockSpec((tm * TOP_K,), lambda i: (jnp.minimum(i + 1, n_tiles - 1),), memory_space=pltpu.SMEM),
                  pl.BlockSpec((tm, d), lambda i: (i, 0)),
                  pl.BlockSpec((tm, dw), lambda i: (i, 0)),
                  pl.BlockSpec((tm, n_e), lambda i: (i, 0)),
                  pl.BlockSpec(memory_space=pl.ANY),
                  pl.BlockSpec((1, d), lambda i: (0, 0)),
                  pl.BlockSpec((1, 1, d), lambda i: (i // tiles_per_batch, 0, 0))],
        out_specs=pl.BlockSpec((tm, d), lambda i: (i, 0)),
        out_shape=jax.ShapeDtypeStruct((n, d), F32),
        scratch_shapes=[pltpu.VMEM((2, TOP_K * tm, dw), I32), pltpu.SemaphoreType.DMA((2,))],
        compiler_params=_cparams(),
        name="final_mix",
    )(dest_flat, dest_flat, x1, sp, gates, ys, g_post.reshape(1, d), gt)


def kernel(x, c, ctx, c_ctx, w_ada, b_ada, g_pre_mix, g_post_mix, g_pre_ffn, g_post_ffn, w_in, rpb, ssm_a_re, ssm_a_im, ssm_log_dt, ssm_b_re, ssm_b_im, ssm_c_re, ssm_c_im, ssm_d, w_glu, b_glu, g_attn_out, g_ssm_out, w_out, w_router, b_router, w_exp_gate, w_exp_up, w_exp_down, w_sh_gate, w_sh_up, w_sh_down):
    b, l, d = x.shape
    lc = ctx.shape[1]
    assert w_ada.shape[0] == 1 and b + 1 <= 8
    n_in = w_in.shape[-1]
    d_ssm = w_glu.shape[-1]
    d_attn = d - d_ssm
    n_heads = d_attn // HEAD_DIM
    n = b * l

    c8 = jnp.concatenate([c, c_ctx[None], jnp.zeros((8 - b - 1, d), F32)], axis=0)
    mod = ada_mod(c8, w_ada[0], b_ada[0]).reshape(8, 6, 1, d)
    sh_m, sc_m, gt_m, sh_f, sc_f, gt_f = [mod[:b, j] for j in range(6)]
    csh_m, csc_m = mod[b:b + 1, 0], mod[b:b + 1, 1]

    w_in_b = w_in[0].astype(BF16)
    tn = PROJ_TN
    p_lat = mod_proj(x, g_pre_mix[0], sc_m, sh_m, w_in_b, 0, n_in, 512, tn)
    p_ctx = mod_proj(ctx.reshape(1, b * lc, d), g_pre_mix[0], csc_m, csh_m, w_in_b,
                     d_attn // tn, n_in - d_attn, 512, tn).reshape(b, lc, n_in - d_attn)

    attn = neighborhood_attention(p_lat, p_ctx, rpb[0], n_heads)

    mats = _ssm_mats(ssm_a_re[0], ssm_a_im[0], ssm_log_dt[0], ssm_b_re[0], ssm_b_im[0],
                     ssm_c_re[0], ssm_c_im[0], ssm_d[0])
    y = s5_mixer(p_lat[..., 3 * d_attn:], p_ctx[..., 2 * d_attn:], mats)
    ssm = s5_glu(y.reshape(n, d_ssm), w_glu[0].astype(BF16), b_glu[0])

    mix = merge_proj(attn.reshape(n, d_attn), ssm, g_attn_out[0], g_ssm_out[0], w_out[0].astype(BF16))
    x1, hp, idx, gates, mask = post_mix(x, mix.reshape(b, l, d), g_post_mix[0], gt_m, g_pre_ffn[0], sc_f, sh_f,
                                        w_router[0].astype(BF16), b_router[0])

    n_e = w_router.shape[-1]
    m = n * TOP_K
    idx = idx.reshape(n, n_e)
    rank, counts = route_ranks(mask.reshape(n, n_e), idx)
    padded = (counts + EXPERT_BLK - 1) // EXPERT_BLK * EXPERT_BLK
    pad_end = jnp.cumsum(padded).astype(I32)
    dest = route_dest(rank, idx, pad_end - padded)
    n_blocks = m // EXPERT_BLK + n_e
    slots = n_blocks * EXPERT_BLK
    block_e = jnp.minimum(jnp.searchsorted(pad_end, jnp.arange(n_blocks) * EXPERT_BLK, side='right'),
                          n_e - 1).astype(I32)
    n_used = (pad_end[-1] // EXPERT_BLK).astype(I32).reshape(1)

    hp2 = hp.reshape(n, d // 2)
    xs, sp = moe_dispatch(hp2, dest, pad_end, padded.astype(I32), slots,
                          jnp.concatenate([w_sh_gate[0], w_sh_up[0]], axis=-1).astype(BF16), w_sh_down[0].astype(BF16))
    ys = expert_ffn(xs, block_e, n_used, w_exp_gate[0], w_exp_up[0], w_exp_down[0], EXPERT_BLK)
    out = final_mix(x1.reshape(n, d), sp, gates.reshape(n, n_e), ys, dest, g_post_ffn[0], gt_f, l // COMBINE_TM)
    return out.reshape(b, l, d)
```

```python
import functools

import numpy as np
import jax
import jax.numpy as jnp
from jax import lax
from jax.experimental import pallas as pl
from jax.experimental.pallas import tpu as pltpu

F32 = jnp.float32
BF16 = jnp.bfloat16
I32 = jnp.int32

GRID_W = 64
HEAD_DIM = 128
WIN_ROWS = 8
WIN_COLS = 16
ROPE_THETA = 10000.0
SSM_GROUP_CH = 16
SSM_STATE = 64
TOP_K = 8
ROUTED_SCALE = 2.5
EPS = 1e-6
NEG_INF = -1e30

Q_ROWS = 4
BAND_ROWS = Q_ROWS + WIN_ROWS - 1
VMEM_LIMIT = 56 * 1024 * 1024
EXPERT_BLK = 512
PROJ_TN = 1024
DISPATCH_TM = 512
COMBINE_TM = 128


def _cparams():
    return pltpu.CompilerParams(vmem_limit_bytes=VMEM_LIMIT)


def _rms(x, g):
    return x * lax.rsqrt(jnp.mean(x * x, axis=-1, keepdims=True) + EPS) * g


def _pack_halves(lo, hi):
    lo_bits = lax.bitcast_convert_type(lo.astype(BF16).astype(F32), I32)
    hi_bits = lax.bitcast_convert_type(hi.astype(BF16).astype(F32), I32)
    return (hi_bits & jnp.int32(-65536)) | lax.shift_right_logical(lo_bits, jnp.int32(16))


def _unpack_halves(w):
    lo = lax.bitcast_convert_type(lax.shift_left(w, jnp.int32(16)), F32)
    hi = lax.bitcast_convert_type(w & jnp.int32(-65536), F32)
    return lo, hi


def _ada_kernel(c_ref, w_ref, b_ref, o_ref):
    a = jax.nn.silu(c_ref[...]).astype(BF16)
    o_ref[...] = jnp.dot(a, w_ref[...].astype(BF16), preferred_element_type=F32) + b_ref[...]


def ada_mod(c8, w_ada, b_ada):
    d, n = w_ada.shape
    tn = 512
    return pl.pallas_call(
        _ada_kernel,
        grid=(n // tn,),
        in_specs=[pl.BlockSpec((8, d), lambda j: (0, 0)),
                  pl.BlockSpec((d, tn), lambda j: (0, j)),
                  pl.BlockSpec((1, tn), lambda j: (0, j))],
        out_specs=pl.BlockSpec((8, tn), lambda j: (0, j)),
        out_shape=jax.ShapeDtypeStruct((8, n), F32),
        compiler_params=_cparams(),
        name="ada_mod",
    )(c8, w_ada, b_ada.reshape(1, n))


def _modproj_kernel(x_ref, g_ref, sc_ref, sh_ref, w_ref, o_ref, h_scr):
    @pl.when(pl.program_id(2) == 0)
    def _():
        h = _rms(x_ref[0], g_ref[...]) * (1.0 + sc_ref[0]) + sh_ref[0]
        h_scr[...] = h.astype(BF16)

    o_ref[0] = jnp.dot(h_scr[...], w_ref[...], preferred_element_type=F32).astype(o_ref.dtype)


def mod_proj(x, g, scale, shift, w_bf16, col_blk_off, n_out, tm, tn):
    b, l, d = x.shape
    return pl.pallas_call(
        _modproj_kernel,
        grid=(b, l // tm, n_out // tn),
        in_specs=[pl.BlockSpec((1, tm, d), lambda bi, i, j: (bi, i, 0)),
                  pl.BlockSpec((1, d), lambda bi, i, j: (0, 0)),
                  pl.BlockSpec((1, 1, d), lambda bi, i, j: (bi, 0, 0)),
                  pl.BlockSpec((1, 1, d), lambda bi, i, j: (bi, 0, 0)),
                  pl.BlockSpec((d, tn), lambda bi, i, j: (0, j + col_blk_off))],
        out_specs=pl.BlockSpec((1, tm, tn), lambda bi, i, j: (bi, i, j)),
        out_shape=jax.ShapeDtypeStruct((b, l, n_out), BF16),
        scratch_shapes=[pltpu.VMEM((tm, d), BF16)],
        compiler_params=_cparams(),
        name="mod_proj",
    )(x, g.reshape(1, d), scale, shift, w_bf16)


def _rope_tables(l):
    half = HEAD_DIM // 2
    quarter = half // 2
    inv_freq = 1.0 / (ROPE_THETA ** (jnp.arange(0, half, 2, dtype=F32) / half))
    rows = (jnp.arange(l) // GRID_W).astype(F32)
    cols = (jnp.arange(l) % GRID_W).astype(F32)

    def cs(pos):
        ang = pos[:, None] * inv_freq[None, :]
        return jnp.cos(ang), jnp.sin(ang)

    cr, sr = cs(rows)
    cc, sc = cs(cols)
    zero = jnp.zeros((l, quarter), F32)
    cos = jnp.concatenate([cr, cr, cc, cc], axis=-1)
    s_lo = jnp.concatenate([-sr, zero, -sc, zero], axis=-1)
    s_hi = jnp.concatenate([zero, sr, zero, sc], axis=-1)
    return cos, s_lo, s_hi


def _attn_bias_plan(rows_n):
    kh = min(WIN_ROWS, rows_n)
    plan = []
    for r0, sb in [(0, 0), (Q_ROWS, 0), (rows_n - Q_ROWS, rows_n - BAND_ROWS)]:
        qrow = r0 + np.arange(Q_ROWS)
        krow = sb + np.arange(BAND_ROWS)
        rs = np.clip(qrow - kh // 2, 0, rows_n - kh)
        ok = (krow[None, :] >= rs[:, None]) & (krow[None, :] < rs[:, None] + kh)
        ro = krow[None, :] - qrow[:, None] + (WIN_ROWS - 1)
        plan.append(np.where(ok, ro, 2 * WIN_ROWS - 1))
    return np.stack(plan)


def _attn_toe(rpb):
    col = np.arange(GRID_W)
    cstart = np.clip(col - WIN_COLS // 2, 0, GRID_W - WIN_COLS)
    ok_c = (col[None, :] >= cstart[:, None]) & (col[None, :] < cstart[:, None] + WIN_COLS)
    co = np.clip(col[None, :] - col[:, None], -(WIN_COLS - 1), WIN_COLS - 1) + (WIN_COLS - 1)
    onehot = (co[None] == np.arange(2 * WIN_COLS - 1)[:, None, None]).astype(np.float32)
    toe = jnp.einsum('hrc,cqk->hrqk', rpb.astype(F32), onehot, precision=lax.Precision.HIGHEST)
    toe = jnp.where(ok_c[None, None], toe, NEG_INF)
    return jnp.concatenate([toe, jnp.full((rpb.shape[0], 1, GRID_W, GRID_W), NEG_INF, F32)], axis=1)


def _attn_kernel(q_ref, k_ref, v_ref, kc_ref, vc_ref, toe_ref, cos_ref, slo_ref, shi_ref, o_ref,
                 qr_scr, qs_scr, kr_scr, bias_scr, *, n_blk, rows_n, plan):
    quarter = HEAD_DIM // 4
    qn = Q_ROWS * GRID_W
    kn = BAND_ROWS * GRID_W
    scale = HEAD_DIM ** -0.5

    def rope(x, sl):
        return (x * cos_ref[sl, :] + pltpu.roll(x, HEAD_DIM - quarter, 1) * slo_ref[sl, :]
                + pltpu.roll(x, quarter, 1) * shi_ref[sl, :])

    def rope_body(i, _):
        sl = pl.ds(pl.multiple_of(i * qn, qn), qn)
        q = q_ref[0, sl, :].astype(F32) * scale
        qs_scr[sl, :] = q.astype(BF16)
        qr_scr[sl, :] = rope(q, sl).astype(BF16)
        kr_scr[sl, :] = rope(k_ref[0, sl, :].astype(F32), sl).astype(BF16)
        return 0

    lax.fori_loop(0, n_blk, rope_body, 0)

    @pl.when(pl.program_id(1) == 0)
    def _():
        for case in range(plan.shape[0]):
            for qi in range(Q_ROWS):
                for kj in range(BAND_ROWS):
                    bias_scr[case, qi * GRID_W:(qi + 1) * GRID_W, kj * GRID_W:(kj + 1) * GRID_W] = (
                        toe_ref[0, int(plan[case, qi, kj])])

    kc = kc_ref[0]
    vc = vc_ref[0]
    nt = (((1,), (1,)), ((), ()))

    def body(i, _):
        sb = jnp.clip(i * Q_ROWS - WIN_ROWS // 2, 0, rows_n - BAND_ROWS)
        ks = pl.ds(pl.multiple_of(sb * GRID_W, GRID_W), kn)
        qs = pl.ds(pl.multiple_of(i * qn, qn), qn)
        case = jnp.where(i == 0, 0, jnp.where(i == n_blk - 1, 2, 1))
        s = lax.dot_general(qr_scr[qs, :], kr_scr[ks, :], nt, preferred_element_type=F32) + bias_scr[case]
        sc = lax.dot_general(qs_scr[qs, :], kc, nt, preferred_element_type=F32)
        m = jnp.maximum(jnp.max(s, axis=-1, keepdims=True), jnp.max(sc, axis=-1, keepdims=True))
        p = jnp.exp(s - m)
        pc = jnp.exp(sc - m)
        den = jnp.sum(p, axis=-1, keepdims=True) + jnp.sum(pc, axis=-1, keepdims=True)
        o = (jnp.dot(p.astype(BF16), v_ref[0, ks, :], preferred_element_type=F32)
             + jnp.dot(pc.astype(BF16), vc, preferred_element_type=F32))
        o_ref[0, qs, :] = (o / den).astype(o_ref.dtype)
        return 0

    lax.fori_loop(0, n_blk, body, 0, unroll=4)


def neighborhood_attention(p_lat, p_ctx, rpb, n_heads):
    b, l, _ = p_lat.shape
    lc = p_ctx.shape[1]
    rows_n = l // GRID_W
    assert rows_n % Q_ROWS == 0 and rows_n >= BAND_ROWS + Q_ROWS
    n_blk = rows_n // Q_ROWS
    plan = _attn_bias_plan(rows_n)
    cos, s_lo, s_hi = _rope_tables(l)
    qn, kn = Q_ROWS * GRID_W, BAND_ROWS * GRID_W
    h = n_heads
    tab = pl.BlockSpec((l, HEAD_DIM), lambda hi, bi: (0, 0))
    return pl.pallas_call(
        functools.partial(_attn_kernel, n_blk=n_blk, rows_n=rows_n, plan=plan),
        grid=(h, b),
        in_specs=[pl.BlockSpec((1, l, HEAD_DIM), lambda hi, bi: (bi, 0, hi)),
                  pl.BlockSpec((1, l, HEAD_DIM), lambda hi, bi: (bi, 0, hi + h)),
                  pl.BlockSpec((1, l, HEAD_DIM), lambda hi, bi: (bi, 0, hi + 2 * h)),
                  pl.BlockSpec((1, lc, HEAD_DIM), lambda hi, bi: (bi, 0, hi)),
                  pl.BlockSpec((1, lc, HEAD_DIM), lambda hi, bi: (bi, 0, hi + h)),
                  pl.BlockSpec((1, 2 * WIN_ROWS, GRID_W, GRID_W), lambda hi, bi: (hi, 0, 0, 0)),
                  tab, tab, tab],
        out_specs=pl.BlockSpec((1, l, HEAD_DIM), lambda hi, bi: (bi, 0, hi)),
        out_shape=jax.ShapeDtypeStruct((b, l, h * HEAD_DIM), BF16),
        scratch_shapes=[pltpu.VMEM((l, HEAD_DIM), BF16)] * 3 + [pltpu.VMEM((plan.shape[0], qn, kn), F32)],
        compiler_params=_cparams(),
        name="nbr_attn",
    )(p_lat, p_lat, p_lat, p_ctx, p_ctx, _attn_toe(rpb), cos, s_lo, s_hi)


SSM_CHUNK = 16
SSM_ROWS = 8
SSM_PACK = 8


def _ssm_mats(a_re, a_im, log_dt, b_re, b_im, c_re, c_im, d_skip):
    t = SSM_CHUNK
    hi = lax.Precision.HIGHEST
    a = lax.complex(a_re.astype(F32), a_im.astype(F32))
    dta = jnp.exp(log_dt.astype(F32))[..., None] * a
    a_bar = jnp.exp(dta)
    b_bar = ((a_bar - 1.0) / a)[..., None] * lax.complex(b_re.astype(F32), b_im.astype(F32))
    cm = lax.complex(c_re.astype(F32), c_im.astype(F32))
    k = jnp.arange(t + 1, dtype=F32)
    ap = jnp.exp(dta[..., None] * k)
    g, p, n = a.shape[1], a.shape[2], b_re.shape[-1]
    kern = jnp.einsum('dgnp,dgpl,dgpm->dglnm', cm, ap[..., :t], b_bar, precision=hi).real
    s_i = np.arange(t)[:, None]
    t_i = np.arange(t)[None, :]
    kf = kern[0][:, np.clip(t_i - s_i, 0, t - 1)]
    kb = kern[1][:, np.clip(s_i - t_i, 0, t - 1)]
    eye = jnp.eye(n, dtype=F32)
    m_mat = (jnp.where((s_i <= t_i)[None, :, :, None, None], kf, 0.0)
             + jnp.where((s_i >= t_i)[None, :, :, None, None], kb, 0.0)
             + (s_i == t_i)[None, :, :, None, None] * (d_skip.astype(F32)[:, None, None, :, None] * eye))
    m_mat = jnp.transpose(m_mat, (0, 1, 4, 2, 3)).reshape(g, t * n, t * n)
    wf = ap[0][:, :, t - 1::-1][..., :t, None] * b_bar[0][:, :, None, :]
    wb = ap[1][:, :, :t, None] * b_bar[1][:, :, None, :]
    to_rows = lambda z: jnp.transpose(z, (0, 2, 3, 1)).reshape(g, t * n, p)
    w_mat = jnp.concatenate([to_rows(wf.real), to_rows(wb.real), to_rows(wf.imag), to_rows(wb.imag)], axis=-1)
    zf = jnp.transpose(cm[0], (0, 2, 1))[:, :, None, :] * ap[0][:, :, 1:t + 1, None]
    zb = jnp.transpose(cm[1], (0, 2, 1))[:, :, None, :] * ap[1][:, :, t:0:-1, None]
    flat = lambda z: z.reshape(g, p, t * n)
    r_mat = jnp.concatenate([flat(zf.real), flat(zb.real), -flat(zf.imag), -flat(zb.imag)], axis=1)
    a_t = ap[..., t]
    a_vec = jnp.stack([jnp.concatenate([a_t[0].real, a_t[1].real], -1),
                       jnp.concatenate([a_t[0].imag, a_t[1].imag], -1)], axis=1)
    a_vec = jnp.pad(a_vec, ((0, 0), (0, SSM_ROWS - 2), (0, 0)))
    return m_mat.astype(BF16), w_mat.astype(BF16), r_mat.astype(BF16), a_vec


def _s5_kernel(u_ref, sel_ref, selt_ref, m_ref, w_ref, r_ref, a_ref, y_ref, v_scr, ent_scr, *, n_ctx, n_chunks):
    u = jnp.dot(u_ref[0], sel_ref[0], preferred_element_type=F32).astype(BF16)
    p2 = a_ref.shape[-1]
    p = p2 // 2
    v_scr[...] = jnp.dot(u, w_ref[0], preferred_element_type=F32)
    a_re = jnp.broadcast_to(a_ref[0, 0:1, :], (SSM_ROWS, p2))
    a_im = jnp.broadcast_to(a_ref[0, 1:2, :], (SSM_ROWS, p2))
    is_fwd = lax.broadcasted_iota(I32, (SSM_ROWS, p2), 1) < p

    def step(j, carry):
        s_re, s_im = carry
        cb = jnp.where(j < n_ctx, n_ctx - 1 - j, n_chunks + n_ctx - 1 - j)
        rf = pl.ds(pl.multiple_of(j * SSM_ROWS, SSM_ROWS), SSM_ROWS)
        rb = pl.ds(pl.multiple_of(cb * SSM_ROWS, SSM_ROWS), SSM_ROWS)
        ent_scr[rf, 0:p] = s_re[:, :p]
        ent_scr[rb, p:p2] = s_re[:, p:]
        ent_scr[rf, p2:p2 + p] = s_im[:, :p]
        ent_scr[rb, p2 + p:] = s_im[:, p:]
        in_re = jnp.where(is_fwd, v_scr[rf, :p2], v_scr[rb, :p2])
        in_im = jnp.where(is_fwd, v_scr[rf, p2:], v_scr[rb, p2:])
        return a_re * s_re - a_im * s_im + in_re, a_re * s_im + a_im * s_re + in_im

    zero = jnp.zeros((SSM_ROWS, p2), F32)
    lax.fori_loop(0, n_chunks, step, (zero, zero))
    y = (jnp.dot(u, m_ref[0], preferred_element_type=F32)
         + jnp.dot(ent_scr[...].astype(BF16), r_ref[0], preferred_element_type=F32)).astype(BF16)
    def part(c):
        return jnp.dot(y, selt_ref[0, :, c:c + PACK_CHUNK], preferred_element_type=F32).astype(y_ref.dtype)

    @pl.when(pl.program_id(1) == 0)
    def _():
        for c in range(0, y_ref.shape[-1], PACK_CHUNK):
            y_ref[0, :, c:c + PACK_CHUNK] = part(c)

    @pl.when(pl.program_id(1) > 0)
    def _():
        for c in range(0, y_ref.shape[-1], PACK_CHUNK):
            y_ref[0, :, c:c + PACK_CHUNK] = y_ref[0, :, c:c + PACK_CHUNK] + part(c)


def s5_mixer(u_lat, u_ctx, mats):
    m_mat, w_mat, r_mat, a_vec = mats
    b, l, dch = u_lat.shape
    lc = u_ctx.shape[1]
    g = m_mat.shape[0]
    n = dch // g
    t = SSM_CHUNK
    tn = t * n
    gp = SSM_PACK
    packs, lanes = g // gp, gp * n
    n_ctx, n_chunks = lc // t, (lc + l) // t
    assert b <= SSM_ROWS and lc % t == 0 and l % t == 0 and g % gp == 0
    z = jnp.concatenate([u_ctx, u_lat], axis=1).reshape(b, n_chunks, t, packs, lanes)
    z = jnp.pad(jnp.transpose(z, (3, 1, 0, 2, 4)), ((0, 0), (0, 0), (0, SSM_ROWS - b), (0, 0), (0, 0)))
    rows = n_chunks * SSM_ROWS
    z = z.reshape(packs, rows, t * lanes)
    ri = jnp.arange(t * lanes)
    ci = jnp.arange(tn)
    same = (ri[:, None] // lanes == ci[None, :] // n) & (ri[:, None] % n == ci[None, :] % n)
    sel = (same[None] & ((ri[None, :, None] % lanes) // n == jnp.arange(gp)[:, None, None])).astype(BF16)
    selt = jnp.transpose(sel, (0, 2, 1))
    grp = lambda k: pl.BlockSpec((1, k, tn), lambda pi, qi: (pi * gp + qi, 0, 0))
    pack_blk = pl.BlockSpec((1, rows, t * lanes), lambda pi, qi: (pi, 0, 0))
    y = pl.pallas_call(
        functools.partial(_s5_kernel, n_ctx=n_ctx, n_chunks=n_chunks),
        grid=(packs, gp),
        in_specs=[pack_blk,
                  pl.BlockSpec((1, t * lanes, tn), lambda pi, qi: (qi, 0, 0)),
                  pl.BlockSpec((1, tn, t * lanes), lambda pi, qi: (qi, 0, 0)),
                  grp(tn), grp(tn), grp(r_mat.shape[1]),
                  pl.BlockSpec((1, SSM_ROWS, a_vec.shape[-1]), lambda pi, qi: (pi * gp + qi, 0, 0))],
        out_specs=pack_blk,
        out_shape=jax.ShapeDtypeStruct((packs, rows, t * lanes), BF16),
        scratch_shapes=[pltpu.VMEM((rows, w_mat.shape[-1]), F32), pltpu.VMEM((rows, r_mat.shape[1]), F32)],
        compiler_params=_cparams(),
        name="s5_mixer",
    )(z, sel, selt, m_mat, w_mat, r_mat, a_vec)
    y = y.reshape(packs, n_chunks, SSM_ROWS, t, lanes)[:, n_ctx:, :b]
    return jnp.transpose(y, (2, 1, 3, 0, 4)).reshape(b, l, dch)


def _glu_kernel(y_ref, w_ref, b_ref, o_ref):
    z = jax.nn.gelu(y_ref[...].astype(F32))
    gate = jax.nn.sigmoid(jnp.dot(z.astype(BF16), w_ref[...], preferred_element_type=F32) + b_ref[...])
    o_ref[...] = (z * gate).astype(o_ref.dtype)


def s5_glu(y, w_bf16, b_glu, tm=512):
    n, d = y.shape
    return pl.pallas_call(
        _glu_kernel,
        grid=(n // tm,),
        in_specs=[pl.BlockSpec((tm, d), lambda i: (i, 0)),
                  pl.BlockSpec((d, d), lambda i: (0, 0)),
                  pl.BlockSpec((1, d), lambda i: (0, 0))],
        out_specs=pl.BlockSpec((tm, d), lambda i: (i, 0)),
        out_shape=jax.ShapeDtypeStruct((n, d), BF16),
        compiler_params=_cparams(),
        name="s5_glu",
    )(y, w_bf16, b_glu.reshape(1, d))


def _merge_kernel(a_ref, s_ref, ga_ref, gs_ref, w_ref, o_ref, h_scr):
    da = a_ref.shape[-1]

    @pl.when(pl.program_id(1) == 0)
    def _():
        h_scr[:, :da] = _rms(a_ref[...].astype(F32), ga_ref[...]).astype(BF16)
        h_scr[:, da:] = _rms(s_ref[...].astype(F32), gs_ref[...]).astype(BF16)

    o_ref[...] = jnp.dot(h_scr[...], w_ref[...], preferred_element_type=F32)


def merge_proj(attn, ssm, g_attn, g_ssm, w_bf16, tm=512, tn=1024):
    n, da = attn.shape
    ds = ssm.shape[1]
    d, dout = w_bf16.shape
    return pl.pallas_call(
        _merge_kernel,
        grid=(n // tm, dout // tn),
        in_specs=[pl.BlockSpec((tm, da), lambda i, j: (i, 0)),
                  pl.BlockSpec((tm, ds), lambda i, j: (i, 0)),
                  pl.BlockSpec((1, da), lambda i, j: (0, 0)),
                  pl.BlockSpec((1, ds), lambda i, j: (0, 0)),
                  pl.BlockSpec((d, tn), lambda i, j: (0, j))],
        out_specs=pl.BlockSpec((tm, tn), lambda i, j: (i, j)),
        out_shape=jax.ShapeDtypeStruct((n, dout), F32),
        scratch_shapes=[pltpu.VMEM((tm, d), BF16)],
        compiler_params=_cparams(),
        name="merge_proj",
    )(attn, ssm, g_attn.reshape(1, da), g_ssm.reshape(1, ds), w_bf16)


def _post_mix_kernel(x_ref, mix_ref, gpost_ref, gt_ref, gpre_ref, sc_ref, sh_ref, wr_ref, br_ref,
                     x1_ref, hp_ref, idx_ref, gate_ref, mask_ref):
    x1 = x_ref[0] + gt_ref[0] * _rms(mix_ref[0], gpost_ref[...])
    x1_ref[0] = x1
    h = _rms(x1, gpre_ref[...]) * (1.0 + sc_ref[0]) + sh_ref[0]
    half = h.shape[-1] // 2
    hp_ref[0] = _pack_halves(h[:, :half], h[:, half:])
    scores = jax.nn.sigmoid(jnp.dot(h.astype(BF16), wr_ref[...], preferred_element_type=F32))
    n_e = scores.shape[-1]
    lane = lax.broadcasted_iota(I32, scores.shape, 1)
    biased = scores + br_ref[...]
    idx_out = jnp.zeros(scores.shape, I32)
    sel_out = jnp.zeros(scores.shape, F32)
    mask = jnp.zeros(scores.shape, jnp.bool_)
    for k in range(TOP_K):
        m = jnp.max(biased, axis=-1, keepdims=True)
        ik = jnp.min(jnp.where(biased == m, lane, n_e), axis=-1, keepdims=True)
        hit = lane == ik
        sel_k = jnp.sum(jnp.where(hit, scores, 0.0), axis=-1, keepdims=True)
        idx_out = jnp.where(lane == k, ik, idx_out)
        sel_out = jnp.where(lane == k, sel_k, sel_out)
        mask = jnp.logical_or(mask, hit)
        biased = jnp.where(hit, -jnp.inf, biased)
    idx_ref[0] = idx_out
    gate_ref[0] = sel_out / jnp.sum(sel_out, axis=-1, keepdims=True) * ROUTED_SCALE
    mask_ref[0] = mask.astype(BF16)


def post_mix(x, mix, g_post, gt, g_pre, scale, shift, wr_bf16, b_router, tm=256):
    b, l, d = x.shape
    n_e = wr_bf16.shape[1]
    row = pl.BlockSpec((1, tm, d), lambda bi, i: (bi, i, 0))
    prow = pl.BlockSpec((1, tm, d // 2), lambda bi, i: (bi, i, 0))
    vec = pl.BlockSpec((1, d), lambda bi, i: (0, 0))
    bvec = pl.BlockSpec((1, 1, d), lambda bi, i: (bi, 0, 0))
    small = pl.BlockSpec((1, tm, n_e), lambda bi, i: (bi, i, 0))
    return pl.pallas_call(
        _post_mix_kernel,
        grid=(b, l // tm),
        in_specs=[row, row, vec, bvec, vec, bvec, bvec,
                  pl.BlockSpec((d, n_e), lambda bi, i: (0, 0)),
                  pl.BlockSpec((1, n_e), lambda bi, i: (0, 0))],
        out_specs=[row, prow, small, small, small],
        out_shape=[jax.ShapeDtypeStruct((b, l, d), F32),
                   jax.ShapeDtypeStruct((b, l, d // 2), I32),
                   jax.ShapeDtypeStruct((b, l, n_e), I32),
                   jax.ShapeDtypeStruct((b, l, n_e), F32),
                   jax.ShapeDtypeStruct((b, l, n_e), BF16)],
        compiler_params=_cparams(),
        name="post_mix",
    )(x, mix, g_post.reshape(1, d), gt, g_pre.reshape(1, d), scale, shift, wr_bf16, b_router.reshape(1, n_e))


def _rank_kernel(mask_ref, idx_ref, rank_ref, cnt_ref, carry):
    @pl.when(pl.program_id(0) == 0)
    def _():
        carry[...] = jnp.zeros_like(carry)

    m = mask_ref[...]
    tm = m.shape[0]
    earlier = (lax.broadcasted_iota(I32, (tm, tm), 1) < lax.broadcasted_iota(I32, (tm, tm), 0)).astype(BF16)
    excl = jnp.dot(earlier, m, preferred_element_type=F32) + carry[0:1, :]
    lane = lax.broadcasted_iota(I32, m.shape, 1)
    idx = idx_ref[...]
    out = jnp.zeros(m.shape, F32)
    for k in range(TOP_K):
        rk = jnp.sum(jnp.where(lane == idx[:, k:k + 1], excl, 0.0), axis=-1, keepdims=True)
        out = jnp.where(lane == k, rk, out)
    rank_ref[...] = out.astype(I32)
    carry[0:1, :] = carry[0:1, :] + jnp.sum(m.astype(F32), axis=0, keepdims=True)
    cnt_ref[...] = carry[...]


def route_ranks(mask, idx, tm=512):
    n, n_e = mask.shape
    blk = pl.BlockSpec((tm, n_e), lambda i: (i, 0))
    rank, cnt = pl.pallas_call(
        _rank_kernel,
        grid=(n // tm,),
        in_specs=[blk, blk],
        out_specs=[blk, pl.BlockSpec((8, n_e), lambda i: (0, 0))],
        out_shape=[jax.ShapeDtypeStruct((n, n_e), I32), jax.ShapeDtypeStruct((8, n_e), F32)],
        scratch_shapes=[pltpu.VMEM((8, n_e), F32)],
        compiler_params=_cparams(),
        name="route_ranks",
    )(mask, idx)
    return rank, cnt[0].astype(I32)


def _dest_kernel(rank_ref, idx_ref, ps_ref, dest_ref):
    idx = idx_ref[...]
    lane = lax.broadcasted_iota(I32, idx.shape, 1)
    start = jnp.zeros(idx.shape, F32)
    for k in range(TOP_K):
        sk = jnp.sum(jnp.where(lane == idx[:, k:k + 1], ps_ref[...], 0.0), axis=-1, keepdims=True)
        start = jnp.where(lane == k, sk, start)
    dest_ref[...] = rank_ref[...] + start.astype(I32)


def route_dest(rank, idx, pad_start, tm=512):
    n, n_e = rank.shape
    blk = pl.BlockSpec((tm, n_e), lambda i: (i, 0))
    dest = pl.pallas_call(
        _dest_kernel,
        grid=(n // tm,),
        in_specs=[blk, blk, pl.BlockSpec((1, n_e), lambda i: (0, 0))],
        out_specs=blk,
        out_shape=jax.ShapeDtypeStruct((n, n_e), I32),
        compiler_params=_cparams(),
        name="route_dest",
    )(rank, idx, pad_start.astype(F32).reshape(1, n_e))
    return dest[:, :TOP_K].reshape(n * TOP_K)


def _dispatch_kernel(pe_ref, pd_ref, dest_ref, h_ref, wgu_ref, wd_ref, xs_ref, sp_ref, z_scr, sem, *, n_e, tm, blk):
    def zero_copy(e):
        start = pl.multiple_of(pe_ref[e] - blk, blk)
        return pltpu.make_async_copy(z_scr, xs_ref.at[pl.ds(start, blk), :], sem)

    @pl.when(pl.program_id(0) == 0)
    def _():
        z_scr[...] = jnp.zeros_like(z_scr)

        def start(e, _):
            @pl.when(pd_ref[e] > 0)
            def _():
                zero_copy(e).start()
            return 0

        def wait(e, _):
            @pl.when(pd_ref[e] > 0)
            def _():
                zero_copy(e).wait()
            return 0

        lax.fori_loop(0, n_e, start, 0)
        lax.fori_loop(0, n_e, wait, 0)

    sub = 8
    half = h_ref.shape[-1]
    n_ch = half // PACK_CHUNK
    rows_ph = tm // (2 * n_ch)

    def issue_rows(r0):
        for g in range(0, rows_ph, sub):
            tile = h_ref.at[pl.ds(r0 + g, sub), :]
            for tt in range(sub):
                for k in range(TOP_K):
                    pltpu.make_async_copy(tile.at[pl.ds(tt, 1), :],
                                          xs_ref.at[pl.ds(dest_ref[(r0 + g + tt) * TOP_K + k], 1), :],
                                          sem).start(priority=k % 2)

    acc = None
    for c in range(n_ch):
        issue_rows(c * rows_ph)
        cl = slice(c * PACK_CHUNK, (c + 1) * PACK_CHUNK)
        ch = slice(half + c * PACK_CHUNK, half + (c + 1) * PACK_CHUNK)
        lo, hi = _unpack_halves(h_ref[:, cl])
        part = (jnp.dot(lo.astype(BF16), wgu_ref[cl, :], preferred_element_type=F32)
                + jnp.dot(hi.astype(BF16), wgu_ref[ch, :], preferred_element_type=F32))
        acc = part if acc is None else acc + part
    de = acc.shape[-1] // 2
    hid = (jax.nn.silu(acc[:, :de]) * acc[:, de:]).astype(BF16)
    for c in range(n_ch):
        issue_rows((n_ch + c) * rows_ph)
        cl = slice(c * PACK_CHUNK, (c + 1) * PACK_CHUNK)
        ch = slice(half + c * PACK_CHUNK, half + (c + 1) * PACK_CHUNK)
        sp_ref[:, cl] = _pack_halves(jnp.dot(hid, wd_ref[:, cl], preferred_element_type=F32),
                                     jnp.dot(hid, wd_ref[:, ch], preferred_element_type=F32))
    rows = tm * TOP_K
    pltpu.make_async_copy(xs_ref.at[pl.ds(0, rows), :], xs_ref.at[pl.ds(0, rows), :], sem).wait()


def moe_dispatch(hp, dest_flat, pad_end, padded, slots, wsgu, wsd, tm=DISPATCH_TM):
    n, dw = hp.shape
    n_e = pad_end.shape[0]
    d, de2 = wsgu.shape
    row = pl.BlockSpec((tm, dw), lambda i, pe, pd: (i, 0))
    grid_spec = pltpu.PrefetchScalarGridSpec(
        num_scalar_prefetch=2,
        grid=(n // tm,),
        in_specs=[pl.BlockSpec((tm * TOP_K,), lambda i, pe, pd: (i,), memory_space=pltpu.SMEM),
                  row,
                  pl.BlockSpec((d, de2), lambda i, pe, pd: (0, 0)),
                  pl.BlockSpec((de2 // 2, d), lambda i, pe, pd: (0, 0))],
        out_specs=[pl.BlockSpec(memory_space=pl.ANY), row],
        scratch_shapes=[pltpu.VMEM((EXPERT_BLK, dw), I32), pltpu.SemaphoreType.DMA(())],
    )
    return pl.pallas_call(
        functools.partial(_dispatch_kernel, n_e=n_e, tm=tm, blk=EXPERT_BLK),
        grid_spec=grid_spec,
        out_shape=[jax.ShapeDtypeStruct((slots, dw), I32), jax.ShapeDtypeStruct((n, dw), I32)],
        compiler_params=_cparams(),
        name="moe_dispatch",
    )(pad_end, padded, dest_flat, hp, wsgu, wsd)


PACK_CHUNK = 512


def _swiglu_hidden(xp_ref, wgu):
    half = xp_ref.shape[-1]
    acc = None
    for c in range(half // PACK_CHUNK):
        cl = slice(c * PACK_CHUNK, (c + 1) * PACK_CHUNK)
        ch = slice(half + c * PACK_CHUNK, half + (c + 1) * PACK_CHUNK)
        lo, hi = _unpack_halves(xp_ref[:, cl])
        part = (jnp.dot(lo.astype(BF16), wgu(cl), preferred_element_type=F32)
                + jnp.dot(hi.astype(BF16), wgu(ch), preferred_element_type=F32))
        acc = part if acc is None else acc + part
    de = acc.shape[-1] // 2
    return (jax.nn.silu(acc[:, :de]) * acc[:, de:]).astype(BF16)


def _down_packed(hid, wd, o_ref):
    half = o_ref.shape[-1]
    for c in range(half // PACK_CHUNK):
        cl = slice(c * PACK_CHUNK, (c + 1) * PACK_CHUNK)
        ch = slice(half + c * PACK_CHUNK, half + (c + 1) * PACK_CHUNK)
        o_ref[:, cl] = _pack_halves(jnp.dot(hid, wd(cl), preferred_element_type=F32),
                                    jnp.dot(hid, wd(ch), preferred_element_type=F32))


def _expert_weights(sched, i, w_hbm, w_buf, sem):
    be_ref, first_ref, slot_ref, nxt_ref = sched
    s = slot_ref[i]

    def copies(e, sl):
        return [pltpu.make_async_copy(h.at[e], v.at[sl], sem.at[j, sl]) for j, (h, v) in enumerate(zip(w_hbm, w_buf))]

    @pl.when(first_ref[i] == 1)
    def _():
        @pl.when(i == 0)
        def _():
            for c in copies(be_ref[0], 0):
                c.start()

        for c in copies(be_ref[i], s):
            c.wait()

        @pl.when(nxt_ref[i] >= 0)
        def _():
            for c in copies(nxt_ref[i], 1 - s):
                c.start()

    return s


def _expert_up_kernel(be_ref, nu_ref, first_ref, slot_ref, nxt_ref, x_ref, wg_hbm, wu_hbm, hid_ref,
                      wg_buf, wu_buf, sem):
    i = pl.program_id(0)
    used = i < nu_ref[0]
    s = _expert_weights((be_ref, first_ref, slot_ref, nxt_ref), i, (wg_hbm, wu_hbm), (wg_buf, wu_buf), sem)

    def wgu(sl):
        return jnp.concatenate([wg_buf[s, sl, :].astype(BF16), wu_buf[s, sl, :].astype(BF16)], axis=-1)

    @pl.when(used)
    def _():
        hid_ref[...] = _swiglu_hidden(x_ref, wgu)

    @pl.when(jnp.logical_not(used))
    def _():
        hid_ref[...] = jnp.zeros_like(hid_ref)


def _expert_down_kernel(be_ref, nu_ref, first_ref, slot_ref, nxt_ref, hid_ref, wd_hbm, o_ref, wd_buf, sem):
    i = pl.program_id(0)
    used = i < nu_ref[0]
    s = _expert_weights((be_ref, first_ref, slot_ref, nxt_ref), i, (wd_hbm,), (wd_buf,), sem)

    @pl.when(used)
    def _():
        _down_packed(hid_ref[...], lambda sl: wd_buf[s, :, sl].astype(BF16), o_ref)

    @pl.when(jnp.logical_not(used))
    def _():
        o_ref[...] = jnp.zeros_like(o_ref)


def _expert_schedule(block_e, n_used):
    nb = block_e.shape[0]
    ar = jnp.arange(nb, dtype=I32)
    used = ar < n_used[0]
    first = used & ((ar == 0) | (block_e != jnp.roll(block_e, 1)))
    slot = jnp.where(used, (jnp.cumsum(first.astype(I32)) - 1) % 2, 0).astype(I32)
    nf = jnp.where(first, ar, nb)
    nxt_idx = jnp.concatenate([lax.cummin(nf, axis=0, reverse=True)[1:], jnp.full((1,), nb, I32)])
    nxt = jnp.where(nxt_idx < nb, block_e[jnp.minimum(nxt_idx, nb - 1)], -1).astype(I32)
    return first.astype(I32), slot, nxt


def expert_ffn(xs, block_e, n_used, wg, wu, wd, blk):
    slots, dw = xs.shape
    d, de = wg.shape[1], wg.shape[2]
    n_blocks = slots // blk
    sched = (block_e, n_used) + _expert_schedule(block_e, n_used)
    cur = lambda i, nu: jnp.minimum(i, nu[0] - 1)
    hbm = pl.BlockSpec(memory_space=pl.ANY)
    up_spec = pltpu.PrefetchScalarGridSpec(
        num_scalar_prefetch=5,
        grid=(n_blocks,),
        in_specs=[pl.BlockSpec((blk, dw), lambda i, be, nu, *_: (cur(i, nu), 0)), hbm, hbm],
        out_specs=pl.BlockSpec((blk, de), lambda i, *_: (i, 0)),
        scratch_shapes=[pltpu.VMEM((2, d, de), F32), pltpu.VMEM((2, d, de), F32), pltpu.SemaphoreType.DMA((2, 2))],
    )
    hid = pl.pallas_call(
        _expert_up_kernel,
        grid_spec=up_spec,
        out_shape=jax.ShapeDtypeStruct((slots, de), BF16),
        compiler_params=_cparams(),
        name="expert_up",
    )(*sched, xs, wg, wu)
    down_spec = pltpu.PrefetchScalarGridSpec(
        num_scalar_prefetch=5,
        grid=(n_blocks,),
        in_specs=[pl.BlockSpec((blk, de), lambda i, be, nu, *_: (cur(i, nu), 0)), hbm],
        out_specs=pl.BlockSpec((blk, dw), lambda i, *_: (i, 0)),
        scratch_shapes=[pltpu.VMEM((2, de, d), F32), pltpu.SemaphoreType.DMA((1, 2))],
    )
    return pl.pallas_call(
        _expert_down_kernel,
        grid_spec=down_spec,
        out_shape=jax.ShapeDtypeStruct((slots, dw), I32),
        compiler_params=_cparams(),
        name="expert_down",
    )(*sched, hid, wd)


def _final_kernel(dc_ref, dn_ref, x1_ref, sp_ref, gate_ref, ys_ref, gpost_ref, gt_ref, o_ref, buf, sem, *, tm, n_tiles):
    i = pl.program_id(0)
    slot = i % 2
    sub = 8

    def issue_rows(dref, s, j):
        base = pl.multiple_of(j * sub, sub)
        for tt in range(sub):
            for k in range(TOP_K):
                tile = buf.at[s, pl.ds(pl.multiple_of(k * tm + base, sub), sub), :]
                pltpu.make_async_copy(ys_ref.at[pl.ds(dref[(base + tt) * TOP_K + k], 1), :],
                                      tile.at[pl.ds(tt, 1), :],
                                      sem.at[s]).start(priority=k % 2)

    def reduce_rows(s, j):
        rs = pl.ds(pl.multiple_of(j * sub, sub), sub)
        g = gate_ref[rs, :]
        acc_lo, acc_hi = _unpack_halves(sp_ref[rs, :])
        for k in range(TOP_K):
            lo, hi = _unpack_halves(buf[s, pl.ds(pl.multiple_of(k * tm + j * sub, sub), sub), :])
            gk = g[:, k:k + 1]
            acc_lo = acc_lo + gk * lo
            acc_hi = acc_hi + gk * hi
        ffn = jnp.concatenate([acc_lo, acc_hi], axis=-1)
        o_ref[rs, :] = x1_ref[rs, :] + gt_ref[0] * _rms(ffn, gpost_ref[...])

    @pl.when(i == 0)
    def _():
        def first(j, c):
            issue_rows(dc_ref, 0, j)
            return c
        lax.fori_loop(0, tm // sub, first, 0)

    def wait_slot(s):
        pltpu.make_async_copy(ys_ref.at[pl.ds(0, TOP_K * tm), :], buf.at[s], sem.at[s]).wait()

    wait_slot(slot)

    for s in range(2):
        @pl.when(slot == s)
        def _():
            def body(j, c):
                issue_rows(dn_ref, 1 - s, j)
                reduce_rows(s, j)
                return c
            lax.fori_loop(0, tm // sub, body, 0, unroll=4)

    @pl.when(i == n_tiles - 1)
    def _():
        wait_slot(1 - slot)


def final_mix(x1, sp, gates, ys, dest_flat, g_post, gt, tiles_per_batch, tm=COMBINE_TM):
    n, d = x1.shape
    dw = sp.shape[1]
    n_e = gates.shape[1]
    n_tiles = n // tm
    return pl.pallas_call(
        functools.partial(_final_kernel, tm=tm, n_tiles=n_tiles),
        grid=(n_tiles,),
        in_specs=[pl.BlockSpec((tm * TOP_K,), lambda i: (i,), memory_space=pltpu.SMEM),
                  pl.BlockSpec((tm * TOP_K,), lambda i: (jnp.minimum(i + 1, n_tiles - 1),), memory_space=pltpu.SMEM),
                  pl.BlockSpec((tm, d), lambda i: (i, 0)),
                  pl.BlockSpec((tm, dw), lambda i: (i, 0)),
                  pl.BlockSpec((tm, n_e), lambda i: (i, 0)),
                  pl.BlockSpec(memory_space=pl.ANY),
                  pl.BlockSpec((1, d), lambda i: (0, 0)),
                  pl.BlockSpec((1, 1, d), lambda i: (i // tiles_per_batch, 0, 0))],
        out_specs=pl.BlockSpec((tm, d), lambda i: (i, 0)),
        out_shape=jax.ShapeDtypeStruct((n, d), F32),
        scratch_shapes=[pltpu.VMEM((2, TOP_K * tm, dw), I32), pltpu.SemaphoreType.DMA((2,))],
        compiler_params=_cparams(),
        name="final_mix",
    )(dest_flat, dest_flat, x1, sp, gates, ys, g_post.reshape(1, d), gt)


def kernel(x, c, ctx, c_ctx, w_ada, b_ada, g_pre_mix, g_post_mix, g_pre_ffn, g_post_ffn, w_in, rpb, ssm_a_re, ssm_a_im, ssm_log_dt, ssm_b_re, ssm_b_im, ssm_c_re, ssm_c_im, ssm_d, w_glu, b_glu, g_attn_out, g_ssm_out, w_out, w_router, b_router, w_exp_gate, w_exp_up, w_exp_down, w_sh_gate, w_sh_up, w_sh_down):
    b, l, d = x.shape
    lc = ctx.shape[1]
    assert w_ada.shape[0] == 1 and b + 1 <= 8
    n_in = w_in.shape[-1]
    d_ssm = w_glu.shape[-1]
    d_attn = d - d_ssm
    n_heads = d_attn // HEAD_DIM
    n = b * l

    c8 = jnp.concatenate([c, c_ctx[None], jnp.zeros((8 - b - 1, d), F32)], axis=0)
    mod = ada_mod(c8, w_ada[0], b_ada[0]).reshape(8, 6, 1, d)
    sh_m, sc_m, gt_m, sh_f, sc_f, gt_f = [mod[:b, j] for j in range(6)]
    csh_m, csc_m = mod[b:b + 1, 0], mod[b:b + 1, 1]

    w_in_b = w_in[0].astype(BF16)
    tn = PROJ_TN
    p_lat = mod_proj(x, g_pre_mix[0], sc_m, sh_m, w_in_b, 0, n_in, 512, tn)
    p_ctx = mod_proj(ctx.reshape(1, b * lc, d), g_pre_mix[0], csc_m, csh_m, w_in_b,
                     d_attn // tn, n_in - d_attn, 512, tn).reshape(b, lc, n_in - d_attn)

    attn = neighborhood_attention(p_lat, p_ctx, rpb[0], n_heads)

    mats = _ssm_mats(ssm_a_re[0], ssm_a_im[0], ssm_log_dt[0], ssm_b_re[0], ssm_b_im[0],
                     ssm_c_re[0], ssm_c_im[0], ssm_d[0])
    y = s5_mixer(p_lat[..., 3 * d_attn:], p_ctx[..., 2 * d_attn:], mats)
    ssm = s5_glu(y.reshape(n, d_ssm), w_glu[0].astype(BF16), b_glu[0])

    mix = merge_proj(attn.reshape(n, d_attn), ssm, g_attn_out[0], g_ssm_out[0], w_out[0].astype(BF16))
    x1, hp, idx, gates, mask = post_mix(x, mix.reshape(b, l, d), g_post_mix[0], gt_m, g_pre_ffn[0], sc_f, sh_f,
                                        w_router[0].astype(BF16), b_router[0])

    n_e = w_router.shape[-1]
    m = n * TOP_K
    idx = idx.reshape(n, n_e)
    rank, counts = route_ranks(mask.reshape(n, n_e), idx)
    padded = (counts + EXPERT_BLK - 1) // EXPERT_BLK * EXPERT_BLK
    pad_end = jnp.cumsum(padded).astype(I32)
    dest = route_dest(rank, idx, pad_end - padded)
    n_blocks = m // EXPERT_BLK + n_e
    slots = n_blocks * EXPERT_BLK
    block_e = jnp.minimum(jnp.searchsorted(pad_end, jnp.arange(n_blocks) * EXPERT_BLK, side='right'),
                          n_e - 1).astype(I32)
    n_used = (pad_end[-1] // EXPERT_BLK).astype(I32).reshape(1)

    hp2 = hp.reshape(n, d // 2)
    xs, sp = moe_dispatch(hp2, dest, pad_end, padded.astype(I32), slots,
                          jnp.concatenate([w_sh_gate[0], w_sh_up[0]], axis=-1).astype(BF16), w_sh_down[0].astype(BF16))
    ys = expert_ffn(xs, block_e, n_used, w_exp_gate[0], w_exp_up[0], w_exp_down[0], EXPERT_BLK)
    out = final_mix(x1.reshape(n, d), sp, gates.reshape(n, n_e), ys, dest, g_post_ffn[0], gt_f, l // COMBINE_TM)
    return out.reshape(b, l, d)
```

```python
import functools

import numpy as np
import jax
import jax.numpy as jnp
from jax import lax
from jax.experimental import pallas as pl
from jax.experimental.pallas import tpu as pltpu

F32 = jnp.float32
BF16 = jnp.bfloat16
I32 = jnp.int32

GRID_W = 64
HEAD_DIM = 128
WIN_ROWS = 8
WIN_COLS = 16
ROPE_THETA = 10000.0
SSM_GROUP_CH = 16
SSM_STATE = 64
TOP_K = 8
ROUTED_SCALE = 2.5
EPS = 1e-6
NEG_INF = -1e30

Q_ROWS = 4
BAND_ROWS = Q_ROWS + WIN_ROWS - 1
VMEM_LIMIT = 56 * 1024 * 1024
EXPERT_BLK = 512
PROJ_TN = 1024
DISPATCH_TM = 512
COMBINE_TM = 128


def _cparams():
    return pltpu.CompilerParams(vmem_limit_bytes=VMEM_LIMIT)


def _rms(x, g):
    return x * lax.rsqrt(jnp.mean(x * x, axis=-1, keepdims=True) + EPS) * g


def _pack_halves(lo, hi):
    lo_bits = lax.bitcast_convert_type(lo.astype(BF16).astype(F32), I32)
    hi_bits = lax.bitcast_convert_type(hi.astype(BF16).astype(F32), I32)
    return (hi_bits & jnp.int32(-65536)) | lax.shift_right_logical(lo_bits, jnp.int32(16))


def _unpack_halves(w):
    lo = lax.bitcast_convert_type(lax.shift_left(w, jnp.int32(16)), F32)
    hi = lax.bitcast_convert_type(w & jnp.int32(-65536), F32)
    return lo, hi


def _ada_kernel(c_ref, w_ref, b_ref, o_ref):
    a = jax.nn.silu(c_ref[...]).astype(BF16)
    o_ref[...] = jnp.dot(a, w_ref[...].astype(BF16), preferred_element_type=F32) + b_ref[...]


def ada_mod(c8, w_ada, b_ada):
    d, n = w_ada.shape
    tn = 512
    return pl.pallas_call(
        _ada_kernel,
        grid=(n // tn,),
        in_specs=[pl.BlockSpec((8, d), lambda j: (0, 0)),
                  pl.BlockSpec((d, tn), lambda j: (0, j)),
                  pl.BlockSpec((1, tn), lambda j: (0, j))],
        out_specs=pl.BlockSpec((8, tn), lambda j: (0, j)),
        out_shape=jax.ShapeDtypeStruct((8, n), F32),
        compiler_params=_cparams(),
        name="ada_mod",
    )(c8, w_ada, b_ada.reshape(1, n))


def _modproj_kernel(x_ref, g_ref, sc_ref, sh_ref, w_ref, o_ref, h_scr):
    @pl.when(pl.program_id(2) == 0)
    def _():
        h = _rms(x_ref[0], g_ref[...]) * (1.0 + sc_ref[0]) + sh_ref[0]
        h_scr[...] = h.astype(BF16)

    o_ref[0] = jnp.dot(h_scr[...], w_ref[...], preferred_element_type=F32).astype(o_ref.dtype)


def mod_proj(x, g, scale, shift, w_bf16, col_blk_off, n_out, tm, tn):
    b, l, d = x.shape
    return pl.pallas_call(
        _modproj_kernel,
        grid=(b, l // tm, n_out // tn),
        in_specs=[pl.BlockSpec((1, tm, d), lambda bi, i, j: (bi, i, 0)),
                  pl.BlockSpec((1, d), lambda bi, i, j: (0, 0)),
                  pl.BlockSpec((1, 1, d), lambda bi, i, j: (bi, 0, 0)),
                  pl.BlockSpec((1, 1, d), lambda bi, i, j: (bi, 0, 0)),
                  pl.BlockSpec((d, tn), lambda bi, i, j: (0, j + col_blk_off))],
        out_specs=pl.BlockSpec((1, tm, tn), lambda bi, i, j: (bi, i, j)),
        out_shape=jax.ShapeDtypeStruct((b, l, n_out), BF16),
        scratch_shapes=[pltpu.VMEM((tm, d), BF16)],
        compiler_params=_cparams(),
        name="mod_proj",
    )(x, g.reshape(1, d), scale, shift, w_bf16)


def _rope_tables(l):
    half = HEAD_DIM // 2
    quarter = half // 2
    inv_freq = 1.0 / (ROPE_THETA ** (jnp.arange(0, half, 2, dtype=F32) / half))
    rows = (jnp.arange(l) // GRID_W).astype(F32)
    cols = (jnp.arange(l) % GRID_W).astype(F32)

    def cs(pos):
        ang = pos[:, None] * inv_freq[None, :]
        return jnp.cos(ang), jnp.sin(ang)

    cr, sr = cs(rows)
    cc, sc = cs(cols)
    zero = jnp.zeros((l, quarter), F32)
    cos = jnp.concatenate([cr, cr, cc, cc], axis=-1)
    s_lo = jnp.concatenate([-sr, zero, -sc, zero], axis=-1)
    s_hi = jnp.concatenate([zero, sr, zero, sc], axis=-1)
    return cos, s_lo, s_hi


def _attn_bias_plan(rows_n):
    kh = min(WIN_ROWS, rows_n)
    plan = []
    for r0, sb in [(0, 0), (Q_ROWS, 0), (rows_n - Q_ROWS, rows_n - BAND_ROWS)]:
        qrow = r0 + np.arange(Q_ROWS)
        krow = sb + np.arange(BAND_ROWS)
        rs = np.clip(qrow - kh // 2, 0, rows_n - kh)
        ok = (krow[None, :] >= rs[:, None]) & (krow[None, :] < rs[:, None] + kh)
        ro = krow[None, :] - qrow[:, None] + (WIN_ROWS - 1)
        plan.append(np.where(ok, ro, 2 * WIN_ROWS - 1))
    return np.stack(plan)


def _attn_toe(rpb):
    col = np.arange(GRID_W)
    cstart = np.clip(col - WIN_COLS // 2, 0, GRID_W - WIN_COLS)
    ok_c = (col[None, :] >= cstart[:, None]) & (col[None, :] < cstart[:, None] + WIN_COLS)
    co = np.clip(col[None, :] - col[:, None], -(WIN_COLS - 1), WIN_COLS - 1) + (WIN_COLS - 1)
    onehot = (co[None] == np.arange(2 * WIN_COLS - 1)[:, None, None]).astype(np.float32)
    toe = jnp.einsum('hrc,cqk->hrqk', rpb.astype(F32), onehot, precision=lax.Precision.HIGHEST)
    toe = jnp.where(ok_c[None, None], toe, NEG_INF)
    return jnp.concatenate([toe, jnp.full((rpb.shape[0], 1, GRID_W, GRID_W), NEG_INF, F32)], axis=1)


def _attn_kernel(q_ref, k_ref, v_ref, kc_ref, vc_ref, toe_ref, cos_ref, slo_ref, shi_ref, o_ref,
                 qr_scr, qs_scr, kr_scr, bias_scr, *, n_blk, rows_n, plan):
    quarter = HEAD_DIM // 4
    qn = Q_ROWS * GRID_W
    kn = BAND_ROWS * GRID_W
    scale = HEAD_DIM ** -0.5

    def rope(x, sl):
        return (x * cos_ref[sl, :] + pltpu.roll(x, HEAD_DIM - quarter, 1) * slo_ref[sl, :]
                + pltpu.roll(x, quarter, 1) * shi_ref[sl, :])

    def rope_body(i, _):
        sl = pl.ds(pl.multiple_of(i * qn, qn), qn)
        q = q_ref[0, sl, :].astype(F32) * scale
        qs_scr[sl, :] = q.astype(BF16)
        qr_scr[sl, :] = rope(q, sl).astype(BF16)
        kr_scr[sl, :] = rope(k_ref[0, sl, :].astype(F32), sl).astype(BF16)
        return 0

    lax.fori_loop(0, n_blk, rope_body, 0)

    @pl.when(pl.program_id(1) == 0)
    def _():
        for case in range(plan.shape[0]):
            for qi in range(Q_ROWS):
                for kj in range(BAND_ROWS):
                    bias_scr[case, qi * GRID_W:(qi + 1) * GRID_W, kj * GRID_W:(kj + 1) * GRID_W] = (
                        toe_ref[0, int(plan[case, qi, kj])])

    kc = kc_ref[0]
    vc = vc_ref[0]
    nt = (((1,), (1,)), ((), ()))

    def body(i, _):
        sb = jnp.clip(i * Q_ROWS - WIN_ROWS // 2, 0, rows_n - BAND_ROWS)
        ks = pl.ds(pl.multiple_of(sb * GRID_W, GRID_W), kn)
        qs = pl.ds(pl.multiple_of(i * qn, qn), qn)
        case = jnp.where(i == 0, 0, jnp.where(i == n_blk - 1, 2, 1))
        s = lax.dot_general(qr_scr[qs, :], kr_scr[ks, :], nt, preferred_element_type=F32) + bias_scr[case]
        sc = lax.dot_general(qs_scr[qs, :], kc, nt, preferred_element_type=F32)
        m = jnp.maximum(jnp.max(s, axis=-1, keepdims=True), jnp.max(sc, axis=-1, keepdims=True))
        p = jnp.exp(s - m)
        pc = jnp.exp(sc - m)
        den = jnp.sum(p, axis=-1, keepdims=True) + jnp.sum(pc, axis=-1, keepdims=True)
        o = (jnp.dot(p.astype(BF16), v_ref[0, ks, :], preferred_element_type=F32)
             + jnp.dot(pc.astype(BF16), vc, preferred_element_type=F32))
        o_ref[0, qs, :] = (o / den).astype(o_ref.dtype)
        return 0

    lax.fori_loop(0, n_blk, body, 0, unroll=4)


def neighborhood_attention(p_lat, p_ctx, rpb, n_heads):
    b, l, _ = p_lat.shape
    lc = p_ctx.shape[1]
    rows_n = l // GRID_W
    assert rows_n % Q_ROWS == 0 and rows_n >= BAND_ROWS + Q_ROWS
    n_blk = rows_n // Q_ROWS
    plan = _attn_bias_plan(rows_n)
    cos, s_lo, s_hi = _rope_tables(l)
    qn, kn = Q_ROWS * GRID_W, BAND_ROWS * GRID_W
    h = n_heads
    tab = pl.BlockSpec((l, HEAD_DIM), lambda hi, bi: (0, 0))
    return pl.pallas_call(
        functools.partial(_attn_kernel, n_blk=n_blk, rows_n=rows_n, plan=plan),
        grid=(h, b),
        in_specs=[pl.BlockSpec((1, l, HEAD_DIM), lambda hi, bi: (bi, 0, hi)),
                  pl.BlockSpec((1, l, HEAD_DIM), lambda hi, bi: (bi, 0, hi + h)),
                  pl.BlockSpec((1, l, HEAD_DIM), lambda hi, bi: (bi, 0, hi + 2 * h)),
                  pl.BlockSpec((1, lc, HEAD_DIM), lambda hi, bi: (bi, 0, hi)),
                  pl.BlockSpec((1, lc, HEAD_DIM), lambda hi, bi: (bi, 0, hi + h)),
                  pl.BlockSpec((1, 2 * WIN_ROWS, GRID_W, GRID_W), lambda hi, bi: (hi, 0, 0, 0)),
                  tab, tab, tab],
        out_specs=pl.BlockSpec((1, l, HEAD_DIM), lambda hi, bi: (bi, 0, hi)),
        out_shape=jax.ShapeDtypeStruct((b, l, h * HEAD_DIM), BF16),
        scratch_shapes=[pltpu.VMEM((l, HEAD_DIM), BF16)] * 3 + [pltpu.VMEM((plan.shape[0], qn, kn), F32)],
        compiler_params=_cparams(),
        name="nbr_attn",
    )(p_lat, p_lat, p_lat, p_ctx, p_ctx, _attn_toe(rpb), cos, s_lo, s_hi)


SSM_CHUNK = 16
SSM_ROWS = 8
SSM_PACK = 8


def _ssm_mats(a_re, a_im, log_dt, b_re, b_im, c_re, c_im, d_skip):
    t = SSM_CHUNK
    hi = lax.Precision.HIGHEST
    a = lax.complex(a_re.astype(F32), a_im.astype(F32))
    dta = jnp.exp(log_dt.astype(F32))[..., None] * a
    a_bar = jnp.exp(dta)
    b_bar = ((a_bar - 1.0) / a)[..., None] * lax.complex(b_re.astype(F32), b_im.astype(F32))
    cm = lax.complex(c_re.astype(F32), c_im.astype(F32))
    k = jnp.arange(t + 1, dtype=F32)
    ap = jnp.exp(dta[..., None] * k)
    g, p, n = a.shape[1], a.shape[2], b_re.shape[-1]
    kern = jnp.einsum('dgnp,dgpl,dgpm->dglnm', cm, ap[..., :t], b_bar, precision=hi).real
    s_i = np.arange(t)[:, None]
    t_i = np.arange(t)[None, :]
    kf = kern[0][:, np.clip(t_i - s_i, 0, t - 1)]
    kb = kern[1][:, np.clip(s_i - t_i, 0, t - 1)]
    eye = jnp.eye(n, dtype=F32)
    m_mat = (jnp.where((s_i <= t_i)[None, :, :, None, None], kf, 0.0)
             + jnp.where((s_i >= t_i)[None, :, :, None, None], kb, 0.0)
             + (s_i == t_i)[None, :, :, None, None] * (d_skip.astype(F32)[:, None, None, :, None] * eye))
    m_mat = jnp.transpose(m_mat, (0, 1, 4, 2, 3)).reshape(g, t * n, t * n)
    wf = ap[0][:, :, t - 1::-1][..., :t, None] * b_bar[0][:, :, None, :]
    wb = ap[1][:, :, :t, None] * b_bar[1][:, :, None, :]
    to_rows = lambda z: jnp.transpose(z, (0, 2, 3, 1)).reshape(g, t * n, p)
    w_mat = jnp.concatenate([to_rows(wf.real), to_rows(wb.real), to_rows(wf.imag), to_rows(wb.imag)], axis=-1)
    zf = jnp.transpose(cm[0], (0, 2, 1))[:, :, None, :] * ap[0][:, :, 1:t + 1, None]
    zb = jnp.transpose(cm[1], (0, 2, 1))[:, :, None, :] * ap[1][:, :, t:0:-1, None]
    flat = lambda z: z.reshape(g, p, t * n)
    r_mat = jnp.concatenate([flat(zf.real), flat(zb.real), -flat(zf.imag), -flat(zb.imag)], axis=1)
    a_t = ap[..., t]
    a_vec = jnp.stack([jnp.concatenate([a_t[0].real, a_t[1].real], -1),
                       jnp.concatenate([a_t[0].imag, a_t[1].imag], -1)], axis=1)
    a_vec = jnp.pad(a_vec, ((0, 0), (0, SSM_ROWS - 2), (0, 0)))
    return m_mat.astype(BF16), w_mat.astype(BF16), r_mat.astype(BF16), a_vec


def _s5_kernel(u_ref, sel_ref, selt_ref, m_ref, w_ref, r_ref, a_ref, y_ref, v_scr, ent_scr, *, n_ctx, n_chunks):
    u = jnp.dot(u_ref[0], sel_ref[0], preferred_element_type=F32).astype(BF16)
    p2 = a_ref.shape[-1]
    p = p2 // 2
    v_scr[...] = jnp.dot(u, w_ref[0], preferred_element_type=F32)
    a_re = jnp.broadcast_to(a_ref[0, 0:1, :], (SSM_ROWS, p2))
    a_im = jnp.broadcast_to(a_ref[0, 1:2, :], (SSM_ROWS, p2))
    is_fwd = lax.broadcasted_iota(I32, (SSM_ROWS, p2), 1) < p

    def step(j, carry):
        s_re, s_im = carry
        cb = jnp.where(j < n_ctx, n_ctx - 1 - j, n_chunks + n_ctx - 1 - j)
        rf = pl.ds(pl.multiple_of(j * SSM_ROWS, SSM_ROWS), SSM_ROWS)
        rb = pl.ds(pl.multiple_of(cb * SSM_ROWS, SSM_ROWS), SSM_ROWS)
        ent_scr[rf, 0:p] = s_re[:, :p]
        ent_scr[rb, p:p2] = s_re[:, p:]
        ent_scr[rf, p2:p2 + p] = s_im[:, :p]
        ent_scr[rb, p2 + p:] = s_im[:, p:]
        in_re = jnp.where(is_fwd, v_scr[rf, :p2], v_scr[rb, :p2])
        in_im = jnp.where(is_fwd, v_scr[rf, p2:], v_scr[rb, p2:])
        return a_re * s_re - a_im * s_im + in_re, a_re * s_im + a_im * s_re + in_im

    zero = jnp.zeros((SSM_ROWS, p2), F32)
    lax.fori_loop(0, n_chunks, step, (zero, zero))
    y = (jnp.dot(u, m_ref[0], preferred_element_type=F32)
         + jnp.dot(ent_scr[...].astype(BF16), r_ref[0], preferred_element_type=F32)).astype(BF16)
    def part(c):
        return jnp.dot(y, selt_ref[0, :, c:c + PACK_CHUNK], preferred_element_type=F32).astype(y_ref.dtype)

    @pl.when(pl.program_id(1) == 0)
    def _():
        for c in range(0, y_ref.shape[-1], PACK_CHUNK):
            y_ref[0, :, c:c + PACK_CHUNK] = part(c)

    @pl.when(pl.program_id(1) > 0)
    def _():
        for c in range(0, y_ref.shape[-1], PACK_CHUNK):
            y_ref[0, :, c:c + PACK_CHUNK] = y_ref[0, :, c:c + PACK_CHUNK] + part(c)


def s5_mixer(u_lat, u_ctx, mats):
    m_mat, w_mat, r_mat, a_vec = mats
    b, l, dch = u_lat.shape
    lc = u_ctx.shape[1]
    g = m_mat.shape[0]
    n = dch // g
    t = SSM_CHUNK
    tn = t * n
    gp = SSM_PACK
    packs, lanes = g // gp, gp * n
    n_ctx, n_chunks = lc // t, (lc + l) // t
    assert b <= SSM_ROWS and lc % t == 0 and l % t == 0 and g % gp == 0
    z = jnp.concatenate([u_ctx, u_lat], axis=1).reshape(b, n_chunks, t, packs, lanes)
    z = jnp.pad(jnp.transpose(z, (3, 1, 0, 2, 4)), ((0, 0), (0, 0), (0, SSM_ROWS - b), (0, 0), (0, 0)))
    rows = n_chunks * SSM_ROWS
    z = z.reshape(packs, rows, t * lanes)
    ri = jnp.arange(t * lanes)
    ci = jnp.arange(tn)
    same = (ri[:, None] // lanes == ci[None, :] // n) & (ri[:, None] % n == ci[None, :] % n)
    sel = (same[None] & ((ri[None, :, None] % lanes) // n == jnp.arange(gp)[:, None, None])).astype(BF16)
    selt = jnp.transpose(sel, (0, 2, 1))
    grp = lambda k: pl.BlockSpec((1, k, tn), lambda pi, qi: (pi * gp + qi, 0, 0))
    pack_blk = pl.BlockSpec((1, rows, t * lanes), lambda pi, qi: (pi, 0, 0))
    y = pl.pallas_call(
        functools.partial(_s5_kernel, n_ctx=n_ctx, n_chunks=n_chunks),
        grid=(packs, gp),
        in_specs=[pack_blk,
                  pl.BlockSpec((1, t * lanes, tn), lambda pi, qi: (qi, 0, 0)),
                  pl.BlockSpec((1, tn, t * lanes), lambda pi, qi: (qi, 0, 0)),
                  grp(tn), grp(tn), grp(r_mat.shape[1]),
                  pl.BlockSpec((1, SSM_ROWS, a_vec.shape[-1]), lambda pi, qi: (pi * gp + qi, 0, 0))],
        out_specs=pack_blk,
        out_shape=jax.ShapeDtypeStruct((packs, rows, t * lanes), BF16),
        scratch_shapes=[pltpu.VMEM((rows, w_mat.shape[-1]), F32), pltpu.VMEM((rows, r_mat.shape[1]), F32)],
        compiler_params=_cparams(),
        name="s5_mixer",
    )(z, sel, selt, m_mat, w_mat, r_mat, a_vec)
    y = y.reshape(packs, n_chunks, SSM_ROWS, t, lanes)[:, n_ctx:, :b]
    return jnp.transpose(y, (2, 1, 3, 0, 4)).reshape(b, l, dch)


def _glu_kernel(y_ref, w_ref, b_ref, o_ref):
    z = jax.nn.gelu(y_ref[...].astype(F32))
    gate = jax.nn.sigmoid(jnp.dot(z.astype(BF16), w_ref[...], preferred_element_type=F32) + b_ref[...])
    o_ref[...] = (z * gate).astype(o_ref.dtype)


def s5_glu(y, w_bf16, b_glu, tm=512):
    n, d = y.shape
    return pl.pallas_call(
        _glu_kernel,
        grid=(n // tm,),
        in_specs=[pl.BlockSpec((tm, d), lambda i: (i, 0)),
                  pl.BlockSpec((d, d), lambda i: (0, 0)),
                  pl.BlockSpec((1, d), lambda i: (0, 0))],
        out_specs=pl.BlockSpec((tm, d), lambda i: (i, 0)),
        out_shape=jax.ShapeDtypeStruct((n, d), BF16),
        compiler_params=_cparams(),
        name="s5_glu",
    )(y, w_bf16, b_glu.reshape(1, d))


def _merge_kernel(a_ref, s_ref, ga_ref, gs_ref, w_ref, o_ref, h_scr):
    da = a_ref.shape[-1]

    @pl.when(pl.program_id(1) == 0)
    def _():
        h_scr[:, :da] = _rms(a_ref[...].astype(F32), ga_ref[...]).astype(BF16)
        h_scr[:, da:] = _rms(s_ref[...].astype(F32), gs_ref[...]).astype(BF16)

    o_ref[...] = jnp.dot(h_scr[...], w_ref[...], preferred_element_type=F32)


def merge_proj(attn, ssm, g_attn, g_ssm, w_bf16, tm=512, tn=1024):
    n, da = attn.shape
    ds = ssm.shape[1]
    d, dout = w_bf16.shape
    return pl.pallas_call(
        _merge_kernel,
        grid=(n // tm, dout // tn),
        in_specs=[pl.BlockSpec((tm, da), lambda i, j: (i, 0)),
                  pl.BlockSpec((tm, ds), lambda i, j: (i, 0)),
                  pl.BlockSpec((1, da), lambda i, j: (0, 0)),
                  pl.BlockSpec((1, ds), lambda i, j: (0, 0)),
                  pl.BlockSpec((d, tn), lambda i, j: (0, j))],
        out_specs=pl.BlockSpec((tm, tn), lambda i, j: (i, j)),
        out_shape=jax.ShapeDtypeStruct((n, dout), F32),
        scratch_shapes=[pltpu.VMEM((tm, d), BF16)],
        compiler_params=_cparams(),
        name="merge_proj",
    )(attn, ssm, g_attn.reshape(1, da), g_ssm.reshape(1, ds), w_bf16)


def _post_mix_kernel(x_ref, mix_ref, gpost_ref, gt_ref, gpre_ref, sc_ref, sh_ref, wr_ref, br_ref,
                     x1_ref, hp_ref, idx_ref, gate_ref, mask_ref):
    x1 = x_ref[0] + gt_ref[0] * _rms(mix_ref[0], gpost_ref[...])
    x1_ref[0] = x1
    h = _rms(x1, gpre_ref[...]) * (1.0 + sc_ref[0]) + sh_ref[0]
    half = h.shape[-1] // 2
    hp_ref[0] = _pack_halves(h[:, :half], h[:, half:])
    scores = jax.nn.sigmoid(jnp.dot(h.astype(BF16), wr_ref[...], preferred_element_type=F32))
    n_e = scores.shape[-1]
    lane = lax.broadcasted_iota(I32, scores.shape, 1)
    biased = scores + br_ref[...]
    idx_out = jnp.zeros(scores.shape, I32)
    sel_out = jnp.zeros(scores.shape, F32)
    mask = jnp.zeros(scores.shape, jnp.bool_)
    for k in range(TOP_K):
        m = jnp.max(biased, axis=-1, keepdims=True)
        ik = jnp.min(jnp.where(biased == m, lane, n_e), axis=-1, keepdims=True)
        hit = lane == ik
        sel_k = jnp.sum(jnp.where(hit, scores, 0.0), axis=-1, keepdims=True)
        idx_out = jnp.where(lane == k, ik, idx_out)
        sel_out = jnp.where(lane == k, sel_k, sel_out)
        mask = jnp.logical_or(mask, hit)
        biased = jnp.where(hit, -jnp.inf, biased)
    idx_ref[0] = idx_out
    gate_ref[0] = sel_out / jnp.sum(sel_out, axis=-1, keepdims=True) * ROUTED_SCALE
    mask_ref[0] = mask.astype(BF16)


def post_mix(x, mix, g_post, gt, g_pre, scale, shift, wr_bf16, b_router, tm=256):
    b, l, d = x.shape
    n_e = wr_bf16.shape[1]
    row = pl.BlockSpec((1, tm, d), lambda bi, i: (bi, i, 0))
    prow = pl.BlockSpec((1, tm, d // 2), lambda bi, i: (bi, i, 0))
    vec = pl.BlockSpec((1, d), lambda bi, i: (0, 0))
    bvec = pl.BlockSpec((1, 1, d), lambda bi, i: (bi, 0, 0))
    small = pl.BlockSpec((1, tm, n_e), lambda bi, i: (bi, i, 0))
    return pl.pallas_call(
        _post_mix_kernel,
        grid=(b, l // tm),
        in_specs=[row, row, vec, bvec, vec, bvec, bvec,
                  pl.BlockSpec((d, n_e), lambda bi, i: (0, 0)),
                  pl.BlockSpec((1, n_e), lambda bi, i: (0, 0))],
        out_specs=[row, prow, small, small, small],
        out_shape=[jax.ShapeDtypeStruct((b, l, d), F32),
                   jax.ShapeDtypeStruct((b, l, d // 2), I32),
                   jax.ShapeDtypeStruct((b, l, n_e), I32),
                   jax.ShapeDtypeStruct((b, l, n_e), F32),
                   jax.ShapeDtypeStruct((b, l, n_e), BF16)],
        compiler_params=_cparams(),
        name="post_mix",
    )(x, mix, g_post.reshape(1, d), gt, g_pre.reshape(1, d), scale, shift, wr_bf16, b_router.reshape(1, n_e))


def _rank_kernel(mask_ref, idx_ref, rank_ref, cnt_ref, carry):
    @pl.when(pl.program_id(0) == 0)
    def _():
        carry[...] = jnp.zeros_like(carry)

    m = mask_ref[...]
    tm = m.shape[0]
    earlier = (lax.broadcasted_iota(I32, (tm, tm), 1) < lax.broadcasted_iota(I32, (tm, tm), 0)).astype(BF16)
    excl = jnp.dot(earlier, m, preferred_element_type=F32) + carry[0:1, :]
    lane = lax.broadcasted_iota(I32, m.shape, 1)
    idx = idx_ref[...]
    out = jnp.zeros(m.shape, F32)
    for k in range(TOP_K):
        rk = jnp.sum(jnp.where(lane == idx[:, k:k + 1], excl, 0.0), axis=-1, keepdims=True)
        out = jnp.where(lane == k, rk, out)
    rank_ref[...] = out.astype(I32)
    carry[0:1, :] = carry[0:1, :] + jnp.sum(m.astype(F32), axis=0, keepdims=True)
    cnt_ref[...] = carry[...]


def route_ranks(mask, idx, tm=512):
    n, n_e = mask.shape
    blk = pl.BlockSpec((tm, n_e), lambda i: (i, 0))
    rank, cnt = pl.pallas_call(
        _rank_kernel,
        grid=(n // tm,),
        in_specs=[blk, blk],
        out_specs=[blk, pl.BlockSpec((8, n_e), lambda i: (0, 0))],
        out_shape=[jax.ShapeDtypeStruct((n, n_e), I32), jax.ShapeDtypeStruct((8, n_e), F32)],
        scratch_shapes=[pltpu.VMEM((8, n_e), F32)],
        compiler_params=_cparams(),
        name="route_ranks",
    )(mask, idx)
    return rank, cnt[0].astype(I32)


def _dest_kernel(rank_ref, idx_ref, ps_ref, dest_ref):
    idx = idx_ref[...]
    lane = lax.broadcasted_iota(I32, idx.shape, 1)
    start = jnp.zeros(idx.shape, F32)
    for k in range(TOP_K):
        sk = jnp.sum(jnp.where(lane == idx[:, k:k + 1], ps_ref[...], 0.0), axis=-1, keepdims=True)
        start = jnp.where(lane == k, sk, start)
    dest_ref[...] = rank_ref[...] + start.astype(I32)


def route_dest(rank, idx, pad_start, tm=512):
    n, n_e = rank.shape
    blk = pl.BlockSpec((tm, n_e), lambda i: (i, 0))
    dest = pl.pallas_call(
        _dest_kernel,
        grid=(n // tm,),
        in_specs=[blk, blk, pl.BlockSpec((1, n_e), lambda i: (0, 0))],
        out_specs=blk,
        out_shape=jax.ShapeDtypeStruct((n, n_e), I32),
        compiler_params=_cparams(),
        name="route_dest",
    )(rank, idx, pad_start.astype(F32).reshape(1, n_e))
    return dest[:, :TOP_K].reshape(n * TOP_K)


def _dispatch_kernel(pe_ref, pd_ref, dest_ref, h_ref, wgu_ref, wd_ref, xs_ref, sp_ref, z_scr, sem, *, n_e, tm, blk):
    def zero_copy(e):
        start = pl.multiple_of(pe_ref[e] - blk, blk)
        return pltpu.make_async_copy(z_scr, xs_ref.at[pl.ds(start, blk), :], sem)

    @pl.when(pl.program_id(0) == 0)
    def _():
        z_scr[...] = jnp.zeros_like(z_scr)

        def start(e, _):
            @pl.when(pd_ref[e] > 0)
            def _():
                zero_copy(e).start()
            return 0

        def wait(e, _):
            @pl.when(pd_ref[e] > 0)
            def _():
                zero_copy(e).wait()
            return 0

        lax.fori_loop(0, n_e, start, 0)
        lax.fori_loop(0, n_e, wait, 0)

    sub = 8
    half = h_ref.shape[-1]
    n_ch = half // PACK_CHUNK
    rows_ph = tm // (2 * n_ch)

    def issue_rows(r0):
        for g in range(0, rows_ph, sub):
            tile = h_ref.at[pl.ds(r0 + g, sub), :]
            for tt in range(sub):
                for k in range(TOP_K):
                    pltpu.make_async_copy(tile.at[pl.ds(tt, 1), :],
                                          xs_ref.at[pl.ds(dest_ref[(r0 + g + tt) * TOP_K + k], 1), :],
                                          sem).start(priority=k % 2)

    acc = None
    for c in range(n_ch):
        issue_rows(c * rows_ph)
        cl = slice(c * PACK_CHUNK, (c + 1) * PACK_CHUNK)
        ch = slice(half + c * PACK_CHUNK, half + (c + 1) * PACK_CHUNK)
        lo, hi = _unpack_halves(h_ref[:, cl])
        part = (jnp.dot(lo.astype(BF16), wgu_ref[cl, :], preferred_element_type=F32)
                + jnp.dot(hi.astype(BF16), wgu_ref[ch, :], preferred_element_type=F32))
        acc = part if acc is None else acc + part
    de = acc.shape[-1] // 2
    hid = (jax.nn.silu(acc[:, :de]) * acc[:, de:]).astype(BF16)
    for c in range(n_ch):
        issue_rows((n_ch + c) * rows_ph)
        cl = slice(c * PACK_CHUNK, (c + 1) * PACK_CHUNK)
        ch = slice(half + c * PACK_CHUNK, half + (c + 1) * PACK_CHUNK)
        sp_ref[:, cl] = _pack_halves(jnp.dot(hid, wd_ref[:, cl], preferred_element_type=F32),
                                     jnp.dot(hid, wd_ref[:, ch], preferred_element_type=F32))
    rows = tm * TOP_K
    pltpu.make_async_copy(xs_ref.at[pl.ds(0, rows), :], xs_ref.at[pl.ds(0, rows), :], sem).wait()


def moe_dispatch(hp, dest_flat, pad_end, padded, slots, wsgu, wsd, tm=DISPATCH_TM):
    n, dw = hp.shape
    n_e = pad_end.shape[0]
    d, de2 = wsgu.shape
    row = pl.BlockSpec((tm, dw), lambda i, pe, pd: (i, 0))
    grid_spec = pltpu.PrefetchScalarGridSpec(
        num_scalar_prefetch=2,
        grid=(n // tm,),
        in_specs=[pl.BlockSpec((tm * TOP_K,), lambda i, pe, pd: (i,), memory_space=pltpu.SMEM),
                  row,
                  pl.BlockSpec((d, de2), lambda i, pe, pd: (0, 0)),
                  pl.BlockSpec((de2 // 2, d), lambda i, pe, pd: (0, 0))],
        out_specs=[pl.BlockSpec(memory_space=pl.ANY), row],
        scratch_shapes=[pltpu.VMEM((EXPERT_BLK, dw), I32), pltpu.SemaphoreType.DMA(())],
    )
    return pl.pallas_call(
        functools.partial(_dispatch_kernel, n_e=n_e, tm=tm, blk=EXPERT_BLK),
        grid_spec=grid_spec,
        out_shape=[jax.ShapeDtypeStruct((slots, dw), I32), jax.ShapeDtypeStruct((n, dw), I32)],
        compiler_params=_cparams(),
        name="moe_dispatch",
    )(pad_end, padded, dest_flat, hp, wsgu, wsd)


PACK_CHUNK = 512


def _swiglu_hidden(xp_ref, wgu):
    half = xp_ref.shape[-1]
    acc = None
    for c in range(half // PACK_CHUNK):
        cl = slice(c * PACK_CHUNK, (c + 1) * PACK_CHUNK)
        ch = slice(half + c * PACK_CHUNK, half + (c + 1) * PACK_CHUNK)
        lo, hi = _unpack_halves(xp_ref[:, cl])
        part = (jnp.dot(lo.astype(BF16), wgu(cl), preferred_element_type=F32)
                + jnp.dot(hi.astype(BF16), wgu(ch), preferred_element_type=F32))
        acc = part if acc is None else acc + part
    de = acc.shape[-1] // 2
    return (jax.nn.silu(acc[:, :de]) * acc[:, de:]).astype(BF16)


def _down_packed(hid, wd, o_ref):
    half = o_ref.shape[-1]
    for c in range(half // PACK_CHUNK):
        cl = slice(c * PACK_CHUNK, (c + 1) * PACK_CHUNK)
        ch = slice(half + c * PACK_CHUNK, half + (c + 1) * PACK_CHUNK)
        o_ref[:, cl] = _pack_halves(jnp.dot(hid, wd(cl), preferred_element_type=F32),
                                    jnp.dot(hid, wd(ch), preferred_element_type=F32))


def _expert_weights(sched, i, w_hbm, w_buf, sem):
    be_ref, first_ref, slot_ref, nxt_ref = sched
    s = slot_ref[i]

    def copies(e, sl):
        return [pltpu.make_async_copy(h.at[e], v.at[sl], sem.at[j, sl]) for j, (h, v) in enumerate(zip(w_hbm, w_buf))]

    @pl.when(first_ref[i] == 1)
    def _():
        @pl.when(i == 0)
        def _():
            for c in copies(be_ref[0], 0):
                c.start()

        for c in copies(be_ref[i], s):
            c.wait()

        @pl.when(nxt_ref[i] >= 0)
        def _():
            for c in copies(nxt_ref[i], 1 - s):
                c.start()

    return s


def _expert_up_kernel(be_ref, nu_ref, first_ref, slot_ref, nxt_ref, x_ref, wg_hbm, wu_hbm, hid_ref,
                      wg_buf, wu_buf, sem):
    i = pl.program_id(0)
    used = i < nu_ref[0]
    s = _expert_weights((be_ref, first_ref, slot_ref, nxt_ref), i, (wg_hbm, wu_hbm), (wg_buf, wu_buf), sem)

    def wgu(sl):
        return jnp.concatenate([wg_buf[s, sl, :].astype(BF16), wu_buf[s, sl, :].astype(BF16)], axis=-1)

    @pl.when(used)
    def _():
        hid_ref[...] = _swiglu_hidden(x_ref, wgu)

    @pl.when(jnp.logical_not(used))
    def _():
        hid_ref[...] = jnp.zeros_like(hid_ref)


def _expert_down_kernel(be_ref, nu_ref, first_ref, slot_ref, nxt_ref, hid_ref, wd_hbm, o_ref, wd_buf, sem):
    i = pl.program_id(0)
    used = i < nu_ref[0]
    s = _expert_weights((be_ref, first_ref, slot_ref, nxt_ref), i, (wd_hbm,), (wd_buf,), sem)

    @pl.when(used)
    def _():
        _down_packed(hid_ref[...], lambda sl: wd_buf[s, :, sl].astype(BF16), o_ref)

    @pl.when(jnp.logical_not(used))
    def _():
        o_ref[...] = jnp.zeros_like(o_ref)


def _expert_schedule(block_e, n_used):
    nb = block_e.shape[0]
    ar = jnp.arange(nb, dtype=I32)
    used = ar < n_used[0]
    first = used & ((ar == 0) | (block_e != jnp.roll(block_e, 1)))
    slot = jnp.where(used, (jnp.cumsum(first.astype(I32)) - 1) % 2, 0).astype(I32)
    nf = jnp.where(first, ar, nb)
    nxt_idx = jnp.concatenate([lax.cummin(nf, axis=0, reverse=True)[1:], jnp.full((1,), nb, I32)])
    nxt = jnp.where(nxt_idx < nb, block_e[jnp.minimum(nxt_idx, nb - 1)], -1).astype(I32)
    return first.astype(I32), slot, nxt


def expert_ffn(xs, block_e, n_used, wg, wu, wd, blk):
    slots, dw = xs.shape
    d, de = wg.shape[1], wg.shape[2]
    n_blocks = slots // blk
    sched = (block_e, n_used) + _expert_schedule(block_e, n_used)
    cur = lambda i, nu: jnp.minimum(i, nu[0] - 1)
    hbm = pl.BlockSpec(memory_space=pl.ANY)
    up_spec = pltpu.PrefetchScalarGridSpec(
        num_scalar_prefetch=5,
        grid=(n_blocks,),
        in_specs=[pl.BlockSpec((blk, dw), lambda i, be, nu, *_: (cur(i, nu), 0)), hbm, hbm],
        out_specs=pl.BlockSpec((blk, de), lambda i, *_: (i, 0)),
        scratch_shapes=[pltpu.VMEM((2, d, de), F32), pltpu.VMEM((2, d, de), F32), pltpu.SemaphoreType.DMA((2, 2))],
    )
    hid = pl.pallas_call(
        _expert_up_kernel,
        grid_spec=up_spec,
        out_shape=jax.ShapeDtypeStruct((slots, de), BF16),
        compiler_params=_cparams(),
        name="expert_up",
    )(*sched, xs, wg, wu)
    down_spec = pltpu.PrefetchScalarGridSpec(
        num_scalar_prefetch=5,
        grid=(n_blocks,),
        in_specs=[pl.BlockSpec((blk, de), lambda i, be, nu, *_: (cur(i, nu), 0)), hbm],
        out_specs=pl.BlockSpec((blk, dw), lambda i, *_: (i, 0)),
        scratch_shapes=[pltpu.VMEM((2, de, d), F32), pltpu.SemaphoreType.DMA((1, 2))],
    )
    return pl.pallas_call(
        _expert_down_kernel,
        grid_spec=down_spec,
        out_shape=jax.ShapeDtypeStruct((slots, dw), I32),
        compiler_params=_cparams(),
        name="expert_down",
    )(*sched, hid, wd)


def _final_kernel(dc_ref, dn_ref, x1_ref, sp_ref, gate_ref, ys_ref, gpost_ref, gt_ref, o_ref, buf, sem, *, tm, n_tiles):
    i = pl.program_id(0)
    slot = i % 2
    sub = 8

    def aligned(v):
        return v if isinstance(v, int) else pl.multiple_of(v, sub)

    def issue_rows(dref, s, j):
        base = aligned(j * sub)
        for tt in range(sub):
            for k in range(TOP_K):
                tile = buf.at[s, pl.ds(aligned(k * tm + base), sub), :]
                pltpu.make_async_copy(ys_ref.at[pl.ds(dref[(base + tt) * TOP_K + k], 1), :],
                                      tile.at[pl.ds(tt, 1), :],
                                      sem.at[s]).start(priority=k % 2)

    def reduce_rows(s, j):
        rs = pl.ds(aligned(j * sub), sub)
        g = gate_ref[rs, :]
        acc_lo, acc_hi = _unpack_halves(sp_ref[rs, :])
        for k in range(TOP_K):
            lo, hi = _unpack_halves(buf[s, pl.ds(aligned(k * tm + j * sub), sub), :])
            gk = g[:, k:k + 1]
            acc_lo = acc_lo + gk * lo
            acc_hi = acc_hi + gk * hi
        ffn = jnp.concatenate([acc_lo, acc_hi], axis=-1)
        o_ref[rs, :] = x1_ref[rs, :] + gt_ref[0] * _rms(ffn, gpost_ref[...])

    @pl.when(i == 0)
    def _():
        def first(j, c):
            issue_rows(dc_ref, 0, j)
            return c
        lax.fori_loop(0, tm // sub, first, 0)

    def wait_slot(s):
        pltpu.make_async_copy(ys_ref.at[pl.ds(0, TOP_K * tm), :], buf.at[s], sem.at[s]).wait()

    wait_slot(slot)

    for s in range(2):
        @pl.when(slot == s)
        def _():
            for j in range(tm // sub):
                issue_rows(dn_ref, 1 - s, j)
                reduce_rows(s, j)

    @pl.when(i == n_tiles - 1)
    def _():
        wait_slot(1 - slot)


def final_mix(x1, sp, gates, ys, dest_flat, g_post, gt, tiles_per_batch, tm=COMBINE_TM):
    n, d = x1.shape
    dw = sp.shape[1]
    n_e = gates.shape[1]
    n_tiles = n // tm
    return pl.pallas_call(
        functools.partial(_final_kernel, tm=tm, n_tiles=n_tiles),
        grid=(n_tiles,),
        in_specs=[pl.BlockSpec((tm * TOP_K,), lambda i: (i,), memory_space=pltpu.SMEM),
                  pl.BlockSpec((tm * TOP_K,), lambda i: (jnp.minimum(i + 1, n_tiles - 1),), memory_space=pltpu.SMEM),
                  pl.BlockSpec((tm, d), lambda i: (i, 0)),
                  pl.BlockSpec((tm, dw), lambda i: (i, 0)),
                  pl.BlockSpec((tm, n_e), lambda i: (i, 0)),
                  pl.BlockSpec(memory_space=pl.ANY),
                  pl.BlockSpec((1, d), lambda i: (0, 0)),
                  pl.BlockSpec((1, 1, d), lambda i: (i // tiles_per_batch, 0, 0))],
        out_specs=pl.BlockSpec((tm, d), lambda i: (i, 0)),
        out_shape=jax.ShapeDtypeStruct((n, d), F32),
        scratch_shapes=[pltpu.VMEM((2, TOP_K * tm, dw), I32), pltpu.SemaphoreType.DMA((2,))],
        compiler_params=_cparams(),
        name="final_mix",
    )(dest_flat, dest_flat, x1, sp, gates, ys, g_post.reshape(1, d), gt)


def kernel(x, c, ctx, c_ctx, w_ada, b_ada, g_pre_mix, g_post_mix, g_pre_ffn, g_post_ffn, w_in, rpb, ssm_a_re, ssm_a_im, ssm_log_dt, ssm_b_re, ssm_b_im, ssm_c_re, ssm_c_im, ssm_d, w_glu, b_glu, g_attn_out, g_ssm_out, w_out, w_router, b_router, w_exp_gate, w_exp_up, w_exp_down, w_sh_gate, w_sh_up, w_sh_down):
    b, l, d = x.shape
    lc = ctx.shape[1]
    assert w_ada.shape[0] == 1 and b + 1 <= 8
    n_in = w_in.shape[-1]
    d_ssm = w_glu.shape[-1]
    d_attn = d - d_ssm
    n_heads = d_attn // HEAD_DIM
    n = b * l

    c8 = jnp.concatenate([c, c_ctx[None], jnp.zeros((8 - b - 1, d), F32)], axis=0)
    mod = ada_mod(c8, w_ada[0], b_ada[0]).reshape(8, 6, 1, d)
    sh_m, sc_m, gt_m, sh_f, sc_f, gt_f = [mod[:b, j] for j in range(6)]
    csh_m, csc_m = mod[b:b + 1, 0], mod[b:b + 1, 1]

    w_in_b = w_in[0].astype(BF16)
    tn = PROJ_TN
    p_lat = mod_proj(x, g_pre_mix[0], sc_m, sh_m, w_in_b, 0, n_in, 512, tn)
    p_ctx = mod_proj(ctx.reshape(1, b * lc, d), g_pre_mix[0], csc_m, csh_m, w_in_b,
                     d_attn // tn, n_in - d_attn, 512, tn).reshape(b, lc, n_in - d_attn)

    attn = neighborhood_attention(p_lat, p_ctx, rpb[0], n_heads)

    mats = _ssm_mats(ssm_a_re[0], ssm_a_im[0], ssm_log_dt[0], ssm_b_re[0], ssm_b_im[0],
                     ssm_c_re[0], ssm_c_im[0], ssm_d[0])
    y = s5_mixer(p_lat[..., 3 * d_attn:], p_ctx[..., 2 * d_attn:], mats)
    ssm = s5_glu(y.reshape(n, d_ssm), w_glu[0].astype(BF16), b_glu[0])

    mix = merge_proj(attn.reshape(n, d_attn), ssm, g_attn_out[0], g_ssm_out[0], w_out[0].astype(BF16))
    x1, hp, idx, gates, mask = post_mix(x, mix.reshape(b, l, d), g_post_mix[0], gt_m, g_pre_ffn[0], sc_f, sh_f,
                                        w_router[0].astype(BF16), b_router[0])

    n_e = w_router.shape[-1]
    m = n * TOP_K
    idx = idx.reshape(n, n_e)
    rank, counts = route_ranks(mask.reshape(n, n_e), idx)
    padded = (counts + EXPERT_BLK - 1) // EXPERT_BLK * EXPERT_BLK
    pad_end = jnp.cumsum(padded).astype(I32)
    dest = route_dest(rank, idx, pad_end - padded)
    n_blocks = m // EXPERT_BLK + n_e
    slots = n_blocks * EXPERT_BLK
    block_e = jnp.minimum(jnp.searchsorted(pad_end, jnp.arange(n_blocks) * EXPERT_BLK, side='right'),
                          n_e - 1).astype(I32)
    n_used = (pad_end[-1] // EXPERT_BLK).astype(I32).reshape(1)

    hp2 = hp.reshape(n, d // 2)
    xs, sp = moe_dispatch(hp2, dest, pad_end, padded.astype(I32), slots,
                          jnp.concatenate([w_sh_gate[0], w_sh_up[0]], axis=-1).astype(BF16), w_sh_down[0].astype(BF16))
    ys = expert_ffn(xs, block_e, n_used, w_exp_gate[0], w_exp_up[0], w_exp_down[0], EXPERT_BLK)
    out = final_mix(x1.reshape(n, d), sp, gates.reshape(n, n_e), ys, dest, g_post_ffn[0], gt_f, l // COMBINE_TM)
    return out.reshape(b, l, d)
```

```python
import functools

import numpy as np
import jax
import jax.numpy as jnp
from jax import lax
from jax.experimental import pallas as pl
from jax.experimental.pallas import tpu as pltpu

F32 = jnp.float32
BF16 = jnp.bfloat16
I32 = jnp.int32

GRID_W = 64
HEAD_DIM = 128
WIN_ROWS = 8
WIN_COLS = 16
ROPE_THETA = 10000.0
SSM_GROUP_CH = 16
SSM_STATE = 64
TOP_K = 8
ROUTED_SCALE = 2.5
EPS = 1e-6
NEG_INF = -1e30

Q_ROWS = 4
BAND_ROWS = Q_ROWS + WIN_ROWS - 1
VMEM_LIMIT = 56 * 1024 * 1024
EXPERT_BLK = 512
PROJ_TN = 1024
DISPATCH_TM = 512
COMBINE_TM = 128


def _cparams():
    return pltpu.CompilerParams(vmem_limit_bytes=VMEM_LIMIT)


def _rms(x, g):
    return x * lax.rsqrt(jnp.mean(x * x, axis=-1, keepdims=True) + EPS) * g


def _pack_halves(lo, hi):
    lo_bits = lax.bitcast_convert_type(lo.astype(BF16).astype(F32), I32)
    hi_bits = lax.bitcast_convert_type(hi.astype(BF16).astype(F32), I32)
    return (hi_bits & jnp.int32(-65536)) | lax.shift_right_logical(lo_bits, jnp.int32(16))


def _unpack_halves(w):
    lo = lax.bitcast_convert_type(lax.shift_left(w, jnp.int32(16)), F32)
    hi = lax.bitcast_convert_type(w & jnp.int32(-65536), F32)
    return lo, hi


def _ada_kernel(c_ref, w_ref, b_ref, o_ref):
    a = jax.nn.silu(c_ref[...]).astype(BF16)
    o_ref[...] = jnp.dot(a, w_ref[...].astype(BF16), preferred_element_type=F32) + b_ref[...]


def ada_mod(c8, w_ada, b_ada):
    d, n = w_ada.shape
    tn = 512
    return pl.pallas_call(
        _ada_kernel,
        grid=(n // tn,),
        in_specs=[pl.BlockSpec((8, d), lambda j: (0, 0)),
                  pl.BlockSpec((d, tn), lambda j: (0, j)),
                  pl.BlockSpec((1, tn), lambda j: (0, j))],
        out_specs=pl.BlockSpec((8, tn), lambda j: (0, j)),
        out_shape=jax.ShapeDtypeStruct((8, n), F32),
        compiler_params=_cparams(),
        name="ada_mod",
    )(c8, w_ada, b_ada.reshape(1, n))


def _modproj_kernel(x_ref, g_ref, sc_ref, sh_ref, w_ref, o_ref, h_scr):
    @pl.when(pl.program_id(2) == 0)
    def _():
        h = _rms(x_ref[0], g_ref[...]) * (1.0 + sc_ref[0]) + sh_ref[0]
        h_scr[...] = h.astype(BF16)

    o_ref[0] = jnp.dot(h_scr[...], w_ref[...], preferred_element_type=F32).astype(o_ref.dtype)


def mod_proj(x, g, scale, shift, w_bf16, col_blk_off, n_out, tm, tn):
    b, l, d = x.shape
    return pl.pallas_call(
        _modproj_kernel,
        grid=(b, l // tm, n_out // tn),
        in_specs=[pl.BlockSpec((1, tm, d), lambda bi, i, j: (bi, i, 0)),
                  pl.BlockSpec((1, d), lambda bi, i, j: (0, 0)),
                  pl.BlockSpec((1, 1, d), lambda bi, i, j: (bi, 0, 0)),
                  pl.BlockSpec((1, 1, d), lambda bi, i, j: (bi, 0, 0)),
                  pl.BlockSpec((d, tn), lambda bi, i, j: (0, j + col_blk_off))],
        out_specs=pl.BlockSpec((1, tm, tn), lambda bi, i, j: (bi, i, j)),
        out_shape=jax.ShapeDtypeStruct((b, l, n_out), BF16),
        scratch_shapes=[pltpu.VMEM((tm, d), BF16)],
        compiler_params=_cparams(),
        name="mod_proj",
    )(x, g.reshape(1, d), scale, shift, w_bf16)


def _rope_tables(l):
    half = HEAD_DIM // 2
    quarter = half // 2
    inv_freq = 1.0 / (ROPE_THETA ** (jnp.arange(0, half, 2, dtype=F32) / half))
    rows = (jnp.arange(l) // GRID_W).astype(F32)
    cols = (jnp.arange(l) % GRID_W).astype(F32)

    def cs(pos):
        ang = pos[:, None] * inv_freq[None, :]
        return jnp.cos(ang), jnp.sin(ang)

    cr, sr = cs(rows)
    cc, sc = cs(cols)
    zero = jnp.zeros((l, quarter), F32)
    cos = jnp.concatenate([cr, cr, cc, cc], axis=-1)
    s_lo = jnp.concatenate([-sr, zero, -sc, zero], axis=-1)
    s_hi = jnp.concatenate([zero, sr, zero, sc], axis=-1)
    return cos, s_lo, s_hi


def _attn_bias_plan(rows_n):
    kh = min(WIN_ROWS, rows_n)
    plan = []
    for r0, sb in [(0, 0), (Q_ROWS, 0), (rows_n - Q_ROWS, rows_n - BAND_ROWS)]:
        qrow = r0 + np.arange(Q_ROWS)
        krow = sb + np.arange(BAND_ROWS)
        rs = np.clip(qrow - kh // 2, 0, rows_n - kh)
        ok = (krow[None, :] >= rs[:, None]) & (krow[None, :] < rs[:, None] + kh)
        ro = krow[None, :] - qrow[:, None] + (WIN_ROWS - 1)
        plan.append(np.where(ok, ro, 2 * WIN_ROWS - 1))
    return np.stack(plan)


def _attn_toe(rpb):
    col = np.arange(GRID_W)
    cstart = np.clip(col - WIN_COLS // 2, 0, GRID_W - WIN_COLS)
    ok_c = (col[None, :] >= cstart[:, None]) & (col[None, :] < cstart[:, None] + WIN_COLS)
    co = np.clip(col[None, :] - col[:, None], -(WIN_COLS - 1), WIN_COLS - 1) + (WIN_COLS - 1)
    onehot = (co[None] == np.arange(2 * WIN_COLS - 1)[:, None, None]).astype(np.float32)
    toe = jnp.einsum('hrc,cqk->hrqk', rpb.astype(F32), onehot, precision=lax.Precision.HIGHEST)
    toe = jnp.where(ok_c[None, None], toe, NEG_INF)
    return jnp.concatenate([toe, jnp.full((rpb.shape[0], 1, GRID_W, GRID_W), NEG_INF, F32)], axis=1)


def _attn_kernel(q_ref, k_ref, v_ref, kc_ref, vc_ref, toe_ref, cos_ref, slo_ref, shi_ref, o_ref,
                 qr_scr, qs_scr, kr_scr, bias_scr, *, n_blk, rows_n, plan):
    quarter = HEAD_DIM // 4
    qn = Q_ROWS * GRID_W
    kn = BAND_ROWS * GRID_W
    scale = HEAD_DIM ** -0.5

    def rope(x, sl):
        return (x * cos_ref[sl, :] + pltpu.roll(x, HEAD_DIM - quarter, 1) * slo_ref[sl, :]
                + pltpu.roll(x, quarter, 1) * shi_ref[sl, :])

    def rope_body(i, _):
        sl = pl.ds(pl.multiple_of(i * qn, qn), qn)
        q = q_ref[0, sl, :].astype(F32) * scale
        qs_scr[sl, :] = q.astype(BF16)
        qr_scr[sl, :] = rope(q, sl).astype(BF16)
        kr_scr[sl, :] = rope(k_ref[0, sl, :].astype(F32), sl).astype(BF16)
        return 0

    lax.fori_loop(0, n_blk, rope_body, 0)

    @pl.when(pl.program_id(1) == 0)
    def _():
        for case in range(plan.shape[0]):
            for qi in range(Q_ROWS):
                for kj in range(BAND_ROWS):
                    bias_scr[case, qi * GRID_W:(qi + 1) * GRID_W, kj * GRID_W:(kj + 1) * GRID_W] = (
                        toe_ref[0, int(plan[case, qi, kj])])

    kc = kc_ref[0]
    vc = vc_ref[0]
    nt = (((1,), (1,)), ((), ()))

    def body(i, _):
        sb = jnp.clip(i * Q_ROWS - WIN_ROWS // 2, 0, rows_n - BAND_ROWS)
        ks = pl.ds(pl.multiple_of(sb * GRID_W, GRID_W), kn)
        qs = pl.ds(pl.multiple_of(i * qn, qn), qn)
        case = jnp.where(i == 0, 0, jnp.where(i == n_blk - 1, 2, 1))
        s = lax.dot_general(qr_scr[qs, :], kr_scr[ks, :], nt, preferred_element_type=F32) + bias_scr[case]
        sc = lax.dot_general(qs_scr[qs, :], kc, nt, preferred_element_type=F32)
        m = jnp.maximum(jnp.max(s, axis=-1, keepdims=True), jnp.max(sc, axis=-1, keepdims=True))
        p = jnp.exp(s - m)
        pc = jnp.exp(sc - m)
        den = jnp.sum(p, axis=-1, keepdims=True) + jnp.sum(pc, axis=-1, keepdims=True)
        o = (jnp.dot(p.astype(BF16), v_ref[0, ks, :], preferred_element_type=F32)
             + jnp.dot(pc.astype(BF16), vc, preferred_element_type=F32))
        o_ref[0, qs, :] = (o / den).astype(o_ref.dtype)
        return 0

    lax.fori_loop(0, n_blk, body, 0, unroll=4)


def neighborhood_attention(p_lat, p_ctx, rpb, n_heads):
    b, l, _ = p_lat.shape
    lc = p_ctx.shape[1]
    rows_n = l // GRID_W
    assert rows_n % Q_ROWS == 0 and rows_n >= BAND_ROWS + Q_ROWS
    n_blk = rows_n // Q_ROWS
    plan = _attn_bias_plan(rows_n)
    cos, s_lo, s_hi = _rope_tables(l)
    qn, kn = Q_ROWS * GRID_W, BAND_ROWS * GRID_W
    h = n_heads
    tab = pl.BlockSpec((l, HEAD_DIM), lambda hi, bi: (0, 0))
    return pl.pallas_call(
        functools.partial(_attn_kernel, n_blk=n_blk, rows_n=rows_n, plan=plan),
        grid=(h, b),
        in_specs=[pl.BlockSpec((1, l, HEAD_DIM), lambda hi, bi: (bi, 0, hi)),
                  pl.BlockSpec((1, l, HEAD_DIM), lambda hi, bi: (bi, 0, hi + h)),
                  pl.BlockSpec((1, l, HEAD_DIM), lambda hi, bi: (bi, 0, hi + 2 * h)),
                  pl.BlockSpec((1, lc, HEAD_DIM), lambda hi, bi: (bi, 0, hi)),
                  pl.BlockSpec((1, lc, HEAD_DIM), lambda hi, bi: (bi, 0, hi + h)),
                  pl.BlockSpec((1, 2 * WIN_ROWS, GRID_W, GRID_W), lambda hi, bi: (hi, 0, 0, 0)),
                  tab, tab, tab],
        out_specs=pl.BlockSpec((1, l, HEAD_DIM), lambda hi, bi: (bi, 0, hi)),
        out_shape=jax.ShapeDtypeStruct((b, l, h * HEAD_DIM), BF16),
        scratch_shapes=[pltpu.VMEM((l, HEAD_DIM), BF16)] * 3 + [pltpu.VMEM((plan.shape[0], qn, kn), F32)],
        compiler_params=_cparams(),
        name="nbr_attn",
    )(p_lat, p_lat, p_lat, p_ctx, p_ctx, _attn_toe(rpb), cos, s_lo, s_hi)


SSM_CHUNK = 16
SSM_ROWS = 8
SSM_PACK = 8


def _ssm_mats(a_re, a_im, log_dt, b_re, b_im, c_re, c_im, d_skip):
    t = SSM_CHUNK
    hi = lax.Precision.HIGHEST
    a = lax.complex(a_re.astype(F32), a_im.astype(F32))
    dta = jnp.exp(log_dt.astype(F32))[..., None] * a
    a_bar = jnp.exp(dta)
    b_bar = ((a_bar - 1.0) / a)[..., None] * lax.complex(b_re.astype(F32), b_im.astype(F32))
    cm = lax.complex(c_re.astype(F32), c_im.astype(F32))
    k = jnp.arange(t + 1, dtype=F32)
    ap = jnp.exp(dta[..., None] * k)
    g, p, n = a.shape[1], a.shape[2], b_re.shape[-1]
    kern = jnp.einsum('dgnp,dgpl,dgpm->dglnm', cm, ap[..., :t], b_bar, precision=hi).real
    s_i = np.arange(t)[:, None]
    t_i = np.arange(t)[None, :]
    lag = np.arange(t)[:, None, None]
    oh = np.stack([(t_i - s_i)[None] == lag, (s_i - t_i)[None] == lag]).astype(np.float32)
    eye = jnp.eye(n, dtype=F32)
    m_mat = (jnp.einsum('dlst,dglnm->gsmtn', oh, kern, precision=hi)
             + jnp.einsum('st,gn,nm->gsmtn', np.eye(t, dtype=np.float32), d_skip.astype(F32), eye, precision=hi))
    m_mat = m_mat.reshape(g, t * n, t * n)
    wf = ap[0][:, :, t - 1::-1][..., :t, None] * b_bar[0][:, :, None, :]
    wb = ap[1][:, :, :t, None] * b_bar[1][:, :, None, :]
    to_rows = lambda z: jnp.transpose(z, (0, 2, 3, 1)).reshape(g, t * n, p)
    w_mat = jnp.concatenate([to_rows(wf.real), to_rows(wb.real), to_rows(wf.imag), to_rows(wb.imag)], axis=-1)
    zf = jnp.transpose(cm[0], (0, 2, 1))[:, :, None, :] * ap[0][:, :, 1:t + 1, None]
    zb = jnp.transpose(cm[1], (0, 2, 1))[:, :, None, :] * ap[1][:, :, t:0:-1, None]
    flat = lambda z: z.reshape(g, p, t * n)
    r_mat = jnp.concatenate([flat(zf.real), flat(zb.real), -flat(zf.imag), -flat(zb.imag)], axis=1)
    a_t = ap[..., t]
    a_vec = jnp.stack([jnp.concatenate([a_t[0].real, a_t[1].real], -1),
                       jnp.concatenate([a_t[0].imag, a_t[1].imag], -1)], axis=1)
    a_vec = jnp.pad(a_vec, ((0, 0), (0, SSM_ROWS - 2), (0, 0)))
    return m_mat.astype(BF16), w_mat.astype(BF16), r_mat.astype(BF16), a_vec


def _s5_kernel(u_ref, sel_ref, selt_ref, m_ref, w_ref, r_ref, a_ref, y_ref, v_scr, ent_scr, *, n_ctx, n_chunks):
    u = jnp.dot(u_ref[0], sel_ref[0], preferred_element_type=F32).astype(BF16)
    p2 = a_ref.shape[-1]
    p = p2 // 2
    v_scr[...] = jnp.dot(u, w_ref[0], preferred_element_type=F32)
    a_re = jnp.broadcast_to(a_ref[0, 0:1, :], (SSM_ROWS, p2))
    a_im = jnp.broadcast_to(a_ref[0, 1:2, :], (SSM_ROWS, p2))
    is_fwd = lax.broadcasted_iota(I32, (SSM_ROWS, p2), 1) < p

    def step(j, carry):
        s_re, s_im = carry
        cb = jnp.where(j < n_ctx, n_ctx - 1 - j, n_chunks + n_ctx - 1 - j)
        rf = pl.ds(pl.multiple_of(j * SSM_ROWS, SSM_ROWS), SSM_ROWS)
        rb = pl.ds(pl.multiple_of(cb * SSM_ROWS, SSM_ROWS), SSM_ROWS)
        ent_scr[rf, 0:p] = s_re[:, :p]
        ent_scr[rb, p:p2] = s_re[:, p:]
        ent_scr[rf, p2:p2 + p] = s_im[:, :p]
        ent_scr[rb, p2 + p:] = s_im[:, p:]
        in_re = jnp.where(is_fwd, v_scr[rf, :p2], v_scr[rb, :p2])
        in_im = jnp.where(is_fwd, v_scr[rf, p2:], v_scr[rb, p2:])
        return a_re * s_re - a_im * s_im + in_re, a_re * s_im + a_im * s_re + in_im

    zero = jnp.zeros((SSM_ROWS, p2), F32)
    lax.fori_loop(0, n_chunks, step, (zero, zero))
    y = (jnp.dot(u, m_ref[0], preferred_element_type=F32)
         + jnp.dot(ent_scr[...].astype(BF16), r_ref[0], preferred_element_type=F32)).astype(BF16)
    def part(c):
        return jnp.dot(y, selt_ref[0, :, c:c + PACK_CHUNK], preferred_element_type=F32).astype(y_ref.dtype)

    @pl.when(pl.program_id(1) == 0)
    def _():
        for c in range(0, y_ref.shape[-1], PACK_CHUNK):
            y_ref[0, :, c:c + PACK_CHUNK] = part(c)

    @pl.when(pl.program_id(1) > 0)
    def _():
        for c in range(0, y_ref.shape[-1], PACK_CHUNK):
            y_ref[0, :, c:c + PACK_CHUNK] = y_ref[0, :, c:c + PACK_CHUNK] + part(c)


def s5_mixer(u_lat, u_ctx, mats):
    m_mat, w_mat, r_mat, a_vec = mats
    b, l, dch = u_lat.shape
    lc = u_ctx.shape[1]
    g = m_mat.shape[0]
    n = dch // g
    t = SSM_CHUNK
    tn = t * n
    gp = SSM_PACK
    packs, lanes = g // gp, gp * n
    n_ctx, n_chunks = lc // t, (lc + l) // t
    assert b <= SSM_ROWS and lc % t == 0 and l % t == 0 and g % gp == 0
    z = jnp.concatenate([u_ctx, u_lat], axis=1).reshape(b, n_chunks, t, packs, lanes)
    z = jnp.pad(jnp.transpose(z, (3, 1, 0, 2, 4)), ((0, 0), (0, 0), (0, SSM_ROWS - b), (0, 0), (0, 0)))
    rows = n_chunks * SSM_ROWS
    z = z.reshape(packs, rows, t * lanes)
    ri = jnp.arange(t * lanes)
    ci = jnp.arange(tn)
    same = (ri[:, None] // lanes == ci[None, :] // n) & (ri[:, None] % n == ci[None, :] % n)
    sel = (same[None] & ((ri[None, :, None] % lanes) // n == jnp.arange(gp)[:, None, None])).astype(BF16)
    selt = jnp.transpose(sel, (0, 2, 1))
    grp = lambda k: pl.BlockSpec((1, k, tn), lambda pi, qi: (pi * gp + qi, 0, 0))
    pack_blk = pl.BlockSpec((1, rows, t * lanes), lambda pi, qi: (pi, 0, 0))
    y = pl.pallas_call(
        functools.partial(_s5_kernel, n_ctx=n_ctx, n_chunks=n_chunks),
        grid=(packs, gp),
        in_specs=[pack_blk,
                  pl.BlockSpec((1, t * lanes, tn), lambda pi, qi: (qi, 0, 0)),
                  pl.BlockSpec((1, tn, t * lanes), lambda pi, qi: (qi, 0, 0)),
                  grp(tn), grp(tn), grp(r_mat.shape[1]),
                  pl.BlockSpec((1, SSM_ROWS, a_vec.shape[-1]), lambda pi, qi: (pi * gp + qi, 0, 0))],
        out_specs=pack_blk,
        out_shape=jax.ShapeDtypeStruct((packs, rows, t * lanes), BF16),
        scratch_shapes=[pltpu.VMEM((rows, w_mat.shape[-1]), F32), pltpu.VMEM((rows, r_mat.shape[1]), F32)],
        compiler_params=_cparams(),
        name="s5_mixer",
    )(z, sel, selt, m_mat, w_mat, r_mat, a_vec)
    y = y.reshape(packs, n_chunks, SSM_ROWS, t, lanes)[:, n_ctx:, :b]
    return jnp.transpose(y, (2, 1, 3, 0, 4)).reshape(b, l, dch)


def _glu_kernel(y_ref, w_ref, b_ref, o_ref):
    z = jax.nn.gelu(y_ref[...].astype(F32))
    gate = jax.nn.sigmoid(jnp.dot(z.astype(BF16), w_ref[...], preferred_element_type=F32) + b_ref[...])
    o_ref[...] = (z * gate).astype(o_ref.dtype)


def s5_glu(y, w_bf16, b_glu, tm=512):
    n, d = y.shape
    return pl.pallas_call(
        _glu_kernel,
        grid=(n // tm,),
        in_specs=[pl.BlockSpec((tm, d), lambda i: (i, 0)),
                  pl.BlockSpec((d, d), lambda i: (0, 0)),
                  pl.BlockSpec((1, d), lambda i: (0, 0))],
        out_specs=pl.BlockSpec((tm, d), lambda i: (i, 0)),
        out_shape=jax.ShapeDtypeStruct((n, d), BF16),
        compiler_params=_cparams(),
        name="s5_glu",
    )(y, w_bf16, b_glu.reshape(1, d))


def _merge_kernel(a_ref, s_ref, ga_ref, gs_ref, w_ref, o_ref, h_scr):
    da = a_ref.shape[-1]

    @pl.when(pl.program_id(1) == 0)
    def _():
        h_scr[:, :da] = _rms(a_ref[...].astype(F32), ga_ref[...]).astype(BF16)
        h_scr[:, da:] = _rms(s_ref[...].astype(F32), gs_ref[...]).astype(BF16)

    o_ref[...] = jnp.dot(h_scr[...], w_ref[...], preferred_element_type=F32)


def merge_proj(attn, ssm, g_attn, g_ssm, w_bf16, tm=512, tn=1024):
    n, da = attn.shape
    ds = ssm.shape[1]
    d, dout = w_bf16.shape
    return pl.pallas_call(
        _merge_kernel,
        grid=(n // tm, dout // tn),
        in_specs=[pl.BlockSpec((tm, da), lambda i, j: (i, 0)),
                  pl.BlockSpec((tm, ds), lambda i, j: (i, 0)),
                  pl.BlockSpec((1, da), lambda i, j: (0, 0)),
                  pl.BlockSpec((1, ds), lambda i, j: (0, 0)),
                  pl.BlockSpec((d, tn), lambda i, j: (0, j))],
        out_specs=pl.BlockSpec((tm, tn), lambda i, j: (i, j)),
        out_shape=jax.ShapeDtypeStruct((n, dout), F32),
        scratch_shapes=[pltpu.VMEM((tm, d), BF16)],
        compiler_params=_cparams(),
        name="merge_proj",
    )(attn, ssm, g_attn.reshape(1, da), g_ssm.reshape(1, ds), w_bf16)


def _post_mix_kernel(x_ref, mix_ref, gpost_ref, gt_ref, gpre_ref, sc_ref, sh_ref, wr_ref, br_ref,
                     x1_ref, hp_ref, idx_ref, gate_ref, mask_ref):
    x1 = x_ref[0] + gt_ref[0] * _rms(mix_ref[0], gpost_ref[...])
    x1_ref[0] = x1
    h = _rms(x1, gpre_ref[...]) * (1.0 + sc_ref[0]) + sh_ref[0]
    half = h.shape[-1] // 2
    hp_ref[0] = _pack_halves(h[:, :half], h[:, half:])
    scores = jax.nn.sigmoid(jnp.dot(h.astype(BF16), wr_ref[...], preferred_element_type=F32))
    n_e = scores.shape[-1]
    lane = lax.broadcasted_iota(I32, scores.shape, 1)
    biased = scores + br_ref[...]
    idx_out = jnp.zeros(scores.shape, I32)
    sel_out = jnp.zeros(scores.shape, F32)
    mask = jnp.zeros(scores.shape, jnp.bool_)
    for k in range(TOP_K):
        m = jnp.max(biased, axis=-1, keepdims=True)
        ik = jnp.min(jnp.where(biased == m, lane, n_e), axis=-1, keepdims=True)
        hit = lane == ik
        sel_k = jnp.sum(jnp.where(hit, scores, 0.0), axis=-1, keepdims=True)
        idx_out = jnp.where(lane == k, ik, idx_out)
        sel_out = jnp.where(lane == k, sel_k, sel_out)
        mask = jnp.logical_or(mask, hit)
        biased = jnp.where(hit, -jnp.inf, biased)
    idx_ref[0] = idx_out
    gate_ref[0] = sel_out / jnp.sum(sel_out, axis=-1, keepdims=True) * ROUTED_SCALE
    mask_ref[0] = mask.astype(BF16)


def post_mix(x, mix, g_post, gt, g_pre, scale, shift, wr_bf16, b_router, tm=256):
    b, l, d = x.shape
    n_e = wr_bf16.shape[1]
    row = pl.BlockSpec((1, tm, d), lambda bi, i: (bi, i, 0))
    prow = pl.BlockSpec((1, tm, d // 2), lambda bi, i: (bi, i, 0))
    vec = pl.BlockSpec((1, d), lambda bi, i: (0, 0))
    bvec = pl.BlockSpec((1, 1, d), lambda bi, i: (bi, 0, 0))
    small = pl.BlockSpec((1, tm, n_e), lambda bi, i: (bi, i, 0))
    return pl.pallas_call(
        _post_mix_kernel,
        grid=(b, l // tm),
        in_specs=[row, row, vec, bvec, vec, bvec, bvec,
                  pl.BlockSpec((d, n_e), lambda bi, i: (0, 0)),
                  pl.BlockSpec((1, n_e), lambda bi, i: (0, 0))],
        out_specs=[row, prow, small, small, small],
        out_shape=[jax.ShapeDtypeStruct((b, l, d), F32),
                   jax.ShapeDtypeStruct((b, l, d // 2), I32),
                   jax.ShapeDtypeStruct((b, l, n_e), I32),
                   jax.ShapeDtypeStruct((b, l, n_e), F32),
                   jax.ShapeDtypeStruct((b, l, n_e), BF16)],
        compiler_params=_cparams(),
        name="post_mix",
    )(x, mix, g_post.reshape(1, d), gt, g_pre.reshape(1, d), scale, shift, wr_bf16, b_router.reshape(1, n_e))


def _rank_kernel(mask_ref, idx_ref, rank_ref, cnt_ref, carry):
    @pl.when(pl.program_id(0) == 0)
    def _():
        carry[...] = jnp.zeros_like(carry)

    m = mask_ref[...]
    tm = m.shape[0]
    earlier = (lax.broadcasted_iota(I32, (tm, tm), 1) < lax.broadcasted_iota(I32, (tm, tm), 0)).astype(BF16)
    excl = jnp.dot(earlier, m, preferred_element_type=F32) + carry[0:1, :]
    lane = lax.broadcasted_iota(I32, m.shape, 1)
    idx = idx_ref[...]
    out = jnp.zeros(m.shape, F32)
    for k in range(TOP_K):
        rk = jnp.sum(jnp.where(lane == idx[:, k:k + 1], excl, 0.0), axis=-1, keepdims=True)
        out = jnp.where(lane == k, rk, out)
    rank_ref[...] = out.astype(I32)
    carry[0:1, :] = carry[0:1, :] + jnp.sum(m.astype(F32), axis=0, keepdims=True)
    cnt_ref[...] = carry[...]


def route_ranks(mask, idx, tm=512):
    n, n_e = mask.shape
    blk = pl.BlockSpec((tm, n_e), lambda i: (i, 0))
    rank, cnt = pl.pallas_call(
        _rank_kernel,
        grid=(n // tm,),
        in_specs=[blk, blk],
        out_specs=[blk, pl.BlockSpec((8, n_e), lambda i: (0, 0))],
        out_shape=[jax.ShapeDtypeStruct((n, n_e), I32), jax.ShapeDtypeStruct((8, n_e), F32)],
        scratch_shapes=[pltpu.VMEM((8, n_e), F32)],
        compiler_params=_cparams(),
        name="route_ranks",
    )(mask, idx)
    return rank, cnt[0].astype(I32)


def _dest_kernel(rank_ref, idx_ref, ps_ref, dest_ref):
    idx = idx_ref[...]
    lane = lax.broadcasted_iota(I32, idx.shape, 1)
    start = jnp.zeros(idx.shape, F32)
    for k in range(TOP_K):
        sk = jnp.sum(jnp.where(lane == idx[:, k:k + 1], ps_ref[...], 0.0), axis=-1, keepdims=True)
        start = jnp.where(lane == k, sk, start)
    dest_ref[...] = rank_ref[...] + start.astype(I32)


def route_dest(rank, idx, pad_start, tm=512):
    n, n_e = rank.shape
    blk = pl.BlockSpec((tm, n_e), lambda i: (i, 0))
    dest = pl.pallas_call(
        _dest_kernel,
        grid=(n // tm,),
        in_specs=[blk, blk, pl.BlockSpec((1, n_e), lambda i: (0, 0))],
        out_specs=blk,
        out_shape=jax.ShapeDtypeStruct((n, n_e), I32),
        compiler_params=_cparams(),
        name="route_dest",
    )(rank, idx, pad_start.astype(F32).reshape(1, n_e))
    return dest[:, :TOP_K].reshape(n * TOP_K)


def _dispatch_kernel(pe_ref, pd_ref, dest_ref, h_ref, wgu_ref, wd_ref, xs_ref, sp_ref, z_scr, sem, *, n_e, tm, blk):
    def zero_copy(e):
        start = pl.multiple_of(pe_ref[e] - blk, blk)
        return pltpu.make_async_copy(z_scr, xs_ref.at[pl.ds(start, blk), :], sem)

    @pl.when(pl.program_id(0) == 0)
    def _():
        z_scr[...] = jnp.zeros_like(z_scr)

        def start(e, _):
            @pl.when(pd_ref[e] > 0)
            def _():
                zero_copy(e).start()
            return 0

        def wait(e, _):
            @pl.when(pd_ref[e] > 0)
            def _():
                zero_copy(e).wait()
            return 0

        lax.fori_loop(0, n_e, start, 0)
        lax.fori_loop(0, n_e, wait, 0)

    sub = 8
    half = h_ref.shape[-1]
    n_ch = half // PACK_CHUNK
    rows_ph = tm // (2 * n_ch)

    def issue_rows(r0):
        for g in range(0, rows_ph, sub):
            tile = h_ref.at[pl.ds(r0 + g, sub), :]
            for tt in range(sub):
                for k in range(TOP_K):
                    pltpu.make_async_copy(tile.at[pl.ds(tt, 1), :],
                                          xs_ref.at[pl.ds(dest_ref[(r0 + g + tt) * TOP_K + k], 1), :],
                                          sem).start(priority=k % 2)

    acc = None
    for c in range(n_ch):
        issue_rows(c * rows_ph)
        cl = slice(c * PACK_CHUNK, (c + 1) * PACK_CHUNK)
        ch = slice(half + c * PACK_CHUNK, half + (c + 1) * PACK_CHUNK)
        lo, hi = _unpack_halves(h_ref[:, cl])
        part = (jnp.dot(lo.astype(BF16), wgu_ref[cl, :], preferred_element_type=F32)
                + jnp.dot(hi.astype(BF16), wgu_ref[ch, :], preferred_element_type=F32))
        acc = part if acc is None else acc + part
    de = acc.shape[-1] // 2
    hid = (jax.nn.silu(acc[:, :de]) * acc[:, de:]).astype(BF16)
    for c in range(n_ch):
        issue_rows((n_ch + c) * rows_ph)
        cl = slice(c * PACK_CHUNK, (c + 1) * PACK_CHUNK)
        ch = slice(half + c * PACK_CHUNK, half + (c + 1) * PACK_CHUNK)
        sp_ref[:, cl] = _pack_halves(jnp.dot(hid, wd_ref[:, cl], preferred_element_type=F32),
                                     jnp.dot(hid, wd_ref[:, ch], preferred_element_type=F32))
    rows = tm * TOP_K
    pltpu.make_async_copy(xs_ref.at[pl.ds(0, rows), :], xs_ref.at[pl.ds(0, rows), :], sem).wait()


def moe_dispatch(hp, dest_flat, pad_end, padded, slots, wsgu, wsd, tm=DISPATCH_TM):
    n, dw = hp.shape
    n_e = pad_end.shape[0]
    d, de2 = wsgu.shape
    row = pl.BlockSpec((tm, dw), lambda i, pe, pd: (i, 0))
    grid_spec = pltpu.PrefetchScalarGridSpec(
        num_scalar_prefetch=2,
        grid=(n // tm,),
        in_specs=[pl.BlockSpec((tm * TOP_K,), lambda i, pe, pd: (i,), memory_space=pltpu.SMEM),
                  row,
                  pl.BlockSpec((d, de2), lambda i, pe, pd: (0, 0)),
                  pl.BlockSpec((de2 // 2, d), lambda i, pe, pd: (0, 0))],
        out_specs=[pl.BlockSpec(memory_space=pl.ANY), row],
        scratch_shapes=[pltpu.VMEM((EXPERT_BLK, dw), I32), pltpu.SemaphoreType.DMA(())],
    )
    return pl.pallas_call(
        functools.partial(_dispatch_kernel, n_e=n_e, tm=tm, blk=EXPERT_BLK),
        grid_spec=grid_spec,
        out_shape=[jax.ShapeDtypeStruct((slots, dw), I32), jax.ShapeDtypeStruct((n, dw), I32)],
        compiler_params=_cparams(),
        name="moe_dispatch",
    )(pad_end, padded, dest_flat, hp, wsgu, wsd)


PACK_CHUNK = 512


def _swiglu_hidden(xp_ref, wgu):
    half = xp_ref.shape[-1]
    acc = None
    for c in range(half // PACK_CHUNK):
        cl = slice(c * PACK_CHUNK, (c + 1) * PACK_CHUNK)
        ch = slice(half + c * PACK_CHUNK, half + (c + 1) * PACK_CHUNK)
        lo, hi = _unpack_halves(xp_ref[:, cl])
        part = (jnp.dot(lo.astype(BF16), wgu(cl), preferred_element_type=F32)
                + jnp.dot(hi.astype(BF16), wgu(ch), preferred_element_type=F32))
        acc = part if acc is None else acc + part
    de = acc.shape[-1] // 2
    return (jax.nn.silu(acc[:, :de]) * acc[:, de:]).astype(BF16)


def _down_packed(hid, wd, o_ref):
    half = o_ref.shape[-1]
    for c in range(half // PACK_CHUNK):
        cl = slice(c * PACK_CHUNK, (c + 1) * PACK_CHUNK)
        ch = slice(half + c * PACK_CHUNK, half + (c + 1) * PACK_CHUNK)
        o_ref[:, cl] = _pack_halves(jnp.dot(hid, wd(cl), preferred_element_type=F32),
                                    jnp.dot(hid, wd(ch), preferred_element_type=F32))


def _expert_weights(sched, i, w_hbm, w_buf, sem):
    be_ref, first_ref, slot_ref, nxt_ref = sched
    s = slot_ref[i]

    def copies(e, sl):
        return [pltpu.make_async_copy(h.at[e], v.at[sl], sem.at[j, sl]) for j, (h, v) in enumerate(zip(w_hbm, w_buf))]

    @pl.when(first_ref[i] == 1)
    def _():
        @pl.when(i == 0)
        def _():
            for c in copies(be_ref[0], 0):
                c.start()

        for c in copies(be_ref[i], s):
            c.wait()

        @pl.when(nxt_ref[i] >= 0)
        def _():
            for c in copies(nxt_ref[i], 1 - s):
                c.start()

    return s


def _expert_up_kernel(be_ref, nu_ref, first_ref, slot_ref, nxt_ref, x_ref, wg_hbm, wu_hbm, hid_ref,
                      wg_buf, wu_buf, sem):
    i = pl.program_id(0)
    used = i < nu_ref[0]
    s = _expert_weights((be_ref, first_ref, slot_ref, nxt_ref), i, (wg_hbm, wu_hbm), (wg_buf, wu_buf), sem)

    def wgu(sl):
        return jnp.concatenate([wg_buf[s, sl, :].astype(BF16), wu_buf[s, sl, :].astype(BF16)], axis=-1)

    @pl.when(used)
    def _():
        hid_ref[...] = _swiglu_hidden(x_ref, wgu)

    @pl.when(jnp.logical_not(used))
    def _():
        hid_ref[...] = jnp.zeros_like(hid_ref)


def _expert_down_kernel(be_ref, nu_ref, first_ref, slot_ref, nxt_ref, hid_ref, wd_hbm, o_ref, wd_buf, sem):
    i = pl.program_id(0)
    used = i < nu_ref[0]
    s = _expert_weights((be_ref, first_ref, slot_ref, nxt_ref), i, (wd_hbm,), (wd_buf,), sem)

    @pl.when(used)
    def _():
        _down_packed(hid_ref[...], lambda sl: wd_buf[s, :, sl].astype(BF16), o_ref)

    @pl.when(jnp.logical_not(used))
    def _():
        o_ref[...] = jnp.zeros_like(o_ref)


def _expert_schedule(block_e, n_used):
    nb = block_e.shape[0]
    ar = jnp.arange(nb, dtype=I32)
    used = ar < n_used[0]
    first = used & ((ar == 0) | (block_e != jnp.roll(block_e, 1)))
    slot = jnp.where(used, (jnp.cumsum(first.astype(I32)) - 1) % 2, 0).astype(I32)
    nf = jnp.where(first, ar, nb)
    nxt_idx = jnp.concatenate([lax.cummin(nf, axis=0, reverse=True)[1:], jnp.full((1,), nb, I32)])
    nxt = jnp.where(nxt_idx < nb, block_e[jnp.minimum(nxt_idx, nb - 1)], -1).astype(I32)
    return first.astype(I32), slot, nxt


def expert_ffn(xs, block_e, n_used, wg, wu, wd, blk):
    slots, dw = xs.shape
    d, de = wg.shape[1], wg.shape[2]
    n_blocks = slots // blk
    sched = (block_e, n_used) + _expert_schedule(block_e, n_used)
    cur = lambda i, nu: jnp.minimum(i, nu[0] - 1)
    hbm = pl.BlockSpec(memory_space=pl.ANY)
    up_spec = pltpu.PrefetchScalarGridSpec(
        num_scalar_prefetch=5,
        grid=(n_blocks,),
        in_specs=[pl.BlockSpec((blk, dw), lambda i, be, nu, *_: (cur(i, nu), 0)), hbm, hbm],
        out_specs=pl.BlockSpec((blk, de), lambda i, *_: (i, 0)),
        scratch_shapes=[pltpu.VMEM((2, d, de), F32), pltpu.VMEM((2, d, de), F32), pltpu.SemaphoreType.DMA((2, 2))],
    )
    hid = pl.pallas_call(
        _expert_up_kernel,
        grid_spec=up_spec,
        out_shape=jax.ShapeDtypeStruct((slots, de), BF16),
        compiler_params=_cparams(),
        name="expert_up",
    )(*sched, xs, wg, wu)
    down_spec = pltpu.PrefetchScalarGridSpec(
        num_scalar_prefetch=5,
        grid=(n_blocks,),
        in_specs=[pl.BlockSpec((blk, de), lambda i, be, nu, *_: (cur(i, nu), 0)), hbm],
        out_specs=pl.BlockSpec((blk, dw), lambda i, *_: (i, 0)),
        scratch_shapes=[pltpu.VMEM((2, de, d), F32), pltpu.SemaphoreType.DMA((1, 2))],
    )
    return pl.pallas_call(
        _expert_down_kernel,
        grid_spec=down_spec,
        out_shape=jax.ShapeDtypeStruct((slots, dw), I32),
        compiler_params=_cparams(),
        name="expert_down",
    )(*sched, hid, wd)


def _final_kernel(dc_ref, dn_ref, x1_ref, sp_ref, gate_ref, ys_ref, gpost_ref, gt_ref, o_ref, buf, sem, *, tm, n_tiles):
    i = pl.program_id(0)
    slot = i % 2
    sub = 8

    def aligned(v):
        return v if isinstance(v, int) else pl.multiple_of(v, sub)

    def issue_rows(dref, s, j):
        base = aligned(j * sub)
        for tt in range(sub):
            for k in range(TOP_K):
                tile = buf.at[s, pl.ds(aligned(k * tm + base), sub), :]
                pltpu.make_async_copy(ys_ref.at[pl.ds(dref[(base + tt) * TOP_K + k], 1), :],
                                      tile.at[pl.ds(tt, 1), :],
                                      sem.at[s]).start(priority=k % 2)

    def reduce_rows(s, j):
        rs = pl.ds(aligned(j * sub), sub)
        g = gate_ref[rs, :]
        acc_lo, acc_hi = _unpack_halves(sp_ref[rs, :])
        for k in range(TOP_K):
            lo, hi = _unpack_halves(buf[s, pl.ds(aligned(k * tm + j * sub), sub), :])
            gk = g[:, k:k + 1]
            acc_lo = acc_lo + gk * lo
            acc_hi = acc_hi + gk * hi
        ffn = jnp.concatenate([acc_lo, acc_hi], axis=-1)
        o_ref[rs, :] = x1_ref[rs, :] + gt_ref[0] * _rms(ffn, gpost_ref[...])

    @pl.when(i == 0)
    def _():
        def first(j, c):
            issue_rows(dc_ref, 0, j)
            return c
        lax.fori_loop(0, tm // sub, first, 0)

    def wait_slot(s):
        pltpu.make_async_copy(ys_ref.at[pl.ds(0, TOP_K * tm), :], buf.at[s], sem.at[s]).wait()

    wait_slot(slot)

    for s in range(2):
        @pl.when(slot == s)
        def _():
            for j in range(tm // sub):
                issue_rows(dn_ref, 1 - s, j)
                reduce_rows(s, j)

    @pl.when(i == n_tiles - 1)
    def _():
        wait_slot(1 - slot)


def final_mix(x1, sp, gates, ys, dest_flat, g_post, gt, tiles_per_batch, tm=COMBINE_TM):
    n, d = x1.shape
    dw = sp.shape[1]
    n_e = gates.shape[1]
    n_tiles = n // tm
    return pl.pallas_call(
        functools.partial(_final_kernel, tm=tm, n_tiles=n_tiles),
        grid=(n_tiles,),
        in_specs=[pl.BlockSpec((tm * TOP_K,), lambda i: (i,), memory_space=pltpu.SMEM),
                  pl.BlockSpec((tm * TOP_K,), lambda i: (jnp.minimum(i + 1, n_tiles - 1),), memory_space=pltpu.SMEM),
                  pl.BlockSpec((tm, d), lambda i: (i, 0)),
                  pl.BlockSpec((tm, dw), lambda i: (i, 0)),
                  pl.BlockSpec((tm, n_e), lambda i: (i, 0)),
                  pl.BlockSpec(memory_space=pl.ANY),
                  pl.BlockSpec((1, d), lambda i: (0, 0)),
                  pl.BlockSpec((1, 1, d), lambda i: (i // tiles_per_batch, 0, 0))],
        out_specs=pl.BlockSpec((tm, d), lambda i: (i, 0)),
        out_shape=jax.ShapeDtypeStruct((n, d), F32),
        scratch_shapes=[pltpu.VMEM((2, TOP_K * tm, dw), I32), pltpu.SemaphoreType.DMA((2,))],
        compiler_params=_cparams(),
        name="final_mix",
    )(dest_flat, dest_flat, x1, sp, gates, ys, g_post.reshape(1, d), gt)


def kernel(x, c, ctx, c_ctx, w_ada, b_ada, g_pre_mix, g_post_mix, g_pre_ffn, g_post_ffn, w_in, rpb, ssm_a_re, ssm_a_im, ssm_log_dt, ssm_b_re, ssm_b_im, ssm_c_re, ssm_c_im, ssm_d, w_glu, b_glu, g_attn_out, g_ssm_out, w_out, w_router, b_router, w_exp_gate, w_exp_up, w_exp_down, w_sh_gate, w_sh_up, w_sh_down):
    b, l, d = x.shape
    lc = ctx.shape[1]
    assert w_ada.shape[0] == 1 and b + 1 <= 8
    n_in = w_in.shape[-1]
    d_ssm = w_glu.shape[-1]
    d_attn = d - d_ssm
    n_heads = d_attn // HEAD_DIM
    n = b * l

    c8 = jnp.concatenate([c, c_ctx[None], jnp.zeros((8 - b - 1, d), F32)], axis=0)
    mod = ada_mod(c8, w_ada[0], b_ada[0]).reshape(8, 6, 1, d)
    sh_m, sc_m, gt_m, sh_f, sc_f, gt_f = [mod[:b, j] for j in range(6)]
    csh_m, csc_m = mod[b:b + 1, 0], mod[b:b + 1, 1]

    w_in_b = w_in[0].astype(BF16)
    tn = PROJ_TN
    p_lat = mod_proj(x, g_pre_mix[0], sc_m, sh_m, w_in_b, 0, n_in, 512, tn)
    p_ctx = mod_proj(ctx.reshape(1, b * lc, d), g_pre_mix[0], csc_m, csh_m, w_in_b,
                     d_attn // tn, n_in - d_attn, 512, tn).reshape(b, lc, n_in - d_attn)

    attn = neighborhood_attention(p_lat, p_ctx, rpb[0], n_heads)

    mats = _ssm_mats(ssm_a_re[0], ssm_a_im[0], ssm_log_dt[0], ssm_b_re[0], ssm_b_im[0],
                     ssm_c_re[0], ssm_c_im[0], ssm_d[0])
    y = s5_mixer(p_lat[..., 3 * d_attn:], p_ctx[..., 2 * d_attn:], mats)
    ssm = s5_glu(y.reshape(n, d_ssm), w_glu[0].astype(BF16), b_glu[0])

    mix = merge_proj(attn.reshape(n, d_attn), ssm, g_attn_out[0], g_ssm_out[0], w_out[0].astype(BF16))
    x1, hp, idx, gates, mask = post_mix(x, mix.reshape(b, l, d), g_post_mix[0], gt_m, g_pre_ffn[0], sc_f, sh_f,
                                        w_router[0].astype(BF16), b_router[0])

    n_e = w_router.shape[-1]
    m = n * TOP_K
    idx = idx.reshape(n, n_e)
    rank, counts = route_ranks(mask.reshape(n, n_e), idx)
    padded = (counts + EXPERT_BLK - 1) // EXPERT_BLK * EXPERT_BLK
    pad_end = jnp.cumsum(padded).astype(I32)
    dest = route_dest(rank, idx, pad_end - padded)
    n_blocks = m // EXPERT_BLK + n_e
    slots = n_blocks * EXPERT_BLK
    block_e = jnp.minimum(jnp.searchsorted(pad_end, jnp.arange(n_blocks) * EXPERT_BLK, side='right'),
                          n_e - 1).astype(I32)
    n_used = (pad_end[-1] // EXPERT_BLK).astype(I32).reshape(1)

    hp2 = hp.reshape(n, d // 2)
    xs, sp = moe_dispatch(hp2, dest, pad_end, padded.astype(I32), slots,
                          jnp.concatenate([w_sh_gate[0], w_sh_up[0]], axis=-1).astype(BF16), w_sh_down[0].astype(BF16))
    ys = expert_ffn(xs, block_e, n_used, w_exp_gate[0], w_exp_up[0], w_exp_down[0], EXPERT_BLK)
    out = final_mix(x1.reshape(n, d), sp, gates.reshape(n, n_e), ys, dest, g_post_ffn[0], gt_f, l // COMBINE_TM)
    return out.reshape(b, l, d)
```

```python
import functools

import numpy as np
import jax
import jax.numpy as jnp
from jax import lax
from jax.experimental import pallas as pl
from jax.experimental.pallas import tpu as pltpu

F32 = jnp.float32
BF16 = jnp.bfloat16
I32 = jnp.int32

GRID_W = 64
HEAD_DIM = 128
WIN_ROWS = 8
WIN_COLS = 16
ROPE_THETA = 10000.0
SSM_GROUP_CH = 16
SSM_STATE = 64
TOP_K = 8
ROUTED_SCALE = 2.5
EPS = 1e-6
NEG_INF = -1e30

Q_ROWS = 4
BAND_ROWS = Q_ROWS + WIN_ROWS - 1
VMEM_LIMIT = 56 * 1024 * 1024
EXPERT_BLK = 512
PROJ_TN = 1024
DISPATCH_TM = 512
COMBINE_TM = 128


def _cparams():
    return pltpu.CompilerParams(vmem_limit_bytes=VMEM_LIMIT)


def _rms(x, g):
    return x * lax.rsqrt(jnp.mean(x * x, axis=-1, keepdims=True) + EPS) * g


def _pack_halves(lo, hi):
    lo_bits = lax.bitcast_convert_type(lo.astype(BF16).astype(F32), I32)
    hi_bits = lax.bitcast_convert_type(hi.astype(BF16).astype(F32), I32)
    return (hi_bits & jnp.int32(-65536)) | lax.shift_right_logical(lo_bits, jnp.int32(16))


def _unpack_halves(w):
    lo = lax.bitcast_convert_type(lax.shift_left(w, jnp.int32(16)), F32)
    hi = lax.bitcast_convert_type(w & jnp.int32(-65536), F32)
    return lo, hi


def _ada_kernel(c_ref, w_ref, b_ref, o_ref):
    a = jax.nn.silu(c_ref[...]).astype(BF16)
    o_ref[...] = jnp.dot(a, w_ref[...].astype(BF16), preferred_element_type=F32) + b_ref[...]


def ada_mod(c8, w_ada, b_ada):
    d, n = w_ada.shape
    tn = 512
    return pl.pallas_call(
        _ada_kernel,
        grid=(n // tn,),
        in_specs=[pl.BlockSpec((8, d), lambda j: (0, 0)),
                  pl.BlockSpec((d, tn), lambda j: (0, j)),
                  pl.BlockSpec((1, tn), lambda j: (0, j))],
        out_specs=pl.BlockSpec((8, tn), lambda j: (0, j)),
        out_shape=jax.ShapeDtypeStruct((8, n), F32),
        compiler_params=_cparams(),
        name="ada_mod",
    )(c8, w_ada, b_ada.reshape(1, n))


def _modproj_kernel(x_ref, g_ref, sc_ref, sh_ref, w_ref, o_ref, h_scr):
    @pl.when(pl.program_id(2) == 0)
    def _():
        h = _rms(x_ref[0], g_ref[...]) * (1.0 + sc_ref[0]) + sh_ref[0]
        h_scr[...] = h.astype(BF16)

    o_ref[0] = jnp.dot(h_scr[...], w_ref[...], preferred_element_type=F32).astype(o_ref.dtype)


def mod_proj(x, g, scale, shift, w_bf16, col_blk_off, n_out, tm, tn):
    b, l, d = x.shape
    return pl.pallas_call(
        _modproj_kernel,
        grid=(b, l // tm, n_out // tn),
        in_specs=[pl.BlockSpec((1, tm, d), lambda bi, i, j: (bi, i, 0)),
                  pl.BlockSpec((1, d), lambda bi, i, j: (0, 0)),
                  pl.BlockSpec((1, 1, d), lambda bi, i, j: (bi, 0, 0)),
                  pl.BlockSpec((1, 1, d), lambda bi, i, j: (bi, 0, 0)),
                  pl.BlockSpec((d, tn), lambda bi, i, j: (0, j + col_blk_off))],
        out_specs=pl.BlockSpec((1, tm, tn), lambda bi, i, j: (bi, i, j)),
        out_shape=jax.ShapeDtypeStruct((b, l, n_out), BF16),
        scratch_shapes=[pltpu.VMEM((tm, d), BF16)],
        compiler_params=_cparams(),
        name="mod_proj",
    )(x, g.reshape(1, d), scale, shift, w_bf16)


def _rope_tables(l):
    half = HEAD_DIM // 2
    quarter = half // 2
    inv_freq = 1.0 / (ROPE_THETA ** (jnp.arange(0, half, 2, dtype=F32) / half))
    rows = (jnp.arange(l) // GRID_W).astype(F32)
    cols = (jnp.arange(l) % GRID_W).astype(F32)

    def cs(pos):
        ang = pos[:, None] * inv_freq[None, :]
        return jnp.cos(ang), jnp.sin(ang)

    cr, sr = cs(rows)
    cc, sc = cs(cols)
    zero = jnp.zeros((l, quarter), F32)
    cos = jnp.concatenate([cr, cr, cc, cc], axis=-1)
    s_lo = jnp.concatenate([-sr, zero, -sc, zero], axis=-1)
    s_hi = jnp.concatenate([zero, sr, zero, sc], axis=-1)
    return cos, s_lo, s_hi


def _attn_bias_plan(rows_n):
    kh = min(WIN_ROWS, rows_n)
    plan = []
    for r0, sb in [(0, 0), (Q_ROWS, 0), (rows_n - Q_ROWS, rows_n - BAND_ROWS)]:
        qrow = r0 + np.arange(Q_ROWS)
        krow = sb + np.arange(BAND_ROWS)
        rs = np.clip(qrow - kh // 2, 0, rows_n - kh)
        ok = (krow[None, :] >= rs[:, None]) & (krow[None, :] < rs[:, None] + kh)
        ro = krow[None, :] - qrow[:, None] + (WIN_ROWS - 1)
        plan.append(np.where(ok, ro, 2 * WIN_ROWS - 1))
    return np.stack(plan)


def _attn_toe(rpb):
    col = np.arange(GRID_W)
    cstart = np.clip(col - WIN_COLS // 2, 0, GRID_W - WIN_COLS)
    ok_c = (col[None, :] >= cstart[:, None]) & (col[None, :] < cstart[:, None] + WIN_COLS)
    co = np.clip(col[None, :] - col[:, None], -(WIN_COLS - 1), WIN_COLS - 1) + (WIN_COLS - 1)
    onehot = (co[None] == np.arange(2 * WIN_COLS - 1)[:, None, None]).astype(np.float32)
    toe = jnp.einsum('hrc,cqk->hrqk', rpb.astype(F32), onehot, precision=lax.Precision.HIGHEST)
    toe = jnp.where(ok_c[None, None], toe, NEG_INF)
    return jnp.concatenate([toe, jnp.full((rpb.shape[0], 1, GRID_W, GRID_W), NEG_INF, F32)], axis=1)


def _attn_kernel(q_ref, k_ref, v_ref, kc_ref, vc_ref, toe_ref, cos_ref, slo_ref, shi_ref, o_ref,
                 qr_scr, qs_scr, kr_scr, bias_scr, *, n_blk, rows_n, plan):
    quarter = HEAD_DIM // 4
    qn = Q_ROWS * GRID_W
    kn = BAND_ROWS * GRID_W
    scale = HEAD_DIM ** -0.5

    def rope(x, sl):
        return (x.astype(F32) * cos_ref[sl, :] + pltpu.roll(x, HEAD_DIM - quarter, 1).astype(F32) * slo_ref[sl, :]
                + pltpu.roll(x, quarter, 1).astype(F32) * shi_ref[sl, :])

    def rope_body(i, _):
        sl = pl.ds(pl.multiple_of(i * qn, qn), qn)
        q = q_ref[0, sl, :]
        qs_scr[sl, :] = (q.astype(F32) * scale).astype(BF16)
        qr_scr[sl, :] = (rope(q, sl) * scale).astype(BF16)
        kr_scr[sl, :] = rope(k_ref[0, sl, :], sl).astype(BF16)
        return 0

    lax.fori_loop(0, n_blk, rope_body, 0)

    @pl.when(pl.program_id(1) == 0)
    def _():
        for case in range(plan.shape[0]):
            for qi in range(Q_ROWS):
                for kj in range(BAND_ROWS):
                    bias_scr[case, qi * GRID_W:(qi + 1) * GRID_W, kj * GRID_W:(kj + 1) * GRID_W] = (
                        toe_ref[0, int(plan[case, qi, kj])])

    kc = kc_ref[0]
    vc = vc_ref[0]
    nt = (((1,), (1,)), ((), ()))

    def body(i, _):
        sb = jnp.clip(i * Q_ROWS - WIN_ROWS // 2, 0, rows_n - BAND_ROWS)
        ks = pl.ds(pl.multiple_of(sb * GRID_W, GRID_W), kn)
        qs = pl.ds(pl.multiple_of(i * qn, qn), qn)
        case = jnp.where(i == 0, 0, jnp.where(i == n_blk - 1, 2, 1))
        s = lax.dot_general(qr_scr[qs, :], kr_scr[ks, :], nt, preferred_element_type=F32) + bias_scr[case]
        sc = lax.dot_general(qs_scr[qs, :], kc, nt, preferred_element_type=F32)
        m = jnp.maximum(jnp.max(s, axis=-1, keepdims=True), jnp.max(sc, axis=-1, keepdims=True))
        p = jnp.exp(s - m)
        pc = jnp.exp(sc - m)
        den = jnp.sum(p, axis=-1, keepdims=True) + jnp.sum(pc, axis=-1, keepdims=True)
        o = (jnp.dot(p.astype(BF16), v_ref[0, ks, :], preferred_element_type=F32)
             + jnp.dot(pc.astype(BF16), vc, preferred_element_type=F32))
        o_ref[0, qs, :] = (o / den).astype(o_ref.dtype)
        return 0

    lax.fori_loop(0, n_blk, body, 0, unroll=4)


def neighborhood_attention(p_lat, p_ctx, rpb, n_heads):
    b, l, _ = p_lat.shape
    lc = p_ctx.shape[1]
    rows_n = l // GRID_W
    assert rows_n % Q_ROWS == 0 and rows_n >= BAND_ROWS + Q_ROWS
    n_blk = rows_n // Q_ROWS
    plan = _attn_bias_plan(rows_n)
    cos, s_lo, s_hi = _rope_tables(l)
    qn, kn = Q_ROWS * GRID_W, BAND_ROWS * GRID_W
    h = n_heads
    tab = pl.BlockSpec((l, HEAD_DIM), lambda hi, bi: (0, 0))
    return pl.pallas_call(
        functools.partial(_attn_kernel, n_blk=n_blk, rows_n=rows_n, plan=plan),
        grid=(h, b),
        in_specs=[pl.BlockSpec((1, l, HEAD_DIM), lambda hi, bi: (bi, 0, hi)),
                  pl.BlockSpec((1, l, HEAD_DIM), lambda hi, bi: (bi, 0, hi + h)),
                  pl.BlockSpec((1, l, HEAD_DIM), lambda hi, bi: (bi, 0, hi + 2 * h)),
                  pl.BlockSpec((1, lc, HEAD_DIM), lambda hi, bi: (bi, 0, hi)),
                  pl.BlockSpec((1, lc, HEAD_DIM), lambda hi, bi: (bi, 0, hi + h)),
                  pl.BlockSpec((1, 2 * WIN_ROWS, GRID_W, GRID_W), lambda hi, bi: (hi, 0, 0, 0)),
                  tab, tab, tab],
        out_specs=pl.BlockSpec((1, l, HEAD_DIM), lambda hi, bi: (bi, 0, hi)),
        out_shape=jax.ShapeDtypeStruct((b, l, h * HEAD_DIM), BF16),
        scratch_shapes=[pltpu.VMEM((l, HEAD_DIM), BF16)] * 3 + [pltpu.VMEM((plan.shape[0], qn, kn), F32)],
        compiler_params=_cparams(),
        name="nbr_attn",
    )(p_lat, p_lat, p_lat, p_ctx, p_ctx, _attn_toe(rpb), cos, s_lo, s_hi)


SSM_CHUNK = 16
SSM_ROWS = 8
SSM_PACK = 8


def _ssm_mats(a_re, a_im, log_dt, b_re, b_im, c_re, c_im, d_skip):
    t = SSM_CHUNK
    hi = lax.Precision.HIGHEST
    a = lax.complex(a_re.astype(F32), a_im.astype(F32))
    dta = jnp.exp(log_dt.astype(F32))[..., None] * a
    a_bar = jnp.exp(dta)
    b_bar = ((a_bar - 1.0) / a)[..., None] * lax.complex(b_re.astype(F32), b_im.astype(F32))
    cm = lax.complex(c_re.astype(F32), c_im.astype(F32))
    k = jnp.arange(t + 1, dtype=F32)
    ap = jnp.exp(dta[..., None] * k)
    g, p, n = a.shape[1], a.shape[2], b_re.shape[-1]
    kern = jnp.einsum('dgnp,dgpl,dgpm->dglnm', cm, ap[..., :t], b_bar, precision=hi).real
    s_i = np.arange(t)[:, None]
    t_i = np.arange(t)[None, :]
    lag = np.arange(t)[:, None, None]
    oh = np.stack([(t_i - s_i)[None] == lag, (s_i - t_i)[None] == lag]).astype(np.float32)
    eye = jnp.eye(n, dtype=F32)
    m_mat = (jnp.einsum('dlst,dglnm->gsmtn', oh, kern, precision=hi)
             + jnp.einsum('st,gn,nm->gsmtn', np.eye(t, dtype=np.float32), d_skip.astype(F32), eye, precision=hi))
    m_mat = m_mat.reshape(g, t * n, t * n)
    wf = ap[0][:, :, t - 1::-1][..., :t, None] * b_bar[0][:, :, None, :]
    wb = ap[1][:, :, :t, None] * b_bar[1][:, :, None, :]
    to_rows = lambda z: jnp.transpose(z, (0, 2, 3, 1)).reshape(g, t * n, p)
    w_mat = jnp.concatenate([to_rows(wf.real), to_rows(wb.real), to_rows(wf.imag), to_rows(wb.imag)], axis=-1)
    zf = jnp.transpose(cm[0], (0, 2, 1))[:, :, None, :] * ap[0][:, :, 1:t + 1, None]
    zb = jnp.transpose(cm[1], (0, 2, 1))[:, :, None, :] * ap[1][:, :, t:0:-1, None]
    flat = lambda z: z.reshape(g, p, t * n)
    r_mat = jnp.concatenate([flat(zf.real), flat(zb.real), -flat(zf.imag), -flat(zb.imag)], axis=1)
    a_t = ap[..., t]
    a_vec = jnp.stack([jnp.concatenate([a_t[0].real, a_t[1].real], -1),
                       jnp.concatenate([a_t[0].imag, a_t[1].imag], -1)], axis=1)
    a_vec = jnp.pad(a_vec, ((0, 0), (0, SSM_ROWS - 2), (0, 0)))
    return m_mat.astype(BF16), w_mat.astype(BF16), r_mat.astype(BF16), a_vec


def _s5_kernel(u_ref, sel_ref, selt_ref, m_ref, w_ref, r_ref, a_ref, y_ref, v_scr, ent_scr, *, n_ctx, n_chunks):
    u = jnp.dot(u_ref[0], sel_ref[0], preferred_element_type=F32).astype(BF16)
    p2 = a_ref.shape[-1]
    p = p2 // 2
    v_scr[...] = jnp.dot(u, w_ref[0], preferred_element_type=F32)
    a_re = jnp.broadcast_to(a_ref[0, 0:1, :], (SSM_ROWS, p2))
    a_im = jnp.broadcast_to(a_ref[0, 1:2, :], (SSM_ROWS, p2))
    is_fwd = lax.broadcasted_iota(I32, (SSM_ROWS, p2), 1) < p

    def step(j, carry):
        s_re, s_im = carry
        cb = jnp.where(j < n_ctx, n_ctx - 1 - j, n_chunks + n_ctx - 1 - j)
        rf = pl.ds(pl.multiple_of(j * SSM_ROWS, SSM_ROWS), SSM_ROWS)
        rb = pl.ds(pl.multiple_of(cb * SSM_ROWS, SSM_ROWS), SSM_ROWS)
        ent_scr[rf, 0:p] = s_re[:, :p]
        ent_scr[rb, p:p2] = s_re[:, p:]
        ent_scr[rf, p2:p2 + p] = s_im[:, :p]
        ent_scr[rb, p2 + p:] = s_im[:, p:]
        in_re = jnp.where(is_fwd, v_scr[rf, :p2], v_scr[rb, :p2])
        in_im = jnp.where(is_fwd, v_scr[rf, p2:], v_scr[rb, p2:])
        return a_re * s_re - a_im * s_im + in_re, a_re * s_im + a_im * s_re + in_im

    zero = jnp.zeros((SSM_ROWS, p2), F32)
    lax.fori_loop(0, n_chunks, step, (zero, zero))
    y = (jnp.dot(u, m_ref[0], preferred_element_type=F32)
         + jnp.dot(ent_scr[...].astype(BF16), r_ref[0], preferred_element_type=F32)).astype(BF16)
    def part(c):
        return jnp.dot(y, selt_ref[0, :, c:c + PACK_CHUNK], preferred_element_type=F32).astype(y_ref.dtype)

    @pl.when(pl.program_id(1) == 0)
    def _():
        for c in range(0, y_ref.shape[-1], PACK_CHUNK):
            y_ref[0, :, c:c + PACK_CHUNK] = part(c)

    @pl.when(pl.program_id(1) > 0)
    def _():
        for c in range(0, y_ref.shape[-1], PACK_CHUNK):
            y_ref[0, :, c:c + PACK_CHUNK] = y_ref[0, :, c:c + PACK_CHUNK] + part(c)


def s5_mixer(u_lat, u_ctx, mats):
    m_mat, w_mat, r_mat, a_vec = mats
    b, l, dch = u_lat.shape
    lc = u_ctx.shape[1]
    g = m_mat.shape[0]
    n = dch // g
    t = SSM_CHUNK
    tn = t * n
    gp = SSM_PACK
    packs, lanes = g // gp, gp * n
    n_ctx, n_chunks = lc // t, (lc + l) // t
    assert b <= SSM_ROWS and lc % t == 0 and l % t == 0 and g % gp == 0
    z = jnp.concatenate([u_ctx, u_lat], axis=1).reshape(b, n_chunks, t, packs, lanes)
    z = jnp.pad(jnp.transpose(z, (3, 1, 0, 2, 4)), ((0, 0), (0, 0), (0, SSM_ROWS - b), (0, 0), (0, 0)))
    rows = n_chunks * SSM_ROWS
    z = z.reshape(packs, rows, t * lanes)
    ri = jnp.arange(t * lanes)
    ci = jnp.arange(tn)
    same = (ri[:, None] // lanes == ci[None, :] // n) & (ri[:, None] % n == ci[None, :] % n)
    sel = (same[None] & ((ri[None, :, None] % lanes) // n == jnp.arange(gp)[:, None, None])).astype(BF16)
    selt = jnp.transpose(sel, (0, 2, 1))
    grp = lambda k: pl.BlockSpec((1, k, tn), lambda pi, qi: (pi * gp + qi, 0, 0))
    pack_blk = pl.BlockSpec((1, rows, t * lanes), lambda pi, qi: (pi, 0, 0))
    y = pl.pallas_call(
        functools.partial(_s5_kernel, n_ctx=n_ctx, n_chunks=n_chunks),
        grid=(packs, gp),
        in_specs=[pack_blk,
                  pl.BlockSpec((1, t * lanes, tn), lambda pi, qi: (qi, 0, 0)),
                  pl.BlockSpec((1, tn, t * lanes), lambda pi, qi: (qi, 0, 0)),
                  grp(tn), grp(tn), grp(r_mat.shape[1]),
                  pl.BlockSpec((1, SSM_ROWS, a_vec.shape[-1]), lambda pi, qi: (pi * gp + qi, 0, 0))],
        out_specs=pack_blk,
        out_shape=jax.ShapeDtypeStruct((packs, rows, t * lanes), BF16),
        scratch_shapes=[pltpu.VMEM((rows, w_mat.shape[-1]), F32), pltpu.VMEM((rows, r_mat.shape[1]), F32)],
        compiler_params=_cparams(),
        name="s5_mixer",
    )(z, sel, selt, m_mat, w_mat, r_mat, a_vec)
    y = y.reshape(packs, n_chunks, SSM_ROWS, t, lanes)[:, n_ctx:, :b]
    return jnp.transpose(y, (2, 1, 3, 0, 4)).reshape(b, l, dch)


def _glu_kernel(y_ref, w_ref, b_ref, o_ref):
    z = jax.nn.gelu(y_ref[...].astype(F32))
    gate = jax.nn.sigmoid(jnp.dot(z.astype(BF16), w_ref[...], preferred_element_type=F32) + b_ref[...])
    o_ref[...] = (z * gate).astype(o_ref.dtype)


def s5_glu(y, w_bf16, b_glu, tm=512):
    n, d = y.shape
    return pl.pallas_call(
        _glu_kernel,
        grid=(n // tm,),
        in_specs=[pl.BlockSpec((tm, d), lambda i: (i, 0)),
                  pl.BlockSpec((d, d), lambda i: (0, 0)),
                  pl.BlockSpec((1, d), lambda i: (0, 0))],
        out_specs=pl.BlockSpec((tm, d), lambda i: (i, 0)),
        out_shape=jax.ShapeDtypeStruct((n, d), BF16),
        compiler_params=_cparams(),
        name="s5_glu",
    )(y, w_bf16, b_glu.reshape(1, d))


def _merge_kernel(a_ref, s_ref, ga_ref, gs_ref, w_ref, o_ref, h_scr):
    da = a_ref.shape[-1]

    @pl.when(pl.program_id(1) == 0)
    def _():
        h_scr[:, :da] = _rms(a_ref[...].astype(F32), ga_ref[...]).astype(BF16)
        h_scr[:, da:] = _rms(s_ref[...].astype(F32), gs_ref[...]).astype(BF16)

    o_ref[...] = jnp.dot(h_scr[...], w_ref[...], preferred_element_type=F32)


def merge_proj(attn, ssm, g_attn, g_ssm, w_bf16, tm=512, tn=1024):
    n, da = attn.shape
    ds = ssm.shape[1]
    d, dout = w_bf16.shape
    return pl.pallas_call(
        _merge_kernel,
        grid=(n // tm, dout // tn),
        in_specs=[pl.BlockSpec((tm, da), lambda i, j: (i, 0)),
                  pl.BlockSpec((tm, ds), lambda i, j: (i, 0)),
                  pl.BlockSpec((1, da), lambda i, j: (0, 0)),
                  pl.BlockSpec((1, ds), lambda i, j: (0, 0)),
                  pl.BlockSpec((d, tn), lambda i, j: (0, j))],
        out_specs=pl.BlockSpec((tm, tn), lambda i, j: (i, j)),
        out_shape=jax.ShapeDtypeStruct((n, dout), F32),
        scratch_shapes=[pltpu.VMEM((tm, d), BF16)],
        compiler_params=_cparams(),
        name="merge_proj",
    )(attn, ssm, g_attn.reshape(1, da), g_ssm.reshape(1, ds), w_bf16)


def _post_mix_kernel(x_ref, mix_ref, gpost_ref, gt_ref, gpre_ref, sc_ref, sh_ref, wr_ref, br_ref,
                     x1_ref, hp_ref, idx_ref, gate_ref, mask_ref):
    x1 = x_ref[0] + gt_ref[0] * _rms(mix_ref[0], gpost_ref[...])
    x1_ref[0] = x1
    h = _rms(x1, gpre_ref[...]) * (1.0 + sc_ref[0]) + sh_ref[0]
    half = h.shape[-1] // 2
    hp_ref[0] = _pack_halves(h[:, :half], h[:, half:])
    scores = jax.nn.sigmoid(jnp.dot(h.astype(BF16), wr_ref[...], preferred_element_type=F32))
    n_e = scores.shape[-1]
    lane = lax.broadcasted_iota(I32, scores.shape, 1)
    biased = scores + br_ref[...]
    idx_out = jnp.zeros(scores.shape, I32)
    sel_out = jnp.zeros(scores.shape, F32)
    mask = jnp.zeros(scores.shape, jnp.bool_)
    for k in range(TOP_K):
        m = jnp.max(biased, axis=-1, keepdims=True)
        ik = jnp.min(jnp.where(biased == m, lane, n_e), axis=-1, keepdims=True)
        hit = lane == ik
        sel_k = jnp.sum(jnp.where(hit, scores, 0.0), axis=-1, keepdims=True)
        idx_out = jnp.where(lane == k, ik, idx_out)
        sel_out = jnp.where(lane == k, sel_k, sel_out)
        mask = jnp.logical_or(mask, hit)
        biased = jnp.where(hit, -jnp.inf, biased)
    idx_ref[0] = idx_out
    gate_ref[0] = sel_out / jnp.sum(sel_out, axis=-1, keepdims=True) * ROUTED_SCALE
    mask_ref[0] = mask.astype(BF16)


def post_mix(x, mix, g_post, gt, g_pre, scale, shift, wr_bf16, b_router, tm=256):
    b, l, d = x.shape
    n_e = wr_bf16.shape[1]
    row = pl.BlockSpec((1, tm, d), lambda bi, i: (bi, i, 0))
    prow = pl.BlockSpec((1, tm, d // 2), lambda bi, i: (bi, i, 0))
    vec = pl.BlockSpec((1, d), lambda bi, i: (0, 0))
    bvec = pl.BlockSpec((1, 1, d), lambda bi, i: (bi, 0, 0))
    small = pl.BlockSpec((1, tm, n_e), lambda bi, i: (bi, i, 0))
    return pl.pallas_call(
        _post_mix_kernel,
        grid=(b, l // tm),
        in_specs=[row, row, vec, bvec, vec, bvec, bvec,
                  pl.BlockSpec((d, n_e), lambda bi, i: (0, 0)),
                  pl.BlockSpec((1, n_e), lambda bi, i: (0, 0))],
        out_specs=[row, prow, small, small, small],
        out_shape=[jax.ShapeDtypeStruct((b, l, d), F32),
                   jax.ShapeDtypeStruct((b, l, d // 2), I32),
                   jax.ShapeDtypeStruct((b, l, n_e), I32),
                   jax.ShapeDtypeStruct((b, l, n_e), F32),
                   jax.ShapeDtypeStruct((b, l, n_e), BF16)],
        compiler_params=_cparams(),
        name="post_mix",
    )(x, mix, g_post.reshape(1, d), gt, g_pre.reshape(1, d), scale, shift, wr_bf16, b_router.reshape(1, n_e))


def _rank_kernel(mask_ref, idx_ref, rank_ref, cnt_ref, carry):
    @pl.when(pl.program_id(0) == 0)
    def _():
        carry[...] = jnp.zeros_like(carry)

    m = mask_ref[...]
    tm = m.shape[0]
    earlier = (lax.broadcasted_iota(I32, (tm, tm), 1) < lax.broadcasted_iota(I32, (tm, tm), 0)).astype(BF16)
    excl = jnp.dot(earlier, m, preferred_element_type=F32) + carry[0:1, :]
    lane = lax.broadcasted_iota(I32, m.shape, 1)
    idx = idx_ref[...]
    out = jnp.zeros(m.shape, F32)
    for k in range(TOP_K):
        rk = jnp.sum(jnp.where(lane == idx[:, k:k + 1], excl, 0.0), axis=-1, keepdims=True)
        out = jnp.where(lane == k, rk, out)
    rank_ref[...] = out.astype(I32)
    carry[0:1, :] = carry[0:1, :] + jnp.sum(m.astype(F32), axis=0, keepdims=True)
    cnt_ref[...] = carry[...]


def route_ranks(mask, idx, tm=512):
    n, n_e = mask.shape
    blk = pl.BlockSpec((tm, n_e), lambda i: (i, 0))
    rank, cnt = pl.pallas_call(
        _rank_kernel,
        grid=(n // tm,),
        in_specs=[blk, blk],
        out_specs=[blk, pl.BlockSpec((8, n_e), lambda i: (0, 0))],
        out_shape=[jax.ShapeDtypeStruct((n, n_e), I32), jax.ShapeDtypeStruct((8, n_e), F32)],
        scratch_shapes=[pltpu.VMEM((8, n_e), F32)],
        compiler_params=_cparams(),
        name="route_ranks",
    )(mask, idx)
    return rank, cnt[0].astype(I32)


def _dest_kernel(rank_ref, idx_ref, ps_ref, dest_ref):
    idx = idx_ref[...]
    lane = lax.broadcasted_iota(I32, idx.shape, 1)
    start = jnp.zeros(idx.shape, F32)
    for k in range(TOP_K):
        sk = jnp.sum(jnp.where(lane == idx[:, k:k + 1], ps_ref[...], 0.0), axis=-1, keepdims=True)
        start = jnp.where(lane == k, sk, start)
    dest_ref[...] = rank_ref[...] + start.astype(I32)


def route_dest(rank, idx, pad_start, tm=512):
    n, n_e = rank.shape
    blk = pl.BlockSpec((tm, n_e), lambda i: (i, 0))
    dest = pl.pallas_call(
        _dest_kernel,
        grid=(n // tm,),
        in_specs=[blk, blk, pl.BlockSpec((1, n_e), lambda i: (0, 0))],
        out_specs=blk,
        out_shape=jax.ShapeDtypeStruct((n, n_e), I32),
        compiler_params=_cparams(),
        name="route_dest",
    )(rank, idx, pad_start.astype(F32).reshape(1, n_e))
    return dest[:, :TOP_K].reshape(n * TOP_K)


def _dispatch_kernel(pe_ref, pd_ref, dest_ref, h_ref, wgu_ref, wd_ref, xs_ref, sp_ref, z_scr, sem, *, n_e, tm, blk):
    def zero_copy(e):
        start = pl.multiple_of(pe_ref[e] - blk, blk)
        return pltpu.make_async_copy(z_scr, xs_ref.at[pl.ds(start, blk), :], sem)

    @pl.when(pl.program_id(0) == 0)
    def _():
        z_scr[...] = jnp.zeros_like(z_scr)

        def start(e, _):
            @pl.when(pd_ref[e] > 0)
            def _():
                zero_copy(e).start()
            return 0

        def wait(e, _):
            @pl.when(pd_ref[e] > 0)
            def _():
                zero_copy(e).wait()
            return 0

        lax.fori_loop(0, n_e, start, 0)
        lax.fori_loop(0, n_e, wait, 0)

    sub = 8
    half = h_ref.shape[-1]
    n_ch = half // PACK_CHUNK
    rows_ph = tm // (2 * n_ch)

    def issue_rows(r0):
        for g in range(0, rows_ph, sub):
            tile = h_ref.at[pl.ds(r0 + g, sub), :]
            for tt in range(sub):
                for k in range(TOP_K):
                    pltpu.make_async_copy(tile.at[pl.ds(tt, 1), :],
                                          xs_ref.at[pl.ds(dest_ref[(r0 + g + tt) * TOP_K + k], 1), :],
                                          sem).start(priority=k % 2)

    acc = None
    for c in range(n_ch):
        issue_rows(c * rows_ph)
        cl = slice(c * PACK_CHUNK, (c + 1) * PACK_CHUNK)
        ch = slice(half + c * PACK_CHUNK, half + (c + 1) * PACK_CHUNK)
        lo, hi = _unpack_halves(h_ref[:, cl])
        part = (jnp.dot(lo.astype(BF16), wgu_ref[cl, :], preferred_element_type=F32)
                + jnp.dot(hi.astype(BF16), wgu_ref[ch, :], preferred_element_type=F32))
        acc = part if acc is None else acc + part
    de = acc.shape[-1] // 2
    hid = (jax.nn.silu(acc[:, :de]) * acc[:, de:]).astype(BF16)
    for c in range(n_ch):
        issue_rows((n_ch + c) * rows_ph)
        cl = slice(c * PACK_CHUNK, (c + 1) * PACK_CHUNK)
        ch = slice(half + c * PACK_CHUNK, half + (c + 1) * PACK_CHUNK)
        sp_ref[:, cl] = _pack_halves(jnp.dot(hid, wd_ref[:, cl], preferred_element_type=F32),
                                     jnp.dot(hid, wd_ref[:, ch], preferred_element_type=F32))
    rows = tm * TOP_K
    pltpu.make_async_copy(xs_ref.at[pl.ds(0, rows), :], xs_ref.at[pl.ds(0, rows), :], sem).wait()


def moe_dispatch(hp, dest_flat, pad_end, padded, slots, wsgu, wsd, tm=DISPATCH_TM):
    n, dw = hp.shape
    n_e = pad_end.shape[0]
    d, de2 = wsgu.shape
    row = pl.BlockSpec((tm, dw), lambda i, pe, pd: (i, 0))
    grid_spec = pltpu.PrefetchScalarGridSpec(
        num_scalar_prefetch=2,
        grid=(n // tm,),
        in_specs=[pl.BlockSpec((tm * TOP_K,), lambda i, pe, pd: (i,), memory_space=pltpu.SMEM),
                  row,
                  pl.BlockSpec((d, de2), lambda i, pe, pd: (0, 0)),
                  pl.BlockSpec((de2 // 2, d), lambda i, pe, pd: (0, 0))],
        out_specs=[pl.BlockSpec(memory_space=pl.ANY), row],
        scratch_shapes=[pltpu.VMEM((EXPERT_BLK, dw), I32), pltpu.SemaphoreType.DMA(())],
    )
    return pl.pallas_call(
        functools.partial(_dispatch_kernel, n_e=n_e, tm=tm, blk=EXPERT_BLK),
        grid_spec=grid_spec,
        out_shape=[jax.ShapeDtypeStruct((slots, dw), I32), jax.ShapeDtypeStruct((n, dw), I32)],
        compiler_params=_cparams(),
        name="moe_dispatch",
    )(pad_end, padded, dest_flat, hp, wsgu, wsd)


PACK_CHUNK = 512


def _swiglu_hidden(xp_ref, wgu):
    half = xp_ref.shape[-1]
    acc = None
    for c in range(half // PACK_CHUNK):
        cl = slice(c * PACK_CHUNK, (c + 1) * PACK_CHUNK)
        ch = slice(half + c * PACK_CHUNK, half + (c + 1) * PACK_CHUNK)
        lo, hi = _unpack_halves(xp_ref[:, cl])
        part = (jnp.dot(lo.astype(BF16), wgu(cl), preferred_element_type=F32)
                + jnp.dot(hi.astype(BF16), wgu(ch), preferred_element_type=F32))
        acc = part if acc is None else acc + part
    de = acc.shape[-1] // 2
    return (jax.nn.silu(acc[:, :de]) * acc[:, de:]).astype(BF16)


def _down_packed(hid, wd, o_ref):
    half = o_ref.shape[-1]
    for c in range(half // PACK_CHUNK):
        cl = slice(c * PACK_CHUNK, (c + 1) * PACK_CHUNK)
        ch = slice(half + c * PACK_CHUNK, half + (c + 1) * PACK_CHUNK)
        o_ref[:, cl] = _pack_halves(jnp.dot(hid, wd(cl), preferred_element_type=F32),
                                    jnp.dot(hid, wd(ch), preferred_element_type=F32))


def _expert_weights(sched, i, w_hbm, w_buf, sem):
    be_ref, first_ref, slot_ref, nxt_ref = sched
    s = slot_ref[i]

    def copies(e, sl):
        return [pltpu.make_async_copy(h.at[e], v.at[sl], sem.at[j, sl]) for j, (h, v) in enumerate(zip(w_hbm, w_buf))]

    @pl.when(first_ref[i] == 1)
    def _():
        @pl.when(i == 0)
        def _():
            for c in copies(be_ref[0], 0):
                c.start()

        for c in copies(be_ref[i], s):
            c.wait()

        @pl.when(nxt_ref[i] >= 0)
        def _():
            for c in copies(nxt_ref[i], 1 - s):
                c.start()

    return s


def _expert_up_kernel(be_ref, nu_ref, first_ref, slot_ref, nxt_ref, x_ref, wg_hbm, wu_hbm, hid_ref,
                      wg_buf, wu_buf, sem):
    i = pl.program_id(0)
    used = i < nu_ref[0]
    s = _expert_weights((be_ref, first_ref, slot_ref, nxt_ref), i, (wg_hbm, wu_hbm), (wg_buf, wu_buf), sem)

    def wgu(sl):
        return jnp.concatenate([wg_buf[s, sl, :].astype(BF16), wu_buf[s, sl, :].astype(BF16)], axis=-1)

    @pl.when(used)
    def _():
        hid_ref[...] = _swiglu_hidden(x_ref, wgu)

    @pl.when(jnp.logical_not(used))
    def _():
        hid_ref[...] = jnp.zeros_like(hid_ref)


def _expert_down_kernel(be_ref, nu_ref, first_ref, slot_ref, nxt_ref, hid_ref, wd_hbm, o_ref, wd_buf, sem):
    i = pl.program_id(0)
    used = i < nu_ref[0]
    s = _expert_weights((be_ref, first_ref, slot_ref, nxt_ref), i, (wd_hbm,), (wd_buf,), sem)

    @pl.when(used)
    def _():
        _down_packed(hid_ref[...], lambda sl: wd_buf[s, :, sl].astype(BF16), o_ref)

    @pl.when(jnp.logical_not(used))
    def _():
        o_ref[...] = jnp.zeros_like(o_ref)


def _expert_schedule(block_e, n_used):
    nb = block_e.shape[0]
    ar = jnp.arange(nb, dtype=I32)
    used = ar < n_used[0]
    first = used & ((ar == 0) | (block_e != jnp.roll(block_e, 1)))
    slot = jnp.where(used, (jnp.cumsum(first.astype(I32)) - 1) % 2, 0).astype(I32)
    nf = jnp.where(first, ar, nb)
    nxt_idx = jnp.concatenate([lax.cummin(nf, axis=0, reverse=True)[1:], jnp.full((1,), nb, I32)])
    nxt = jnp.where(nxt_idx < nb, block_e[jnp.minimum(nxt_idx, nb - 1)], -1).astype(I32)
    return first.astype(I32), slot, nxt


def expert_ffn(xs, block_e, n_used, wg, wu, wd, blk):
    slots, dw = xs.shape
    d, de = wg.shape[1], wg.shape[2]
    n_blocks = slots // blk
    sched = (block_e, n_used) + _expert_schedule(block_e, n_used)
    cur = lambda i, nu: jnp.minimum(i, nu[0] - 1)
    hbm = pl.BlockSpec(memory_space=pl.ANY)
    up_spec = pltpu.PrefetchScalarGridSpec(
        num_scalar_prefetch=5,
        grid=(n_blocks,),
        in_specs=[pl.BlockSpec((blk, dw), lambda i, be, nu, *_: (cur(i, nu), 0)), hbm, hbm],
        out_specs=pl.BlockSpec((blk, de), lambda i, *_: (i, 0)),
        scratch_shapes=[pltpu.VMEM((2, d, de), F32), pltpu.VMEM((2, d, de), F32), pltpu.SemaphoreType.DMA((2, 2))],
    )
    hid = pl.pallas_call(
        _expert_up_kernel,
        grid_spec=up_spec,
        out_shape=jax.ShapeDtypeStruct((slots, de), BF16),
        compiler_params=_cparams(),
        name="expert_up",
    )(*sched, xs, wg, wu)
    down_spec = pltpu.PrefetchScalarGridSpec(
        num_scalar_prefetch=5,
        grid=(n_blocks,),
        in_specs=[pl.BlockSpec((blk, de), lambda i, be, nu, *_: (cur(i, nu), 0)), hbm],
        out_specs=pl.BlockSpec((blk, dw), lambda i, *_: (i, 0)),
        scratch_shapes=[pltpu.VMEM((2, de, d), F32), pltpu.SemaphoreType.DMA((1, 2))],
    )
    return pl.pallas_call(
        _expert_down_kernel,
        grid_spec=down_spec,
        out_shape=jax.ShapeDtypeStruct((slots, dw), I32),
        compiler_params=_cparams(),
        name="expert_down",
    )(*sched, hid, wd)


def _final_kernel(dc_ref, dn_ref, x1_ref, sp_ref, gate_ref, ys_ref, gpost_ref, gt_ref, o_ref, buf, sem, *, tm, n_tiles):
    i = pl.program_id(0)
    slot = i % 2
    sub = 8

    def aligned(v):
        return v if isinstance(v, int) else pl.multiple_of(v, sub)

    def issue_rows(dref, s, j):
        base = aligned(j * sub)
        for tt in range(sub):
            for k in range(TOP_K):
                tile = buf.at[s, pl.ds(aligned(k * tm + base), sub), :]
                pltpu.make_async_copy(ys_ref.at[pl.ds(dref[(base + tt) * TOP_K + k], 1), :],
                                      tile.at[pl.ds(tt, 1), :],
                                      sem.at[s]).start(priority=k % 2)

    def reduce_rows(s, j):
        rs = pl.ds(aligned(j * sub), sub)
        g = gate_ref[rs, :]
        acc_lo, acc_hi = _unpack_halves(sp_ref[rs, :])
        for k in range(TOP_K):
            lo, hi = _unpack_halves(buf[s, pl.ds(aligned(k * tm + j * sub), sub), :])
            gk = g[:, k:k + 1]
            acc_lo = acc_lo + gk * lo
            acc_hi = acc_hi + gk * hi
        ffn = jnp.concatenate([acc_lo, acc_hi], axis=-1)
        o_ref[rs, :] = x1_ref[rs, :] + gt_ref[0] * _rms(ffn, gpost_ref[...])

    @pl.when(i == 0)
    def _():
        def first(j, c):
            issue_rows(dc_ref, 0, j)
            return c
        lax.fori_loop(0, tm // sub, first, 0)

    def wait_slot(s):
        pltpu.make_async_copy(ys_ref.at[pl.ds(0, TOP_K * tm), :], buf.at[s], sem.at[s]).wait()

    wait_slot(slot)

    for s in range(2):
        @pl.when(slot == s)
        def _():
            for j in range(tm // sub):
                issue_rows(dn_ref, 1 - s, j)
                reduce_rows(s, j)

    @pl.when(i == n_tiles - 1)
    def _():
        wait_slot(1 - slot)


def final_mix(x1, sp, gates, ys, dest_flat, g_post, gt, tiles_per_batch, tm=COMBINE_TM):
    n, d = x1.shape
    dw = sp.shape[1]
    n_e = gates.shape[1]
    n_tiles = n // tm
    return pl.pallas_call(
        functools.partial(_final_kernel, tm=tm, n_tiles=n_tiles),
        grid=(n_tiles,),
        in_specs=[pl.BlockSpec((tm * TOP_K,), lambda i: (i,), memory_space=pltpu.SMEM),
                  pl.BlockSpec((tm * TOP_K,), lambda i: (jnp.minimum(i + 1, n_tiles - 1),), memory_space=pltpu.SMEM),
                  pl.BlockSpec((tm, d), lambda i: (i, 0)),
                  pl.BlockSpec((tm, dw), lambda i: (i, 0)),
                  pl.BlockSpec((tm, n_e), lambda i: (i, 0)),
                  pl.BlockSpec(memory_space=pl.ANY),
                  pl.BlockSpec((1, d), lambda i: (0, 0)),
                  pl.BlockSpec((1, 1, d), lambda i: (i // tiles_per_batch, 0, 0))],
        out_specs=pl.BlockSpec((tm, d), lambda i: (i, 0)),
        out_shape=jax.ShapeDtypeStruct((n, d), F32),
        scratch_shapes=[pltpu.VMEM((2, TOP_K * tm, dw), I32), pltpu.SemaphoreType.DMA((2,))],
        compiler_params=_cparams(),
        name="final_mix",
    )(dest_flat, dest_flat, x1, sp, gates, ys, g_post.reshape(1, d), gt)


def kernel(x, c, ctx, c_ctx, w_ada, b_ada, g_pre_mix, g_post_mix, g_pre_ffn, g_post_ffn, w_in, rpb, ssm_a_re, ssm_a_im, ssm_log_dt, ssm_b_re, ssm_b_im, ssm_c_re, ssm_c_im, ssm_d, w_glu, b_glu, g_attn_out, g_ssm_out, w_out, w_router, b_router, w_exp_gate, w_exp_up, w_exp_down, w_sh_gate, w_sh_up, w_sh_down):
    b, l, d = x.shape
    lc = ctx.shape[1]
    assert w_ada.shape[0] == 1 and b + 1 <= 8
    n_in = w_in.shape[-1]
    d_ssm = w_glu.shape[-1]
    d_attn = d - d_ssm
    n_heads = d_attn // HEAD_DIM
    n = b * l

    c8 = jnp.concatenate([c, c_ctx[None], jnp.zeros((8 - b - 1, d), F32)], axis=0)
    mod = ada_mod(c8, w_ada[0], b_ada[0]).reshape(8, 6, 1, d)
    sh_m, sc_m, gt_m, sh_f, sc_f, gt_f = [mod[:b, j] for j in range(6)]
    csh_m, csc_m = mod[b:b + 1, 0], mod[b:b + 1, 1]

    w_in_b = w_in[0].astype(BF16)
    tn = PROJ_TN
    p_lat = mod_proj(x, g_pre_mix[0], sc_m, sh_m, w_in_b, 0, n_in, 512, tn)
    p_ctx = mod_proj(ctx.reshape(1, b * lc, d), g_pre_mix[0], csc_m, csh_m, w_in_b,
                     d_attn // tn, n_in - d_attn, 512, tn).reshape(b, lc, n_in - d_attn)

    attn = neighborhood_attention(p_lat, p_ctx, rpb[0], n_heads)

    mats = _ssm_mats(ssm_a_re[0], ssm_a_im[0], ssm_log_dt[0], ssm_b_re[0], ssm_b_im[0],
                     ssm_c_re[0], ssm_c_im[0], ssm_d[0])
    y = s5_mixer(p_lat[..., 3 * d_attn:], p_ctx[..., 2 * d_attn:], mats)
    ssm = s5_glu(y.reshape(n, d_ssm), w_glu[0].astype(BF16), b_glu[0])

    mix = merge_proj(attn.reshape(n, d_attn), ssm, g_attn_out[0], g_ssm_out[0], w_out[0].astype(BF16))
    x1, hp, idx, gates, mask = post_mix(x, mix.reshape(b, l, d), g_post_mix[0], gt_m, g_pre_ffn[0], sc_f, sh_f,
                                        w_router[0].astype(BF16), b_router[0])

    n_e = w_router.shape[-1]
    m = n * TOP_K
    idx = idx.reshape(n, n_e)
    rank, counts = route_ranks(mask.reshape(n, n_e), idx)
    padded = (counts + EXPERT_BLK - 1) // EXPERT_BLK * EXPERT_BLK
    pad_end = jnp.cumsum(padded).astype(I32)
    dest = route_dest(rank, idx, pad_end - padded)
    n_blocks = m // EXPERT_BLK + n_e
    slots = n_blocks * EXPERT_BLK
    block_e = jnp.minimum(jnp.searchsorted(pad_end, jnp.arange(n_blocks) * EXPERT_BLK, side='right'),
                          n_e - 1).astype(I32)
    n_used = (pad_end[-1] // EXPERT_BLK).astype(I32).reshape(1)

    hp2 = hp.reshape(n, d // 2)
    xs, sp = moe_dispatch(hp2, dest, pad_end, padded.astype(I32), slots,
                          jnp.concatenate([w_sh_gate[0], w_sh_up[0]], axis=-1).astype(BF16), w_sh_down[0].astype(BF16))
    ys = expert_ffn(xs, block_e, n_used, w_exp_gate[0], w_exp_up[0], w_exp_down[0], EXPERT_BLK)
    out = final_mix(x1.reshape(n, d), sp, gates.reshape(n, n_e), ys, dest, g_post_ffn[0], gt_f, l // COMBINE_TM)
    return out.reshape(b, l, d)
```

```python
import functools

import numpy as np
import jax
import jax.numpy as jnp
from jax import lax
from jax.experimental import pallas as pl
from jax.experimental.pallas import tpu as pltpu

F32 = jnp.float32
BF16 = jnp.bfloat16
I32 = jnp.int32

GRID_W = 64
HEAD_DIM = 128
WIN_ROWS = 8
WIN_COLS = 16
ROPE_THETA = 10000.0
SSM_GROUP_CH = 16
SSM_STATE = 64
TOP_K = 8
ROUTED_SCALE = 2.5
EPS = 1e-6
NEG_INF = -1e30

Q_ROWS = 4
BAND_ROWS = Q_ROWS + WIN_ROWS - 1
VMEM_LIMIT = 56 * 1024 * 1024
EXPERT_BLK = 512
PROJ_TN = 1024
DISPATCH_TM = 512
COMBINE_TM = 128


def _cparams():
    return pltpu.CompilerParams(vmem_limit_bytes=VMEM_LIMIT)


def _rms(x, g):
    return x * lax.rsqrt(jnp.mean(x * x, axis=-1, keepdims=True) + EPS) * g


def _pack_halves(lo, hi):
    lo_bits = lax.bitcast_convert_type(lo.astype(BF16).astype(F32), I32)
    hi_bits = lax.bitcast_convert_type(hi.astype(BF16).astype(F32), I32)
    return (hi_bits & jnp.int32(-65536)) | lax.shift_right_logical(lo_bits, jnp.int32(16))


def _unpack_halves(w):
    lo = lax.bitcast_convert_type(lax.shift_left(w, jnp.int32(16)), F32)
    hi = lax.bitcast_convert_type(w & jnp.int32(-65536), F32)
    return lo, hi


def _ada_kernel(c_ref, w_ref, b_ref, o_ref):
    a = jax.nn.silu(c_ref[...]).astype(BF16)
    o_ref[...] = jnp.dot(a, w_ref[...].astype(BF16), preferred_element_type=F32) + b_ref[...]


def ada_mod(c8, w_ada, b_ada):
    d, n = w_ada.shape
    tn = 512
    return pl.pallas_call(
        _ada_kernel,
        grid=(n // tn,),
        in_specs=[pl.BlockSpec((8, d), lambda j: (0, 0)),
                  pl.BlockSpec((d, tn), lambda j: (0, j)),
                  pl.BlockSpec((1, tn), lambda j: (0, j))],
        out_specs=pl.BlockSpec((8, tn), lambda j: (0, j)),
        out_shape=jax.ShapeDtypeStruct((8, n), F32),
        compiler_params=_cparams(),
        name="ada_mod",
    )(c8, w_ada, b_ada.reshape(1, n))


def _modproj_kernel(x_ref, g_ref, sc_ref, sh_ref, w_ref, o_ref, h_scr):
    @pl.when(pl.program_id(2) == 0)
    def _():
        h = _rms(x_ref[0], g_ref[...]) * (1.0 + sc_ref[0]) + sh_ref[0]
        h_scr[...] = h.astype(BF16)

    o_ref[0] = jnp.dot(h_scr[...], w_ref[...], preferred_element_type=F32).astype(o_ref.dtype)


def mod_proj(x, g, scale, shift, w_bf16, col_blk_off, n_out, tm, tn):
    b, l, d = x.shape
    return pl.pallas_call(
        _modproj_kernel,
        grid=(b, l // tm, n_out // tn),
        in_specs=[pl.BlockSpec((1, tm, d), lambda bi, i, j: (bi, i, 0)),
                  pl.BlockSpec((1, d), lambda bi, i, j: (0, 0)),
                  pl.BlockSpec((1, 1, d), lambda bi, i, j: (bi, 0, 0)),
                  pl.BlockSpec((1, 1, d), lambda bi, i, j: (bi, 0, 0)),
                  pl.BlockSpec((d, tn), lambda bi, i, j: (0, j + col_blk_off))],
        out_specs=pl.BlockSpec((1, tm, tn), lambda bi, i, j: (bi, i, j)),
        out_shape=jax.ShapeDtypeStruct((b, l, n_out), BF16),
        scratch_shapes=[pltpu.VMEM((tm, d), BF16)],
        compiler_params=_cparams(),
        name="mod_proj",
    )(x, g.reshape(1, d), scale, shift, w_bf16)


def _rope_tables(l):
    half = HEAD_DIM // 2
    quarter = half // 2
    inv_freq = 1.0 / (ROPE_THETA ** (jnp.arange(0, half, 2, dtype=F32) / half))
    rows = (jnp.arange(l) // GRID_W).astype(F32)
    cols = (jnp.arange(l) % GRID_W).astype(F32)

    def cs(pos):
        ang = pos[:, None] * inv_freq[None, :]
        return jnp.cos(ang), jnp.sin(ang)

    cr, sr = cs(rows)
    cc, sc = cs(cols)
    zero = jnp.zeros((l, quarter), F32)
    cos = jnp.concatenate([cr, cr, cc, cc], axis=-1)
    s_lo = jnp.concatenate([-sr, zero, -sc, zero], axis=-1)
    s_hi = jnp.concatenate([zero, sr, zero, sc], axis=-1)
    return cos, s_lo, s_hi


def _attn_bias_plan(rows_n):
    kh = min(WIN_ROWS, rows_n)
    plan = []
    for r0, sb in [(0, 0), (Q_ROWS, 0), (rows_n - Q_ROWS, rows_n - BAND_ROWS)]:
        qrow = r0 + np.arange(Q_ROWS)
        krow = sb + np.arange(BAND_ROWS)
        rs = np.clip(qrow - kh // 2, 0, rows_n - kh)
        ok = (krow[None, :] >= rs[:, None]) & (krow[None, :] < rs[:, None] + kh)
        ro = krow[None, :] - qrow[:, None] + (WIN_ROWS - 1)
        plan.append(np.where(ok, ro, 2 * WIN_ROWS - 1))
    return np.stack(plan)


def _attn_toe(rpb):
    col = np.arange(GRID_W)
    cstart = np.clip(col - WIN_COLS // 2, 0, GRID_W - WIN_COLS)
    ok_c = (col[None, :] >= cstart[:, None]) & (col[None, :] < cstart[:, None] + WIN_COLS)
    co = np.clip(col[None, :] - col[:, None], -(WIN_COLS - 1), WIN_COLS - 1) + (WIN_COLS - 1)
    onehot = (co[None] == np.arange(2 * WIN_COLS - 1)[:, None, None]).astype(np.float32)
    toe = jnp.einsum('hrc,cqk->hrqk', rpb.astype(F32), onehot, precision=lax.Precision.HIGHEST)
    toe = jnp.where(ok_c[None, None], toe, NEG_INF)
    return jnp.concatenate([toe, jnp.full((rpb.shape[0], 1, GRID_W, GRID_W), NEG_INF, F32)], axis=1)


def _attn_kernel(q_ref, k_ref, v_ref, kc_ref, vc_ref, toe_ref, cos_ref, slo_ref, shi_ref, o_ref,
                 qr_scr, qs_scr, kr_scr, bias_scr, *, n_blk, rows_n, plan):
    quarter = HEAD_DIM // 4
    qn = Q_ROWS * GRID_W
    kn = BAND_ROWS * GRID_W
    scale = HEAD_DIM ** -0.5

    def rope(x, sl):
        return (x.astype(F32) * cos_ref[sl, :] + pltpu.roll(x, HEAD_DIM - quarter, 1).astype(F32) * slo_ref[sl, :]
                + pltpu.roll(x, quarter, 1).astype(F32) * shi_ref[sl, :])

    def rope_body(i, _):
        sl = pl.ds(pl.multiple_of(i * qn, qn), qn)
        q = q_ref[0, sl, :]
        qs_scr[sl, :] = (q.astype(F32) * scale).astype(BF16)
        qr_scr[sl, :] = (rope(q, sl) * scale).astype(BF16)
        kr_scr[sl, :] = rope(k_ref[0, sl, :], sl).astype(BF16)
        return 0

    lax.fori_loop(0, n_blk, rope_body, 0)

    @pl.when(pl.program_id(1) == 0)
    def _():
        for case in range(plan.shape[0]):
            for qi in range(Q_ROWS):
                for kj in range(BAND_ROWS):
                    bias_scr[case, qi * GRID_W:(qi + 1) * GRID_W, kj * GRID_W:(kj + 1) * GRID_W] = (
                        toe_ref[0, int(plan[case, qi, kj])])

    kc = kc_ref[0]
    vc = vc_ref[0]
    nt = (((1,), (1,)), ((), ()))

    def body(i, _):
        sb = jnp.clip(i * Q_ROWS - WIN_ROWS // 2, 0, rows_n - BAND_ROWS)
        ks = pl.ds(pl.multiple_of(sb * GRID_W, GRID_W), kn)
        qs = pl.ds(pl.multiple_of(i * qn, qn), qn)
        case = jnp.where(i == 0, 0, jnp.where(i == n_blk - 1, 2, 1))
        s = lax.dot_general(qr_scr[qs, :], kr_scr[ks, :], nt, preferred_element_type=F32) + bias_scr[case]
        sc = lax.dot_general(qs_scr[qs, :], kc, nt, preferred_element_type=F32)
        m = jnp.maximum(jnp.max(s, axis=-1, keepdims=True), jnp.max(sc, axis=-1, keepdims=True))
        p = jnp.exp(s - m)
        pc = jnp.exp(sc - m)
        den = jnp.sum(p, axis=-1, keepdims=True) + jnp.sum(pc, axis=-1, keepdims=True)
        o = (jnp.dot(p.astype(BF16), v_ref[0, ks, :], preferred_element_type=F32)
             + jnp.dot(pc.astype(BF16), vc, preferred_element_type=F32))
        o_ref[0, qs, :] = (o / den).astype(o_ref.dtype)
        return 0

    lax.fori_loop(0, n_blk, body, 0, unroll=4)


def neighborhood_attention(p_lat, p_ctx, rpb, n_heads):
    b, l, _ = p_lat.shape
    lc = p_ctx.shape[1]
    rows_n = l // GRID_W
    assert rows_n % Q_ROWS == 0 and rows_n >= BAND_ROWS + Q_ROWS
    n_blk = rows_n // Q_ROWS
    plan = _attn_bias_plan(rows_n)
    cos, s_lo, s_hi = _rope_tables(l)
    qn, kn = Q_ROWS * GRID_W, BAND_ROWS * GRID_W
    h = n_heads
    tab = pl.BlockSpec((l, HEAD_DIM), lambda hi, bi: (0, 0))
    return pl.pallas_call(
        functools.partial(_attn_kernel, n_blk=n_blk, rows_n=rows_n, plan=plan),
        grid=(h, b),
        in_specs=[pl.BlockSpec((1, l, HEAD_DIM), lambda hi, bi: (bi, 0, hi)),
                  pl.BlockSpec((1, l, HEAD_DIM), lambda hi, bi: (bi, 0, hi + h)),
                  pl.BlockSpec((1, l, HEAD_DIM), lambda hi, bi: (bi, 0, hi + 2 * h)),
                  pl.BlockSpec((1, lc, HEAD_DIM), lambda hi, bi: (bi, 0, hi)),
                  pl.BlockSpec((1, lc, HEAD_DIM), lambda hi, bi: (bi, 0, hi + h)),
                  pl.BlockSpec((1, 2 * WIN_ROWS, GRID_W, GRID_W), lambda hi, bi: (hi, 0, 0, 0)),
                  tab, tab, tab],
        out_specs=pl.BlockSpec((1, l, HEAD_DIM), lambda hi, bi: (bi, 0, hi)),
        out_shape=jax.ShapeDtypeStruct((b, l, h * HEAD_DIM), BF16),
        scratch_shapes=[pltpu.VMEM((l, HEAD_DIM), BF16)] * 3 + [pltpu.VMEM((plan.shape[0], qn, kn), F32)],
        compiler_params=_cparams(),
        name="nbr_attn",
    )(p_lat, p_lat, p_lat, p_ctx, p_ctx, _attn_toe(rpb), cos, s_lo, s_hi)


SSM_CHUNK = 16
SSM_ROWS = 8
SSM_PACK = 8


def _ssm_mats(a_re, a_im, log_dt, b_re, b_im, c_re, c_im, d_skip):
    t = SSM_CHUNK
    hi = lax.Precision.HIGHEST
    a = lax.complex(a_re.astype(F32), a_im.astype(F32))
    dta = jnp.exp(log_dt.astype(F32))[..., None] * a
    a_bar = jnp.exp(dta)
    b_bar = ((a_bar - 1.0) / a)[..., None] * lax.complex(b_re.astype(F32), b_im.astype(F32))
    cm = lax.complex(c_re.astype(F32), c_im.astype(F32))
    k = jnp.arange(t + 1, dtype=F32)
    ap = jnp.exp(dta[..., None] * k)
    g, p, n = a.shape[1], a.shape[2], b_re.shape[-1]
    kern = jnp.einsum('dgnp,dgpl,dgpm->dglnm', cm, ap[..., :t], b_bar, precision=hi).real
    s_i = np.arange(t)[:, None]
    t_i = np.arange(t)[None, :]
    lag = np.arange(t)[:, None, None]
    oh = np.stack([(t_i - s_i)[None] == lag, (s_i - t_i)[None] == lag]).astype(np.float32)
    eye = jnp.eye(n, dtype=F32)
    m_mat = (jnp.einsum('dlst,dglnm->gsmtn', oh, kern, precision=hi)
             + jnp.einsum('st,gn,nm->gsmtn', np.eye(t, dtype=np.float32), d_skip.astype(F32), eye, precision=hi))
    m_mat = m_mat.reshape(g, t * n, t * n)
    wf = ap[0][:, :, t - 1::-1][..., :t, None] * b_bar[0][:, :, None, :]
    wb = ap[1][:, :, :t, None] * b_bar[1][:, :, None, :]
    to_rows = lambda z: jnp.transpose(z, (0, 2, 3, 1)).reshape(g, t * n, p)
    w_mat = jnp.concatenate([to_rows(wf.real), to_rows(wb.real), to_rows(wf.imag), to_rows(wb.imag)], axis=-1)
    zf = jnp.transpose(cm[0], (0, 2, 1))[:, :, None, :] * ap[0][:, :, 1:t + 1, None]
    zb = jnp.transpose(cm[1], (0, 2, 1))[:, :, None, :] * ap[1][:, :, t:0:-1, None]
    flat = lambda z: z.reshape(g, p, t * n)
    r_mat = jnp.concatenate([flat(zf.real), flat(zb.real), -flat(zf.imag), -flat(zb.imag)], axis=1)
    a_t = ap[..., t]
    a_vec = jnp.stack([jnp.concatenate([a_t[0].real, a_t[1].real], -1),
                       jnp.concatenate([a_t[0].imag, a_t[1].imag], -1)], axis=1)
    a_vec = jnp.pad(a_vec, ((0, 0), (0, SSM_ROWS - 2), (0, 0)))
    return m_mat.astype(BF16), w_mat.astype(BF16), r_mat.astype(BF16), a_vec


def _s5_kernel(u_ref, sel_ref, selt_ref, m_ref, w_ref, r_ref, a_ref, y_ref, v_scr, ent_scr, *, n_ctx, n_chunks):
    u = jnp.dot(u_ref[0], sel_ref[0], preferred_element_type=F32).astype(BF16)
    p2 = a_ref.shape[-1]
    p = p2 // 2
    v_scr[...] = jnp.dot(u, w_ref[0], preferred_element_type=F32)
    a_re = jnp.broadcast_to(a_ref[0, 0:1, :], (SSM_ROWS, p2))
    a_im = jnp.broadcast_to(a_ref[0, 1:2, :], (SSM_ROWS, p2))
    is_fwd = lax.broadcasted_iota(I32, (SSM_ROWS, p2), 1) < p

    def step(j, carry):
        s_re, s_im = carry
        cb = jnp.where(j < n_ctx, n_ctx - 1 - j, n_chunks + n_ctx - 1 - j)
        rf = pl.ds(pl.multiple_of(j * SSM_ROWS, SSM_ROWS), SSM_ROWS)
        rb = pl.ds(pl.multiple_of(cb * SSM_ROWS, SSM_ROWS), SSM_ROWS)
        ent_scr[rf, 0:p] = s_re[:, :p]
        ent_scr[rb, p:p2] = s_re[:, p:]
        ent_scr[rf, p2:p2 + p] = s_im[:, :p]
        ent_scr[rb, p2 + p:] = s_im[:, p:]
        in_re = jnp.where(is_fwd, v_scr[rf, :p2], v_scr[rb, :p2])
        in_im = jnp.where(is_fwd, v_scr[rf, p2:], v_scr[rb, p2:])
        return a_re * s_re - a_im * s_im + in_re, a_re * s_im + a_im * s_re + in_im

    zero = jnp.zeros((SSM_ROWS, p2), F32)
    lax.fori_loop(0, n_chunks, step, (zero, zero))
    y = (jnp.dot(u, m_ref[0], preferred_element_type=F32)
         + jnp.dot(ent_scr[...].astype(BF16), r_ref[0], preferred_element_type=F32)).astype(BF16)
    def part(c):
        return jnp.dot(y, selt_ref[0, :, c:c + PACK_CHUNK], preferred_element_type=F32).astype(y_ref.dtype)

    @pl.when(pl.program_id(1) == 0)
    def _():
        for c in range(0, y_ref.shape[-1], PACK_CHUNK):
            y_ref[0, :, c:c + PACK_CHUNK] = part(c)

    @pl.when(pl.program_id(1) > 0)
    def _():
        for c in range(0, y_ref.shape[-1], PACK_CHUNK):
            y_ref[0, :, c:c + PACK_CHUNK] = y_ref[0, :, c:c + PACK_CHUNK] + part(c)


def s5_mixer(u_lat, u_ctx, mats):
    m_mat, w_mat, r_mat, a_vec = mats
    b, l, dch = u_lat.shape
    lc = u_ctx.shape[1]
    g = m_mat.shape[0]
    n = dch // g
    t = SSM_CHUNK
    tn = t * n
    gp = SSM_PACK
    packs, lanes = g // gp, gp * n
    n_ctx, n_chunks = lc // t, (lc + l) // t
    assert b <= SSM_ROWS and lc % t == 0 and l % t == 0 and g % gp == 0
    z = jnp.concatenate([u_ctx, u_lat], axis=1).reshape(b, n_chunks, t, packs, lanes)
    z = jnp.pad(jnp.transpose(z, (3, 1, 0, 2, 4)), ((0, 0), (0, 0), (0, SSM_ROWS - b), (0, 0), (0, 0)))
    rows = n_chunks * SSM_ROWS
    z = z.reshape(packs, rows, t * lanes)
    ri = jnp.arange(t * lanes)
    ci = jnp.arange(tn)
    same = (ri[:, None] // lanes == ci[None, :] // n) & (ri[:, None] % n == ci[None, :] % n)
    sel = (same[None] & ((ri[None, :, None] % lanes) // n == jnp.arange(gp)[:, None, None])).astype(BF16)
    selt = jnp.transpose(sel, (0, 2, 1))
    grp = lambda k: pl.BlockSpec((1, k, tn), lambda pi, qi: (pi * gp + qi, 0, 0))
    pack_blk = pl.BlockSpec((1, rows, t * lanes), lambda pi, qi: (pi, 0, 0))
    y = pl.pallas_call(
        functools.partial(_s5_kernel, n_ctx=n_ctx, n_chunks=n_chunks),
        grid=(packs, gp),
        in_specs=[pack_blk,
                  pl.BlockSpec((1, t * lanes, tn), lambda pi, qi: (qi, 0, 0)),
                  pl.BlockSpec((1, tn, t * lanes), lambda pi, qi: (qi, 0, 0)),
                  grp(tn), grp(tn), grp(r_mat.shape[1]),
                  pl.BlockSpec((1, SSM_ROWS, a_vec.shape[-1]), lambda pi, qi: (pi * gp + qi, 0, 0))],
        out_specs=pack_blk,
        out_shape=jax.ShapeDtypeStruct((packs, rows, t * lanes), BF16),
        scratch_shapes=[pltpu.VMEM((rows, w_mat.shape[-1]), F32), pltpu.VMEM((rows, r_mat.shape[1]), F32)],
        compiler_params=_cparams(),
        name="s5_mixer",
    )(z, sel, selt, m_mat, w_mat, r_mat, a_vec)
    y = y.reshape(packs, n_chunks, SSM_ROWS, t, lanes)[:, n_ctx:, :b]
    return jnp.transpose(y, (2, 1, 3, 0, 4)).reshape(b, l, dch)


def _glu_kernel(y_ref, w_ref, b_ref, o_ref):
    z = jax.nn.gelu(y_ref[...].astype(F32))
    gate = jax.nn.sigmoid(jnp.dot(z.astype(BF16), w_ref[...], preferred_element_type=F32) + b_ref[...])
    o_ref[...] = (z * gate).astype(o_ref.dtype)


def s5_glu(y, w_bf16, b_glu, tm=512):
    n, d = y.shape
    return pl.pallas_call(
        _glu_kernel,
        grid=(n // tm,),
        in_specs=[pl.BlockSpec((tm, d), lambda i: (i, 0)),
                  pl.BlockSpec((d, d), lambda i: (0, 0)),
                  pl.BlockSpec((1, d), lambda i: (0, 0))],
        out_specs=pl.BlockSpec((tm, d), lambda i: (i, 0)),
        out_shape=jax.ShapeDtypeStruct((n, d), BF16),
        compiler_params=_cparams(),
        name="s5_glu",
    )(y, w_bf16, b_glu.reshape(1, d))


def _merge_kernel(a_ref, s_ref, ga_ref, gs_ref, w_ref, o_ref, h_scr):
    da = a_ref.shape[-1]

    @pl.when(pl.program_id(1) == 0)
    def _():
        h_scr[:, :da] = _rms(a_ref[...].astype(F32), ga_ref[...]).astype(BF16)
        h_scr[:, da:] = _rms(s_ref[...].astype(F32), gs_ref[...]).astype(BF16)

    o_ref[...] = jnp.dot(h_scr[...], w_ref[...], preferred_element_type=F32)


def merge_proj(attn, ssm, g_attn, g_ssm, w_bf16, tm=512, tn=1024):
    n, da = attn.shape
    ds = ssm.shape[1]
    d, dout = w_bf16.shape
    return pl.pallas_call(
        _merge_kernel,
        grid=(n // tm, dout // tn),
        in_specs=[pl.BlockSpec((tm, da), lambda i, j: (i, 0)),
                  pl.BlockSpec((tm, ds), lambda i, j: (i, 0)),
                  pl.BlockSpec((1, da), lambda i, j: (0, 0)),
                  pl.BlockSpec((1, ds), lambda i, j: (0, 0)),
                  pl.BlockSpec((d, tn), lambda i, j: (0, j))],
        out_specs=pl.BlockSpec((tm, tn), lambda i, j: (i, j)),
        out_shape=jax.ShapeDtypeStruct((n, dout), F32),
        scratch_shapes=[pltpu.VMEM((tm, d), BF16)],
        compiler_params=_cparams(),
        name="merge_proj",
    )(attn, ssm, g_attn.reshape(1, da), g_ssm.reshape(1, ds), w_bf16)


def _post_mix_kernel(x_ref, mix_ref, gpost_ref, gt_ref, gpre_ref, sc_ref, sh_ref, wr_ref, br_ref,
                     x1_ref, hp_ref, idx_ref, gate_ref, mask_ref):
    x1 = x_ref[0] + gt_ref[0] * _rms(mix_ref[0], gpost_ref[...])
    x1_ref[0] = x1
    h = _rms(x1, gpre_ref[...]) * (1.0 + sc_ref[0]) + sh_ref[0]
    half = h.shape[-1] // 2
    hp_ref[0] = _pack_halves(h[:, :half], h[:, half:])
    scores = jax.nn.sigmoid(jnp.dot(h.astype(BF16), wr_ref[...], preferred_element_type=F32))
    n_e = scores.shape[-1]
    lane = lax.broadcasted_iota(I32, scores.shape, 1)
    biased = scores + br_ref[...]
    idx_out = jnp.zeros(scores.shape, I32)
    sel_out = jnp.zeros(scores.shape, F32)
    mask = jnp.zeros(scores.shape, jnp.bool_)
    for k in range(TOP_K):
        m = jnp.max(biased, axis=-1, keepdims=True)
        ik = jnp.min(jnp.where(biased == m, lane, n_e), axis=-1, keepdims=True)
        hit = lane == ik
        sel_k = jnp.sum(jnp.where(hit, scores, 0.0), axis=-1, keepdims=True)
        idx_out = jnp.where(lane == k, ik, idx_out)
        sel_out = jnp.where(lane == k, sel_k, sel_out)
        mask = jnp.logical_or(mask, hit)
        biased = jnp.where(hit, -jnp.inf, biased)
    idx_ref[0] = idx_out
    gate_ref[0] = sel_out / jnp.sum(sel_out, axis=-1, keepdims=True) * ROUTED_SCALE
    mask_ref[0] = mask.astype(BF16)


def post_mix(x, mix, g_post, gt, g_pre, scale, shift, wr_bf16, b_router, tm=256):
    b, l, d = x.shape
    n_e = wr_bf16.shape[1]
    row = pl.BlockSpec((1, tm, d), lambda bi, i: (bi, i, 0))
    prow = pl.BlockSpec((1, tm, d // 2), lambda bi, i: (bi, i, 0))
    vec = pl.BlockSpec((1, d), lambda bi, i: (0, 0))
    bvec = pl.BlockSpec((1, 1, d), lambda bi, i: (bi, 0, 0))
    small = pl.BlockSpec((1, tm, n_e), lambda bi, i: (bi, i, 0))
    return pl.pallas_call(
        _post_mix_kernel,
        grid=(b, l // tm),
        in_specs=[row, row, vec, bvec, vec, bvec, bvec,
                  pl.BlockSpec((d, n_e), lambda bi, i: (0, 0)),
                  pl.BlockSpec((1, n_e), lambda bi, i: (0, 0))],
        out_specs=[row, prow, small, small, small],
        out_shape=[jax.ShapeDtypeStruct((b, l, d), F32),
                   jax.ShapeDtypeStruct((b, l, d // 2), I32),
                   jax.ShapeDtypeStruct((b, l, n_e), I32),
                   jax.ShapeDtypeStruct((b, l, n_e), F32),
                   jax.ShapeDtypeStruct((b, l, n_e), BF16)],
        compiler_params=_cparams(),
        name="post_mix",
    )(x, mix, g_post.reshape(1, d), gt, g_pre.reshape(1, d), scale, shift, wr_bf16, b_router.reshape(1, n_e))


def _rank_kernel(mask_ref, idx_ref, rank_ref, cnt_ref, carry):
    @pl.when(pl.program_id(0) == 0)
    def _():
        carry[...] = jnp.zeros_like(carry)

    m = mask_ref[...]
    tm = m.shape[0]
    earlier = (lax.broadcasted_iota(I32, (tm, tm), 1) < lax.broadcasted_iota(I32, (tm, tm), 0)).astype(BF16)
    excl = jnp.dot(earlier, m, preferred_element_type=F32) + carry[0:1, :]
    lane = lax.broadcasted_iota(I32, m.shape, 1)
    idx = idx_ref[...]
    out = jnp.zeros(m.shape, F32)
    for k in range(TOP_K):
        rk = jnp.sum(jnp.where(lane == idx[:, k:k + 1], excl, 0.0), axis=-1, keepdims=True)
        out = jnp.where(lane == k, rk, out)
    rank_ref[...] = out.astype(I32)
    carry[0:1, :] = carry[0:1, :] + jnp.sum(m.astype(F32), axis=0, keepdims=True)
    cnt_ref[...] = carry[...]


def route_ranks(mask, idx, tm=512):
    n, n_e = mask.shape
    blk = pl.BlockSpec((tm, n_e), lambda i: (i, 0))
    rank, cnt = pl.pallas_call(
        _rank_kernel,
        grid=(n // tm,),
        in_specs=[blk, blk],
        out_specs=[blk, pl.BlockSpec((8, n_e), lambda i: (0, 0))],
        out_shape=[jax.ShapeDtypeStruct((n, n_e), I32), jax.ShapeDtypeStruct((8, n_e), F32)],
        scratch_shapes=[pltpu.VMEM((8, n_e), F32)],
        compiler_params=_cparams(),
        name="route_ranks",
    )(mask, idx)
    return rank, cnt[0].astype(I32)


def _dest_kernel(rank_ref, idx_ref, ps_ref, dest_ref):
    idx = idx_ref[...]
    lane = lax.broadcasted_iota(I32, idx.shape, 1)
    start = jnp.zeros(idx.shape, F32)
    for k in range(TOP_K):
        sk = jnp.sum(jnp.where(lane == idx[:, k:k + 1], ps_ref[...], 0.0), axis=-1, keepdims=True)
        start = jnp.where(lane == k, sk, start)
    dest_ref[...] = rank_ref[...] + start.astype(I32)


def route_dest(rank, idx, pad_start, tm=512):
    n, n_e = rank.shape
    blk = pl.BlockSpec((tm, n_e), lambda i: (i, 0))
    dest = pl.pallas_call(
        _dest_kernel,
        grid=(n // tm,),
        in_specs=[blk, blk, pl.BlockSpec((1, n_e), lambda i: (0, 0))],
        out_specs=blk,
        out_shape=jax.ShapeDtypeStruct((n, n_e), I32),
        compiler_params=_cparams(),
        name="route_dest",
    )(rank, idx, pad_start.astype(F32).reshape(1, n_e))
    return dest[:, :TOP_K].reshape(n * TOP_K)


def _dispatch_kernel(pe_ref, pd_ref, dest_ref, h_ref, wgu_ref, wd_ref, xs_ref, sp_ref, z_scr, sem, *, n_e, tm, blk):
    def zero_copy(e):
        start = pl.multiple_of(pe_ref[e] - blk, blk)
        return pltpu.make_async_copy(z_scr, xs_ref.at[pl.ds(start, blk), :], sem)

    @pl.when(pl.program_id(0) == 0)
    def _():
        z_scr[...] = jnp.zeros_like(z_scr)

        def start(e, _):
            @pl.when(pd_ref[e] > 0)
            def _():
                zero_copy(e).start()
            return 0

        def wait(e, _):
            @pl.when(pd_ref[e] > 0)
            def _():
                zero_copy(e).wait()
            return 0

        lax.fori_loop(0, n_e, start, 0)
        lax.fori_loop(0, n_e, wait, 0)

    sub = 8
    half = h_ref.shape[-1]
    n_ch = half // PACK_CHUNK
    rows_ph = tm // (2 * n_ch)

    def issue_rows(r0):
        for g in range(0, rows_ph, sub):
            tile = h_ref.at[pl.ds(r0 + g, sub), :]
            for tt in range(sub):
                for k in range(TOP_K):
                    pltpu.make_async_copy(tile.at[pl.ds(tt, 1), :],
                                          xs_ref.at[pl.ds(dest_ref[(r0 + g + tt) * TOP_K + k], 1), :],
                                          sem).start(priority=k % 2)

    acc = None
    for c in range(n_ch):
        issue_rows(c * rows_ph)
        cl = slice(c * PACK_CHUNK, (c + 1) * PACK_CHUNK)
        ch = slice(half + c * PACK_CHUNK, half + (c + 1) * PACK_CHUNK)
        lo, hi = _unpack_halves(h_ref[:, cl])
        part = (jnp.dot(lo.astype(BF16), wgu_ref[cl, :], preferred_element_type=F32)
                + jnp.dot(hi.astype(BF16), wgu_ref[ch, :], preferred_element_type=F32))
        acc = part if acc is None else acc + part
    de = acc.shape[-1] // 2
    hid = (jax.nn.silu(acc[:, :de]) * acc[:, de:]).astype(BF16)
    for c in range(n_ch):
        issue_rows((n_ch + c) * rows_ph)
        cl = slice(c * PACK_CHUNK, (c + 1) * PACK_CHUNK)
        ch = slice(half + c * PACK_CHUNK, half + (c + 1) * PACK_CHUNK)
        sp_ref[:, cl] = _pack_halves(jnp.dot(hid, wd_ref[:, cl], preferred_element_type=F32),
                                     jnp.dot(hid, wd_ref[:, ch], preferred_element_type=F32))
    rows = tm * TOP_K
    pltpu.make_async_copy(xs_ref.at[pl.ds(0, rows), :], xs_ref.at[pl.ds(0, rows), :], sem).wait()


def moe_dispatch(hp, dest_flat, pad_end, padded, slots, wsgu, wsd, tm=DISPATCH_TM):
    n, dw = hp.shape
    n_e = pad_end.shape[0]
    d, de2 = wsgu.shape
    row = pl.BlockSpec((tm, dw), lambda i, pe, pd: (i, 0))
    grid_spec = pltpu.PrefetchScalarGridSpec(
        num_scalar_prefetch=2,
        grid=(n // tm,),
        in_specs=[pl.BlockSpec((tm * TOP_K,), lambda i, pe, pd: (i,), memory_space=pltpu.SMEM),
                  row,
                  pl.BlockSpec((d, de2), lambda i, pe, pd: (0, 0)),
                  pl.BlockSpec((de2 // 2, d), lambda i, pe, pd: (0, 0))],
        out_specs=[pl.BlockSpec(memory_space=pl.ANY), row],
        scratch_shapes=[pltpu.VMEM((EXPERT_BLK, dw), I32), pltpu.SemaphoreType.DMA(())],
    )
    return pl.pallas_call(
        functools.partial(_dispatch_kernel, n_e=n_e, tm=tm, blk=EXPERT_BLK),
        grid_spec=grid_spec,
        out_shape=[jax.ShapeDtypeStruct((slots, dw), I32), jax.ShapeDtypeStruct((n, dw), I32)],
        compiler_params=_cparams(),
        name="moe_dispatch",
    )(pad_end, padded, dest_flat, hp, wsgu, wsd)


PACK_CHUNK = 512


def _swiglu_hidden(xp_ref, wgu):
    half = xp_ref.shape[-1]
    acc = None
    for c in range(half // PACK_CHUNK):
        cl = slice(c * PACK_CHUNK, (c + 1) * PACK_CHUNK)
        ch = slice(half + c * PACK_CHUNK, half + (c + 1) * PACK_CHUNK)
        lo, hi = _unpack_halves(xp_ref[:, cl])
        part = (jnp.dot(lo.astype(BF16), wgu(cl), preferred_element_type=F32)
                + jnp.dot(hi.astype(BF16), wgu(ch), preferred_element_type=F32))
        acc = part if acc is None else acc + part
    de = acc.shape[-1] // 2
    return (jax.nn.silu(acc[:, :de]) * acc[:, de:]).astype(BF16)


def _down_packed(hid, wd, o_ref):
    half = o_ref.shape[-1]
    for c in range(half // PACK_CHUNK):
        cl = slice(c * PACK_CHUNK, (c + 1) * PACK_CHUNK)
        ch = slice(half + c * PACK_CHUNK, half + (c + 1) * PACK_CHUNK)
        o_ref[:, cl] = _pack_halves(jnp.dot(hid, wd(cl), preferred_element_type=F32),
                                    jnp.dot(hid, wd(ch), preferred_element_type=F32))


def _expert_weights(sched, i, w_hbm, w_buf, sem):
    be_ref, first_ref, slot_ref, nxt_ref = sched
    s = slot_ref[i]

    def copies(e, sl):
        return [pltpu.make_async_copy(h.at[e], v.at[sl], sem.at[j, sl]) for j, (h, v) in enumerate(zip(w_hbm, w_buf))]

    @pl.when(first_ref[i] == 1)
    def _():
        @pl.when(i == 0)
        def _():
            for c in copies(be_ref[0], 0):
                c.start()

        for c in copies(be_ref[i], s):
            c.wait()

        @pl.when(nxt_ref[i] >= 0)
        def _():
            for c in copies(nxt_ref[i], 1 - s):
                c.start()

    return s


def _expert_up_kernel(be_ref, nu_ref, first_ref, slot_ref, nxt_ref, x_ref, wg_hbm, wu_hbm, hid_ref,
                      wg_buf, wu_buf, sem):
    i = pl.program_id(0)
    used = i < nu_ref[0]
    s = _expert_weights((be_ref, first_ref, slot_ref, nxt_ref), i, (wg_hbm, wu_hbm), (wg_buf, wu_buf), sem)

    def wgu(sl):
        return jnp.concatenate([wg_buf[s, sl, :].astype(BF16), wu_buf[s, sl, :].astype(BF16)], axis=-1)

    @pl.when(used)
    def _():
        hid_ref[...] = _swiglu_hidden(x_ref, wgu)


def _expert_down_kernel(be_ref, nu_ref, first_ref, slot_ref, nxt_ref, hid_ref, wd_hbm, o_ref, wd_buf, sem):
    i = pl.program_id(0)
    used = i < nu_ref[0]
    s = _expert_weights((be_ref, first_ref, slot_ref, nxt_ref), i, (wd_hbm,), (wd_buf,), sem)

    @pl.when(used)
    def _():
        _down_packed(hid_ref[...], lambda sl: wd_buf[s, :, sl].astype(BF16), o_ref)


def _expert_schedule(block_e, n_used):
    nb = block_e.shape[0]
    ar = jnp.arange(nb, dtype=I32)
    used = ar < n_used[0]
    first = used & ((ar == 0) | (block_e != jnp.roll(block_e, 1)))
    slot = jnp.where(used, (jnp.cumsum(first.astype(I32)) - 1) % 2, 0).astype(I32)
    nf = jnp.where(first, ar, nb)
    nxt_idx = jnp.concatenate([lax.cummin(nf, axis=0, reverse=True)[1:], jnp.full((1,), nb, I32)])
    nxt = jnp.where(nxt_idx < nb, block_e[jnp.minimum(nxt_idx, nb - 1)], -1).astype(I32)
    return first.astype(I32), slot, nxt


def expert_ffn(xs, block_e, n_used, wg, wu, wd, blk):
    slots, dw = xs.shape
    d, de = wg.shape[1], wg.shape[2]
    n_blocks = slots // blk
    sched = (block_e, n_used) + _expert_schedule(block_e, n_used)
    cur = lambda i, nu: jnp.minimum(i, nu[0] - 1)
    hbm = pl.BlockSpec(memory_space=pl.ANY)
    up_spec = pltpu.PrefetchScalarGridSpec(
        num_scalar_prefetch=5,
        grid=(n_blocks,),
        in_specs=[pl.BlockSpec((blk, dw), lambda i, be, nu, *_: (cur(i, nu), 0)), hbm, hbm],
        out_specs=pl.BlockSpec((blk, de), lambda i, be, nu, *_: (cur(i, nu), 0)),
        scratch_shapes=[pltpu.VMEM((2, d, de), F32), pltpu.VMEM((2, d, de), F32), pltpu.SemaphoreType.DMA((2, 2))],
    )
    hid = pl.pallas_call(
        _expert_up_kernel,
        grid_spec=up_spec,
        out_shape=jax.ShapeDtypeStruct((slots, de), BF16),
        compiler_params=_cparams(),
        name="expert_up",
    )(*sched, xs, wg, wu)
    down_spec = pltpu.PrefetchScalarGridSpec(
        num_scalar_prefetch=5,
        grid=(n_blocks,),
        in_specs=[pl.BlockSpec((blk, de), lambda i, be, nu, *_: (cur(i, nu), 0)), hbm],
        out_specs=pl.BlockSpec((blk, dw), lambda i, be, nu, *_: (cur(i, nu), 0)),
        scratch_shapes=[pltpu.VMEM((2, de, d), F32), pltpu.SemaphoreType.DMA((1, 2))],
    )
    return pl.pallas_call(
        _expert_down_kernel,
        grid_spec=down_spec,
        out_shape=jax.ShapeDtypeStruct((slots, dw), I32),
        compiler_params=_cparams(),
        name="expert_down",
    )(*sched, hid, wd)


def _final_kernel(dc_ref, dn_ref, x1_ref, sp_ref, gate_ref, ys_ref, gpost_ref, gt_ref, o_ref, buf, sem, *, tm, n_tiles):
    i = pl.program_id(0)
    slot = i % 2
    sub = 8

    def aligned(v):
        return v if isinstance(v, int) else pl.multiple_of(v, sub)

    def issue_rows(dref, s, j):
        base = aligned(j * sub)
        for tt in range(sub):
            for k in range(TOP_K):
                tile = buf.at[s, pl.ds(aligned(k * tm + base), sub), :]
                pltpu.make_async_copy(ys_ref.at[pl.ds(dref[(base + tt) * TOP_K + k], 1), :],
                                      tile.at[pl.ds(tt, 1), :],
                                      sem.at[s]).start(priority=k % 2)

    def reduce_rows(s, j):
        rs = pl.ds(aligned(j * sub), sub)
        g = gate_ref[rs, :]
        acc_lo, acc_hi = _unpack_halves(sp_ref[rs, :])
        for k in range(TOP_K):
            lo, hi = _unpack_halves(buf[s, pl.ds(aligned(k * tm + j * sub), sub), :])
            gk = g[:, k:k + 1]
            acc_lo = acc_lo + gk * lo
            acc_hi = acc_hi + gk * hi
        ffn = jnp.concatenate([acc_lo, acc_hi], axis=-1)
        o_ref[rs, :] = x1_ref[rs, :] + gt_ref[0] * _rms(ffn, gpost_ref[...])

    @pl.when(i == 0)
    def _():
        def first(j, c):
            issue_rows(dc_ref, 0, j)
            return c
        lax.fori_loop(0, tm // sub, first, 0)

    def wait_slot(s):
        pltpu.make_async_copy(ys_ref.at[pl.ds(0, TOP_K * tm), :], buf.at[s], sem.at[s]).wait()

    wait_slot(slot)

    for s in range(2):
        @pl.when(slot == s)
        def _():
            for j in range(tm // sub):
                issue_rows(dn_ref, 1 - s, j)
                reduce_rows(s, j)

    @pl.when(i == n_tiles - 1)
    def _():
        wait_slot(1 - slot)


def final_mix(x1, sp, gates, ys, dest_flat, g_post, gt, tiles_per_batch, tm=COMBINE_TM):
    n, d = x1.shape
    dw = sp.shape[1]
    n_e = gates.shape[1]
    n_tiles = n // tm
    return pl.pallas_call(
        functools.partial(_final_kernel, tm=tm, n_tiles=n_tiles),
        grid=(n_tiles,),
        in_specs=[pl.BlockSpec((tm * TOP_K,), lambda i: (i,), memory_space=pltpu.SMEM),
                  pl.BlockSpec((tm * TOP_K,), lambda i: (jnp.minimum(i + 1, n_tiles - 1),), memory_space=pltpu.SMEM),
                  pl.BlockSpec((tm, d), lambda i: (i, 0)),
                  pl.BlockSpec((tm, dw), lambda i: (i, 0)),
                  pl.BlockSpec((tm, n_e), lambda i: (i, 0)),
                  pl.BlockSpec(memory_space=pl.ANY),
                  pl.BlockSpec((1, d), lambda i: (0, 0)),
                  pl.BlockSpec((1, 1, d), lambda i: (i // tiles_per_batch, 0, 0))],
        out_specs=pl.BlockSpec((tm, d), lambda i: (i, 0)),
        out_shape=jax.ShapeDtypeStruct((n, d), F32),
        scratch_shapes=[pltpu.VMEM((2, TOP_K * tm, dw), I32), pltpu.SemaphoreType.DMA((2,))],
        compiler_params=_cparams(),
        name="final_mix",
    )(dest_flat, dest_flat, x1, sp, gates, ys, g_post.reshape(1, d), gt)


def kernel(x, c, ctx, c_ctx, w_ada, b_ada, g_pre_mix, g_post_mix, g_pre_ffn, g_post_ffn, w_in, rpb, ssm_a_re, ssm_a_im, ssm_log_dt, ssm_b_re, ssm_b_im, ssm_c_re, ssm_c_im, ssm_d, w_glu, b_glu, g_attn_out, g_ssm_out, w_out, w_router, b_router, w_exp_gate, w_exp_up, w_exp_down, w_sh_gate, w_sh_up, w_sh_down):
    b, l, d = x.shape
    lc = ctx.shape[1]
    assert w_ada.shape[0] == 1 and b + 1 <= 8
    n_in = w_in.shape[-1]
    d_ssm = w_glu.shape[-1]
    d_attn = d - d_ssm
    n_heads = d_attn // HEAD_DIM
    n = b * l

    c8 = jnp.concatenate([c, c_ctx[None], jnp.zeros((8 - b - 1, d), F32)], axis=0)
    mod = ada_mod(c8, w_ada[0], b_ada[0]).reshape(8, 6, 1, d)
    sh_m, sc_m, gt_m, sh_f, sc_f, gt_f = [mod[:b, j] for j in range(6)]
    csh_m, csc_m = mod[b:b + 1, 0], mod[b:b + 1, 1]

    w_in_b = w_in[0].astype(BF16)
    tn = PROJ_TN
    p_lat = mod_proj(x, g_pre_mix[0], sc_m, sh_m, w_in_b, 0, n_in, 512, tn)
    p_ctx = mod_proj(ctx.reshape(1, b * lc, d), g_pre_mix[0], csc_m, csh_m, w_in_b,
                     d_attn // tn, n_in - d_attn, 512, tn).reshape(b, lc, n_in - d_attn)

    attn = neighborhood_attention(p_lat, p_ctx, rpb[0], n_heads)

    mats = _ssm_mats(ssm_a_re[0], ssm_a_im[0], ssm_log_dt[0], ssm_b_re[0], ssm_b_im[0],
                     ssm_c_re[0], ssm_c_im[0], ssm_d[0])
    y = s5_mixer(p_lat[..., 3 * d_attn:], p_ctx[..., 2 * d_attn:], mats)
    ssm = s5_glu(y.reshape(n, d_ssm), w_glu[0].astype(BF16), b_glu[0])

    mix = merge_proj(attn.reshape(n, d_attn), ssm, g_attn_out[0], g_ssm_out[0], w_out[0].astype(BF16))
    x1, hp, idx, gates, mask = post_mix(x, mix.reshape(b, l, d), g_post_mix[0], gt_m, g_pre_ffn[0], sc_f, sh_f,
                                        w_router[0].astype(BF16), b_router[0])

    n_e = w_router.shape[-1]
    m = n * TOP_K
    idx = idx.reshape(n, n_e)
    rank, counts = route_ranks(mask.reshape(n, n_e), idx)
    padded = (counts + EXPERT_BLK - 1) // EXPERT_BLK * EXPERT_BLK
    pad_end = jnp.cumsum(padded).astype(I32)
    dest = route_dest(rank, idx, pad_end - padded)
    n_blocks = m // EXPERT_BLK + n_e
    slots = n_blocks * EXPERT_BLK
    block_e = jnp.minimum(jnp.searchsorted(pad_end, jnp.arange(n_blocks) * EXPERT_BLK, side='right'),
                          n_e - 1).astype(I32)
    n_used = (pad_end[-1] // EXPERT_BLK).astype(I32).reshape(1)

    hp2 = hp.reshape(n, d // 2)
    xs, sp = moe_dispatch(hp2, dest, pad_end, padded.astype(I32), slots,
                          jnp.concatenate([w_sh_gate[0], w_sh_up[0]], axis=-1).astype(BF16), w_sh_down[0].astype(BF16))
    ys = expert_ffn(xs, block_e, n_used, w_exp_gate[0], w_exp_up[0], w_exp_down[0], EXPERT_BLK)
    out = final_mix(x1.reshape(n, d), sp, gates.reshape(n, n_e), ys, dest, g_post_ffn[0], gt_f, l // COMBINE_TM)
    return out.reshape(b, l, d)
```

```python
import functools

import numpy as np
import jax
import jax.numpy as jnp
from jax import lax
from jax.experimental import pallas as pl
from jax.experimental.pallas import tpu as pltpu

F32 = jnp.float32
BF16 = jnp.bfloat16
I32 = jnp.int32

GRID_W = 64
HEAD_DIM = 128
WIN_ROWS = 8
WIN_COLS = 16
ROPE_THETA = 10000.0
SSM_GROUP_CH = 16
SSM_STATE = 64
TOP_K = 8
ROUTED_SCALE = 2.5
EPS = 1e-6
NEG_INF = -1e30

Q_ROWS = 4
BAND_ROWS = Q_ROWS + WIN_ROWS - 1
VMEM_LIMIT = 56 * 1024 * 1024
EXPERT_BLK = 512
PROJ_TN = 1024
DISPATCH_TM = 512
COMBINE_TM = 128


def _cparams():
    return pltpu.CompilerParams(vmem_limit_bytes=VMEM_LIMIT)


def _rms(x, g):
    return x * lax.rsqrt(jnp.mean(x * x, axis=-1, keepdims=True) + EPS) * g


def _pack_halves(lo, hi):
    lo_bits = lax.bitcast_convert_type(lo.astype(BF16).astype(F32), I32)
    hi_bits = lax.bitcast_convert_type(hi.astype(BF16).astype(F32), I32)
    return (hi_bits & jnp.int32(-65536)) | lax.shift_right_logical(lo_bits, jnp.int32(16))


def _unpack_halves(w):
    lo = lax.bitcast_convert_type(lax.shift_left(w, jnp.int32(16)), F32)
    hi = lax.bitcast_convert_type(w & jnp.int32(-65536), F32)
    return lo, hi


def _ada_kernel(c_ref, w_ref, b_ref, o_ref):
    a = jax.nn.silu(c_ref[...]).astype(BF16)
    o_ref[...] = jnp.dot(a, w_ref[...].astype(BF16), preferred_element_type=F32) + b_ref[...]


def ada_mod(c8, w_ada, b_ada):
    d, n = w_ada.shape
    tn = 512
    return pl.pallas_call(
        _ada_kernel,
        grid=(n // tn,),
        in_specs=[pl.BlockSpec((8, d), lambda j: (0, 0)),
                  pl.BlockSpec((d, tn), lambda j: (0, j)),
                  pl.BlockSpec((1, tn), lambda j: (0, j))],
        out_specs=pl.BlockSpec((8, tn), lambda j: (0, j)),
        out_shape=jax.ShapeDtypeStruct((8, n), F32),
        compiler_params=_cparams(),
        name="ada_mod",
    )(c8, w_ada, b_ada.reshape(1, n))


def _modproj_kernel(x_ref, g_ref, sc_ref, sh_ref, w_ref, o_ref, h_scr):
    @pl.when(pl.program_id(2) == 0)
    def _():
        h = _rms(x_ref[0], g_ref[...]) * (1.0 + sc_ref[0]) + sh_ref[0]
        h_scr[...] = h.astype(BF16)

    o_ref[0] = jnp.dot(h_scr[...], w_ref[...], preferred_element_type=F32).astype(o_ref.dtype)


def mod_proj(x, g, scale, shift, w_bf16, col_blk_off, n_out, tm, tn):
    b, l, d = x.shape
    return pl.pallas_call(
        _modproj_kernel,
        grid=(b, l // tm, n_out // tn),
        in_specs=[pl.BlockSpec((1, tm, d), lambda bi, i, j: (bi, i, 0)),
                  pl.BlockSpec((1, d), lambda bi, i, j: (0, 0)),
                  pl.BlockSpec((1, 1, d), lambda bi, i, j: (bi, 0, 0)),
                  pl.BlockSpec((1, 1, d), lambda bi, i, j: (bi, 0, 0)),
                  pl.BlockSpec((d, tn), lambda bi, i, j: (0, j + col_blk_off))],
        out_specs=pl.BlockSpec((1, tm, tn), lambda bi, i, j: (bi, i, j)),
        out_shape=jax.ShapeDtypeStruct((b, l, n_out), BF16),
        scratch_shapes=[pltpu.VMEM((tm, d), BF16)],
        compiler_params=_cparams(),
        name="mod_proj",
    )(x, g.reshape(1, d), scale, shift, w_bf16)


def _rope_tables(l):
    half = HEAD_DIM // 2
    quarter = half // 2
    inv_freq = 1.0 / (ROPE_THETA ** (jnp.arange(0, half, 2, dtype=F32) / half))
    rows = (jnp.arange(l) // GRID_W).astype(F32)
    cols = (jnp.arange(l) % GRID_W).astype(F32)

    def cs(pos):
        ang = pos[:, None] * inv_freq[None, :]
        return jnp.cos(ang), jnp.sin(ang)

    cr, sr = cs(rows)
    cc, sc = cs(cols)
    zero = jnp.zeros((l, quarter), F32)
    cos = jnp.concatenate([cr, cr, cc, cc], axis=-1)
    s_lo = jnp.concatenate([-sr, zero, -sc, zero], axis=-1)
    s_hi = jnp.concatenate([zero, sr, zero, sc], axis=-1)
    return cos, s_lo, s_hi


def _attn_bias_plan(rows_n):
    kh = min(WIN_ROWS, rows_n)
    plan = []
    for r0, sb in [(0, 0), (Q_ROWS, 0), (rows_n - Q_ROWS, rows_n - BAND_ROWS)]:
        qrow = r0 + np.arange(Q_ROWS)
        krow = sb + np.arange(BAND_ROWS)
        rs = np.clip(qrow - kh // 2, 0, rows_n - kh)
        ok = (krow[None, :] >= rs[:, None]) & (krow[None, :] < rs[:, None] + kh)
        ro = krow[None, :] - qrow[:, None] + (WIN_ROWS - 1)
        plan.append(np.where(ok, ro, 2 * WIN_ROWS - 1))
    return np.stack(plan)


def _attn_toe(rpb):
    col = np.arange(GRID_W)
    cstart = np.clip(col - WIN_COLS // 2, 0, GRID_W - WIN_COLS)
    ok_c = (col[None, :] >= cstart[:, None]) & (col[None, :] < cstart[:, None] + WIN_COLS)
    co = np.clip(col[None, :] - col[:, None], -(WIN_COLS - 1), WIN_COLS - 1) + (WIN_COLS - 1)
    onehot = (co[None] == np.arange(2 * WIN_COLS - 1)[:, None, None]).astype(np.float32)
    toe = jnp.einsum('hrc,cqk->hrqk', rpb.astype(F32), onehot, precision=lax.Precision.HIGHEST)
    toe = jnp.where(ok_c[None, None], toe, NEG_INF)
    return jnp.concatenate([toe, jnp.full((rpb.shape[0], 1, GRID_W, GRID_W), NEG_INF, F32)], axis=1)


def _attn_kernel(q_ref, k_ref, v_ref, kc_ref, vc_ref, toe_ref, cos_ref, slo_ref, shi_ref, o_ref,
                 qr_scr, qs_scr, kr_scr, bias_scr, *, n_blk, rows_n, plan):
    quarter = HEAD_DIM // 4
    qn = Q_ROWS * GRID_W
    kn = BAND_ROWS * GRID_W
    scale = HEAD_DIM ** -0.5

    def rope(x, sl):
        return (x.astype(F32) * cos_ref[sl, :] + pltpu.roll(x, HEAD_DIM - quarter, 1).astype(F32) * slo_ref[sl, :]
                + pltpu.roll(x, quarter, 1).astype(F32) * shi_ref[sl, :])

    def rope_body(i, _):
        sl = pl.ds(pl.multiple_of(i * qn, qn), qn)
        q = q_ref[0, sl, :]
        qs_scr[sl, :] = (q.astype(F32) * scale).astype(BF16)
        qr_scr[sl, :] = (rope(q, sl) * scale).astype(BF16)
        kr_scr[sl, :] = rope(k_ref[0, sl, :], sl).astype(BF16)
        return 0

    lax.fori_loop(0, n_blk, rope_body, 0)

    @pl.when(pl.program_id(1) == 0)
    def _():
        for case in range(plan.shape[0]):
            for qi in range(Q_ROWS):
                for kj in range(BAND_ROWS):
                    bias_scr[case, qi * GRID_W:(qi + 1) * GRID_W, kj * GRID_W:(kj + 1) * GRID_W] = (
                        toe_ref[0, int(plan[case, qi, kj])])

    kc = kc_ref[0]
    vc = vc_ref[0]
    nt = (((1,), (1,)), ((), ()))

    def body(i, _):
        sb = jnp.clip(i * Q_ROWS - WIN_ROWS // 2, 0, rows_n - BAND_ROWS)
        ks = pl.ds(pl.multiple_of(sb * GRID_W, GRID_W), kn)
        qs = pl.ds(pl.multiple_of(i * qn, qn), qn)
        case = jnp.where(i == 0, 0, jnp.where(i == n_blk - 1, 2, 1))
        s = lax.dot_general(qr_scr[qs, :], kr_scr[ks, :], nt, preferred_element_type=F32) + bias_scr[case]
        sc = lax.dot_general(qs_scr[qs, :], kc, nt, preferred_element_type=F32)
        m = jnp.maximum(jnp.max(s, axis=-1, keepdims=True), jnp.max(sc, axis=-1, keepdims=True))
        p = jnp.exp(s - m)
        pc = jnp.exp(sc - m)
        den = jnp.sum(p, axis=-1, keepdims=True) + jnp.sum(pc, axis=-1, keepdims=True)
        o = (jnp.dot(p.astype(BF16), v_ref[0, ks, :], preferred_element_type=F32)
             + jnp.dot(pc.astype(BF16), vc, preferred_element_type=F32))
        o_ref[0, qs, :] = (o / den).astype(o_ref.dtype)
        return 0

    lax.fori_loop(0, n_blk, body, 0, unroll=4)


def neighborhood_attention(p_lat, p_ctx, rpb, n_heads):
    b, l, _ = p_lat.shape
    lc = p_ctx.shape[1]
    rows_n = l // GRID_W
    assert rows_n % Q_ROWS == 0 and rows_n >= BAND_ROWS + Q_ROWS
    n_blk = rows_n // Q_ROWS
    plan = _attn_bias_plan(rows_n)
    cos, s_lo, s_hi = _rope_tables(l)
    qn, kn = Q_ROWS * GRID_W, BAND_ROWS * GRID_W
    h = n_heads
    tab = pl.BlockSpec((l, HEAD_DIM), lambda hi, bi: (0, 0))
    return pl.pallas_call(
        functools.partial(_attn_kernel, n_blk=n_blk, rows_n=rows_n, plan=plan),
        grid=(h, b),
        in_specs=[pl.BlockSpec((1, l, HEAD_DIM), lambda hi, bi: (bi, 0, hi)),
                  pl.BlockSpec((1, l, HEAD_DIM), lambda hi, bi: (bi, 0, hi + h)),
                  pl.BlockSpec((1, l, HEAD_DIM), lambda hi, bi: (bi, 0, hi + 2 * h)),
                  pl.BlockSpec((1, lc, HEAD_DIM), lambda hi, bi: (bi, 0, hi)),
                  pl.BlockSpec((1, lc, HEAD_DIM), lambda hi, bi: (bi, 0, hi + h)),
                  pl.BlockSpec((1, 2 * WIN_ROWS, GRID_W, GRID_W), lambda hi, bi: (hi, 0, 0, 0)),
                  tab, tab, tab],
        out_specs=pl.BlockSpec((1, l, HEAD_DIM), lambda hi, bi: (bi, 0, hi)),
        out_shape=jax.ShapeDtypeStruct((b, l, h * HEAD_DIM), BF16),
        scratch_shapes=[pltpu.VMEM((l, HEAD_DIM), BF16)] * 3 + [pltpu.VMEM((plan.shape[0], qn, kn), F32)],
        compiler_params=_cparams(),
        name="nbr_attn",
    )(p_lat, p_lat, p_lat, p_ctx, p_ctx, _attn_toe(rpb), cos, s_lo, s_hi)


SSM_CHUNK = 16
SSM_ROWS = 8
SSM_PACK = 8


def _ssm_mats(a_re, a_im, log_dt, b_re, b_im, c_re, c_im, d_skip):
    t = SSM_CHUNK
    hi = lax.Precision.HIGHEST
    a = lax.complex(a_re.astype(F32), a_im.astype(F32))
    dta = jnp.exp(log_dt.astype(F32))[..., None] * a
    a_bar = jnp.exp(dta)
    b_bar = ((a_bar - 1.0) / a)[..., None] * lax.complex(b_re.astype(F32), b_im.astype(F32))
    cm = lax.complex(c_re.astype(F32), c_im.astype(F32))
    k = jnp.arange(t + 1, dtype=F32)
    ap = jnp.exp(dta[..., None] * k)
    g, p, n = a.shape[1], a.shape[2], b_re.shape[-1]
    kern = jnp.einsum('dgnp,dgpl,dgpm->dglnm', cm, ap[..., :t], b_bar, precision=hi).real
    s_i = np.arange(t)[:, None]
    t_i = np.arange(t)[None, :]
    lag = np.arange(t)[:, None, None]
    oh = np.stack([(t_i - s_i)[None] == lag, (s_i - t_i)[None] == lag]).astype(np.float32)
    eye = jnp.eye(n, dtype=F32)
    m_mat = (jnp.einsum('dlst,dglnm->gsmtn', oh, kern, precision=hi)
             + jnp.einsum('st,gn,nm->gsmtn', np.eye(t, dtype=np.float32), d_skip.astype(F32), eye, precision=hi))
    m_mat = m_mat.reshape(g, t * n, t * n)
    wf = ap[0][:, :, t - 1::-1][..., :t, None] * b_bar[0][:, :, None, :]
    wb = ap[1][:, :, :t, None] * b_bar[1][:, :, None, :]
    to_rows = lambda z: jnp.transpose(z, (0, 2, 3, 1)).reshape(g, t * n, p)
    w_mat = jnp.concatenate([to_rows(wf.real), to_rows(wb.real), to_rows(wf.imag), to_rows(wb.imag)], axis=-1)
    zf = jnp.transpose(cm[0], (0, 2, 1))[:, :, None, :] * ap[0][:, :, 1:t + 1, None]
    zb = jnp.transpose(cm[1], (0, 2, 1))[:, :, None, :] * ap[1][:, :, t:0:-1, None]
    flat = lambda z: z.reshape(g, p, t * n)
    r_mat = jnp.concatenate([flat(zf.real), flat(zb.real), -flat(zf.imag), -flat(zb.imag)], axis=1)
    a_t = ap[..., t]
    a_vec = jnp.stack([jnp.concatenate([a_t[0].real, a_t[1].real], -1),
                       jnp.concatenate([a_t[0].imag, a_t[1].imag], -1)], axis=1)
    a_vec = jnp.pad(a_vec, ((0, 0), (0, SSM_ROWS - 2), (0, 0)))
    return m_mat.astype(BF16), w_mat.astype(BF16), r_mat.astype(BF16), a_vec


def _s5_kernel(u_ref, sel_ref, selt_ref, m_ref, w_ref, r_ref, a_ref, y_ref, v_scr, ent_scr, *, n_ctx, n_chunks):
    u = jnp.dot(u_ref[0], sel_ref[0], preferred_element_type=F32).astype(BF16)
    p2 = a_ref.shape[-1]
    p = p2 // 2
    v_scr[...] = jnp.dot(u, w_ref[0], preferred_element_type=F32)
    a_re = jnp.broadcast_to(a_ref[0, 0:1, :], (SSM_ROWS, p2))
    a_im = jnp.broadcast_to(a_ref[0, 1:2, :], (SSM_ROWS, p2))
    is_fwd = lax.broadcasted_iota(I32, (SSM_ROWS, p2), 1) < p

    def step(j, carry):
        s_re, s_im = carry
        cb = jnp.where(j < n_ctx, n_ctx - 1 - j, n_chunks + n_ctx - 1 - j)
        rf = pl.ds(pl.multiple_of(j * SSM_ROWS, SSM_ROWS), SSM_ROWS)
        rb = pl.ds(pl.multiple_of(cb * SSM_ROWS, SSM_ROWS), SSM_ROWS)
        ent_scr[rf, 0:p] = s_re[:, :p]
        ent_scr[rb, p:p2] = s_re[:, p:]
        ent_scr[rf, p2:p2 + p] = s_im[:, :p]
        ent_scr[rb, p2 + p:] = s_im[:, p:]
        in_re = jnp.where(is_fwd, v_scr[rf, :p2], v_scr[rb, :p2])
        in_im = jnp.where(is_fwd, v_scr[rf, p2:], v_scr[rb, p2:])
        return a_re * s_re - a_im * s_im + in_re, a_re * s_im + a_im * s_re + in_im

    zero = jnp.zeros((SSM_ROWS, p2), F32)
    lax.fori_loop(0, n_chunks, step, (zero, zero))
    y = (jnp.dot(u, m_ref[0], preferred_element_type=F32)
         + jnp.dot(ent_scr[...].astype(BF16), r_ref[0], preferred_element_type=F32)).astype(BF16)
    def part(c):
        return jnp.dot(y, selt_ref[0, :, c:c + PACK_CHUNK], preferred_element_type=F32).astype(y_ref.dtype)

    @pl.when(pl.program_id(1) == 0)
    def _():
        for c in range(0, y_ref.shape[-1], PACK_CHUNK):
            y_ref[0, :, c:c + PACK_CHUNK] = part(c)

    @pl.when(pl.program_id(1) > 0)
    def _():
        for c in range(0, y_ref.shape[-1], PACK_CHUNK):
            y_ref[0, :, c:c + PACK_CHUNK] = y_ref[0, :, c:c + PACK_CHUNK] + part(c)


def s5_mixer(u_lat, u_ctx, mats):
    m_mat, w_mat, r_mat, a_vec = mats
    b, l, dch = u_lat.shape
    lc = u_ctx.shape[1]
    g = m_mat.shape[0]
    n = dch // g
    t = SSM_CHUNK
    tn = t * n
    gp = SSM_PACK
    packs, lanes = g // gp, gp * n
    n_ctx, n_chunks = lc // t, (lc + l) // t
    assert b <= SSM_ROWS and lc % t == 0 and l % t == 0 and g % gp == 0
    z = jnp.concatenate([u_ctx, u_lat], axis=1).reshape(b, n_chunks, t, packs, lanes)
    z = jnp.pad(jnp.transpose(z, (3, 1, 0, 2, 4)), ((0, 0), (0, 0), (0, SSM_ROWS - b), (0, 0), (0, 0)))
    rows = n_chunks * SSM_ROWS
    z = z.reshape(packs, rows, t * lanes)
    ri = jnp.arange(t * lanes)
    ci = jnp.arange(tn)
    same = (ri[:, None] // lanes == ci[None, :] // n) & (ri[:, None] % n == ci[None, :] % n)
    sel = (same[None] & ((ri[None, :, None] % lanes) // n == jnp.arange(gp)[:, None, None])).astype(BF16)
    selt = jnp.transpose(sel, (0, 2, 1))
    grp = lambda k: pl.BlockSpec((1, k, tn), lambda pi, qi: (pi * gp + qi, 0, 0))
    pack_blk = pl.BlockSpec((1, rows, t * lanes), lambda pi, qi: (pi, 0, 0))
    y = pl.pallas_call(
        functools.partial(_s5_kernel, n_ctx=n_ctx, n_chunks=n_chunks),
        grid=(packs, gp),
        in_specs=[pack_blk,
                  pl.BlockSpec((1, t * lanes, tn), lambda pi, qi: (qi, 0, 0)),
                  pl.BlockSpec((1, tn, t * lanes), lambda pi, qi: (qi, 0, 0)),
                  grp(tn), grp(tn), grp(r_mat.shape[1]),
                  pl.BlockSpec((1, SSM_ROWS, a_vec.shape[-1]), lambda pi, qi: (pi * gp + qi, 0, 0))],
        out_specs=pack_blk,
        out_shape=jax.ShapeDtypeStruct((packs, rows, t * lanes), BF16),
        scratch_shapes=[pltpu.VMEM((rows, w_mat.shape[-1]), F32), pltpu.VMEM((rows, r_mat.shape[1]), F32)],
        compiler_params=_cparams(),
        name="s5_mixer",
    )(z, sel, selt, m_mat, w_mat, r_mat, a_vec)
    y = y.reshape(packs, n_chunks, SSM_ROWS, t, lanes)[:, n_ctx:, :b]
    return jnp.transpose(y, (2, 1, 3, 0, 4)).reshape(b, l, dch)


def _glu_kernel(y_ref, w_ref, b_ref, o_ref):
    z = jax.nn.gelu(y_ref[...].astype(F32))
    gate = jax.nn.sigmoid(jnp.dot(z.astype(BF16), w_ref[...], preferred_element_type=F32) + b_ref[...])
    o_ref[...] = (z * gate).astype(o_ref.dtype)


def s5_glu(y, w_bf16, b_glu, tm=512):
    n, d = y.shape
    return pl.pallas_call(
        _glu_kernel,
        grid=(n // tm,),
        in_specs=[pl.BlockSpec((tm, d), lambda i: (i, 0)),
                  pl.BlockSpec((d, d), lambda i: (0, 0)),
                  pl.BlockSpec((1, d), lambda i: (0, 0))],
        out_specs=pl.BlockSpec((tm, d), lambda i: (i, 0)),
        out_shape=jax.ShapeDtypeStruct((n, d), BF16),
        compiler_params=_cparams(),
        name="s5_glu",
    )(y, w_bf16, b_glu.reshape(1, d))


def _merge_kernel(a_ref, s_ref, ga_ref, gs_ref, w_ref, o_ref, h_scr):
    da = a_ref.shape[-1]

    @pl.when(pl.program_id(1) == 0)
    def _():
        h_scr[:, :da] = _rms(a_ref[...].astype(F32), ga_ref[...]).astype(BF16)
        h_scr[:, da:] = _rms(s_ref[...].astype(F32), gs_ref[...]).astype(BF16)

    o_ref[...] = jnp.dot(h_scr[...], w_ref[...], preferred_element_type=F32)


def merge_proj(attn, ssm, g_attn, g_ssm, w_bf16, tm=512, tn=1024):
    n, da = attn.shape
    ds = ssm.shape[1]
    d, dout = w_bf16.shape
    return pl.pallas_call(
        _merge_kernel,
        grid=(n // tm, dout // tn),
        in_specs=[pl.BlockSpec((tm, da), lambda i, j: (i, 0)),
                  pl.BlockSpec((tm, ds), lambda i, j: (i, 0)),
                  pl.BlockSpec((1, da), lambda i, j: (0, 0)),
                  pl.BlockSpec((1, ds), lambda i, j: (0, 0)),
                  pl.BlockSpec((d, tn), lambda i, j: (0, j))],
        out_specs=pl.BlockSpec((tm, tn), lambda i, j: (i, j)),
        out_shape=jax.ShapeDtypeStruct((n, dout), F32),
        scratch_shapes=[pltpu.VMEM((tm, d), BF16)],
        compiler_params=_cparams(),
        name="merge_proj",
    )(attn, ssm, g_attn.reshape(1, da), g_ssm.reshape(1, ds), w_bf16)


def _post_mix_kernel(x_ref, mix_ref, gpost_ref, gt_ref, gpre_ref, sc_ref, sh_ref, wr_ref, br_ref,
                     x1_ref, hp_ref, idx_ref, gate_ref, mask_ref):
    x1 = x_ref[0] + gt_ref[0] * _rms(mix_ref[0], gpost_ref[...])
    x1_ref[0] = x1
    h = _rms(x1, gpre_ref[...]) * (1.0 + sc_ref[0]) + sh_ref[0]
    half = h.shape[-1] // 2
    hp_ref[0] = _pack_halves(h[:, :half], h[:, half:])
    scores = jax.nn.sigmoid(jnp.dot(h.astype(BF16), wr_ref[...], preferred_element_type=F32))
    n_e = scores.shape[-1]
    lane = lax.broadcasted_iota(I32, scores.shape, 1)
    biased = scores + br_ref[...]
    idx_out = jnp.zeros(scores.shape, I32)
    sel_out = jnp.zeros(scores.shape, F32)
    mask = jnp.zeros(scores.shape, jnp.bool_)
    for k in range(TOP_K):
        m = jnp.max(biased, axis=-1, keepdims=True)
        ik = jnp.min(jnp.where(biased == m, lane, n_e), axis=-1, keepdims=True)
        hit = lane == ik
        sel_k = jnp.sum(jnp.where(hit, scores, 0.0), axis=-1, keepdims=True)
        idx_out = jnp.where(lane == k, ik, idx_out)
        sel_out = jnp.where(lane == k, sel_k, sel_out)
        mask = jnp.logical_or(mask, hit)
        biased = jnp.where(hit, -jnp.inf, biased)
    idx_ref[0] = idx_out
    gate_ref[0] = sel_out / jnp.sum(sel_out, axis=-1, keepdims=True) * ROUTED_SCALE
    mask_ref[0] = mask.astype(BF16)


def post_mix(x, mix, g_post, gt, g_pre, scale, shift, wr_bf16, b_router, tm=256):
    b, l, d = x.shape
    n_e = wr_bf16.shape[1]
    row = pl.BlockSpec((1, tm, d), lambda bi, i: (bi, i, 0))
    prow = pl.BlockSpec((1, tm, d // 2), lambda bi, i: (bi, i, 0))
    vec = pl.BlockSpec((1, d), lambda bi, i: (0, 0))
    bvec = pl.BlockSpec((1, 1, d), lambda bi, i: (bi, 0, 0))
    small = pl.BlockSpec((1, tm, n_e), lambda bi, i: (bi, i, 0))
    return pl.pallas_call(
        _post_mix_kernel,
        grid=(b, l // tm),
        in_specs=[row, row, vec, bvec, vec, bvec, bvec,
                  pl.BlockSpec((d, n_e), lambda bi, i: (0, 0)),
                  pl.BlockSpec((1, n_e), lambda bi, i: (0, 0))],
        out_specs=[row, prow, small, small, small],
        out_shape=[jax.ShapeDtypeStruct((b, l, d), F32),
                   jax.ShapeDtypeStruct((b, l, d // 2), I32),
                   jax.ShapeDtypeStruct((b, l, n_e), I32),
                   jax.ShapeDtypeStruct((b, l, n_e), F32),
                   jax.ShapeDtypeStruct((b, l, n_e), BF16)],
        compiler_params=_cparams(),
        name="post_mix",
    )(x, mix, g_post.reshape(1, d), gt, g_pre.reshape(1, d), scale, shift, wr_bf16, b_router.reshape(1, n_e))


def _rank_kernel(mask_ref, idx_ref, rank_ref, cnt_ref, carry):
    @pl.when(pl.program_id(0) == 0)
    def _():
        carry[...] = jnp.zeros_like(carry)

    m = mask_ref[...]
    tm = m.shape[0]
    earlier = (lax.broadcasted_iota(I32, (tm, tm), 1) < lax.broadcasted_iota(I32, (tm, tm), 0)).astype(BF16)
    excl = jnp.dot(earlier, m, preferred_element_type=F32) + carry[0:1, :]
    lane = lax.broadcasted_iota(I32, m.shape, 1)
    idx = idx_ref[...]
    out = jnp.zeros(m.shape, F32)
    for k in range(TOP_K):
        rk = jnp.sum(jnp.where(lane == idx[:, k:k + 1], excl, 0.0), axis=-1, keepdims=True)
        out = jnp.where(lane == k, rk, out)
    rank_ref[...] = out.astype(I32)
    carry[0:1, :] = carry[0:1, :] + jnp.sum(m.astype(F32), axis=0, keepdims=True)
    cnt_ref[...] = carry[...]


def route_ranks(mask, idx, tm=512):
    n, n_e = mask.shape
    blk = pl.BlockSpec((tm, n_e), lambda i: (i, 0))
    rank, cnt = pl.pallas_call(
        _rank_kernel,
        grid=(n // tm,),
        in_specs=[blk, blk],
        out_specs=[blk, pl.BlockSpec((8, n_e), lambda i: (0, 0))],
        out_shape=[jax.ShapeDtypeStruct((n, n_e), I32), jax.ShapeDtypeStruct((8, n_e), F32)],
        scratch_shapes=[pltpu.VMEM((8, n_e), F32)],
        compiler_params=_cparams(),
        name="route_ranks",
    )(mask, idx)
    return rank, cnt[0].astype(I32)


def _dest_kernel(rank_ref, idx_ref, ps_ref, dest_ref):
    idx = idx_ref[...]
    lane = lax.broadcasted_iota(I32, idx.shape, 1)
    start = jnp.zeros(idx.shape, F32)
    for k in range(TOP_K):
        sk = jnp.sum(jnp.where(lane == idx[:, k:k + 1], ps_ref[...], 0.0), axis=-1, keepdims=True)
        start = jnp.where(lane == k, sk, start)
    dest_ref[...] = rank_ref[...] + start.astype(I32)


def route_dest(rank, idx, pad_start, tm=512):
    n, n_e = rank.shape
    blk = pl.BlockSpec((tm, n_e), lambda i: (i, 0))
    dest = pl.pallas_call(
        _dest_kernel,
        grid=(n // tm,),
        in_specs=[blk, blk, pl.BlockSpec((1, n_e), lambda i: (0, 0))],
        out_specs=blk,
        out_shape=jax.ShapeDtypeStruct((n, n_e), I32),
        compiler_params=_cparams(),
        name="route_dest",
    )(rank, idx, pad_start.astype(F32).reshape(1, n_e))
    return dest[:, :TOP_K].reshape(n * TOP_K)


def _dispatch_kernel(pe_ref, pd_ref, dest_ref, h_ref, wgu_ref, wd_ref, xs_ref, sp_ref, z_scr, sem, *, n_e, tm, blk):
    def zero_copy(e, rows):
        start = pl.multiple_of(pe_ref[e] - rows, rows)
        return pltpu.make_async_copy(z_scr.at[pl.ds(0, rows), :], xs_ref.at[pl.ds(start, rows), :], sem)

    def for_padded(e, fn):
        @pl.when(pd_ref[e] > blk // 2)
        def _():
            fn(zero_copy(e, blk))

        @pl.when(jnp.logical_and(pd_ref[e] > 0, pd_ref[e] <= blk // 2))
        def _():
            fn(zero_copy(e, blk // 2))

    @pl.when(pl.program_id(0) == 0)
    def _():
        z_scr[...] = jnp.zeros_like(z_scr)

        def start(e, _):
            for_padded(e, lambda c: c.start())
            return 0

        def wait(e, _):
            for_padded(e, lambda c: c.wait())
            return 0

        lax.fori_loop(0, n_e, start, 0)
        lax.fori_loop(0, n_e, wait, 0)

    sub = 8
    half = h_ref.shape[-1]
    n_ch = half // PACK_CHUNK
    rows_ph = tm // (2 * n_ch)

    def issue_rows(r0):
        for g in range(0, rows_ph, sub):
            tile = h_ref.at[pl.ds(r0 + g, sub), :]
            for tt in range(sub):
                for k in range(TOP_K):
                    pltpu.make_async_copy(tile.at[pl.ds(tt, 1), :],
                                          xs_ref.at[pl.ds(dest_ref[(r0 + g + tt) * TOP_K + k], 1), :],
                                          sem).start(priority=k % 2)

    acc = None
    for c in range(n_ch):
        issue_rows(c * rows_ph)
        cl = slice(c * PACK_CHUNK, (c + 1) * PACK_CHUNK)
        ch = slice(half + c * PACK_CHUNK, half + (c + 1) * PACK_CHUNK)
        lo, hi = _unpack_halves(h_ref[:, cl])
        part = (jnp.dot(lo.astype(BF16), wgu_ref[cl, :], preferred_element_type=F32)
                + jnp.dot(hi.astype(BF16), wgu_ref[ch, :], preferred_element_type=F32))
        acc = part if acc is None else acc + part
    de = acc.shape[-1] // 2
    hid = (jax.nn.silu(acc[:, :de]) * acc[:, de:]).astype(BF16)
    for c in range(n_ch):
        issue_rows((n_ch + c) * rows_ph)
        cl = slice(c * PACK_CHUNK, (c + 1) * PACK_CHUNK)
        ch = slice(half + c * PACK_CHUNK, half + (c + 1) * PACK_CHUNK)
        sp_ref[:, cl] = _pack_halves(jnp.dot(hid, wd_ref[:, cl], preferred_element_type=F32),
                                     jnp.dot(hid, wd_ref[:, ch], preferred_element_type=F32))
    rows = tm * TOP_K
    pltpu.make_async_copy(xs_ref.at[pl.ds(0, rows), :], xs_ref.at[pl.ds(0, rows), :], sem).wait()


def moe_dispatch(hp, dest_flat, pad_end, pad_rows, slots, wsgu, wsd, tm=DISPATCH_TM):
    n, dw = hp.shape
    n_e = pad_end.shape[0]
    d, de2 = wsgu.shape
    row = pl.BlockSpec((tm, dw), lambda i, pe, pd: (i, 0))
    grid_spec = pltpu.PrefetchScalarGridSpec(
        num_scalar_prefetch=2,
        grid=(n // tm,),
        in_specs=[pl.BlockSpec((tm * TOP_K,), lambda i, pe, pd: (i,), memory_space=pltpu.SMEM),
                  row,
                  pl.BlockSpec((d, de2), lambda i, pe, pd: (0, 0)),
                  pl.BlockSpec((de2 // 2, d), lambda i, pe, pd: (0, 0))],
        out_specs=[pl.BlockSpec(memory_space=pl.ANY), row],
        scratch_shapes=[pltpu.VMEM((EXPERT_BLK, dw), I32), pltpu.SemaphoreType.DMA(())],
    )
    return pl.pallas_call(
        functools.partial(_dispatch_kernel, n_e=n_e, tm=tm, blk=EXPERT_BLK),
        grid_spec=grid_spec,
        out_shape=[jax.ShapeDtypeStruct((slots, dw), I32), jax.ShapeDtypeStruct((n, dw), I32)],
        compiler_params=_cparams(),
        name="moe_dispatch",
    )(pad_end, pad_rows, dest_flat, hp, wsgu, wsd)


PACK_CHUNK = 512


def _swiglu_hidden(xp_ref, wgu):
    half = xp_ref.shape[-1]
    acc = None
    for c in range(half // PACK_CHUNK):
        cl = slice(c * PACK_CHUNK, (c + 1) * PACK_CHUNK)
        ch = slice(half + c * PACK_CHUNK, half + (c + 1) * PACK_CHUNK)
        lo, hi = _unpack_halves(xp_ref[:, cl])
        part = (jnp.dot(lo.astype(BF16), wgu(cl), preferred_element_type=F32)
                + jnp.dot(hi.astype(BF16), wgu(ch), preferred_element_type=F32))
        acc = part if acc is None else acc + part
    de = acc.shape[-1] // 2
    return (jax.nn.silu(acc[:, :de]) * acc[:, de:]).astype(BF16)


def _down_packed(hid, wd, o_ref):
    half = o_ref.shape[-1]
    for c in range(half // PACK_CHUNK):
        cl = slice(c * PACK_CHUNK, (c + 1) * PACK_CHUNK)
        ch = slice(half + c * PACK_CHUNK, half + (c + 1) * PACK_CHUNK)
        o_ref[:, cl] = _pack_halves(jnp.dot(hid, wd(cl), preferred_element_type=F32),
                                    jnp.dot(hid, wd(ch), preferred_element_type=F32))


def _expert_weights(sched, i, w_hbm, w_buf, sem):
    be_ref, first_ref, slot_ref, nxt_ref = sched
    s = slot_ref[i]

    def copies(e, sl):
        return [pltpu.make_async_copy(h.at[e], v.at[sl], sem.at[j, sl]) for j, (h, v) in enumerate(zip(w_hbm, w_buf))]

    @pl.when(first_ref[i] == 1)
    def _():
        @pl.when(i == 0)
        def _():
            for c in copies(be_ref[0], 0):
                c.start()

        for c in copies(be_ref[i], s):
            c.wait()

        @pl.when(nxt_ref[i] >= 0)
        def _():
            for c in copies(nxt_ref[i], 1 - s):
                c.start()

    return s


def _expert_up_kernel(be_ref, nu_ref, first_ref, slot_ref, nxt_ref, x_ref, wg_hbm, wu_hbm, hid_ref,
                      wg_buf, wu_buf, sem):
    i = pl.program_id(0)
    used = i < nu_ref[0]
    s = _expert_weights((be_ref, first_ref, slot_ref, nxt_ref), i, (wg_hbm, wu_hbm), (wg_buf, wu_buf), sem)

    def wgu(sl):
        return jnp.concatenate([wg_buf[s, sl, :].astype(BF16), wu_buf[s, sl, :].astype(BF16)], axis=-1)

    @pl.when(used)
    def _():
        hid_ref[...] = _swiglu_hidden(x_ref, wgu)


def _expert_down_kernel(be_ref, nu_ref, first_ref, slot_ref, nxt_ref, hid_ref, wd_hbm, o_ref, wd_buf, sem):
    i = pl.program_id(0)
    used = i < nu_ref[0]
    s = _expert_weights((be_ref, first_ref, slot_ref, nxt_ref), i, (wd_hbm,), (wd_buf,), sem)

    @pl.when(used)
    def _():
        _down_packed(hid_ref[...], lambda sl: wd_buf[s, :, sl].astype(BF16), o_ref)


def _expert_schedule(block_e, n_used):
    nb = block_e.shape[0]
    ar = jnp.arange(nb, dtype=I32)
    used = ar < n_used[0]
    first = used & ((ar == 0) | (block_e != jnp.roll(block_e, 1)))
    slot = jnp.where(used, (jnp.cumsum(first.astype(I32)) - 1) % 2, 0).astype(I32)
    nf = jnp.where(first, ar, nb)
    nxt_idx = jnp.concatenate([lax.cummin(nf, axis=0, reverse=True)[1:], jnp.full((1,), nb, I32)])
    nxt = jnp.where(nxt_idx < nb, block_e[jnp.minimum(nxt_idx, nb - 1)], -1).astype(I32)
    return first.astype(I32), slot, nxt


def expert_ffn(xs, block_e, n_used, wg, wu, wd, blk):
    slots, dw = xs.shape
    d, de = wg.shape[1], wg.shape[2]
    n_blocks = slots // blk
    sched = (block_e, n_used) + _expert_schedule(block_e, n_used)
    cur = lambda i, nu: jnp.minimum(i, nu[0] - 1)
    hbm = pl.BlockSpec(memory_space=pl.ANY)
    up_spec = pltpu.PrefetchScalarGridSpec(
        num_scalar_prefetch=5,
        grid=(n_blocks,),
        in_specs=[pl.BlockSpec((blk, dw), lambda i, be, nu, *_: (cur(i, nu), 0)), hbm, hbm],
        out_specs=pl.BlockSpec((blk, de), lambda i, be, nu, *_: (cur(i, nu), 0)),
        scratch_shapes=[pltpu.VMEM((2, d, de), F32), pltpu.VMEM((2, d, de), F32), pltpu.SemaphoreType.DMA((2, 2))],
    )
    hid = pl.pallas_call(
        _expert_up_kernel,
        grid_spec=up_spec,
        out_shape=jax.ShapeDtypeStruct((slots, de), BF16),
        compiler_params=_cparams(),
        name="expert_up",
    )(*sched, xs, wg, wu)
    down_spec = pltpu.PrefetchScalarGridSpec(
        num_scalar_prefetch=5,
        grid=(n_blocks,),
        in_specs=[pl.BlockSpec((blk, de), lambda i, be, nu, *_: (cur(i, nu), 0)), hbm],
        out_specs=pl.BlockSpec((blk, dw), lambda i, be, nu, *_: (cur(i, nu), 0)),
        scratch_shapes=[pltpu.VMEM((2, de, d), F32), pltpu.SemaphoreType.DMA((1, 2))],
    )
    return pl.pallas_call(
        _expert_down_kernel,
        grid_spec=down_spec,
        out_shape=jax.ShapeDtypeStruct((slots, dw), I32),
        compiler_params=_cparams(),
        name="expert_down",
    )(*sched, hid, wd)


def _final_kernel(dc_ref, dn_ref, x1_ref, sp_ref, gate_ref, ys_ref, gpost_ref, gt_ref, o_ref, buf, sem, *, tm, n_tiles):
    i = pl.program_id(0)
    slot = i % 2
    sub = 8

    def aligned(v):
        return v if isinstance(v, int) else pl.multiple_of(v, sub)

    def issue_rows(dref, s, j):
        base = aligned(j * sub)
        for tt in range(sub):
            for k in range(TOP_K):
                tile = buf.at[s, pl.ds(aligned(k * tm + base), sub), :]
                pltpu.make_async_copy(ys_ref.at[pl.ds(dref[(base + tt) * TOP_K + k], 1), :],
                                      tile.at[pl.ds(tt, 1), :],
                                      sem.at[s]).start(priority=k % 2)

    def reduce_rows(s, j):
        rs = pl.ds(aligned(j * sub), sub)
        g = gate_ref[rs, :]
        acc_lo, acc_hi = _unpack_halves(sp_ref[rs, :])
        for k in range(TOP_K):
            lo, hi = _unpack_halves(buf[s, pl.ds(aligned(k * tm + j * sub), sub), :])
            gk = g[:, k:k + 1]
            acc_lo = acc_lo + gk * lo
            acc_hi = acc_hi + gk * hi
        ffn = jnp.concatenate([acc_lo, acc_hi], axis=-1)
        o_ref[rs, :] = x1_ref[rs, :] + gt_ref[0] * _rms(ffn, gpost_ref[...])

    @pl.when(i == 0)
    def _():
        def first(j, c):
            issue_rows(dc_ref, 0, j)
            return c
        lax.fori_loop(0, tm // sub, first, 0)

    def wait_slot(s):
        pltpu.make_async_copy(ys_ref.at[pl.ds(0, TOP_K * tm), :], buf.at[s], sem.at[s]).wait()

    wait_slot(slot)

    for s in range(2):
        @pl.when(slot == s)
        def _():
            for j in range(tm // sub):
                issue_rows(dn_ref, 1 - s, j)
                reduce_rows(s, j)

    @pl.when(i == n_tiles - 1)
    def _():
        wait_slot(1 - slot)


def final_mix(x1, sp, gates, ys, dest_flat, g_post, gt, tiles_per_batch, tm=COMBINE_TM):
    n, d = x1.shape
    dw = sp.shape[1]
    n_e = gates.shape[1]
    n_tiles = n // tm
    return pl.pallas_call(
        functools.partial(_final_kernel, tm=tm, n_tiles=n_tiles),
        grid=(n_tiles,),
        in_specs=[pl.BlockSpec((tm * TOP_K,), lambda i: (i,), memory_space=pltpu.SMEM),
                  pl.BlockSpec((tm * TOP_K,), lambda i: (jnp.minimum(i + 1, n_tiles - 1),), memory_space=pltpu.SMEM),
                  pl.BlockSpec((tm, d), lambda i: (i, 0)),
                  pl.BlockSpec((tm, dw), lambda i: (i, 0)),
                  pl.BlockSpec((tm, n_e), lambda i: (i, 0)),
                  pl.BlockSpec(memory_space=pl.ANY),
                  pl.BlockSpec((1, d), lambda i: (0, 0)),
                  pl.BlockSpec((1, 1, d), lambda i: (i // tiles_per_batch, 0, 0))],
        out_specs=pl.BlockSpec((tm, d), lambda i: (i, 0)),
        out_shape=jax.ShapeDtypeStruct((n, d), F32),
        scratch_shapes=[pltpu.VMEM((2, TOP_K * tm, dw), I32), pltpu.SemaphoreType.DMA((2,))],
        compiler_params=_cparams(),
        name="final_mix",
    )(dest_flat, dest_flat, x1, sp, gates, ys, g_post.reshape(1, d), gt)


def kernel(x, c, ctx, c_ctx, w_ada, b_ada, g_pre_mix, g_post_mix, g_pre_ffn, g_post_ffn, w_in, rpb, ssm_a_re, ssm_a_im, ssm_log_dt, ssm_b_re, ssm_b_im, ssm_c_re, ssm_c_im, ssm_d, w_glu, b_glu, g_attn_out, g_ssm_out, w_out, w_router, b_router, w_exp_gate, w_exp_up, w_exp_down, w_sh_gate, w_sh_up, w_sh_down):
    b, l, d = x.shape
    lc = ctx.shape[1]
    assert w_ada.shape[0] == 1 and b + 1 <= 8
    n_in = w_in.shape[-1]
    d_ssm = w_glu.shape[-1]
    d_attn = d - d_ssm
    n_heads = d_attn // HEAD_DIM
    n = b * l

    c8 = jnp.concatenate([c, c_ctx[None], jnp.zeros((8 - b - 1, d), F32)], axis=0)
    mod = ada_mod(c8, w_ada[0], b_ada[0]).reshape(8, 6, 1, d)
    sh_m, sc_m, gt_m, sh_f, sc_f, gt_f = [mod[:b, j] for j in range(6)]
    csh_m, csc_m = mod[b:b + 1, 0], mod[b:b + 1, 1]

    w_in_b = w_in[0].astype(BF16)
    tn = PROJ_TN
    p_lat = mod_proj(x, g_pre_mix[0], sc_m, sh_m, w_in_b, 0, n_in, 512, tn)
    p_ctx = mod_proj(ctx.reshape(1, b * lc, d), g_pre_mix[0], csc_m, csh_m, w_in_b,
                     d_attn // tn, n_in - d_attn, 512, tn).reshape(b, lc, n_in - d_attn)

    attn = neighborhood_attention(p_lat, p_ctx, rpb[0], n_heads)

    mats = _ssm_mats(ssm_a_re[0], ssm_a_im[0], ssm_log_dt[0], ssm_b_re[0], ssm_b_im[0],
                     ssm_c_re[0], ssm_c_im[0], ssm_d[0])
    y = s5_mixer(p_lat[..., 3 * d_attn:], p_ctx[..., 2 * d_attn:], mats)
    ssm = s5_glu(y.reshape(n, d_ssm), w_glu[0].astype(BF16), b_glu[0])

    mix = merge_proj(attn.reshape(n, d_attn), ssm, g_attn_out[0], g_ssm_out[0], w_out[0].astype(BF16))
    x1, hp, idx, gates, mask = post_mix(x, mix.reshape(b, l, d), g_post_mix[0], gt_m, g_pre_ffn[0], sc_f, sh_f,
                                        w_router[0].astype(BF16), b_router[0])

    n_e = w_router.shape[-1]
    m = n * TOP_K
    idx = idx.reshape(n, n_e)
    rank, counts = route_ranks(mask.reshape(n, n_e), idx)
    padded = (counts + EXPERT_BLK - 1) // EXPERT_BLK * EXPERT_BLK
    pad_end = jnp.cumsum(padded).astype(I32)
    dest = route_dest(rank, idx, pad_end - padded)
    n_blocks = m // EXPERT_BLK + n_e
    slots = n_blocks * EXPERT_BLK
    block_e = jnp.minimum(jnp.searchsorted(pad_end, jnp.arange(n_blocks) * EXPERT_BLK, side='right'),
                          n_e - 1).astype(I32)
    n_used = (pad_end[-1] // EXPERT_BLK).astype(I32).reshape(1)

    hp2 = hp.reshape(n, d // 2)
    xs, sp = moe_dispatch(hp2, dest, pad_end, (padded - counts).astype(I32), slots,
                          jnp.concatenate([w_sh_gate[0], w_sh_up[0]], axis=-1).astype(BF16), w_sh_down[0].astype(BF16))
    ys = expert_ffn(xs, block_e, n_used, w_exp_gate[0], w_exp_up[0], w_exp_down[0], EXPERT_BLK)
    out = final_mix(x1.reshape(n, d), sp, gates.reshape(n, n_e), ys, dest, g_post_ffn[0], gt_f, l // COMBINE_TM)
    return out.reshape(b, l, d)
```
